```python
import math
import jax, jax.numpy as jnp
from jax import lax
import numpy as np

D_MODEL = 1024
BATCH = 8
SEQ = 2048
DEPTH = 1
DEC_BATCH = 128
DEC_SEQ = 1
PAST_LEN = 16384
PAGE_SIZE = 128

SSM_D_INNER = 2 * D_MODEL
SSM_HEADDIM = 64
SSM_HEADS = SSM_D_INNER // SSM_HEADDIM
SSM_GROUPS = 4
SSM_STATE = 128
SSM_CONV = 4
SSM_CHUNK = 128
SSM_CONV_CH = SSM_D_INNER + 2 * SSM_GROUPS * SSM_STATE
ATT_HEADS = 16
ATT_KV_HEADS = 4
ATT_GROUP = ATT_HEADS // ATT_KV_HEADS
ATT_HEAD_DIM = D_MODEL // ATT_HEADS
ATT_Q_DIM = ATT_HEADS * ATT_HEAD_DIM
ATT_KV_DIM = ATT_KV_HEADS * ATT_HEAD_DIM
WINDOW = 128
ROPE_THETA = 10000.0
MEM_LEN = 256
MEM_HEADS = 4
MEM_HEAD_DIM = D_MODEL // MEM_HEADS
MEM_Q_DIM = MEM_HEADS * MEM_HEAD_DIM
MOE_GROUPS = 4
MOE_EXPERTS_PER_GROUP = 8
MOE_EXPERTS = MOE_GROUPS * MOE_EXPERTS_PER_GROUP
MOE_TOP_K = 2
MOE_D_FF = 256

N_BRANCHES = 3
EPS = 1e-6

OFF_Z = N_BRANCHES * D_MODEL
OFF_XBC = OFF_Z + SSM_D_INNER
OFF_DT = OFF_XBC + SSM_CONV_CH
OFF_Q = OFF_DT + SSM_HEADS
OFF_K = OFF_Q + ATT_Q_DIM
OFF_V = OFF_K + ATT_KV_DIM
OFF_MQ = OFF_V + ATT_KV_DIM
IN_TOTAL = OFF_MQ + MEM_Q_DIM

kernel_name = 'hybrid_ssd_swa_mem_hmoe_step'


def rms_norm(x, g):
    xf = x.astype(jnp.float32)
    y = xf * lax.rsqrt(jnp.mean(xf * xf, axis=-1, keepdims=True) + EPS)
    return (y * g.astype(jnp.float32)).astype(x.dtype)


def rope(x, pos):
    half = x.shape[-1] // 2
    inv = ROPE_THETA ** (-jnp.arange(half, dtype=jnp.float32) / half)
    ang = pos.astype(jnp.float32)[:, None] * inv[None, :]
    cos = jnp.cos(ang)[:, None, :]
    sin = jnp.sin(ang)[:, None, :]
    xf = x.astype(jnp.float32)
    x1, x2 = xf[..., :half], xf[..., half:]
    return jnp.concatenate([x1 * cos - x2 * sin, x2 * cos + x1 * sin], axis=-1).astype(x.dtype)


def front(x, lw):
    hn = rms_norm(x, lw['norm1_g'])
    proj = jnp.einsum('bld,de->ble', hn, lw['w_in'])
    return jnp.split(proj, [OFF_Z, OFF_XBC, OFF_DT, OFF_Q, OFF_K, OFF_V, OFF_MQ], axis=-1)


def causal_conv(xh, w, b):
    L = xh.shape[1] - (SSM_CONV - 1)
    out = xh[:, 0:L] * w[0]
    for j in range(1, SSM_CONV):
        out = out + xh[:, j:j + L] * w[j]
    return out + b


def ssd(x, dt, a, bm, cm, h0, chunk):
    b, L, H, P = x.shape
    G, N = bm.shape[2], bm.shape[3]
    R = H // G
    nc = L // chunk
    f32 = jnp.float32
    xc = x.reshape(b, nc, chunk, G, R, P).astype(f32)
    bc = bm.reshape(b, nc, chunk, G, N).astype(f32)
    cc = cm.reshape(b, nc, chunk, G, N).astype(f32)
    dtc = dt.reshape(b, nc, chunk, G, R)
    acum = jnp.cumsum(dtc * a.reshape(G, R), axis=2)
    t = jnp.arange(chunk)
    causal = (t[:, None] >= t[None, :])[:, :, None, None]
    seg = acum[:, :, :, None] - acum[:, :, None, :]
    decay = jnp.exp(jnp.where(causal, seg, -jnp.inf))
    cb = jnp.einsum('bctgn,bcsgn->bctsg', cc, bc)
    wts = cb[..., None] * decay * dtc[:, :, None]
    y_diag = jnp.einsum('bctsgr,bcsgrp->bctgrp', wts, xc)
    decay_end = jnp.exp(acum[:, :, -1:] - acum)
    states = jnp.einsum('bclgn,bclgr,bclgrp->bcgrpn', bc, decay_end * dtc, xc)
    chunk_decay = jnp.exp(acum[:, :, -1])

    def step(h, inp):
        s, d = inp
        return h * d[..., None, None] + s, h

    h_init = h0.reshape(b, G, R, P, N).astype(f32)
    h_last, h_prev = lax.scan(step, h_init, (jnp.moveaxis(states, 1, 0), jnp.moveaxis(chunk_decay, 1, 0)))
    h_prev = jnp.moveaxis(h_prev, 0, 1)
    y_off = jnp.einsum('bctgn,bcgrpn->bctgrp', cc, h_prev) * jnp.exp(acum)[..., None]
    y = (y_diag + y_off).reshape(b, L, H, P)
    return y.astype(x.dtype), h_last.reshape(b, H, P, N).astype(h0.dtype)


def mamba_branch(z, xbc_hist, dt_raw, h0, chunk, lw):
    b, L = z.shape[0], z.shape[1]
    xbc = jax.nn.silu(causal_conv(xbc_hist, lw['ssm_conv_w'], lw['ssm_conv_b']))
    xs, bm, cm = jnp.split(xbc, [SSM_D_INNER, SSM_D_INNER + SSM_GROUPS * SSM_STATE], axis=-1)
    xs = xs.reshape(b, L, SSM_HEADS, SSM_HEADDIM)
    bm = bm.reshape(b, L, SSM_GROUPS, SSM_STATE)
    cm = cm.reshape(b, L, SSM_GROUPS, SSM_STATE)
    dt = jax.nn.softplus(dt_raw.astype(jnp.float32) + lw['ssm_dt_bias'].astype(jnp.float32))
    a = -jnp.exp(lw['ssm_a_log'].astype(jnp.float32))
    y, h_new = ssd(xs, dt, a, bm, cm, h0, chunk)
    y = y + lw['ssm_d'][:, None].astype(y.dtype) * xs
    y = y.reshape(b, L, SSM_D_INNER)
    return rms_norm(y * jax.nn.silu(z), lw['ssm_norm_g']), h_new


def qkv_prep(q, k, v, pos, lw):
    b, L = q.shape[0], q.shape[1]
    q = rope(rms_norm(q.reshape(b, L, ATT_HEADS, ATT_HEAD_DIM), lw['att_q_norm_g']), pos)
    k = rope(rms_norm(k.reshape(b, L, ATT_KV_HEADS, ATT_HEAD_DIM), lw['att_k_norm_g']), pos)
    return q, k, v.reshape(b, L, ATT_KV_HEADS, ATT_HEAD_DIM)


def window_attend(q, k, v, qpos, kpos, sink):
    s = jnp.einsum('...qkgd,...skd->...kgqs', q, k).astype(jnp.float32) * (ATT_HEAD_DIM ** -0.5)
    diff = qpos[..., :, None] - kpos[..., None, :]
    mask = (diff >= 0) & (diff < WINDOW) & (kpos[..., None, :] >= 0)
    s = jnp.where(mask[..., None, None, :, :], s, -jnp.inf)
    snk = sink.astype(jnp.float32).reshape(ATT_KV_HEADS, ATT_GROUP, 1, 1)
    m = jnp.maximum(jnp.max(s, axis=-1, keepdims=True), snk)
    p = jnp.exp(s - m)
    p = p / (jnp.sum(p, axis=-1, keepdims=True) + jnp.exp(snk - m))
    return jnp.einsum('...kgqs,...skd->...qkgd', p.astype(v.dtype), v)


def swa_prompt(q, k, v, sink):
    b, L = q.shape[0], q.shape[1]
    nb = L // WINDOW
    qb = q.reshape(b, nb, WINDOW, ATT_KV_HEADS, ATT_GROUP, ATT_HEAD_DIM)

    def with_prev(t):
        t = t.reshape(b, nb, WINDOW, ATT_KV_HEADS, ATT_HEAD_DIM)
        prev = jnp.concatenate([jnp.zeros_like(t[:, :1]), t[:, :-1]], axis=1)
        return jnp.concatenate([prev, t], axis=2)

    qpos = jnp.arange(L).reshape(nb, WINDOW)
    kpos = jnp.concatenate([qpos - WINDOW, qpos], axis=1)
    o = window_attend(qb, with_prev(k), with_prev(v), qpos, kpos, sink)
    return o.reshape(b, L, ATT_Q_DIM)


def mem_kv(mem, lw):
    b, M = mem.shape[0], mem.shape[1]
    kv = jnp.einsum('bmd,de->bme', rms_norm(mem, lw['mem_norm_g']), lw['w_mem_kv'])
    kv = kv.reshape(b, M, 2, MEM_HEADS, MEM_HEAD_DIM)
    return rms_norm(kv[:, :, 0], lw['mem_k_norm_g']), kv[:, :, 1]


def mem_attend(mq, mk, mv, lw):
    b, L = mq.shape[0], mq.shape[1]
    q = rms_norm(mq.reshape(b, L, MEM_HEADS, MEM_HEAD_DIM), lw['mem_q_norm_g'])
    s = jnp.einsum('blhd,bmhd->bhlm', q, mk).astype(jnp.float32) * (MEM_HEAD_DIM ** -0.5)
    p = jax.nn.softmax(s, axis=-1).astype(mv.dtype)
    return jnp.einsum('bhlm,bmhd->blhd', p, mv).reshape(b, L, MEM_Q_DIM)


def hier_moe(h, lw):
    T = h.shape[0]
    lg = jnp.einsum('td,dg->tg', h, lw['w_router_group']).astype(jnp.float32) + lw['b_router_group'].astype(jnp.float32)
    pg = jax.nn.softmax(lg, axis=-1)
    g = jnp.argmax(pg, axis=-1)
    pg_top = jnp.take_along_axis(pg, g[:, None], axis=-1)
    le = jnp.einsum('td,de->te', h, lw['w_router_expert']).astype(jnp.float32) + lw['b_router_expert'].astype(jnp.float32)
    le = le.reshape(T, MOE_GROUPS, MOE_EXPERTS_PER_GROUP)
    le_g = jnp.take_along_axis(le, g[:, None, None], axis=1)[:, 0]
    pe = jax.nn.softmax(le_g, axis=-1)
    topv, topi = lax.top_k(pe, MOE_TOP_K)
    wts = pg_top * topv / jnp.sum(topv, axis=-1, keepdims=True)
    eid = g[:, None] * MOE_EXPERTS_PER_GROUP + topi
    comb = jnp.sum(jax.nn.one_hot(eid, MOE_EXPERTS, dtype=jnp.float32) * wts[..., None], axis=1).astype(h.dtype)
    out = jnp.zeros_like(h)
    for e in range(MOE_EXPERTS):
        act = jax.nn.silu(h @ lw['w_exp_gate'][e]) * (h @ lw['w_exp_up'][e])
        out = out + comb[:, e:e + 1] * (act @ lw['w_exp_down'][e])
    return out


def back(x, gates, y_ssm, y_swa, y_mem, lw):
    gt = jax.nn.sigmoid(gates.astype(jnp.float32)).astype(x.dtype)
    gt = gt.reshape(gates.shape[0], gates.shape[1], N_BRANCHES, D_MODEL)
    merged = (gt[:, :, 0] * jnp.einsum('ble,ed->bld', y_ssm, lw['w_br_ssm'])
              + gt[:, :, 1] * jnp.einsum('ble,ed->bld', y_swa, lw['w_br_swa'])
              + gt[:, :, 2] * jnp.einsum('ble,ed->bld', y_mem, lw['w_br_mem']))
    x1 = x + jnp.einsum('bld,de->ble', merged, lw['w_out'])
    b, L = x1.shape[0], x1.shape[1]
    h = rms_norm(x1, lw['norm2_g']).reshape(b * L, D_MODEL)
    return x1 + hier_moe(h, lw).reshape(b, L, D_MODEL)


def prompt_layer(x, mem, lw):
    b, L = x.shape[0], x.shape[1]
    gates, z, xbc, dt_raw, q, k, v, mq = front(x, lw)
    xbc_hist = jnp.pad(xbc, ((0, 0), (SSM_CONV - 1, 0), (0, 0)))
    h0 = jnp.zeros((b, SSM_HEADS, SSM_HEADDIM, SSM_STATE), x.dtype)
    y_ssm, ssm_new = mamba_branch(z, xbc_hist, dt_raw, h0, SSM_CHUNK, lw)
    conv_new = xbc[:, L - (SSM_CONV - 1):]
    q, k, v = qkv_prep(q, k, v, jnp.arange(L), lw)
    y_swa = swa_prompt(q, k, v, lw['att_sink'])
    mk, mv = mem_kv(mem, lw)
    y_mem = mem_attend(mq, mk, mv, lw)
    y = back(x, gates, y_ssm, y_swa, y_mem, lw)
    return y, conv_new, ssm_new, k[:, L - WINDOW:], v[:, L - WINDOW:], mk, mv


def sample_layer(x, conv_st, ssm_st, swa_k, swa_v, mk, mv, lw):
    b, L = x.shape[0], x.shape[1]
    gates, z, xbc, dt_raw, q, k, v, mq = front(x, lw)
    xbc_hist = jnp.concatenate([conv_st, xbc], axis=1)
    y_ssm, ssm_new = mamba_branch(z, xbc_hist, dt_raw, ssm_st, L, lw)
    conv_new = xbc_hist[:, L:]
    qpos = PAST_LEN + jnp.arange(L)
    q, k, v = qkv_prep(q, k, v, qpos, lw)
    kk = jnp.concatenate([swa_k, k], axis=1)
    vv = jnp.concatenate([swa_v, v], axis=1)
    kpos = PAST_LEN - WINDOW + jnp.arange(WINDOW + L)
    qb = q.reshape(b, L, ATT_KV_HEADS, ATT_GROUP, ATT_HEAD_DIM)
    y_swa = window_attend(qb, kk, vv, qpos, kpos, lw['att_sink']).reshape(b, L, ATT_Q_DIM)
    y_mem = mem_attend(mq, mk, mv, lw)
    y = back(x, gates, y_ssm, y_swa, y_mem, lw)
    return y, conv_new, ssm_new, kk[:, L:], vv[:, L:]


def setup_inputs(seed: int = 0) -> dict:
    key = jax.random.key(seed)
    ks = iter(jax.random.split(key, 48))
    f32 = jnp.float32

    def nrm(shape, scale):
        return jax.random.normal(next(ks), shape, f32) * scale

    def gain(shape):
        return 1.0 + nrm(shape, 0.05)

    dt = jnp.exp(jax.random.uniform(next(ks), (DEPTH, SSM_HEADS), f32, math.log(1e-3), math.log(1e-1)))
    dt_bias = dt + jnp.log(-jnp.expm1(-dt))
    a_log = jnp.log(jax.random.uniform(next(ks), (DEPTH, SSM_HEADS), f32, 1.0, 16.0))
    return {
        'x_prompt': nrm((BATCH, SEQ, D_MODEL), 1.0),
        'x_sample': nrm((DEC_BATCH, DEC_SEQ, D_MODEL), 1.0),
        'state_conv': nrm((DEPTH, DEC_BATCH, SSM_CONV - 1, SSM_CONV_CH), 1.0),
        'state_ssm': nrm((DEPTH, DEC_BATCH, SSM_HEADS, SSM_HEADDIM, SSM_STATE), 0.5),
        'cache_swa_k': nrm((DEPTH, DEC_BATCH, WINDOW, ATT_KV_HEADS, ATT_HEAD_DIM), 1.0),
        'cache_swa_v': nrm((DEPTH, DEC_BATCH, WINDOW, ATT_KV_HEADS, ATT_HEAD_DIM), 1.0),
        'cache_mem_k': nrm((DEPTH, DEC_BATCH, MEM_LEN, MEM_HEADS, MEM_HEAD_DIM), 1.0),
        'cache_mem_v': nrm((DEPTH, DEC_BATCH, MEM_LEN, MEM_HEADS, MEM_HEAD_DIM), 1.0),
        'mem_prompt': nrm((BATCH, MEM_LEN, D_MODEL), 1.0),
        'norm1_g': gain((DEPTH, D_MODEL)),
        'w_in': nrm((DEPTH, D_MODEL, IN_TOTAL), D_MODEL ** -0.5),
        'ssm_conv_w': nrm((DEPTH, SSM_CONV, SSM_CONV_CH), SSM_CONV ** -0.5),
        'ssm_conv_b': nrm((DEPTH, SSM_CONV_CH), 0.02),
        'ssm_dt_bias': dt_bias,
        'ssm_a_log': a_log,
        'ssm_d': gain((DEPTH, SSM_HEADS)),
        'ssm_norm_g': gain((DEPTH, SSM_D_INNER)),
        'att_q_norm_g': gain((DEPTH, ATT_HEAD_DIM)),
        'att_k_norm_g': gain((DEPTH, ATT_HEAD_DIM)),
        'att_sink': nrm((DEPTH, ATT_HEADS), 0.5),
        'mem_norm_g': gain((DEPTH, D_MODEL)),
        'w_mem_kv': nrm((DEPTH, D_MODEL, 2 * MEM_Q_DIM), D_MODEL ** -0.5),
        'mem_q_norm_g': gain((DEPTH, MEM_HEAD_DIM)),
        'mem_k_norm_g': gain((DEPTH, MEM_HEAD_DIM)),
        'w_br_ssm': nrm((DEPTH, SSM_D_INNER, D_MODEL), SSM_D_INNER ** -0.5),
        'w_br_swa': nrm((DEPTH, ATT_Q_DIM, D_MODEL), ATT_Q_DIM ** -0.5),
        'w_br_mem': nrm((DEPTH, MEM_Q_DIM, D_MODEL), MEM_Q_DIM ** -0.5),
        'w_out': nrm((DEPTH, D_MODEL, D_MODEL), D_MODEL ** -0.5),
        'norm2_g': gain((DEPTH, D_MODEL)),
        'w_router_group': nrm((DEPTH, D_MODEL, MOE_GROUPS), D_MODEL ** -0.5),
        'b_router_group': nrm((DEPTH, MOE_GROUPS), 0.01),
        'w_router_expert': nrm((DEPTH, D_MODEL, MOE_EXPERTS), D_MODEL ** -0.5),
        'b_router_expert': nrm((DEPTH, MOE_EXPERTS), 0.01),
        'w_exp_gate': nrm((DEPTH, MOE_EXPERTS, D_MODEL, MOE_D_FF), D_MODEL ** -0.5),
        'w_exp_up': nrm((DEPTH, MOE_EXPERTS, D_MODEL, MOE_D_FF), D_MODEL ** -0.5),
        'w_exp_down': nrm((DEPTH, MOE_EXPERTS, MOE_D_FF, D_MODEL), MOE_D_FF ** -0.5),
    }


def reference(x_prompt, x_sample, state_conv, state_ssm, cache_swa_k, cache_swa_v, cache_mem_k, cache_mem_v,
              mem_prompt, norm1_g, w_in, ssm_conv_w, ssm_conv_b, ssm_dt_bias, ssm_a_log, ssm_d, ssm_norm_g,
              att_q_norm_g, att_k_norm_g, att_sink, mem_norm_g, w_mem_kv, mem_q_norm_g, mem_k_norm_g,
              w_br_ssm, w_br_swa, w_br_mem, w_out, norm2_g, w_router_group, b_router_group,
              w_router_expert, b_router_expert, w_exp_gate, w_exp_up, w_exp_down):
    xp, xs = x_prompt, x_sample
    p_conv, p_ssm, p_k, p_v, p_mk, p_mv = [], [], [], [], [], []
    s_conv, s_ssm, s_k, s_v = [], [], [], []
    for l in range(DEPTH):
        lw = dict(norm1_g=norm1_g[l], w_in=w_in[l], ssm_conv_w=ssm_conv_w[l], ssm_conv_b=ssm_conv_b[l],
                  ssm_dt_bias=ssm_dt_bias[l], ssm_a_log=ssm_a_log[l], ssm_d=ssm_d[l], ssm_norm_g=ssm_norm_g[l],
                  att_q_norm_g=att_q_norm_g[l], att_k_norm_g=att_k_norm_g[l], att_sink=att_sink[l],
                  mem_norm_g=mem_norm_g[l], w_mem_kv=w_mem_kv[l], mem_q_norm_g=mem_q_norm_g[l],
                  mem_k_norm_g=mem_k_norm_g[l], w_br_ssm=w_br_ssm[l], w_br_swa=w_br_swa[l],
                  w_br_mem=w_br_mem[l], w_out=w_out[l], norm2_g=norm2_g[l], w_router_group=w_router_group[l],
                  b_router_group=b_router_group[l], w_router_expert=w_router_expert[l],
                  b_router_expert=b_router_expert[l], w_exp_gate=w_exp_gate[l], w_exp_up=w_exp_up[l],
                  w_exp_down=w_exp_down[l])
        xp, c1, c2, c3, c4, c5, c6 = prompt_layer(xp, mem_prompt, lw)
        p_conv.append(c1); p_ssm.append(c2); p_k.append(c3); p_v.append(c4); p_mk.append(c5); p_mv.append(c6)
        xs, d1, d2, d3, d4 = sample_layer(xs, state_conv[l], state_ssm[l], cache_swa_k[l], cache_swa_v[l],
                                          cache_mem_k[l], cache_mem_v[l], lw)
        s_conv.append(d1); s_ssm.append(d2); s_k.append(d3); s_v.append(d4)
    return (xp, xs,
            jnp.stack(p_conv), jnp.stack(p_ssm), jnp.stack(p_k), jnp.stack(p_v), jnp.stack(p_mk), jnp.stack(p_mv),
            jnp.stack(s_conv), jnp.stack(s_ssm), jnp.stack(s_k), jnp.stack(s_v))
```

```python
import functools
import math

import jax
import jax.numpy as jnp
from jax import lax
from jax.experimental import pallas as pl
from jax.experimental.pallas import tpu as pltpu

F32 = jnp.float32
BF16 = jnp.bfloat16
HIGHEST = lax.Precision.HIGHEST

D = 1024
SSM_INNER = 2048
SSM_HEADDIM = 64
SSM_HEADS = 32
SSM_GROUPS = 4
SSM_HPG = SSM_HEADS // SSM_GROUPS
SSM_STATE = 128
SSM_CONV = 4
CONV_CH = SSM_INNER + 2 * SSM_GROUPS * SSM_STATE
CHUNK = 128
ATT_HEADS = 16
ATT_KV = 4
ATT_HD = 64
ATT_KVD = ATT_KV * ATT_HD
WINDOW = 128
ROPE_THETA = 10000.0
MEM_HEADS = 4
MEM_HD = 256
N_EXPERTS = 32
N_GROUPS = 4
EPG = 8
D_FF = 256
EPS = 1e-6
PAST_LEN = 16384

OFF_Z = 3 * D
OFF_XBC = OFF_Z + SSM_INNER
OFF_DT = OFF_XBC + CONV_CH
OFF_Q = OFF_DT + SSM_HEADS
OFF_K = OFF_Q + D
OFF_V = OFF_K + ATT_KVD
OFF_MQ = OFF_V + ATT_KVD

N_MAIN = 3 * D + CONV_CH + SSM_INNER + D + D + 2 * ATT_KVD
COL_XBC_BLK = 1
COL_Z_BLK = 3
COL_Q_BLK = 8
COL_MQ_BLK = 9
COL_K_BLK = 40
COL_V_BLK = 41
LANE = 128
ROUTE_W = 128

VMEM_LIMIT = 56 * 1024 * 1024


def _cparams(sem):
    return pltpu.CompilerParams(dimension_semantics=sem, vmem_limit_bytes=VMEM_LIMIT)


def _sigmoid(x):
    return 1.0 / (1.0 + jnp.exp(-x))


def _silu(x):
    return x * _sigmoid(x)


def _softplus(x):
    return jnp.maximum(x, 0.0) + jnp.log1p(jnp.exp(-jnp.abs(x)))


def _dot(a, b):
    return jnp.dot(a, b, preferred_element_type=F32)


def _dot_nt(a, b):
    return lax.dot_general(a, b, (((1,), (1,)), ((), ())), preferred_element_type=F32)


def _dot_tn(a, b):
    return lax.dot_general(a, b, (((0,), (0,)), ((), ())), preferred_element_type=F32)


def _dot_exact(a, b):
    return jnp.dot(a, b, preferred_element_type=F32, precision=HIGHEST)


def _front_kernel(x_ref, g_ref, w_ref, wdt_ref, o_ref, dt_ref, hn_ref):
    @pl.when(pl.program_id(1) == 0)
    def _():
        x = x_ref[...]
        ms = jnp.mean(x * x, axis=-1, keepdims=True)
        hn = (x * lax.rsqrt(ms + EPS) * g_ref[...]).astype(BF16)
        hn_ref[...] = hn
        dt_ref[...] = _dot(hn, wdt_ref[...])

    o_ref[...] = _dot(hn_ref[...], w_ref[...]).astype(BF16)


def _front(x, g, w_main, w_dt, tm, tn):
    t = x.shape[0]
    return pl.pallas_call(
        _front_kernel,
        grid=(t // tm, N_MAIN // tn),
        in_specs=[
            pl.BlockSpec((tm, D), lambda i, j: (i, 0)),
            pl.BlockSpec((1, D), lambda i, j: (0, 0)),
            pl.BlockSpec((D, tn), lambda i, j: (0, j)),
            pl.BlockSpec((D, LANE), lambda i, j: (0, 0)),
        ],
        out_specs=[
            pl.BlockSpec((tm, tn), lambda i, j: (i, j)),
            pl.BlockSpec((tm, LANE), lambda i, j: (i, 0)),
        ],
        out_shape=[
            jax.ShapeDtypeStruct((t, N_MAIN), BF16),
            jax.ShapeDtypeStruct((t, LANE), F32),
        ],
        scratch_shapes=[pltpu.VMEM((tm, D), BF16)],
        compiler_params=_cparams(("parallel", "arbitrary")),
        name="front",
    )(x, g, w_main, w_dt)


HIST_PAD = 8


def _ssd_kernel(xbc_ref, z_ref, dt_ref, cw_ref, cb_ref, dtb_ref, a_ref, dexp_ref, ng_ref, e_ref,
                y_ref, conv_ref, ssm_ref, hist_ref, h_ref):
    c = pl.program_id(1)

    @pl.when(c == 0)
    def _():
        hist_ref[0:HIST_PAD, :] = jnp.zeros((HIST_PAD, CONV_CH), F32)
        h_ref[...] = jnp.zeros_like(h_ref)

    x_raw = xbc_ref[...].astype(F32)
    hist_ref[HIST_PAD:HIST_PAD + CHUNK, :] = x_raw
    acc = x_raw * cw_ref[SSM_CONV - 1:SSM_CONV, :] + cb_ref[...]
    for j in range(SSM_CONV - 1):
        lo = HIST_PAD - (SSM_CONV - 1) + j
        acc = acc + hist_ref[lo:lo + CHUNK, :] * cw_ref[j:j + 1, :]
    tail = hist_ref[HIST_PAD + CHUNK - (SSM_CONV - 1):HIST_PAD + CHUNK, :]
    hist_ref[HIST_PAD - (SSM_CONV - 1):HIST_PAD, :] = tail
    conv_ref[0] = tail
    act = _silu(acc)

    dt = _softplus(dt_ref[...] + dtb_ref[...])
    da = dt * a_ref[...]
    row = lax.broadcasted_iota(jnp.int32, (CHUNK, CHUNK), 0)
    col = lax.broadcasted_iota(jnp.int32, (CHUNK, CHUNK), 1)
    causal = row >= col
    acum = _dot_exact(causal.astype(F32), da)
    acum_t = acum.T
    dt_t = dt.T
    e = e_ref[...]
    exp_a = _dot_exact(jnp.exp(acum), e)
    last = acum[CHUNK - 1:CHUNK, :]
    w_end = _dot_exact(jnp.exp(last - acum) * dt, e)
    cd = jnp.broadcast_to(jnp.exp(acum_t[:, CHUNK - 1:CHUNK]), (LANE, SSM_STATE))
    lane = lax.broadcasted_iota(jnp.int32, (CHUNK, LANE), 1)
    low_half = lane < SSM_HEADDIM

    xs_off = 0
    b_off = SSM_INNER
    c_off = SSM_INNER + SSM_GROUPS * SSM_STATE
    for g in range(SSM_GROUPS):
        bm = act[:, b_off + g * SSM_STATE:b_off + (g + 1) * SSM_STATE].astype(BF16)
        cm = act[:, c_off + g * SSM_STATE:c_off + (g + 1) * SSM_STATE].astype(BF16)
        cb = _dot_nt(cm, bm)
        gw = SSM_HPG * SSM_HEADDIM
        ch0 = g * gw
        h_prev = h_ref[ch0:ch0 + gw, :]
        y_off = _dot_nt(cm, h_prev.astype(BF16)) * exp_a[:, ch0:ch0 + gw]
        xg = act[:, xs_off + ch0:xs_off + ch0 + gw]
        for pr in range(SSM_HPG // 2):
            xpair = xg[:, pr * LANE:(pr + 1) * LANE].astype(BF16)
            yd = []
            for sub in range(2):
                hd = g * SSM_HPG + pr * 2 + sub
                seg = acum[:, hd:hd + 1] - acum_t[hd:hd + 1, :]
                decay = jnp.exp(jnp.where(causal, seg, -jnp.inf))
                wts = cb * decay * dt_t[hd:hd + 1, :]
                yd.append(_dot(wts.astype(BF16), xpair))
            y_pair = jnp.where(low_half, yd[0], yd[1]) + y_off[:, pr * LANE:(pr + 1) * LANE]
            cl = ch0 + pr * LANE
            xp32 = xg[:, pr * LANE:(pr + 1) * LANE]
            y_pair = y_pair + dexp_ref[:, cl:cl + LANE] * xp32
            y_ref[:, cl:cl + LANE] = y_pair.astype(F32)
        xw = (xg * w_end[:, ch0:ch0 + gw]).astype(BF16)
        states = _dot_tn(xw, bm)
        for r in range(SSM_HPG):
            hd = g * SSM_HPG + r
            r0 = ch0 + r * SSM_HEADDIM
            h_ref[r0:r0 + SSM_HEADDIM, :] = (h_ref[r0:r0 + SSM_HEADDIM, :] * cd[hd:hd + 1, :]
                                             + states[r * SSM_HEADDIM:(r + 1) * SSM_HEADDIM, :])

    ssm_ref[0] = h_ref[...]
    yg = y_ref[...] * _silu(z_ref[...].astype(F32))
    ms = jnp.mean(yg * yg, axis=-1, keepdims=True)
    y_ref[...] = yg * lax.rsqrt(ms + EPS) * ng_ref[...]


def _ssd(proj, dt_raw, cw, cb, dtb, a, dexp, ng, e, nb, nc):
    t = proj.shape[0]
    return pl.pallas_call(
        _ssd_kernel,
        grid=(nb, nc),
        in_specs=[
            pl.BlockSpec((CHUNK, CONV_CH), lambda b, c: (b * nc + c, COL_XBC_BLK)),
            pl.BlockSpec((CHUNK, SSM_INNER), lambda b, c: (b * nc + c, COL_Z_BLK)),
            pl.BlockSpec((CHUNK, LANE), lambda b, c: (b * nc + c, 0)),
            pl.BlockSpec((SSM_CONV, CONV_CH), lambda b, c: (0, 0)),
            pl.BlockSpec((1, CONV_CH), lambda b, c: (0, 0)),
            pl.BlockSpec((1, LANE), lambda b, c: (0, 0)),
            pl.BlockSpec((1, LANE), lambda b, c: (0, 0)),
            pl.BlockSpec((1, SSM_INNER), lambda b, c: (0, 0)),
            pl.BlockSpec((1, SSM_INNER), lambda b, c: (0, 0)),
            pl.BlockSpec((LANE, SSM_INNER), lambda b, c: (0, 0)),
        ],
        out_specs=[
            pl.BlockSpec((CHUNK, SSM_INNER), lambda b, c: (b * nc + c, 0)),
            pl.BlockSpec((1, SSM_CONV - 1, CONV_CH), lambda b, c: (b, 0, 0)),
            pl.BlockSpec((1, SSM_INNER, SSM_STATE), lambda b, c: (b, 0, 0)),
        ],
        out_shape=[
            jax.ShapeDtypeStruct((t, SSM_INNER), F32),
            jax.ShapeDtypeStruct((nb, SSM_CONV - 1, CONV_CH), F32),
            jax.ShapeDtypeStruct((nb, SSM_INNER, SSM_STATE), F32),
        ],
        scratch_shapes=[
            pltpu.VMEM((HIST_PAD + CHUNK, CONV_CH), F32),
            pltpu.VMEM((SSM_INNER, SSM_STATE), F32),
        ],
        compiler_params=_cparams(("parallel", "arbitrary")),
        name="ssd_prompt",
    )(proj, proj, dt_raw, cw, cb, dtb, a, dexp, ng, e)


def _norm_rope_tile(xj, g2, cos_t, sin_lo, sin_hi, low_half):
    sq = xj * xj
    s_lo = jnp.sum(jnp.where(low_half, sq, 0.0), axis=-1, keepdims=True)
    s_hi = jnp.sum(jnp.where(low_half, 0.0, sq), axis=-1, keepdims=True)
    ms = jnp.where(low_half, s_lo, s_hi) * (1.0 / ATT_HD)
    xn = xj * lax.rsqrt(ms + EPS) * g2
    half = ATT_HD // 2
    return (xn * cos_t + pltpu.roll(xn, LANE - half, axis=1) * sin_lo
            + pltpu.roll(xn, half, axis=1) * sin_hi)


def _rope_tables(pos):
    half = ATT_HD // 2
    inv = ROPE_THETA ** (-jnp.arange(half, dtype=F32) / half)
    ang = pos.astype(F32)[:, None] * inv[None, :]
    cos = jnp.cos(ang)
    sin = jnp.sin(ang)
    zero = jnp.zeros_like(sin)
    cos_t = jnp.concatenate([cos, cos, cos, cos], axis=1)
    sin_lo = jnp.concatenate([-sin, zero, -sin, zero], axis=1)
    sin_hi = jnp.concatenate([zero, sin, zero, sin], axis=1)
    return cos_t, sin_lo, sin_hi


def _swa_kernel(q_ref, k_ref, v_ref, qg_ref, kg_ref, cos_ref, slo_ref, shi_ref, sink_ref,
                y_ref, ko_ref, vo_ref, kk_ref, kkr_ref, vv_ref, vvr_ref):
    c = pl.program_id(1)
    w = WINDOW
    lane = lax.broadcasted_iota(jnp.int32, (w, LANE), 1)
    low_half = lane < ATT_HD
    cos_t = cos_ref[...]
    sin_lo = slo_ref[...]
    sin_hi = shi_ref[...]

    @pl.when(c == 0)
    def _():
        zeros = jnp.zeros((w, ATT_KVD), BF16)
        kk_ref[0:w, :] = zeros
        kkr_ref[0:w, :] = zeros
        vv_ref[0:w, :] = zeros
        vvr_ref[0:w, :] = zeros

    kf = k_ref[...].astype(F32)
    vf = v_ref[...].astype(F32)
    for j in range(ATT_KVD // LANE):
        sl = slice(j * LANE, (j + 1) * LANE)
        kn = _norm_rope_tile(kf[:, sl], kg_ref[...], cos_t, sin_lo, sin_hi, low_half)
        ko_ref[0, :, sl] = kn
        kk_ref[w:2 * w, sl] = kn.astype(BF16)
        kkr_ref[w:2 * w, sl] = pltpu.roll(kn, ATT_HD, axis=1).astype(BF16)
        vj = vf[:, sl]
        vv_ref[w:2 * w, sl] = vj.astype(BF16)
        vvr_ref[w:2 * w, sl] = pltpu.roll(vj, ATT_HD, axis=1).astype(BF16)
    vo_ref[0] = vf

    qi = lax.broadcasted_iota(jnp.int32, (w, 2 * w), 0)
    kj = lax.broadcasted_iota(jnp.int32, (w, 2 * w), 1)
    mask = ((kj < w) & (kj > qi) & (c > 0)) | ((kj >= w) & ((kj - w) <= qi))
    scale = ATT_HD ** -0.5

    qf = q_ref[...].astype(F32)
    for j in range(D // LANE):
        sl = slice(j * LANE, (j + 1) * LANE)
        qn = _norm_rope_tile(qf[:, sl], qg_ref[...], cos_t, sin_lo, sin_hi, low_half)
        outs = []
        for par in range(2):
            h = 2 * j + par
            kv = h // (ATT_HEADS // ATT_KV)
            kcol = kv // 2
            ksl = slice(kcol * LANE, (kcol + 1) * LANE)
            same = (kv % 2) == par
            in_half = low_half if par == 0 else jnp.logical_not(low_half)
            qh = jnp.where(in_half, qn, 0.0).astype(BF16)
            kmat = kk_ref[:, ksl] if same else kkr_ref[:, ksl]
            vmat = vv_ref[:, ksl] if same else vvr_ref[:, ksl]
            s = _dot_nt(qh, kmat) * scale
            s = jnp.where(mask, s, -jnp.inf)
            snk = sink_ref[:, h:h + 1]
            m = jnp.maximum(jnp.max(s, axis=-1, keepdims=True), snk)
            p = jnp.exp(s - m)
            den = jnp.sum(p, axis=-1, keepdims=True) + jnp.exp(snk - m)
            outs.append(_dot(p.astype(BF16), vmat) / den)
        y_ref[:, sl] = jnp.where(low_half, outs[0], outs[1]).astype(BF16)

    kk_ref[0:w, :] = kk_ref[w:2 * w, :]
    kkr_ref[0:w, :] = kkr_ref[w:2 * w, :]
    vv_ref[0:w, :] = vv_ref[w:2 * w, :]
    vvr_ref[0:w, :] = vvr_ref[w:2 * w, :]


def _swa(proj, qg2, kg2, cos_t, sin_lo, sin_hi, sink, nb, nc):
    t = proj.shape[0]
    w = WINDOW
    tab = pl.BlockSpec((w, LANE), lambda b, c: (c, 0))
    return pl.pallas_call(
        _swa_kernel,
        grid=(nb, nc),
        in_specs=[
            pl.BlockSpec((w, D), lambda b, c: (b * nc + c, COL_Q_BLK)),
            pl.BlockSpec((w, ATT_KVD), lambda b, c: (b * nc + c, COL_K_BLK)),
            pl.BlockSpec((w, ATT_KVD), lambda b, c: (b * nc + c, COL_V_BLK)),
            pl.BlockSpec((1, LANE), lambda b, c: (0, 0)),
            pl.BlockSpec((1, LANE), lambda b, c: (0, 0)),
            tab, tab, tab,
            pl.BlockSpec((1, ATT_HEADS), lambda b, c: (0, 0)),
        ],
        out_specs=[
            pl.BlockSpec((w, D), lambda b, c: (b * nc + c, 0)),
            pl.BlockSpec((1, w, ATT_KVD), lambda b, c: (b, 0, 0)),
            pl.BlockSpec((1, w, ATT_KVD), lambda b, c: (b, 0, 0)),
        ],
        out_shape=[
            jax.ShapeDtypeStruct((t, D), BF16),
            jax.ShapeDtypeStruct((nb, w, ATT_KVD), F32),
            jax.ShapeDtypeStruct((nb, w, ATT_KVD), F32),
        ],
        scratch_shapes=[pltpu.VMEM((2 * w, ATT_KVD), BF16)] * 4,
        compiler_params=_cparams(("parallel", "arbitrary")),
        name="swa_prompt",
    )(proj, proj, proj, qg2, kg2, cos_t, sin_lo, sin_hi, sink)


def _memkv_kernel(m_ref, g_ref, w_ref, kg_ref, k_ref, v_ref):
    x = m_ref[...]
    ms = jnp.mean(x * x, axis=-1, keepdims=True)
    hn = (x * lax.rsqrt(ms + EPS) * g_ref[...]).astype(BF16)
    kv = _dot(hn, w_ref[...])
    for h in range(MEM_HEADS):
        kh = kv[:, h * MEM_HD:(h + 1) * MEM_HD]
        ms = jnp.mean(kh * kh, axis=-1, keepdims=True)
        k_ref[:, h * MEM_HD:(h + 1) * MEM_HD] = kh * lax.rsqrt(ms + EPS) * kg_ref[...]
    v_ref[...] = kv[:, D:]


def _memkv(mem, g, w, kg, tm):
    t = mem.shape[0]
    return pl.pallas_call(
        _memkv_kernel,
        grid=(t // tm,),
        in_specs=[
            pl.BlockSpec((tm, D), lambda i: (i, 0)),
            pl.BlockSpec((1, D), lambda i: (0, 0)),
            pl.BlockSpec((D, 2 * D), lambda i: (0, 0)),
            pl.BlockSpec((1, MEM_HD), lambda i: (0, 0)),
        ],
        out_specs=[
            pl.BlockSpec((tm, D), lambda i: (i, 0)),
            pl.BlockSpec((tm, D), lambda i: (i, 0)),
        ],
        out_shape=[jax.ShapeDtypeStruct((t, D), F32)] * 2,
        compiler_params=_cparams(("parallel",)),
        name="mem_kv",
    )(mem, g, w, kg)


def _memattn_kernel(q_ref, k_ref, v_ref, qg_ref, y_ref):
    q = q_ref[...].astype(F32)
    scale = MEM_HD ** -0.5
    for h in range(MEM_HEADS):
        sl = slice(h * MEM_HD, (h + 1) * MEM_HD)
        qh = q[:, sl]
        ms = jnp.mean(qh * qh, axis=-1, keepdims=True)
        qn = (qh * lax.rsqrt(ms + EPS) * qg_ref[...]).astype(BF16)
        s = _dot_nt(qn, k_ref[:, sl].astype(BF16)) * scale
        m = jnp.max(s, axis=-1, keepdims=True)
        p = jnp.exp(s - m)
        den = jnp.sum(p, axis=-1, keepdims=True)
        o = _dot(p.astype(BF16), v_ref[:, sl].astype(BF16)) / den
        y_ref[:, sl] = o.astype(BF16)


def _memattn(proj, mk, mv, qg, nb, seq, mem_len, tm):
    t = proj.shape[0]
    nt = seq // tm
    return pl.pallas_call(
        _memattn_kernel,
        grid=(nb, nt),
        in_specs=[
            pl.BlockSpec((tm, D), lambda b, i: (b * nt + i, COL_MQ_BLK)),
            pl.BlockSpec((mem_len, D), lambda b, i: (b, 0)),
            pl.BlockSpec((mem_len, D), lambda b, i: (b, 0)),
            pl.BlockSpec((1, MEM_HD), lambda b, i: (0, 0)),
        ],
        out_specs=pl.BlockSpec((tm, D), lambda b, i: (b * nt + i, 0)),
        out_shape=jax.ShapeDtypeStruct((t, D), BF16),
        compiler_params=_cparams(("parallel", "arbitrary")),
        name="mem_attn_prompt",
    )(proj, mk, mv, qg)


def _route(logits):
    lane = lax.broadcasted_iota(jnp.int32, logits.shape, 1)
    lanef = lane.astype(F32)
    is_g = lane < N_GROUPS
    neg = -jnp.inf
    big = 1e9
    gl = jnp.where(is_g, logits, neg)
    gmax = jnp.max(gl, axis=-1, keepdims=True)
    gidx = jnp.min(jnp.where(is_g & (logits == gmax), lanef, big), axis=-1, keepdims=True)
    pg_top = 1.0 / jnp.sum(jnp.where(is_g, jnp.exp(logits - gmax), 0.0), axis=-1, keepdims=True)
    lo = N_GROUPS + gidx * EPG
    in_g = (lanef >= lo) & (lanef < lo + EPG)
    m1 = jnp.max(jnp.where(in_g, logits, neg), axis=-1, keepdims=True)
    i1 = jnp.min(jnp.where(in_g & (logits == m1), lanef, big), axis=-1, keepdims=True)
    rest = in_g & (lanef != i1)
    m2 = jnp.max(jnp.where(rest, logits, neg), axis=-1, keepdims=True)
    i2 = jnp.min(jnp.where(rest & (logits == m2), lanef, big), axis=-1, keepdims=True)
    r = jnp.exp(m2 - m1)
    w1 = pg_top / (1.0 + r)
    w2 = pg_top * r / (1.0 + r)
    return jnp.where(lanef == i1, w1, 0.0) + jnp.where(lanef == i2, w2, 0.0)


def _back_kernel(x_ref, gt_ref, ys_ref, ya_ref, ym_ref, ws_ref, wa_ref, wm_ref, wo_ref, g2_ref,
                 wr_ref, br_ref, x1_ref, h_ref, comb_ref):
    gt = gt_ref[...].astype(F32)
    merged = (_sigmoid(gt[:, 0:D]) * _dot(ys_ref[...].astype(BF16), ws_ref[...])
              + _sigmoid(gt[:, D:2 * D]) * _dot(ya_ref[...], wa_ref[...])
              + _sigmoid(gt[:, 2 * D:3 * D]) * _dot(ym_ref[...], wm_ref[...]))
    x1 = x_ref[...] + _dot(merged.astype(BF16), wo_ref[...])
    x1_ref[...] = x1
    ms = jnp.mean(x1 * x1, axis=-1, keepdims=True)
    h = x1 * lax.rsqrt(ms + EPS) * g2_ref[...]
    h_ref[...] = h.astype(BF16)
    logits = _dot_exact(h, wr_ref[...]) + br_ref[...]
    comb_ref[...] = _route(logits)


def _back(x, proj, y_ssm, y_swa, y_mem, ws, wa, wm, wo, g2, wr, br, tm):
    t = x.shape[0]
    const = lambda i: (0, 0)
    return pl.pallas_call(
        _back_kernel,
        grid=(t // tm,),
        in_specs=[
            pl.BlockSpec((tm, D), lambda i: (i, 0)),
            pl.BlockSpec((tm, 3 * D), lambda i: (i, 0)),
            pl.BlockSpec((tm, SSM_INNER), lambda i: (i, 0)),
            pl.BlockSpec((tm, D), lambda i: (i, 0)),
            pl.BlockSpec((tm, D), lambda i: (i, 0)),
            pl.BlockSpec((SSM_INNER, D), const),
            pl.BlockSpec((D, D), const),
            pl.BlockSpec((D, D), const),
            pl.BlockSpec((D, D), const),
            pl.BlockSpec((1, D), const),
            pl.BlockSpec((D, ROUTE_W), const),
            pl.BlockSpec((1, ROUTE_W), const),
        ],
        out_specs=[
            pl.BlockSpec((tm, D), lambda i: (i, 0)),
            pl.BlockSpec((tm, D), lambda i: (i, 0)),
            pl.BlockSpec((tm, ROUTE_W), lambda i: (i, 0)),
        ],
        out_shape=[
            jax.ShapeDtypeStruct((t, D), F32),
            jax.ShapeDtypeStruct((t, D), BF16),
            jax.ShapeDtypeStruct((t, ROUTE_W), F32),
        ],
        compiler_params=_cparams(("parallel",)),
        name="back",
    )(x, proj, y_ssm, y_swa, y_mem, ws, wa, wm, wo, g2, wr, br)


def _moe_kernel(x1_ref, h_ref, comb_ref, wg_ref, wu_ref, wd_ref, o_ref):
    e = pl.program_id(1)

    @pl.when(e == 0)
    def _():
        o_ref[...] = x1_ref[...]

    h = h_ref[...]
    lane = lax.broadcasted_iota(jnp.int32, comb_ref.shape, 1)
    cw = jnp.sum(jnp.where(lane == e + N_GROUPS, comb_ref[...], 0.0), axis=-1, keepdims=True)
    act = _silu(_dot(h, wg_ref[0])) * _dot(h, wu_ref[0])
    o_ref[...] += _dot((act * cw).astype(BF16), wd_ref[0])


def _moe(x1, h, comb, wg, wu, wd, tm):
    t = x1.shape[0]
    return pl.pallas_call(
        _moe_kernel,
        grid=(t // tm, N_EXPERTS),
        in_specs=[
            pl.BlockSpec((tm, D), lambda i, e: (i, 0)),
            pl.BlockSpec((tm, D), lambda i, e: (i, 0)),
            pl.BlockSpec((tm, ROUTE_W), lambda i, e: (i, 0)),
            pl.BlockSpec((1, D, D_FF), lambda i, e: (e, 0, 0)),
            pl.BlockSpec((1, D, D_FF), lambda i, e: (e, 0, 0)),
            pl.BlockSpec((1, D_FF, D), lambda i, e: (e, 0, 0)),
        ],
        out_specs=pl.BlockSpec((tm, D), lambda i, e: (i, 0)),
        out_shape=jax.ShapeDtypeStruct((t, D), F32),
        compiler_params=_cparams(("parallel", "arbitrary")),
        name="moe",
    )(x1, h, comb, wg, wu, wd)


def _sample_prep_kernel(xbc_ref, q_ref, k_ref, mq_ref, dt_ref, sc_ref, cw_ref, cb_ref, dtb_ref, a_ref, e_ref,
                        qg_ref, kg_ref, mqg_ref, cos_ref, slo_ref, shi_ref,
                        conv_ref, xs_ref, bc_ref, dtx_ref, dec_ref, qn_ref, kn_ref, mqn_ref):
    n = xbc_ref.shape[0]
    x_raw = xbc_ref[...].astype(F32)
    acc = x_raw * cw_ref[SSM_CONV - 1:SSM_CONV, :] + cb_ref[...]
    for j in range(SSM_CONV - 1):
        acc = acc + sc_ref[:, j * CONV_CH:(j + 1) * CONV_CH] * cw_ref[j:j + 1, :]
    conv_ref[:, 0:(SSM_CONV - 2) * CONV_CH] = sc_ref[:, CONV_CH:(SSM_CONV - 1) * CONV_CH]
    conv_ref[:, (SSM_CONV - 2) * CONV_CH:] = x_raw
    act = _silu(acc)
    xs = act[:, 0:SSM_INNER]
    xs_ref[...] = xs
    bc_ref[...] = act[:, SSM_INNER:]
    dt = _softplus(dt_ref[...] + dtb_ref[...])
    e = e_ref[...]
    dtx = _dot_exact(dt, e) * xs
    dec = _dot_exact(jnp.exp(dt * a_ref[...]), e)
    dtx_ref[...] = dtx.T
    dec_ref[...] = dec.T

    lane = lax.broadcasted_iota(jnp.int32, (n, LANE), 1)
    low_half = lane < ATT_HD
    cos_t = jnp.broadcast_to(cos_ref[...], (n, LANE))
    sin_lo = jnp.broadcast_to(slo_ref[...], (n, LANE))
    sin_hi = jnp.broadcast_to(shi_ref[...], (n, LANE))
    qf = q_ref[...].astype(F32)
    for j in range(D // LANE):
        sl = slice(j * LANE, (j + 1) * LANE)
        qn_ref[:, sl] = _norm_rope_tile(qf[:, sl], qg_ref[...], cos_t, sin_lo, sin_hi, low_half)
    kf = k_ref[...].astype(F32)
    for j in range(ATT_KVD // LANE):
        sl = slice(j * LANE, (j + 1) * LANE)
        kn_ref[:, sl] = _norm_rope_tile(kf[:, sl], kg_ref[...], cos_t, sin_lo, sin_hi, low_half)
    mq = mq_ref[...].astype(F32)
    for h in range(MEM_HEADS):
        sl = slice(h * MEM_HD, (h + 1) * MEM_HD)
        mh = mq[:, sl]
        ms = jnp.mean(mh * mh, axis=-1, keepdims=True)
        mqn_ref[:, sl] = mh * lax.rsqrt(ms + EPS) * mqg_ref[...]


def _sample_prep(proj, dt_raw, sc2d, cw, cb, dtb, a, e, qg2, kg2, mqg, cos_t, sin_lo, sin_hi):
    n = proj.shape[0]
    c2 = lambda i: (0, 0)
    full = lambda shape: pl.BlockSpec(shape, c2)
    return pl.pallas_call(
        _sample_prep_kernel,
        grid=(1,),
        in_specs=[
            pl.BlockSpec((n, CONV_CH), lambda i: (0, COL_XBC_BLK)),
            pl.BlockSpec((n, D), lambda i: (0, COL_Q_BLK)),
            pl.BlockSpec((n, ATT_KVD), lambda i: (0, COL_K_BLK)),
            pl.BlockSpec((n, D), lambda i: (0, COL_MQ_BLK)),
            full((n, LANE)),
            full((n, (SSM_CONV - 1) * CONV_CH)),
            full((SSM_CONV, CONV_CH)),
            full((1, CONV_CH)),
            full((1, LANE)),
            full((1, LANE)),
            full((LANE, SSM_INNER)),
            full((1, LANE)),
            full((1, LANE)),
            full((1, MEM_HD)),
            full((1, LANE)),
            full((1, LANE)),
            full((1, LANE)),
        ],
        out_specs=[
            full((n, (SSM_CONV - 1) * CONV_CH)),
            full((n, SSM_INNER)),
            full((n, 2 * SSM_GROUPS * SSM_STATE)),
            full((SSM_INNER, n)),
            full((SSM_INNER, n)),
            full((n, D)),
            full((n, ATT_KVD)),
            full((n, D)),
        ],
        out_shape=[
            jax.ShapeDtypeStruct((n, (SSM_CONV - 1) * CONV_CH), F32),
            jax.ShapeDtypeStruct((n, SSM_INNER), F32),
            jax.ShapeDtypeStruct((n, 2 * SSM_GROUPS * SSM_STATE), F32),
            jax.ShapeDtypeStruct((SSM_INNER, n), F32),
            jax.ShapeDtypeStruct((SSM_INNER, n), F32),
            jax.ShapeDtypeStruct((n, D), F32),
            jax.ShapeDtypeStruct((n, ATT_KVD), F32),
            jax.ShapeDtypeStruct((n, D), F32),
        ],
        compiler_params=_cparams(("arbitrary",)),
        name="sample_prep",
    )(proj, proj, proj, proj, dt_raw, sc2d, cw, cb, dtb, a, e, qg2, kg2, mqg, cos_t, sin_lo, sin_hi)


SSM_S_TILE = 8


def _sample_ssm_kernel(st_ref, dtx_ref, dec_ref, bc_ref, so_ref, yt_ref):
    i = pl.program_id(0)
    n = dtx_ref.shape[1]

    @pl.when(i == 0)
    def _():
        yt_ref[...] = jnp.zeros_like(yt_ref)

    gw = SSM_HPG * SSM_HEADDIM
    lane = lax.broadcasted_iota(jnp.int32, (gw, n), 1)
    nbc = SSM_GROUPS * SSM_STATE
    for s in range(SSM_S_TILE):
        sel = lane == (i * SSM_S_TILE + s)
        for g in range(SSM_GROUPS):
            rows = slice(g * gw, (g + 1) * gw)
            dtx_c = jnp.sum(jnp.where(sel, dtx_ref[rows, :], 0.0), axis=1, keepdims=True)
            dec_c = jnp.sum(jnp.where(sel, dec_ref[rows, :], 0.0), axis=1, keepdims=True)
            bm = bc_ref[s:s + 1, g * SSM_STATE:(g + 1) * SSM_STATE]
            cm = bc_ref[s:s + 1, nbc + g * SSM_STATE:nbc + (g + 1) * SSM_STATE]
            hn = st_ref[s, rows, :] * dec_c + dtx_c * bm
            so_ref[s, rows, :] = hn
            yc = jnp.sum(hn * cm, axis=1, keepdims=True)
            yt_ref[rows, :] = jnp.where(sel, yc, yt_ref[rows, :])


def _sample_ssm(state, dtx_t, dec_t, bc):
    n = state.shape[0]
    return pl.pallas_call(
        _sample_ssm_kernel,
        grid=(n // SSM_S_TILE,),
        in_specs=[
            pl.BlockSpec((SSM_S_TILE, SSM_INNER, SSM_STATE), lambda i: (i, 0, 0)),
            pl.BlockSpec((SSM_INNER, n), lambda i: (0, 0)),
            pl.BlockSpec((SSM_INNER, n), lambda i: (0, 0)),
            pl.BlockSpec((SSM_S_TILE, 2 * SSM_GROUPS * SSM_STATE), lambda i: (i, 0)),
        ],
        out_specs=[
            pl.BlockSpec((SSM_S_TILE, SSM_INNER, SSM_STATE), lambda i: (i, 0, 0)),
            pl.BlockSpec((SSM_INNER, n), lambda i: (0, 0)),
        ],
        out_shape=[
            jax.ShapeDtypeStruct((n, SSM_INNER, SSM_STATE), F32),
            jax.ShapeDtypeStruct((SSM_INNER, n), F32),
        ],
        compiler_params=_cparams(("arbitrary",)),
        name="sample_ssm",
    )(state, dtx_t, dec_t, bc)


def _sample_post_kernel(yt_ref, xs_ref, z_ref, dexp_ref, ng_ref, y_ref):
    y = yt_ref[...].T + dexp_ref[...] * xs_ref[...]
    yg = y * _silu(z_ref[...].astype(F32))
    ms = jnp.mean(yg * yg, axis=-1, keepdims=True)
    y_ref[...] = yg * lax.rsqrt(ms + EPS) * ng_ref[...]


def _sample_post(y_t, xs, proj, dexp, ng):
    n = xs.shape[0]
    c2 = lambda i: (0, 0)
    return pl.pallas_call(
        _sample_post_kernel,
        grid=(1,),
        in_specs=[
            pl.BlockSpec((SSM_INNER, n), c2),
            pl.BlockSpec((n, SSM_INNER), c2),
            pl.BlockSpec((n, SSM_INNER), lambda i: (0, COL_Z_BLK)),
            pl.BlockSpec((1, SSM_INNER), c2),
            pl.BlockSpec((1, SSM_INNER), c2),
        ],
        out_specs=pl.BlockSpec((n, SSM_INNER), c2),
        out_shape=jax.ShapeDtypeStruct((n, SSM_INNER), F32),
        compiler_params=_cparams(("arbitrary",)),
        name="sample_post",
    )(y_t, xs, proj, dexp, ng)


SWA_S_TILE = 8


def _sample_swa_kernel(q_ref, kn_ref, v_ref, ck_ref, cv_ref, sink_ref, y_ref, ko_ref, vo_ref):
    w = WINDOW
    hrow = lax.broadcasted_iota(jnp.int32, (ATT_HEADS, ATT_KVD), 0)
    lcol = lax.broadcasted_iota(jnp.int32, (ATT_HEADS, ATT_KVD), 1)
    own = (hrow // (ATT_HEADS // ATT_KV)) == (lcol // ATT_HD)
    scale = ATT_HD ** -0.5
    snk = sink_ref[...]
    for s in range(SWA_S_TILE):
        ko_ref[s, 0:w - 1, :] = ck_ref[s, 1:w, :]
        ko_ref[s, w - 1:w, :] = kn_ref[s]
        vo_ref[s, 0:w - 1, :] = cv_ref[s, 1:w, :]
        vo_ref[s, w - 1:w, :] = v_ref[s].astype(F32)
        q = q_ref[s]
        q4 = jnp.concatenate([q, q, q, q], axis=1)
        qrows = jnp.where(own, q4, 0.0).astype(BF16)
        sc = _dot_nt(qrows, ko_ref[s].astype(BF16)) * scale
        m = jnp.maximum(jnp.max(sc, axis=-1, keepdims=True), snk)
        p = jnp.exp(sc - m)
        den = jnp.sum(p, axis=-1, keepdims=True) + jnp.exp(snk - m)
        o = _dot(p.astype(BF16), vo_ref[s].astype(BF16)) / den
        o = jnp.where(own, o, 0.0)
        y_ref[s] = (o[:, 0:ATT_HD] + o[:, ATT_HD:2 * ATT_HD]
                    + o[:, 2 * ATT_HD:3 * ATT_HD] + o[:, 3 * ATT_HD:4 * ATT_HD])


def _sample_swa(q3, kn3, v3, ck, cv, sink_col):
    n = q3.shape[0]
    w = WINDOW
    st = SWA_S_TILE
    return pl.pallas_call(
        _sample_swa_kernel,
        grid=(n // st,),
        in_specs=[
            pl.BlockSpec((st, ATT_HEADS, ATT_HD), lambda i: (i, 0, 0)),
            pl.BlockSpec((st, 1, ATT_KVD), lambda i: (i, 0, 0)),
            pl.BlockSpec((st, 1, ATT_KVD), lambda i: (i, 0, 0)),
            pl.BlockSpec((st, w, ATT_KVD), lambda i: (i, 0, 0)),
            pl.BlockSpec((st, w, ATT_KVD), lambda i: (i, 0, 0)),
            pl.BlockSpec((ATT_HEADS, 1), lambda i: (0, 0)),
        ],
        out_specs=[
            pl.BlockSpec((st, ATT_HEADS, ATT_HD), lambda i: (i, 0, 0)),
            pl.BlockSpec((st, w, ATT_KVD), lambda i: (i, 0, 0)),
            pl.BlockSpec((st, w, ATT_KVD), lambda i: (i, 0, 0)),
        ],
        out_shape=[
            jax.ShapeDtypeStruct((n, ATT_HEADS, ATT_HD), F32),
            jax.ShapeDtypeStruct((n, w, ATT_KVD), F32),
            jax.ShapeDtypeStruct((n, w, ATT_KVD), F32),
        ],
        compiler_params=_cparams(("parallel",)),
        name="sample_swa",
    )(q3, kn3, v3, ck, cv, sink_col)


MEM_S_TILE = 4


def _sample_mem_kernel(q_ref, k_ref, v_ref, y_ref):
    scale = MEM_HD ** -0.5
    for s in range(MEM_S_TILE):
        q = q_ref[s]
        for h in range(MEM_HEADS):
            sl = slice(h * MEM_HD, (h + 1) * MEM_HD)
            sc = jnp.sum(k_ref[s, :, sl] * q[:, sl], axis=1, keepdims=True) * scale
            m = jnp.max(sc, axis=0, keepdims=True)
            p = jnp.exp(sc - m)
            den = jnp.sum(p, axis=0, keepdims=True)
            o = jnp.sum(p * v_ref[s, :, sl], axis=0, keepdims=True) / den
            y_ref[s, :, sl] = o


def _sample_mem(q3, ck, cv):
    n = q3.shape[0]
    mem_len = ck.shape[1]
    st = MEM_S_TILE
    return pl.pallas_call(
        _sample_mem_kernel,
        grid=(n // st,),
        in_specs=[
            pl.BlockSpec((st, 1, D), lambda i: (i, 0, 0)),
            pl.BlockSpec((st, mem_len, D), lambda i: (i, 0, 0)),
            pl.BlockSpec((st, mem_len, D), lambda i: (i, 0, 0)),
        ],
        out_specs=pl.BlockSpec((st, 1, D), lambda i: (i, 0, 0)),
        out_shape=jax.ShapeDtypeStruct((n, 1, D), F32),
        compiler_params=_cparams(("parallel",)),
        name="sample_mem",
    )(q3, ck, cv)


def _prep_weights(lw):
    w_in = lw['w_in']
    cols = [w_in[:, 0:OFF_Z], w_in[:, OFF_XBC:OFF_DT], w_in[:, OFF_Z:OFF_XBC], w_in[:, OFF_Q:OFF_K],
            w_in[:, OFF_MQ:OFF_MQ + D], w_in[:, OFF_K:OFF_V], w_in[:, OFF_V:OFF_MQ]]
    p = {}
    p['w_main'] = jnp.concatenate(cols, axis=1).astype(BF16)
    p['w_dt'] = jnp.pad(w_in[:, OFF_DT:OFF_Q], ((0, 0), (0, LANE - SSM_HEADS))).astype(BF16)
    p['norm1_g'] = lw['norm1_g'].reshape(1, D)
    p['cw'] = lw['ssm_conv_w']
    p['cb'] = lw['ssm_conv_b'].reshape(1, CONV_CH)
    pad_h = (0, LANE - SSM_HEADS)
    p['dtb'] = jnp.pad(lw['ssm_dt_bias'].astype(F32), pad_h).reshape(1, LANE)
    p['a'] = jnp.pad(-jnp.exp(lw['ssm_a_log'].astype(F32)), pad_h).reshape(1, LANE)
    p['dexp'] = jnp.repeat(lw['ssm_d'], SSM_HEADDIM).reshape(1, SSM_INNER)
    p['ssm_ng'] = lw['ssm_norm_g'].reshape(1, SSM_INNER)
    head_of_ch = jnp.arange(SSM_INNER) // SSM_HEADDIM
    p['e'] = (jnp.arange(LANE)[:, None] == head_of_ch[None, :]).astype(F32)
    p['qg2'] = jnp.tile(lw['att_q_norm_g'], 2).reshape(1, LANE)
    p['kg2'] = jnp.tile(lw['att_k_norm_g'], 2).reshape(1, LANE)
    p['sink_row'] = lw['att_sink'].astype(F32).reshape(1, ATT_HEADS)
    p['sink_col'] = lw['att_sink'].astype(F32).reshape(ATT_HEADS, 1)
    p['mem_g'] = lw['mem_norm_g'].reshape(1, D)
    p['w_mem_kv'] = lw['w_mem_kv'].astype(BF16)
    p['mqg'] = lw['mem_q_norm_g'].reshape(1, MEM_HD)
    p['mkg'] = lw['mem_k_norm_g'].reshape(1, MEM_HD)
    p['ws'] = lw['w_br_ssm'].astype(BF16)
    p['wa'] = lw['w_br_swa'].astype(BF16)
    p['wm'] = lw['w_br_mem'].astype(BF16)
    p['wo'] = lw['w_out'].astype(BF16)
    p['g2'] = lw['norm2_g'].reshape(1, D)
    pad_r = ROUTE_W - N_GROUPS - N_EXPERTS
    p['wr'] = jnp.pad(jnp.concatenate([lw['w_router_group'], lw['w_router_expert']], axis=1).astype(F32),
                      ((0, 0), (0, pad_r)))
    p['br'] = jnp.pad(jnp.concatenate([lw['b_router_group'], lw['b_router_expert']]).astype(F32),
                      (0, pad_r)).reshape(1, ROUTE_W)
    p['wg'] = lw['w_exp_gate'].astype(BF16)
    p['wu'] = lw['w_exp_up'].astype(BF16)
    p['wd'] = lw['w_exp_down'].astype(BF16)
    return p


def _pick(n, prefs):
    for c in prefs:
        if n % c == 0:
            return c
    return n


def _tail(x, proj, y_ssm, y_swa, y_mem, p):
    t = x.shape[0]
    x1, h, comb = _back(x, proj, y_ssm, y_swa, y_mem, p['ws'], p['wa'], p['wm'], p['wo'], p['g2'],
                        p['wr'], p['br'], _pick(t, (256, 128)))
    return _moe(x1, h, comb, p['wg'], p['wu'], p['wd'], _pick(t, (1024, 512, 256, 128)))


def _prompt_layer(x, mem, p):
    nb, seq, _ = x.shape
    t = nb * seq
    nc = seq // CHUNK
    xf = x.reshape(t, D)
    proj, dt_raw = _front(xf, p['norm1_g'], p['w_main'], p['w_dt'], _pick(t, (1024, 512, 256, 128)), 1536)
    y_ssm, conv_new, ssm_new = _ssd(proj, dt_raw, p['cw'], p['cb'], p['dtb'], p['a'], p['dexp'],
                                    p['ssm_ng'], p['e'], nb, nc)
    cos_t, sin_lo, sin_hi = _rope_tables(jnp.arange(seq))
    y_swa, k_new, v_new = _swa(proj, p['qg2'], p['kg2'], cos_t, sin_lo, sin_hi, p['sink_row'], nb, nc)
    mem_len = mem.shape[1]
    mk, mv = _memkv(mem.reshape(nb * mem_len, D), p['mem_g'], p['w_mem_kv'], p['mkg'],
                    _pick(nb * mem_len, (512, 256, 128)))
    y_mem = _memattn(proj, mk, mv, p['mqg'], nb, seq, mem_len, _pick(seq, (512, 256, 128)))
    y = _tail(xf, proj, y_ssm, y_swa, y_mem, p)
    return (y.reshape(nb, seq, D), conv_new,
            ssm_new.reshape(nb, SSM_HEADS, SSM_HEADDIM, SSM_STATE),
            k_new.reshape(nb, WINDOW, ATT_KV, ATT_HD), v_new.reshape(nb, WINDOW, ATT_KV, ATT_HD),
            mk.reshape(nb, mem_len, MEM_HEADS, MEM_HD), mv.reshape(nb, mem_len, MEM_HEADS, MEM_HD))


def _sample_layer(x, conv_st, ssm_st, swa_k, swa_v, mem_k, mem_v, p):
    n = x.shape[0]
    xf = x.reshape(n, D)
    proj, dt_raw = _front(xf, p['norm1_g'], p['w_main'], p['w_dt'], n, 1536)
    cos_t, sin_lo, sin_hi = _rope_tables(jnp.full((1,), PAST_LEN, jnp.int32))
    sc2d = conv_st.reshape(n, (SSM_CONV - 1) * CONV_CH)
    conv_new, xs, bc, dtx_t, dec_t, qn, kn, mqn = _sample_prep(
        proj, dt_raw, sc2d, p['cw'], p['cb'], p['dtb'], p['a'], p['e'], p['qg2'], p['kg2'], p['mqg'],
        cos_t, sin_lo, sin_hi)
    ssm_new, y_t = _sample_ssm(ssm_st.reshape(n, SSM_INNER, SSM_STATE), dtx_t, dec_t, bc)
    y_ssm = _sample_post(y_t, xs, proj, p['dexp'], p['ssm_ng'])
    v_raw = proj[:, COL_V_BLK * ATT_KVD:(COL_V_BLK + 1) * ATT_KVD]
    y_swa, k_new, v_new = _sample_swa(
        qn.reshape(n, ATT_HEADS, ATT_HD), kn.reshape(n, 1, ATT_KVD), v_raw.reshape(n, 1, ATT_KVD),
        swa_k.reshape(n, WINDOW, ATT_KVD), swa_v.reshape(n, WINDOW, ATT_KVD), p['sink_col'])
    mem_len = mem_k.shape[1]
    y_mem = _sample_mem(mqn.reshape(n, 1, D), mem_k.reshape(n, mem_len, D), mem_v.reshape(n, mem_len, D))
    y = _tail(xf, proj, y_ssm, y_swa.reshape(n, D).astype(BF16), y_mem.reshape(n, D).astype(BF16), p)
    return (y.reshape(n, 1, D), conv_new.reshape(n, SSM_CONV - 1, CONV_CH),
            ssm_new.reshape(n, SSM_HEADS, SSM_HEADDIM, SSM_STATE),
            k_new.reshape(n, WINDOW, ATT_KV, ATT_HD), v_new.reshape(n, WINDOW, ATT_KV, ATT_HD))


def kernel(x_prompt, x_sample, state_conv, state_ssm, cache_swa_k, cache_swa_v, cache_mem_k, cache_mem_v,
           mem_prompt, norm1_g, w_in, ssm_conv_w, ssm_conv_b, ssm_dt_bias, ssm_a_log, ssm_d, ssm_norm_g,
           att_q_norm_g, att_k_norm_g, att_sink, mem_norm_g, w_mem_kv, mem_q_norm_g, mem_k_norm_g,
           w_br_ssm, w_br_swa, w_br_mem, w_out, norm2_g, w_router_group, b_router_group,
           w_router_expert, b_router_expert, w_exp_gate, w_exp_up, w_exp_down):
    weights = dict(norm1_g=norm1_g, w_in=w_in, ssm_conv_w=ssm_conv_w, ssm_conv_b=ssm_conv_b,
                   ssm_dt_bias=ssm_dt_bias, ssm_a_log=ssm_a_log, ssm_d=ssm_d, ssm_norm_g=ssm_norm_g,
                   att_q_norm_g=att_q_norm_g, att_k_norm_g=att_k_norm_g, att_sink=att_sink,
                   mem_norm_g=mem_norm_g, w_mem_kv=w_mem_kv, mem_q_norm_g=mem_q_norm_g,
                   mem_k_norm_g=mem_k_norm_g, w_br_ssm=w_br_ssm, w_br_swa=w_br_swa, w_br_mem=w_br_mem,
                   w_out=w_out, norm2_g=norm2_g, w_router_group=w_router_group,
                   b_router_group=b_router_group, w_router_expert=w_router_expert,
                   b_router_expert=b_router_expert, w_exp_gate=w_exp_gate, w_exp_up=w_exp_up,
                   w_exp_down=w_exp_down)
    depth = w_in.shape[0]
    xp, xs = x_prompt, x_sample
    outs = [[] for _ in range(10)]
    for l in range(depth):
        p = _prep_weights({k: v[l] for k, v in weights.items()})
        xp, c1, c2, c3, c4, c5, c6 = _prompt_layer(xp, mem_prompt, p)
        xs, d1, d2, d3, d4 = _sample_layer(xs, state_conv[l], state_ssm[l], cache_swa_k[l], cache_swa_v[l],
                                           cache_mem_k[l], cache_mem_v[l], p)
        for lst, val in zip(outs, (c1, c2, c3, c4, c5, c6, d1, d2, d3, d4)):
            lst.append(val)
    return (xp, xs) + tuple(jnp.stack(o) for o in outs)
```

```python
import functools
import math

import jax
import jax.numpy as jnp
from jax import lax
from jax.experimental import pallas as pl
from jax.experimental.pallas import tpu as pltpu

F32 = jnp.float32
BF16 = jnp.bfloat16
HIGHEST = lax.Precision.HIGHEST

D = 1024
SSM_INNER = 2048
SSM_HEADDIM = 64
SSM_HEADS = 32
SSM_GROUPS = 4
SSM_HPG = SSM_HEADS // SSM_GROUPS
SSM_STATE = 128
SSM_CONV = 4
CONV_CH = SSM_INNER + 2 * SSM_GROUPS * SSM_STATE
CHUNK = 128
ATT_HEADS = 16
ATT_KV = 4
ATT_HD = 64
ATT_KVD = ATT_KV * ATT_HD
WINDOW = 128
ROPE_THETA = 10000.0
MEM_HEADS = 4
MEM_HD = 256
N_EXPERTS = 32
N_GROUPS = 4
EPG = 8
D_FF = 256
EPS = 1e-6
PAST_LEN = 16384

OFF_Z = 3 * D
OFF_XBC = OFF_Z + SSM_INNER
OFF_DT = OFF_XBC + CONV_CH
OFF_Q = OFF_DT + SSM_HEADS
OFF_K = OFF_Q + D
OFF_V = OFF_K + ATT_KVD
OFF_MQ = OFF_V + ATT_KVD

N_MAIN = 3 * D + CONV_CH + SSM_INNER + D + D + 2 * ATT_KVD
COL_XBC_BLK = 1
COL_Z_BLK = 3
COL_Q_BLK = 8
COL_MQ_BLK = 9
COL_K_BLK = 40
COL_V_BLK = 41
LANE = 128
ROUTE_W = 128

VMEM_LIMIT = 56 * 1024 * 1024


def _cparams(sem):
    return pltpu.CompilerParams(dimension_semantics=sem, vmem_limit_bytes=VMEM_LIMIT)


def _sigmoid(x):
    return 1.0 / (1.0 + jnp.exp(-x))


def _silu(x):
    return x * _sigmoid(x)


def _softplus(x):
    return jnp.maximum(x, 0.0) + jnp.log1p(jnp.exp(-jnp.abs(x)))


def _dot(a, b):
    return jnp.dot(a, b, preferred_element_type=F32)


def _dot_nt(a, b):
    return lax.dot_general(a, b, (((1,), (1,)), ((), ())), preferred_element_type=F32)


def _dot_tn(a, b):
    return lax.dot_general(a, b, (((0,), (0,)), ((), ())), preferred_element_type=F32)


def _dot_exact(a, b):
    return jnp.dot(a, b, preferred_element_type=F32, precision=HIGHEST)


def _front_kernel(x_ref, g_ref, w_ref, wdt_ref, o_ref, dt_ref, hn_ref):
    @pl.when(pl.program_id(1) == 0)
    def _():
        x = x_ref[...]
        ms = jnp.mean(x * x, axis=-1, keepdims=True)
        hn = (x * lax.rsqrt(ms + EPS) * g_ref[...]).astype(BF16)
        hn_ref[...] = hn
        dt_ref[...] = _dot(hn, wdt_ref[...])

    o_ref[...] = _dot(hn_ref[...], w_ref[...]).astype(BF16)


def _front(x, g, w_main, w_dt, tm, tn):
    t = x.shape[0]
    return pl.pallas_call(
        _front_kernel,
        grid=(t // tm, N_MAIN // tn),
        in_specs=[
            pl.BlockSpec((tm, D), lambda i, j: (i, 0)),
            pl.BlockSpec((1, D), lambda i, j: (0, 0)),
            pl.BlockSpec((D, tn), lambda i, j: (0, j)),
            pl.BlockSpec((D, LANE), lambda i, j: (0, 0)),
        ],
        out_specs=[
            pl.BlockSpec((tm, tn), lambda i, j: (i, j)),
            pl.BlockSpec((tm, LANE), lambda i, j: (i, 0)),
        ],
        out_shape=[
            jax.ShapeDtypeStruct((t, N_MAIN), BF16),
            jax.ShapeDtypeStruct((t, LANE), F32),
        ],
        scratch_shapes=[pltpu.VMEM((tm, D), BF16)],
        compiler_params=_cparams(("parallel", "arbitrary")),
        name="front",
    )(x, g, w_main, w_dt)


HIST_PAD = 8


def _ssd_kernel(xbc_ref, z_ref, dt_ref, cw_ref, cb_ref, dtb_ref, a_ref, dexp_ref, ng_ref, e_ref,
                y_ref, conv_ref, ssm_ref, hist_ref, h_ref):
    c = pl.program_id(1)

    @pl.when(c == 0)
    def _():
        hist_ref[0:HIST_PAD, :] = jnp.zeros((HIST_PAD, CONV_CH), F32)
        h_ref[...] = jnp.zeros_like(h_ref)

    x_raw = xbc_ref[...].astype(F32)
    hist_ref[HIST_PAD:HIST_PAD + CHUNK, :] = x_raw
    acc = x_raw * cw_ref[SSM_CONV - 1:SSM_CONV, :] + cb_ref[...]
    for j in range(SSM_CONV - 1):
        lo = HIST_PAD - (SSM_CONV - 1) + j
        acc = acc + hist_ref[lo:lo + CHUNK, :] * cw_ref[j:j + 1, :]
    tail = hist_ref[HIST_PAD + CHUNK - (SSM_CONV - 1):HIST_PAD + CHUNK, :]
    hist_ref[HIST_PAD - (SSM_CONV - 1):HIST_PAD, :] = tail
    conv_ref[0] = tail
    act = _silu(acc)

    dt = _softplus(dt_ref[...] + dtb_ref[...])
    da = dt * a_ref[...]
    row = lax.broadcasted_iota(jnp.int32, (CHUNK, CHUNK), 0)
    col = lax.broadcasted_iota(jnp.int32, (CHUNK, CHUNK), 1)
    causal = row >= col
    acum = _dot_exact(causal.astype(F32), da)
    acum_t = acum.T
    dt_t = dt.T
    e = e_ref[...]
    exp_a = _dot_exact(jnp.exp(acum), e)
    last = acum[CHUNK - 1:CHUNK, :]
    w_end = _dot_exact(jnp.exp(last - acum) * dt, e)
    cd = jnp.broadcast_to(jnp.exp(acum_t[:, CHUNK - 1:CHUNK]), (LANE, SSM_STATE))
    lane = lax.broadcasted_iota(jnp.int32, (CHUNK, LANE), 1)
    low_half = lane < SSM_HEADDIM

    xs_off = 0
    b_off = SSM_INNER
    c_off = SSM_INNER + SSM_GROUPS * SSM_STATE
    for g in range(SSM_GROUPS):
        bm = act[:, b_off + g * SSM_STATE:b_off + (g + 1) * SSM_STATE].astype(BF16)
        cm = act[:, c_off + g * SSM_STATE:c_off + (g + 1) * SSM_STATE].astype(BF16)
        cb = _dot_nt(cm, bm)
        gw = SSM_HPG * SSM_HEADDIM
        ch0 = g * gw
        h_prev = h_ref[ch0:ch0 + gw, :]
        y_off = _dot_nt(cm, h_prev.astype(BF16)) * exp_a[:, ch0:ch0 + gw]
        xg = act[:, xs_off + ch0:xs_off + ch0 + gw]
        for pr in range(SSM_HPG // 2):
            xpair = xg[:, pr * LANE:(pr + 1) * LANE].astype(BF16)
            yd = []
            for sub in range(2):
                hd = g * SSM_HPG + pr * 2 + sub
                seg = acum[:, hd:hd + 1] - acum_t[hd:hd + 1, :]
                decay = jnp.exp(jnp.where(causal, seg, -jnp.inf))
                wts = cb * decay * dt_t[hd:hd + 1, :]
                yd.append(_dot(wts.astype(BF16), xpair))
            y_pair = jnp.where(low_half, yd[0], yd[1]) + y_off[:, pr * LANE:(pr + 1) * LANE]
            cl = ch0 + pr * LANE
            xp32 = xg[:, pr * LANE:(pr + 1) * LANE]
            y_pair = y_pair + dexp_ref[:, cl:cl + LANE] * xp32
            y_ref[:, cl:cl + LANE] = y_pair.astype(F32)
        xw = (xg * w_end[:, ch0:ch0 + gw]).astype(BF16)
        states = _dot_tn(xw, bm)
        for r in range(SSM_HPG):
            hd = g * SSM_HPG + r
            r0 = ch0 + r * SSM_HEADDIM
            h_ref[r0:r0 + SSM_HEADDIM, :] = (h_ref[r0:r0 + SSM_HEADDIM, :] * cd[hd:hd + 1, :]
                                             + states[r * SSM_HEADDIM:(r + 1) * SSM_HEADDIM, :])

    ssm_ref[0] = h_ref[...]
    yg = y_ref[...] * _silu(z_ref[...].astype(F32))
    ms = jnp.mean(yg * yg, axis=-1, keepdims=True)
    y_ref[...] = yg * lax.rsqrt(ms + EPS) * ng_ref[...]


def _ssd(proj, dt_raw, cw, cb, dtb, a, dexp, ng, e, nb, nc):
    t = proj.shape[0]
    return pl.pallas_call(
        _ssd_kernel,
        grid=(nb, nc),
        in_specs=[
            pl.BlockSpec((CHUNK, CONV_CH), lambda b, c: (b * nc + c, COL_XBC_BLK)),
            pl.BlockSpec((CHUNK, SSM_INNER), lambda b, c: (b * nc + c, COL_Z_BLK)),
            pl.BlockSpec((CHUNK, LANE), lambda b, c: (b * nc + c, 0)),
            pl.BlockSpec((SSM_CONV, CONV_CH), lambda b, c: (0, 0)),
            pl.BlockSpec((1, CONV_CH), lambda b, c: (0, 0)),
            pl.BlockSpec((1, LANE), lambda b, c: (0, 0)),
            pl.BlockSpec((1, LANE), lambda b, c: (0, 0)),
            pl.BlockSpec((1, SSM_INNER), lambda b, c: (0, 0)),
            pl.BlockSpec((1, SSM_INNER), lambda b, c: (0, 0)),
            pl.BlockSpec((LANE, SSM_INNER), lambda b, c: (0, 0)),
        ],
        out_specs=[
            pl.BlockSpec((CHUNK, SSM_INNER), lambda b, c: (b * nc + c, 0)),
            pl.BlockSpec((1, SSM_CONV - 1, CONV_CH), lambda b, c: (b, 0, 0)),
            pl.BlockSpec((1, SSM_INNER, SSM_STATE), lambda b, c: (b, 0, 0)),
        ],
        out_shape=[
            jax.ShapeDtypeStruct((t, SSM_INNER), F32),
            jax.ShapeDtypeStruct((nb, SSM_CONV - 1, CONV_CH), F32),
            jax.ShapeDtypeStruct((nb, SSM_INNER, SSM_STATE), F32),
        ],
        scratch_shapes=[
            pltpu.VMEM((HIST_PAD + CHUNK, CONV_CH), F32),
            pltpu.VMEM((SSM_INNER, SSM_STATE), F32),
        ],
        compiler_params=_cparams(("parallel", "arbitrary")),
        name="ssd_prompt",
    )(proj, proj, dt_raw, cw, cb, dtb, a, dexp, ng, e)


def _norm_rope_tile(xj, g2, cos_t, sin_lo, sin_hi, low_half):
    sq = xj * xj
    s_lo = jnp.sum(jnp.where(low_half, sq, 0.0), axis=-1, keepdims=True)
    s_hi = jnp.sum(jnp.where(low_half, 0.0, sq), axis=-1, keepdims=True)
    ms = jnp.where(low_half, s_lo, s_hi) * (1.0 / ATT_HD)
    xn = xj * lax.rsqrt(ms + EPS) * g2
    half = ATT_HD // 2
    return (xn * cos_t + pltpu.roll(xn, LANE - half, axis=1) * sin_lo
            + pltpu.roll(xn, half, axis=1) * sin_hi)


def _rope_tables(pos):
    half = ATT_HD // 2
    inv = ROPE_THETA ** (-jnp.arange(half, dtype=F32) / half)
    ang = pos.astype(F32)[:, None] * inv[None, :]
    cos = jnp.cos(ang)
    sin = jnp.sin(ang)
    zero = jnp.zeros_like(sin)
    cos_t = jnp.concatenate([cos, cos, cos, cos], axis=1)
    sin_lo = jnp.concatenate([-sin, zero, -sin, zero], axis=1)
    sin_hi = jnp.concatenate([zero, sin, zero, sin], axis=1)
    return cos_t, sin_lo, sin_hi


def _swa_kernel(q_ref, k_ref, v_ref, qg_ref, kg_ref, cos_ref, slo_ref, shi_ref, sink_ref,
                y_ref, ko_ref, vo_ref, kd_ref, vd_ref, qs_ref):
    c = pl.program_id(1)
    w = WINDOW
    grp = ATT_HEADS // ATT_KV
    lane = lax.broadcasted_iota(jnp.int32, (w, LANE), 1)
    low_half = lane < ATT_HD
    cos_t = cos_ref[...]
    sin_lo = slo_ref[...]
    sin_hi = shi_ref[...]

    @pl.when(c == 0)
    def _():
        zeros = jnp.zeros((ATT_KV, w, LANE), BF16)
        kd_ref[:, 0:w, :] = zeros
        vd_ref[:, 0:w, :] = zeros

    kf = k_ref[...].astype(F32)
    vf = v_ref[...].astype(F32)
    for j in range(ATT_KVD // LANE):
        sl = slice(j * LANE, (j + 1) * LANE)
        kn = _norm_rope_tile(kf[:, sl], kg_ref[...], cos_t, sin_lo, sin_hi, low_half)
        ko_ref[0, :, sl] = kn
        kr = pltpu.roll(kn, ATT_HD, axis=1)
        vj = vf[:, sl]
        vr = pltpu.roll(vj, ATT_HD, axis=1)
        kd_ref[2 * j, w:2 * w, :] = jnp.where(low_half, kn, kr).astype(BF16)
        kd_ref[2 * j + 1, w:2 * w, :] = jnp.where(low_half, kr, kn).astype(BF16)
        vd_ref[2 * j, w:2 * w, :] = jnp.where(low_half, vj, vr).astype(BF16)
        vd_ref[2 * j + 1, w:2 * w, :] = jnp.where(low_half, vr, vj).astype(BF16)
    vo_ref[0] = vf

    qf = q_ref[...].astype(F32)
    for j in range(D // LANE):
        sl = slice(j * LANE, (j + 1) * LANE)
        qn = _norm_rope_tile(qf[:, sl], qg_ref[...], cos_t, sin_lo, sin_hi, low_half)
        for par in range(2):
            h = 2 * j + par
            r0 = (h % grp) * w
            in_half = low_half if par == 0 else jnp.logical_not(low_half)
            qs_ref[h // grp, r0:r0 + w, :] = jnp.where(in_half, qn, 0.0).astype(BF16)

    qi = lax.broadcasted_iota(jnp.int32, (grp * w, 2 * w), 0) & (w - 1)
    kj = lax.broadcasted_iota(jnp.int32, (grp * w, 2 * w), 1)
    mask = ((kj < w) & (kj > qi) & (c > 0)) | ((kj >= w) & ((kj - w) <= qi))
    scale = ATT_HD ** -0.5

    for kv in range(ATT_KV):
        snk = jnp.concatenate(
            [jnp.broadcast_to(sink_ref[:, kv * grp + i:kv * grp + i + 1], (w, 1)) for i in range(grp)], axis=0)
        s = _dot_nt(qs_ref[kv], kd_ref[kv]) * scale
        s = jnp.where(mask, s, -jnp.inf)
        m = jnp.maximum(jnp.max(s, axis=-1, keepdims=True), snk)
        p = jnp.exp(s - m)
        den = jnp.sum(p, axis=-1, keepdims=True) + jnp.exp(snk - m)
        o = _dot(p.astype(BF16), vd_ref[kv]) * (1.0 / den)
        for a in range(grp // 2):
            col = kv * (grp // 2) + a
            y_ref[:, col * LANE:(col + 1) * LANE] = jnp.where(
                low_half, o[2 * a * w:(2 * a + 1) * w, :], o[(2 * a + 1) * w:(2 * a + 2) * w, :]).astype(BF16)

    kd_ref[:, 0:w, :] = kd_ref[:, w:2 * w, :]
    vd_ref[:, 0:w, :] = vd_ref[:, w:2 * w, :]


def _swa(proj, qg2, kg2, cos_t, sin_lo, sin_hi, sink, nb, nc):
    t = proj.shape[0]
    w = WINDOW
    tab = pl.BlockSpec((w, LANE), lambda b, c: (c, 0))
    return pl.pallas_call(
        _swa_kernel,
        grid=(nb, nc),
        in_specs=[
            pl.BlockSpec((w, D), lambda b, c: (b * nc + c, COL_Q_BLK)),
            pl.BlockSpec((w, ATT_KVD), lambda b, c: (b * nc + c, COL_K_BLK)),
            pl.BlockSpec((w, ATT_KVD), lambda b, c: (b * nc + c, COL_V_BLK)),
            pl.BlockSpec((1, LANE), lambda b, c: (0, 0)),
            pl.BlockSpec((1, LANE), lambda b, c: (0, 0)),
            tab, tab, tab,
            pl.BlockSpec((1, ATT_HEADS), lambda b, c: (0, 0)),
        ],
        out_specs=[
            pl.BlockSpec((w, D), lambda b, c: (b * nc + c, 0)),
            pl.BlockSpec((1, w, ATT_KVD), lambda b, c: (b, 0, 0)),
            pl.BlockSpec((1, w, ATT_KVD), lambda b, c: (b, 0, 0)),
        ],
        out_shape=[
            jax.ShapeDtypeStruct((t, D), BF16),
            jax.ShapeDtypeStruct((nb, w, ATT_KVD), F32),
            jax.ShapeDtypeStruct((nb, w, ATT_KVD), F32),
        ],
        scratch_shapes=[
            pltpu.VMEM((ATT_KV, 2 * w, LANE), BF16),
            pltpu.VMEM((ATT_KV, 2 * w, LANE), BF16),
            pltpu.VMEM((ATT_KV, (ATT_HEADS // ATT_KV) * w, LANE), BF16),
        ],
        compiler_params=_cparams(("parallel", "arbitrary")),
        name="swa_prompt",
    )(proj, proj, proj, qg2, kg2, cos_t, sin_lo, sin_hi, sink)


def _memkv_kernel(m_ref, g_ref, w_ref, kg_ref, k_ref, v_ref):
    x = m_ref[...]
    ms = jnp.mean(x * x, axis=-1, keepdims=True)
    hn = (x * lax.rsqrt(ms + EPS) * g_ref[...]).astype(BF16)
    kv = _dot(hn, w_ref[...])
    for h in range(MEM_HEADS):
        kh = kv[:, h * MEM_HD:(h + 1) * MEM_HD]
        ms = jnp.mean(kh * kh, axis=-1, keepdims=True)
        k_ref[:, h * MEM_HD:(h + 1) * MEM_HD] = kh * lax.rsqrt(ms + EPS) * kg_ref[...]
    v_ref[...] = kv[:, D:]


def _memkv(mem, g, w, kg, tm):
    t = mem.shape[0]
    return pl.pallas_call(
        _memkv_kernel,
        grid=(t // tm,),
        in_specs=[
            pl.BlockSpec((tm, D), lambda i: (i, 0)),
            pl.BlockSpec((1, D), lambda i: (0, 0)),
            pl.BlockSpec((D, 2 * D), lambda i: (0, 0)),
            pl.BlockSpec((1, MEM_HD), lambda i: (0, 0)),
        ],
        out_specs=[
            pl.BlockSpec((tm, D), lambda i: (i, 0)),
            pl.BlockSpec((tm, D), lambda i: (i, 0)),
        ],
        out_shape=[jax.ShapeDtypeStruct((t, D), F32)] * 2,
        compiler_params=_cparams(("parallel",)),
        name="mem_kv",
    )(mem, g, w, kg)


def _memattn_kernel(q_ref, k_ref, v_ref, qg_ref, y_ref):
    q = q_ref[...].astype(F32)
    scale = MEM_HD ** -0.5
    for h in range(MEM_HEADS):
        sl = slice(h * MEM_HD, (h + 1) * MEM_HD)
        qh = q[:, sl]
        ms = jnp.mean(qh * qh, axis=-1, keepdims=True)
        qn = (qh * lax.rsqrt(ms + EPS) * qg_ref[...]).astype(BF16)
        s = _dot_nt(qn, k_ref[:, sl].astype(BF16)) * scale
        m = jnp.max(s, axis=-1, keepdims=True)
        p = jnp.exp(s - m)
        den = jnp.sum(p, axis=-1, keepdims=True)
        o = _dot(p.astype(BF16), v_ref[:, sl].astype(BF16)) / den
        y_ref[:, sl] = o.astype(BF16)


def _memattn(proj, mk, mv, qg, nb, seq, mem_len, tm):
    t = proj.shape[0]
    nt = seq // tm
    return pl.pallas_call(
        _memattn_kernel,
        grid=(nb, nt),
        in_specs=[
            pl.BlockSpec((tm, D), lambda b, i: (b * nt + i, COL_MQ_BLK)),
            pl.BlockSpec((mem_len, D), lambda b, i: (b, 0)),
            pl.BlockSpec((mem_len, D), lambda b, i: (b, 0)),
            pl.BlockSpec((1, MEM_HD), lambda b, i: (0, 0)),
        ],
        out_specs=pl.BlockSpec((tm, D), lambda b, i: (b * nt + i, 0)),
        out_shape=jax.ShapeDtypeStruct((t, D), BF16),
        compiler_params=_cparams(("parallel", "arbitrary")),
        name="mem_attn_prompt",
    )(proj, mk, mv, qg)


def _route(logits):
    lane = lax.broadcasted_iota(jnp.int32, logits.shape, 1)
    lanef = lane.astype(F32)
    is_g = lane < N_GROUPS
    neg = -jnp.inf
    big = 1e9
    gl = jnp.where(is_g, logits, neg)
    gmax = jnp.max(gl, axis=-1, keepdims=True)
    gidx = jnp.min(jnp.where(is_g & (logits == gmax), lanef, big), axis=-1, keepdims=True)
    pg_top = 1.0 / jnp.sum(jnp.where(is_g, jnp.exp(logits - gmax), 0.0), axis=-1, keepdims=True)
    lo = N_GROUPS + gidx * EPG
    in_g = (lanef >= lo) & (lanef < lo + EPG)
    m1 = jnp.max(jnp.where(in_g, logits, neg), axis=-1, keepdims=True)
    i1 = jnp.min(jnp.where(in_g & (logits == m1), lanef, big), axis=-1, keepdims=True)
    rest = in_g & (lanef != i1)
    m2 = jnp.max(jnp.where(rest, logits, neg), axis=-1, keepdims=True)
    i2 = jnp.min(jnp.where(rest & (logits == m2), lanef, big), axis=-1, keepdims=True)
    r = jnp.exp(m2 - m1)
    w1 = pg_top / (1.0 + r)
    w2 = pg_top * r / (1.0 + r)
    return jnp.where(lanef == i1, w1, 0.0) + jnp.where(lanef == i2, w2, 0.0)


def _back_kernel(x_ref, gt_ref, ys_ref, ya_ref, ym_ref, ws_ref, wa_ref, wm_ref, wo_ref, g2_ref,
                 wr_ref, br_ref, x1_ref, h_ref, comb_ref):
    gt = gt_ref[...].astype(F32)
    merged = (_sigmoid(gt[:, 0:D]) * _dot(ys_ref[...].astype(BF16), ws_ref[...])
              + _sigmoid(gt[:, D:2 * D]) * _dot(ya_ref[...], wa_ref[...])
              + _sigmoid(gt[:, 2 * D:3 * D]) * _dot(ym_ref[...], wm_ref[...]))
    x1 = x_ref[...] + _dot(merged.astype(BF16), wo_ref[...])
    x1_ref[...] = x1
    ms = jnp.mean(x1 * x1, axis=-1, keepdims=True)
    h = x1 * lax.rsqrt(ms + EPS) * g2_ref[...]
    h_ref[...] = h.astype(BF16)
    logits = _dot_exact(h, wr_ref[...]) + br_ref[...]
    comb_ref[...] = _route(logits)


def _back(x, proj, y_ssm, y_swa, y_mem, ws, wa, wm, wo, g2, wr, br, tm):
    t = x.shape[0]
    const = lambda i: (0, 0)
    return pl.pallas_call(
        _back_kernel,
        grid=(t // tm,),
        in_specs=[
            pl.BlockSpec((tm, D), lambda i: (i, 0)),
            pl.BlockSpec((tm, 3 * D), lambda i: (i, 0)),
            pl.BlockSpec((tm, SSM_INNER), lambda i: (i, 0)),
            pl.BlockSpec((tm, D), lambda i: (i, 0)),
            pl.BlockSpec((tm, D), lambda i: (i, 0)),
            pl.BlockSpec((SSM_INNER, D), const),
            pl.BlockSpec((D, D), const),
            pl.BlockSpec((D, D), const),
            pl.BlockSpec((D, D), const),
            pl.BlockSpec((1, D), const),
            pl.BlockSpec((D, ROUTE_W), const),
            pl.BlockSpec((1, ROUTE_W), const),
        ],
        out_specs=[
            pl.BlockSpec((tm, D), lambda i: (i, 0)),
            pl.BlockSpec((tm, D), lambda i: (i, 0)),
            pl.BlockSpec((tm, ROUTE_W), lambda i: (i, 0)),
        ],
        out_shape=[
            jax.ShapeDtypeStruct((t, D), F32),
            jax.ShapeDtypeStruct((t, D), BF16),
            jax.ShapeDtypeStruct((t, ROUTE_W), F32),
        ],
        compiler_params=_cparams(("parallel",)),
        name="back",
    )(x, proj, y_ssm, y_swa, y_mem, ws, wa, wm, wo, g2, wr, br)


def _moe_kernel(x1_ref, h_ref, comb_ref, wg_ref, wu_ref, wd_ref, o_ref):
    e = pl.program_id(1)

    @pl.when(e == 0)
    def _():
        o_ref[...] = x1_ref[...]

    h = h_ref[...]
    lane = lax.broadcasted_iota(jnp.int32, comb_ref.shape, 1)
    cw = jnp.sum(jnp.where(lane == e + N_GROUPS, comb_ref[...], 0.0), axis=-1, keepdims=True)
    act = _silu(_dot(h, wg_ref[0])) * _dot(h, wu_ref[0])
    o_ref[...] += _dot((act * cw).astype(BF16), wd_ref[0])


def _moe(x1, h, comb, wg, wu, wd, tm):
    t = x1.shape[0]
    return pl.pallas_call(
        _moe_kernel,
        grid=(t // tm, N_EXPERTS),
        in_specs=[
            pl.BlockSpec((tm, D), lambda i, e: (i, 0)),
            pl.BlockSpec((tm, D), lambda i, e: (i, 0)),
            pl.BlockSpec((tm, ROUTE_W), lambda i, e: (i, 0)),
            pl.BlockSpec((1, D, D_FF), lambda i, e: (e, 0, 0)),
            pl.BlockSpec((1, D, D_FF), lambda i, e: (e, 0, 0)),
            pl.BlockSpec((1, D_FF, D), lambda i, e: (e, 0, 0)),
        ],
        out_specs=pl.BlockSpec((tm, D), lambda i, e: (i, 0)),
        out_shape=jax.ShapeDtypeStruct((t, D), F32),
        compiler_params=_cparams(("parallel", "arbitrary")),
        name="moe",
    )(x1, h, comb, wg, wu, wd)


def _sample_prep_kernel(xbc_ref, q_ref, k_ref, mq_ref, dt_ref, sc_ref, cw_ref, cb_ref, dtb_ref, a_ref, e_ref,
                        qg_ref, kg_ref, mqg_ref, cos_ref, slo_ref, shi_ref,
                        conv_ref, xs_ref, bc_ref, dtx_ref, dec_ref, qn_ref, kn_ref, mqn_ref):
    n = xbc_ref.shape[0]
    x_raw = xbc_ref[...].astype(F32)
    acc = x_raw * cw_ref[SSM_CONV - 1:SSM_CONV, :] + cb_ref[...]
    for j in range(SSM_CONV - 1):
        acc = acc + sc_ref[:, j * CONV_CH:(j + 1) * CONV_CH] * cw_ref[j:j + 1, :]
    conv_ref[:, 0:(SSM_CONV - 2) * CONV_CH] = sc_ref[:, CONV_CH:(SSM_CONV - 1) * CONV_CH]
    conv_ref[:, (SSM_CONV - 2) * CONV_CH:] = x_raw
    act = _silu(acc)
    xs = act[:, 0:SSM_INNER]
    xs_ref[...] = xs
    bc_ref[...] = act[:, SSM_INNER:]
    dt = _softplus(dt_ref[...] + dtb_ref[...])
    e = e_ref[...]
    dtx = _dot_exact(dt, e) * xs
    dec = _dot_exact(jnp.exp(dt * a_ref[...]), e)
    dtx_ref[...] = dtx.T
    dec_ref[...] = dec.T

    lane = lax.broadcasted_iota(jnp.int32, (n, LANE), 1)
    low_half = lane < ATT_HD
    cos_t = jnp.broadcast_to(cos_ref[...], (n, LANE))
    sin_lo = jnp.broadcast_to(slo_ref[...], (n, LANE))
    sin_hi = jnp.broadcast_to(shi_ref[...], (n, LANE))
    qf = q_ref[...].astype(F32)
    for j in range(D // LANE):
        sl = slice(j * LANE, (j + 1) * LANE)
        qn_ref[:, sl] = _norm_rope_tile(qf[:, sl], qg_ref[...], cos_t, sin_lo, sin_hi, low_half)
    kf = k_ref[...].astype(F32)
    for j in range(ATT_KVD // LANE):
        sl = slice(j * LANE, (j + 1) * LANE)
        kn_ref[:, sl] = _norm_rope_tile(kf[:, sl], kg_ref[...], cos_t, sin_lo, sin_hi, low_half)
    mq = mq_ref[...].astype(F32)
    for h in range(MEM_HEADS):
        sl = slice(h * MEM_HD, (h + 1) * MEM_HD)
        mh = mq[:, sl]
        ms = jnp.mean(mh * mh, axis=-1, keepdims=True)
        mqn_ref[:, sl] = mh * lax.rsqrt(ms + EPS) * mqg_ref[...]


def _sample_prep(proj, dt_raw, sc2d, cw, cb, dtb, a, e, qg2, kg2, mqg, cos_t, sin_lo, sin_hi):
    n = proj.shape[0]
    c2 = lambda i: (0, 0)
    full = lambda shape: pl.BlockSpec(shape, c2)
    return pl.pallas_call(
        _sample_prep_kernel,
        grid=(1,),
        in_specs=[
            pl.BlockSpec((n, CONV_CH), lambda i: (0, COL_XBC_BLK)),
            pl.BlockSpec((n, D), lambda i: (0, COL_Q_BLK)),
            pl.BlockSpec((n, ATT_KVD), lambda i: (0, COL_K_BLK)),
            pl.BlockSpec((n, D), lambda i: (0, COL_MQ_BLK)),
            full((n, LANE)),
            full((n, (SSM_CONV - 1) * CONV_CH)),
            full((SSM_CONV, CONV_CH)),
            full((1, CONV_CH)),
            full((1, LANE)),
            full((1, LANE)),
            full((LANE, SSM_INNER)),
            full((1, LANE)),
            full((1, LANE)),
            full((1, MEM_HD)),
            full((1, LANE)),
            full((1, LANE)),
            full((1, LANE)),
        ],
        out_specs=[
            full((n, (SSM_CONV - 1) * CONV_CH)),
            full((n, SSM_INNER)),
            full((n, 2 * SSM_GROUPS * SSM_STATE)),
            full((SSM_INNER, n)),
            full((SSM_INNER, n)),
            full((n, D)),
            full((n, ATT_KVD)),
            full((n, D)),
        ],
        out_shape=[
            jax.ShapeDtypeStruct((n, (SSM_CONV - 1) * CONV_CH), F32),
            jax.ShapeDtypeStruct((n, SSM_INNER), F32),
            jax.ShapeDtypeStruct((n, 2 * SSM_GROUPS * SSM_STATE), F32),
            jax.ShapeDtypeStruct((SSM_INNER, n), F32),
            jax.ShapeDtypeStruct((SSM_INNER, n), F32),
            jax.ShapeDtypeStruct((n, D), F32),
            jax.ShapeDtypeStruct((n, ATT_KVD), F32),
            jax.ShapeDtypeStruct((n, D), F32),
        ],
        compiler_params=_cparams(("arbitrary",)),
        name="sample_prep",
    )(proj, proj, proj, proj, dt_raw, sc2d, cw, cb, dtb, a, e, qg2, kg2, mqg, cos_t, sin_lo, sin_hi)


SSM_S_TILE = 8


def _sample_ssm_kernel(st_ref, dtx_ref, dec_ref, bc_ref, so_ref, yt_ref):
    i = pl.program_id(0)
    n = dtx_ref.shape[1]

    @pl.when(i == 0)
    def _():
        yt_ref[...] = jnp.zeros_like(yt_ref)

    gw = SSM_HPG * SSM_HEADDIM
    lane = lax.broadcasted_iota(jnp.int32, (gw, n), 1)
    nbc = SSM_GROUPS * SSM_STATE
    for s in range(SSM_S_TILE):
        sel = lane == (i * SSM_S_TILE + s)
        for g in range(SSM_GROUPS):
            rows = slice(g * gw, (g + 1) * gw)
            dtx_c = jnp.sum(jnp.where(sel, dtx_ref[rows, :], 0.0), axis=1, keepdims=True)
            dec_c = jnp.sum(jnp.where(sel, dec_ref[rows, :], 0.0), axis=1, keepdims=True)
            bm = bc_ref[s:s + 1, g * SSM_STATE:(g + 1) * SSM_STATE]
            cm = bc_ref[s:s + 1, nbc + g * SSM_STATE:nbc + (g + 1) * SSM_STATE]
            hn = st_ref[s, rows, :] * dec_c + dtx_c * bm
            so_ref[s, rows, :] = hn
            yc = jnp.sum(hn * cm, axis=1, keepdims=True)
            yt_ref[rows, :] = jnp.where(sel, yc, yt_ref[rows, :])


def _sample_ssm(state, dtx_t, dec_t, bc):
    n = state.shape[0]
    return pl.pallas_call(
        _sample_ssm_kernel,
        grid=(n // SSM_S_TILE,),
        in_specs=[
            pl.BlockSpec((SSM_S_TILE, SSM_INNER, SSM_STATE), lambda i: (i, 0, 0)),
            pl.BlockSpec((SSM_INNER, n), lambda i: (0, 0)),
            pl.BlockSpec((SSM_INNER, n), lambda i: (0, 0)),
            pl.BlockSpec((SSM_S_TILE, 2 * SSM_GROUPS * SSM_STATE), lambda i: (i, 0)),
        ],
        out_specs=[
            pl.BlockSpec((SSM_S_TILE, SSM_INNER, SSM_STATE), lambda i: (i, 0, 0)),
            pl.BlockSpec((SSM_INNER, n), lambda i: (0, 0)),
        ],
        out_shape=[
            jax.ShapeDtypeStruct((n, SSM_INNER, SSM_STATE), F32),
            jax.ShapeDtypeStruct((SSM_INNER, n), F32),
        ],
        compiler_params=_cparams(("arbitrary",)),
        name="sample_ssm",
    )(state, dtx_t, dec_t, bc)


def _sample_post_kernel(yt_ref, xs_ref, z_ref, dexp_ref, ng_ref, y_ref):
    y = yt_ref[...].T + dexp_ref[...] * xs_ref[...]
    yg = y * _silu(z_ref[...].astype(F32))
    ms = jnp.mean(yg * yg, axis=-1, keepdims=True)
    y_ref[...] = yg * lax.rsqrt(ms + EPS) * ng_ref[...]


def _sample_post(y_t, xs, proj, dexp, ng):
    n = xs.shape[0]
    c2 = lambda i: (0, 0)
    return pl.pallas_call(
        _sample_post_kernel,
        grid=(1,),
        in_specs=[
            pl.BlockSpec((SSM_INNER, n), c2),
            pl.BlockSpec((n, SSM_INNER), c2),
            pl.BlockSpec((n, SSM_INNER), lambda i: (0, COL_Z_BLK)),
            pl.BlockSpec((1, SSM_INNER), c2),
            pl.BlockSpec((1, SSM_INNER), c2),
        ],
        out_specs=pl.BlockSpec((n, SSM_INNER), c2),
        out_shape=jax.ShapeDtypeStruct((n, SSM_INNER), F32),
        compiler_params=_cparams(("arbitrary",)),
        name="sample_post",
    )(y_t, xs, proj, dexp, ng)


SWA_S_TILE = 8


def _sample_swa_kernel(q_ref, kn_ref, v_ref, ck_ref, cv_ref, sink_ref, y_ref, ko_ref, vo_ref):
    w = WINDOW
    scale = ATT_HD ** -0.5
    newest = lax.broadcasted_iota(jnp.int32, (ATT_HD, w), 1) == w - 1
    for s in range(SWA_S_TILE):
        for kv in range(ATT_KV):
            kt = jnp.where(newest, kn_ref[s, kv], pltpu.roll(ck_ref[s, kv], w - 1, axis=1))
            vt = jnp.where(newest, v_ref[s, kv], pltpu.roll(cv_ref[s, kv], w - 1, axis=1))
            ko_ref[s, kv] = kt
            vo_ref[s, kv] = vt
            sc = _dot(q_ref[s, kv].astype(BF16), kt.astype(BF16)) * scale
            snk = sink_ref[kv]
            m = jnp.maximum(jnp.max(sc, axis=-1, keepdims=True), snk)
            p = jnp.exp(sc - m)
            den = jnp.sum(p, axis=-1, keepdims=True) + jnp.exp(snk - m)
            y_ref[s, kv] = _dot_nt(p.astype(BF16), vt.astype(BF16)) / den


def _sample_swa(q4, kn4, v4, ck_t, cv_t, sink3, l):
    n = q4.shape[0]
    w = WINDOW
    st = SWA_S_TILE
    grp = ATT_HEADS // ATT_KV
    cache = pl.BlockSpec((None, st, ATT_KV, ATT_HD, w), lambda i: (l, i, 0, 0, 0))
    new = pl.BlockSpec((st, ATT_KV, ATT_HD, 1), lambda i: (i, 0, 0, 0))
    out = pl.BlockSpec((st, ATT_KV, ATT_HD, w), lambda i: (i, 0, 0, 0))
    return pl.pallas_call(
        _sample_swa_kernel,
        grid=(n // st,),
        in_specs=[
            pl.BlockSpec((st, ATT_KV, grp, ATT_HD), lambda i: (i, 0, 0, 0)),
            new, new, cache, cache,
            pl.BlockSpec((ATT_KV, grp, 1), lambda i: (0, 0, 0)),
        ],
        out_specs=[pl.BlockSpec((st, ATT_KV, grp, ATT_HD), lambda i: (i, 0, 0, 0)), out, out],
        out_shape=[
            jax.ShapeDtypeStruct((n, ATT_KV, grp, ATT_HD), F32),
            jax.ShapeDtypeStruct((n, ATT_KV, ATT_HD, w), F32),
            jax.ShapeDtypeStruct((n, ATT_KV, ATT_HD, w), F32),
        ],
        compiler_params=_cparams(("parallel",)),
        name="sample_swa",
    )(q4, kn4, v4, ck_t, cv_t, sink3)


MEM_S_TILE = 4


def _sample_mem_kernel(q_ref, k_ref, v_ref, y_ref):
    scale = MEM_HD ** -0.5
    for s in range(MEM_S_TILE):
        sc = jnp.sum(k_ref[s] * q_ref[s][None], axis=-1, keepdims=True) * scale
        m = jnp.max(sc, axis=0, keepdims=True)
        p = jnp.exp(sc - m)
        den = jnp.sum(p, axis=0, keepdims=True)
        o = jnp.sum(p * v_ref[s], axis=0, keepdims=True) / den
        y_ref[s] = o[0]


def _sample_mem(q3, ck, cv, l):
    n = q3.shape[0]
    mem_len = ck.shape[2]
    st = MEM_S_TILE
    cache = pl.BlockSpec((None, st, mem_len, MEM_HEADS, MEM_HD), lambda i: (l, i, 0, 0, 0))
    return pl.pallas_call(
        _sample_mem_kernel,
        grid=(n // st,),
        in_specs=[pl.BlockSpec((st, MEM_HEADS, MEM_HD), lambda i: (i, 0, 0)), cache, cache],
        out_specs=pl.BlockSpec((st, MEM_HEADS, MEM_HD), lambda i: (i, 0, 0)),
        out_shape=jax.ShapeDtypeStruct((n, MEM_HEADS, MEM_HD), F32),
        compiler_params=_cparams(("parallel",)),
        name="sample_mem",
    )(q3, ck, cv)


def _prep_weights(lw):
    w_in = lw['w_in']
    cols = [w_in[:, 0:OFF_Z], w_in[:, OFF_XBC:OFF_DT], w_in[:, OFF_Z:OFF_XBC], w_in[:, OFF_Q:OFF_K],
            w_in[:, OFF_MQ:OFF_MQ + D], w_in[:, OFF_K:OFF_V], w_in[:, OFF_V:OFF_MQ]]
    p = {}
    p['w_main'] = jnp.concatenate(cols, axis=1).astype(BF16)
    p['w_dt'] = jnp.pad(w_in[:, OFF_DT:OFF_Q], ((0, 0), (0, LANE - SSM_HEADS))).astype(BF16)
    p['norm1_g'] = lw['norm1_g'].reshape(1, D)
    p['cw'] = lw['ssm_conv_w']
    p['cb'] = lw['ssm_conv_b'].reshape(1, CONV_CH)
    pad_h = (0, LANE - SSM_HEADS)
    p['dtb'] = jnp.pad(lw['ssm_dt_bias'].astype(F32), pad_h).reshape(1, LANE)
    p['a'] = jnp.pad(-jnp.exp(lw['ssm_a_log'].astype(F32)), pad_h).reshape(1, LANE)
    p['dexp'] = jnp.repeat(lw['ssm_d'], SSM_HEADDIM).reshape(1, SSM_INNER)
    p['ssm_ng'] = lw['ssm_norm_g'].reshape(1, SSM_INNER)
    head_of_ch = jnp.arange(SSM_INNER) // SSM_HEADDIM
    p['e'] = (jnp.arange(LANE)[:, None] == head_of_ch[None, :]).astype(F32)
    p['qg2'] = jnp.tile(lw['att_q_norm_g'], 2).reshape(1, LANE)
    p['kg2'] = jnp.tile(lw['att_k_norm_g'], 2).reshape(1, LANE)
    p['sink_row'] = lw['att_sink'].astype(F32).reshape(1, ATT_HEADS)
    p['sink3'] = lw['att_sink'].astype(F32).reshape(ATT_KV, ATT_HEADS // ATT_KV, 1)
    p['mem_g'] = lw['mem_norm_g'].reshape(1, D)
    p['w_mem_kv'] = lw['w_mem_kv'].astype(BF16)
    p['mqg'] = lw['mem_q_norm_g'].reshape(1, MEM_HD)
    p['mkg'] = lw['mem_k_norm_g'].reshape(1, MEM_HD)
    p['ws'] = lw['w_br_ssm'].astype(BF16)
    p['wa'] = lw['w_br_swa'].astype(BF16)
    p['wm'] = lw['w_br_mem'].astype(BF16)
    p['wo'] = lw['w_out'].astype(BF16)
    p['g2'] = lw['norm2_g'].reshape(1, D)
    pad_r = ROUTE_W - N_GROUPS - N_EXPERTS
    p['wr'] = jnp.pad(jnp.concatenate([lw['w_router_group'], lw['w_router_expert']], axis=1).astype(F32),
                      ((0, 0), (0, pad_r)))
    p['br'] = jnp.pad(jnp.concatenate([lw['b_router_group'], lw['b_router_expert']]).astype(F32),
                      (0, pad_r)).reshape(1, ROUTE_W)
    p['wg'] = lw['w_exp_gate'].astype(BF16)
    p['wu'] = lw['w_exp_up'].astype(BF16)
    p['wd'] = lw['w_exp_down'].astype(BF16)
    return p


def _pick(n, prefs):
    for c in prefs:
        if n % c == 0:
            return c
    return n


def _tail(x, proj, y_ssm, y_swa, y_mem, p):
    t = x.shape[0]
    x1, h, comb = _back(x, proj, y_ssm, y_swa, y_mem, p['ws'], p['wa'], p['wm'], p['wo'], p['g2'],
                        p['wr'], p['br'], _pick(t, (256, 128)))
    return _moe(x1, h, comb, p['wg'], p['wu'], p['wd'], _pick(t, (1024, 512, 256, 128)))


def _prompt_layer(x, mem, p):
    nb, seq, _ = x.shape
    t = nb * seq
    nc = seq // CHUNK
    xf = x.reshape(t, D)
    proj, dt_raw = _front(xf, p['norm1_g'], p['w_main'], p['w_dt'], _pick(t, (1024, 512, 256, 128)), 1536)
    y_ssm, conv_new, ssm_new = _ssd(proj, dt_raw, p['cw'], p['cb'], p['dtb'], p['a'], p['dexp'],
                                    p['ssm_ng'], p['e'], nb, nc)
    cos_t, sin_lo, sin_hi = _rope_tables(jnp.arange(seq))
    y_swa, k_new, v_new = _swa(proj, p['qg2'], p['kg2'], cos_t, sin_lo, sin_hi, p['sink_row'], nb, nc)
    mem_len = mem.shape[1]
    mk, mv = _memkv(mem.reshape(nb * mem_len, D), p['mem_g'], p['w_mem_kv'], p['mkg'],
                    _pick(nb * mem_len, (512, 256, 128)))
    y_mem = _memattn(proj, mk, mv, p['mqg'], nb, seq, mem_len, _pick(seq, (512, 256, 128)))
    y = _tail(xf, proj, y_ssm, y_swa, y_mem, p)
    return (y.reshape(nb, seq, D), conv_new,
            ssm_new.reshape(nb, SSM_HEADS, SSM_HEADDIM, SSM_STATE),
            k_new.reshape(nb, WINDOW, ATT_KV, ATT_HD), v_new.reshape(nb, WINDOW, ATT_KV, ATT_HD),
            mk.reshape(nb, mem_len, MEM_HEADS, MEM_HD), mv.reshape(nb, mem_len, MEM_HEADS, MEM_HD))


def _sample_layer(x, conv_st, ssm_st, swa_k, swa_v, mem_k, mem_v, l, p):
    n = x.shape[0]
    xf = x.reshape(n, D)
    proj, dt_raw = _front(xf, p['norm1_g'], p['w_main'], p['w_dt'], n, 1536)
    cos_t, sin_lo, sin_hi = _rope_tables(jnp.full((1,), PAST_LEN, jnp.int32))
    sc2d = conv_st.reshape(n, (SSM_CONV - 1) * CONV_CH)
    conv_new, xs, bc, dtx_t, dec_t, qn, kn, mqn = _sample_prep(
        proj, dt_raw, sc2d, p['cw'], p['cb'], p['dtb'], p['a'], p['e'], p['qg2'], p['kg2'], p['mqg'],
        cos_t, sin_lo, sin_hi)
    ssm_new, y_t = _sample_ssm(ssm_st.reshape(n, SSM_INNER, SSM_STATE), dtx_t, dec_t, bc)
    y_ssm = _sample_post(y_t, xs, proj, p['dexp'], p['ssm_ng'])
    grp = ATT_HEADS // ATT_KV
    v_raw = proj[:, COL_V_BLK * ATT_KVD:(COL_V_BLK + 1) * ATT_KVD].astype(F32)
    to_t = (0, 1, 3, 4, 2)
    y_swa, k_new, v_new = _sample_swa(
        qn.reshape(n, ATT_KV, grp, ATT_HD), kn.reshape(n, ATT_KV, ATT_HD, 1),
        v_raw.reshape(n, ATT_KV, ATT_HD, 1), swa_k.transpose(to_t), swa_v.transpose(to_t), p['sink3'], l)
    y_swa = y_swa.reshape(n, D)
    k_new = k_new.transpose(0, 3, 1, 2)
    v_new = v_new.transpose(0, 3, 1, 2)
    y_mem = _sample_mem(mqn.reshape(n, MEM_HEADS, MEM_HD), mem_k, mem_v, l)
    y = _tail(xf, proj, y_ssm, y_swa.astype(BF16), y_mem.reshape(n, D).astype(BF16), p)
    return (y.reshape(n, 1, D), conv_new.reshape(n, SSM_CONV - 1, CONV_CH),
            ssm_new.reshape(n, SSM_HEADS, SSM_HEADDIM, SSM_STATE), k_new, v_new)


def kernel(x_prompt, x_sample, state_conv, state_ssm, cache_swa_k, cache_swa_v, cache_mem_k, cache_mem_v,
           mem_prompt, norm1_g, w_in, ssm_conv_w, ssm_conv_b, ssm_dt_bias, ssm_a_log, ssm_d, ssm_norm_g,
           att_q_norm_g, att_k_norm_g, att_sink, mem_norm_g, w_mem_kv, mem_q_norm_g, mem_k_norm_g,
           w_br_ssm, w_br_swa, w_br_mem, w_out, norm2_g, w_router_group, b_router_group,
           w_router_expert, b_router_expert, w_exp_gate, w_exp_up, w_exp_down):
    weights = dict(norm1_g=norm1_g, w_in=w_in, ssm_conv_w=ssm_conv_w, ssm_conv_b=ssm_conv_b,
                   ssm_dt_bias=ssm_dt_bias, ssm_a_log=ssm_a_log, ssm_d=ssm_d, ssm_norm_g=ssm_norm_g,
                   att_q_norm_g=att_q_norm_g, att_k_norm_g=att_k_norm_g, att_sink=att_sink,
                   mem_norm_g=mem_norm_g, w_mem_kv=w_mem_kv, mem_q_norm_g=mem_q_norm_g,
                   mem_k_norm_g=mem_k_norm_g, w_br_ssm=w_br_ssm, w_br_swa=w_br_swa, w_br_mem=w_br_mem,
                   w_out=w_out, norm2_g=norm2_g, w_router_group=w_router_group,
                   b_router_group=b_router_group, w_router_expert=w_router_expert,
                   b_router_expert=b_router_expert, w_exp_gate=w_exp_gate, w_exp_up=w_exp_up,
                   w_exp_down=w_exp_down)
    depth = w_in.shape[0]
    xp, xs = x_prompt, x_sample
    outs = [[] for _ in range(10)]
    for l in range(depth):
        p = _prep_weights({k: v[l] for k, v in weights.items()})
        xp, c1, c2, c3, c4, c5, c6 = _prompt_layer(xp, mem_prompt, p)
        xs, d1, d2, d3, d4 = _sample_layer(xs, state_conv[l], state_ssm[l], cache_swa_k, cache_swa_v,
                                           cache_mem_k, cache_mem_v, l, p)
        for lst, val in zip(outs, (c1, c2, c3, c4, c5, c6, d1, d2, d3, d4)):
            lst.append(val)
    return (xp, xs) + tuple(jnp.stack(o) for o in outs)
```

```python
import functools
import math

import jax
import jax.numpy as jnp
from jax import lax
from jax.experimental import pallas as pl
from jax.experimental.pallas import tpu as pltpu

F32 = jnp.float32
BF16 = jnp.bfloat16
HIGHEST = lax.Precision.HIGHEST

D = 1024
SSM_INNER = 2048
SSM_HEADDIM = 64
SSM_HEADS = 32
SSM_GROUPS = 4
SSM_HPG = SSM_HEADS // SSM_GROUPS
SSM_STATE = 128
SSM_CONV = 4
CONV_CH = SSM_INNER + 2 * SSM_GROUPS * SSM_STATE
CHUNK = 128
ATT_HEADS = 16
ATT_KV = 4
ATT_HD = 64
ATT_KVD = ATT_KV * ATT_HD
WINDOW = 128
ROPE_THETA = 10000.0
MEM_HEADS = 4
MEM_HD = 256
N_EXPERTS = 32
N_GROUPS = 4
EPG = 8
D_FF = 256
EPS = 1e-6
PAST_LEN = 16384

OFF_Z = 3 * D
OFF_XBC = OFF_Z + SSM_INNER
OFF_DT = OFF_XBC + CONV_CH
OFF_Q = OFF_DT + SSM_HEADS
OFF_K = OFF_Q + D
OFF_V = OFF_K + ATT_KVD
OFF_MQ = OFF_V + ATT_KVD

N_MAIN = 3 * D + CONV_CH + SSM_INNER + D + D + 2 * ATT_KVD
COL_XBC_BLK = 1
COL_Z_BLK = 3
COL_Q_BLK = 8
COL_MQ_BLK = 9
COL_K_BLK = 40
COL_V_BLK = 41
LANE = 128
ROUTE_W = 128

VMEM_LIMIT = 56 * 1024 * 1024


def _cparams(sem):
    return pltpu.CompilerParams(dimension_semantics=sem, vmem_limit_bytes=VMEM_LIMIT)


def _sigmoid(x):
    return 1.0 / (1.0 + jnp.exp(-x))


def _silu(x):
    return x * _sigmoid(x)


def _softplus(x):
    return jnp.maximum(x, 0.0) + jnp.log1p(jnp.exp(-jnp.abs(x)))


def _dot(a, b):
    return jnp.dot(a, b, preferred_element_type=F32)


def _dot_nt(a, b):
    return lax.dot_general(a, b, (((1,), (1,)), ((), ())), preferred_element_type=F32)


def _dot_tn(a, b):
    return lax.dot_general(a, b, (((0,), (0,)), ((), ())), preferred_element_type=F32)


def _dot_exact(a, b):
    return jnp.dot(a, b, preferred_element_type=F32, precision=HIGHEST)


def _front_kernel(x_ref, g_ref, w_ref, wdt_ref, o_ref, dt_ref, hn_ref):
    @pl.when(pl.program_id(1) == 0)
    def _():
        x = x_ref[...]
        ms = jnp.mean(x * x, axis=-1, keepdims=True)
        hn = (x * lax.rsqrt(ms + EPS) * g_ref[...]).astype(BF16)
        hn_ref[...] = hn
        dt_ref[...] = _dot(hn, wdt_ref[...])

    o_ref[...] = _dot(hn_ref[...], w_ref[...]).astype(BF16)


def _front(x, g, w_main, w_dt, tm, tn):
    t = x.shape[0]
    return pl.pallas_call(
        _front_kernel,
        grid=(t // tm, N_MAIN // tn),
        in_specs=[
            pl.BlockSpec((tm, D), lambda i, j: (i, 0)),
            pl.BlockSpec((1, D), lambda i, j: (0, 0)),
            pl.BlockSpec((D, tn), lambda i, j: (0, j)),
            pl.BlockSpec((D, LANE), lambda i, j: (0, 0)),
        ],
        out_specs=[
            pl.BlockSpec((tm, tn), lambda i, j: (i, j)),
            pl.BlockSpec((tm, LANE), lambda i, j: (i, 0)),
        ],
        out_shape=[
            jax.ShapeDtypeStruct((t, N_MAIN), BF16),
            jax.ShapeDtypeStruct((t, LANE), F32),
        ],
        scratch_shapes=[pltpu.VMEM((tm, D), BF16)],
        compiler_params=_cparams(("parallel", "arbitrary")),
        name="front",
    )(x, g, w_main, w_dt)


HIST_PAD = 8


def _ssd_kernel(xbc_ref, z_ref, dt_ref, cw_ref, cb_ref, dtb_ref, a_ref, dexp_ref, ng_ref, e_ref,
                y_ref, conv_ref, ssm_ref, hist_ref, h_ref):
    c = pl.program_id(1)

    @pl.when(c == 0)
    def _():
        hist_ref[0:HIST_PAD, :] = jnp.zeros((HIST_PAD, CONV_CH), F32)
        h_ref[...] = jnp.zeros_like(h_ref)

    x_raw = xbc_ref[...].astype(F32)
    hist_ref[HIST_PAD:HIST_PAD + CHUNK, :] = x_raw
    acc = x_raw * cw_ref[SSM_CONV - 1:SSM_CONV, :] + cb_ref[...]
    for j in range(SSM_CONV - 1):
        lo = HIST_PAD - (SSM_CONV - 1) + j
        acc = acc + hist_ref[lo:lo + CHUNK, :] * cw_ref[j:j + 1, :]
    tail = hist_ref[HIST_PAD + CHUNK - (SSM_CONV - 1):HIST_PAD + CHUNK, :]
    hist_ref[HIST_PAD - (SSM_CONV - 1):HIST_PAD, :] = tail
    conv_ref[0] = tail
    act = _silu(acc)

    dt = _softplus(dt_ref[...] + dtb_ref[...])
    da = dt * a_ref[...]
    row = lax.broadcasted_iota(jnp.int32, (CHUNK, CHUNK), 0)
    col = lax.broadcasted_iota(jnp.int32, (CHUNK, CHUNK), 1)
    causal = row >= col
    acum = _dot_exact(causal.astype(F32), da)
    acum_t = acum.T
    dt_t = dt.T
    e = e_ref[...]
    exp_a = _dot_exact(jnp.exp(acum), e)
    last = acum[CHUNK - 1:CHUNK, :]
    w_end = _dot_exact(jnp.exp(last - acum) * dt, e)
    cd = jnp.broadcast_to(jnp.exp(acum_t[:, CHUNK - 1:CHUNK]), (LANE, SSM_STATE))
    lane = lax.broadcasted_iota(jnp.int32, (CHUNK, LANE), 1)
    low_half = lane < SSM_HEADDIM

    xs_off = 0
    b_off = SSM_INNER
    c_off = SSM_INNER + SSM_GROUPS * SSM_STATE
    for g in range(SSM_GROUPS):
        bm = act[:, b_off + g * SSM_STATE:b_off + (g + 1) * SSM_STATE].astype(BF16)
        cm = act[:, c_off + g * SSM_STATE:c_off + (g + 1) * SSM_STATE].astype(BF16)
        cb = _dot_nt(cm, bm)
        gw = SSM_HPG * SSM_HEADDIM
        ch0 = g * gw
        h_prev = h_ref[ch0:ch0 + gw, :]
        y_off = _dot_nt(cm, h_prev.astype(BF16)) * exp_a[:, ch0:ch0 + gw]
        xg = act[:, xs_off + ch0:xs_off + ch0 + gw]
        for pr in range(SSM_HPG // 2):
            xpair = xg[:, pr * LANE:(pr + 1) * LANE].astype(BF16)
            yd = []
            for sub in range(2):
                hd = g * SSM_HPG + pr * 2 + sub
                seg = acum[:, hd:hd + 1] - acum_t[hd:hd + 1, :]
                decay = jnp.exp(jnp.where(causal, seg, -jnp.inf))
                wts = cb * decay * dt_t[hd:hd + 1, :]
                yd.append(_dot(wts.astype(BF16), xpair))
            y_pair = jnp.where(low_half, yd[0], yd[1]) + y_off[:, pr * LANE:(pr + 1) * LANE]
            cl = ch0 + pr * LANE
            xp32 = xg[:, pr * LANE:(pr + 1) * LANE]
            y_pair = y_pair + dexp_ref[:, cl:cl + LANE] * xp32
            y_ref[:, cl:cl + LANE] = y_pair.astype(F32)
        xw = (xg * w_end[:, ch0:ch0 + gw]).astype(BF16)
        states = _dot_tn(xw, bm)
        for r in range(SSM_HPG):
            hd = g * SSM_HPG + r
            r0 = ch0 + r * SSM_HEADDIM
            h_ref[r0:r0 + SSM_HEADDIM, :] = (h_ref[r0:r0 + SSM_HEADDIM, :] * cd[hd:hd + 1, :]
                                             + states[r * SSM_HEADDIM:(r + 1) * SSM_HEADDIM, :])

    ssm_ref[0] = h_ref[...]
    yg = y_ref[...] * _silu(z_ref[...].astype(F32))
    ms = jnp.mean(yg * yg, axis=-1, keepdims=True)
    y_ref[...] = yg * lax.rsqrt(ms + EPS) * ng_ref[...]


def _ssd(proj, dt_raw, cw, cb, dtb, a, dexp, ng, e, nb, nc):
    t = proj.shape[0]
    return pl.pallas_call(
        _ssd_kernel,
        grid=(nb, nc),
        in_specs=[
            pl.BlockSpec((CHUNK, CONV_CH), lambda b, c: (b * nc + c, COL_XBC_BLK)),
            pl.BlockSpec((CHUNK, SSM_INNER), lambda b, c: (b * nc + c, COL_Z_BLK)),
            pl.BlockSpec((CHUNK, LANE), lambda b, c: (b * nc + c, 0)),
            pl.BlockSpec((SSM_CONV, CONV_CH), lambda b, c: (0, 0)),
            pl.BlockSpec((1, CONV_CH), lambda b, c: (0, 0)),
            pl.BlockSpec((1, LANE), lambda b, c: (0, 0)),
            pl.BlockSpec((1, LANE), lambda b, c: (0, 0)),
            pl.BlockSpec((1, SSM_INNER), lambda b, c: (0, 0)),
            pl.BlockSpec((1, SSM_INNER), lambda b, c: (0, 0)),
            pl.BlockSpec((LANE, SSM_INNER), lambda b, c: (0, 0)),
        ],
        out_specs=[
            pl.BlockSpec((CHUNK, SSM_INNER), lambda b, c: (b * nc + c, 0)),
            pl.BlockSpec((1, SSM_CONV - 1, CONV_CH), lambda b, c: (b, 0, 0)),
            pl.BlockSpec((1, SSM_INNER, SSM_STATE), lambda b, c: (b, 0, 0)),
        ],
        out_shape=[
            jax.ShapeDtypeStruct((t, SSM_INNER), F32),
            jax.ShapeDtypeStruct((nb, SSM_CONV - 1, CONV_CH), F32),
            jax.ShapeDtypeStruct((nb, SSM_INNER, SSM_STATE), F32),
        ],
        scratch_shapes=[
            pltpu.VMEM((HIST_PAD + CHUNK, CONV_CH), F32),
            pltpu.VMEM((SSM_INNER, SSM_STATE), F32),
        ],
        compiler_params=_cparams(("parallel", "arbitrary")),
        name="ssd_prompt",
    )(proj, proj, dt_raw, cw, cb, dtb, a, dexp, ng, e)


def _norm_rope_tile(xj, g2, cos_t, sin_lo, sin_hi, low_half):
    sq = xj * xj
    s_lo = jnp.sum(jnp.where(low_half, sq, 0.0), axis=-1, keepdims=True)
    s_hi = jnp.sum(jnp.where(low_half, 0.0, sq), axis=-1, keepdims=True)
    ms = jnp.where(low_half, s_lo, s_hi) * (1.0 / ATT_HD)
    xn = xj * lax.rsqrt(ms + EPS) * g2
    half = ATT_HD // 2
    return (xn * cos_t + pltpu.roll(xn, LANE - half, axis=1) * sin_lo
            + pltpu.roll(xn, half, axis=1) * sin_hi)


def _rope_tables(pos):
    half = ATT_HD // 2
    inv = ROPE_THETA ** (-jnp.arange(half, dtype=F32) / half)
    ang = pos.astype(F32)[:, None] * inv[None, :]
    cos = jnp.cos(ang)
    sin = jnp.sin(ang)
    zero = jnp.zeros_like(sin)
    cos_t = jnp.concatenate([cos, cos, cos, cos], axis=1)
    sin_lo = jnp.concatenate([-sin, zero, -sin, zero], axis=1)
    sin_hi = jnp.concatenate([zero, sin, zero, sin], axis=1)
    return cos_t, sin_lo, sin_hi


def _swa_kernel(q_ref, k_ref, v_ref, qg_ref, kg_ref, cos_ref, slo_ref, shi_ref, sink_ref,
                y_ref, ko_ref, vo_ref, kd_ref, vd_ref, qs_ref):
    c = pl.program_id(1)
    w = WINDOW
    grp = ATT_HEADS // ATT_KV
    lane = lax.broadcasted_iota(jnp.int32, (w, LANE), 1)
    low_half = lane < ATT_HD
    cos_t = cos_ref[...]
    sin_lo = slo_ref[...]
    sin_hi = shi_ref[...]

    @pl.when(c == 0)
    def _():
        zeros = jnp.zeros((ATT_KV, w, LANE), BF16)
        kd_ref[:, 0:w, :] = zeros
        vd_ref[:, 0:w, :] = zeros

    kf = k_ref[...].astype(F32)
    vf = v_ref[...].astype(F32)
    for j in range(ATT_KVD // LANE):
        sl = slice(j * LANE, (j + 1) * LANE)
        kn = _norm_rope_tile(kf[:, sl], kg_ref[...], cos_t, sin_lo, sin_hi, low_half)
        ko_ref[0, :, sl] = kn
        kr = pltpu.roll(kn, ATT_HD, axis=1)
        vj = vf[:, sl]
        vr = pltpu.roll(vj, ATT_HD, axis=1)
        kd_ref[2 * j, w:2 * w, :] = jnp.where(low_half, kn, kr).astype(BF16)
        kd_ref[2 * j + 1, w:2 * w, :] = jnp.where(low_half, kr, kn).astype(BF16)
        vd_ref[2 * j, w:2 * w, :] = jnp.where(low_half, vj, vr).astype(BF16)
        vd_ref[2 * j + 1, w:2 * w, :] = jnp.where(low_half, vr, vj).astype(BF16)
    vo_ref[0] = vf

    qf = q_ref[...].astype(F32)
    for j in range(D // LANE):
        sl = slice(j * LANE, (j + 1) * LANE)
        qn = _norm_rope_tile(qf[:, sl], qg_ref[...], cos_t, sin_lo, sin_hi, low_half)
        for par in range(2):
            h = 2 * j + par
            r0 = (h % grp) * w
            in_half = low_half if par == 0 else jnp.logical_not(low_half)
            qs_ref[h // grp, r0:r0 + w, :] = jnp.where(in_half, qn, 0.0).astype(BF16)

    qi = lax.broadcasted_iota(jnp.int32, (grp * w, 2 * w), 0) & (w - 1)
    kj = lax.broadcasted_iota(jnp.int32, (grp * w, 2 * w), 1)
    mask = ((kj < w) & (kj > qi) & (c > 0)) | ((kj >= w) & ((kj - w) <= qi))
    scale = ATT_HD ** -0.5

    for kv in range(ATT_KV):
        snk = jnp.concatenate(
            [jnp.broadcast_to(sink_ref[:, kv * grp + i:kv * grp + i + 1], (w, 1)) for i in range(grp)], axis=0)
        s = _dot_nt(qs_ref[kv], kd_ref[kv]) * scale
        s = jnp.where(mask, s, -jnp.inf)
        m = jnp.maximum(jnp.max(s, axis=-1, keepdims=True), snk)
        p = jnp.exp(s - m)
        den = jnp.sum(p, axis=-1, keepdims=True) + jnp.exp(snk - m)
        o = _dot(p.astype(BF16), vd_ref[kv]) * (1.0 / den)
        for a in range(grp // 2):
            col = kv * (grp // 2) + a
            y_ref[:, col * LANE:(col + 1) * LANE] = jnp.where(
                low_half, o[2 * a * w:(2 * a + 1) * w, :], o[(2 * a + 1) * w:(2 * a + 2) * w, :]).astype(BF16)

    kd_ref[:, 0:w, :] = kd_ref[:, w:2 * w, :]
    vd_ref[:, 0:w, :] = vd_ref[:, w:2 * w, :]


def _swa(proj, qg2, kg2, cos_t, sin_lo, sin_hi, sink, nb, nc):
    t = proj.shape[0]
    w = WINDOW
    tab = pl.BlockSpec((w, LANE), lambda b, c: (c, 0))
    return pl.pallas_call(
        _swa_kernel,
        grid=(nb, nc),
        in_specs=[
            pl.BlockSpec((w, D), lambda b, c: (b * nc + c, COL_Q_BLK)),
            pl.BlockSpec((w, ATT_KVD), lambda b, c: (b * nc + c, COL_K_BLK)),
            pl.BlockSpec((w, ATT_KVD), lambda b, c: (b * nc + c, COL_V_BLK)),
            pl.BlockSpec((1, LANE), lambda b, c: (0, 0)),
            pl.BlockSpec((1, LANE), lambda b, c: (0, 0)),
            tab, tab, tab,
            pl.BlockSpec((1, ATT_HEADS), lambda b, c: (0, 0)),
        ],
        out_specs=[
            pl.BlockSpec((w, D), lambda b, c: (b * nc + c, 0)),
            pl.BlockSpec((1, w, ATT_KVD), lambda b, c: (b, 0, 0)),
            pl.BlockSpec((1, w, ATT_KVD), lambda b, c: (b, 0, 0)),
        ],
        out_shape=[
            jax.ShapeDtypeStruct((t, D), BF16),
            jax.ShapeDtypeStruct((nb, w, ATT_KVD), F32),
            jax.ShapeDtypeStruct((nb, w, ATT_KVD), F32),
        ],
        scratch_shapes=[
            pltpu.VMEM((ATT_KV, 2 * w, LANE), BF16),
            pltpu.VMEM((ATT_KV, 2 * w, LANE), BF16),
            pltpu.VMEM((ATT_KV, (ATT_HEADS // ATT_KV) * w, LANE), BF16),
        ],
        compiler_params=_cparams(("parallel", "arbitrary")),
        name="swa_prompt",
    )(proj, proj, proj, qg2, kg2, cos_t, sin_lo, sin_hi, sink)


def _memkv_kernel(m_ref, g_ref, w_ref, kg_ref, k_ref, v_ref):
    x = m_ref[...]
    ms = jnp.mean(x * x, axis=-1, keepdims=True)
    hn = (x * lax.rsqrt(ms + EPS) * g_ref[...]).astype(BF16)
    kv = _dot(hn, w_ref[...])
    for h in range(MEM_HEADS):
        kh = kv[:, h * MEM_HD:(h + 1) * MEM_HD]
        ms = jnp.mean(kh * kh, axis=-1, keepdims=True)
        k_ref[:, h * MEM_HD:(h + 1) * MEM_HD] = kh * lax.rsqrt(ms + EPS) * kg_ref[...]
    v_ref[...] = kv[:, D:]


def _memkv(mem, g, w, kg, tm):
    t = mem.shape[0]
    return pl.pallas_call(
        _memkv_kernel,
        grid=(t // tm,),
        in_specs=[
            pl.BlockSpec((tm, D), lambda i: (i, 0)),
            pl.BlockSpec((1, D), lambda i: (0, 0)),
            pl.BlockSpec((D, 2 * D), lambda i: (0, 0)),
            pl.BlockSpec((1, MEM_HD), lambda i: (0, 0)),
        ],
        out_specs=[
            pl.BlockSpec((tm, D), lambda i: (i, 0)),
            pl.BlockSpec((tm, D), lambda i: (i, 0)),
        ],
        out_shape=[jax.ShapeDtypeStruct((t, D), F32)] * 2,
        compiler_params=_cparams(("parallel",)),
        name="mem_kv",
    )(mem, g, w, kg)


def _memattn_kernel(q_ref, k_ref, v_ref, qg_ref, y_ref):
    q = q_ref[...].astype(F32)
    scale = MEM_HD ** -0.5
    for h in range(MEM_HEADS):
        sl = slice(h * MEM_HD, (h + 1) * MEM_HD)
        qh = q[:, sl]
        ms = jnp.mean(qh * qh, axis=-1, keepdims=True)
        qn = (qh * lax.rsqrt(ms + EPS) * qg_ref[...]).astype(BF16)
        s = _dot_nt(qn, k_ref[:, sl].astype(BF16)) * scale
        m = jnp.max(s, axis=-1, keepdims=True)
        p = jnp.exp(s - m)
        den = jnp.sum(p, axis=-1, keepdims=True)
        o = _dot(p.astype(BF16), v_ref[:, sl].astype(BF16)) / den
        y_ref[:, sl] = o.astype(BF16)


def _memattn(proj, mk, mv, qg, nb, seq, mem_len, tm):
    t = proj.shape[0]
    nt = seq // tm
    return pl.pallas_call(
        _memattn_kernel,
        grid=(nb, nt),
        in_specs=[
            pl.BlockSpec((tm, D), lambda b, i: (b * nt + i, COL_MQ_BLK)),
            pl.BlockSpec((mem_len, D), lambda b, i: (b, 0)),
            pl.BlockSpec((mem_len, D), lambda b, i: (b, 0)),
            pl.BlockSpec((1, MEM_HD), lambda b, i: (0, 0)),
        ],
        out_specs=pl.BlockSpec((tm, D), lambda b, i: (b * nt + i, 0)),
        out_shape=jax.ShapeDtypeStruct((t, D), BF16),
        compiler_params=_cparams(("parallel", "arbitrary")),
        name="mem_attn_prompt",
    )(proj, mk, mv, qg)


def _route(logits):
    lane = lax.broadcasted_iota(jnp.int32, logits.shape, 1)
    lanef = lane.astype(F32)
    is_g = lane < N_GROUPS
    neg = -jnp.inf
    big = 1e9
    gl = jnp.where(is_g, logits, neg)
    gmax = jnp.max(gl, axis=-1, keepdims=True)
    gidx = jnp.min(jnp.where(is_g & (logits == gmax), lanef, big), axis=-1, keepdims=True)
    pg_top = 1.0 / jnp.sum(jnp.where(is_g, jnp.exp(logits - gmax), 0.0), axis=-1, keepdims=True)
    lo = N_GROUPS + gidx * EPG
    in_g = (lanef >= lo) & (lanef < lo + EPG)
    m1 = jnp.max(jnp.where(in_g, logits, neg), axis=-1, keepdims=True)
    i1 = jnp.min(jnp.where(in_g & (logits == m1), lanef, big), axis=-1, keepdims=True)
    rest = in_g & (lanef != i1)
    m2 = jnp.max(jnp.where(rest, logits, neg), axis=-1, keepdims=True)
    i2 = jnp.min(jnp.where(rest & (logits == m2), lanef, big), axis=-1, keepdims=True)
    r = jnp.exp(m2 - m1)
    w1 = pg_top / (1.0 + r)
    w2 = pg_top * r / (1.0 + r)
    comb = jnp.where(lanef == i1, w1, 0.0) + jnp.where(lanef == i2, w2, 0.0)
    return jnp.where(lane == 0, gidx, comb)


def _back_kernel(x_ref, gt_ref, ys_ref, ya_ref, ym_ref, ws_ref, wa_ref, wm_ref, wo_ref, g2_ref,
                 wr_ref, br_ref, x1_ref, h_ref, comb_ref, *, rows_for_dispatch):
    gt = gt_ref[...].astype(F32)
    merged = (_sigmoid(gt[:, 0:D]) * _dot(ys_ref[...].astype(BF16), ws_ref[...])
              + _sigmoid(gt[:, D:2 * D]) * _dot(ya_ref[...], wa_ref[...])
              + _sigmoid(gt[:, 2 * D:3 * D]) * _dot(ym_ref[...], wm_ref[...]))
    x1 = x_ref[...] + _dot(merged.astype(BF16), wo_ref[...])
    x1_ref[...] = x1
    ms = jnp.mean(x1 * x1, axis=-1, keepdims=True)
    h = x1 * lax.rsqrt(ms + EPS) * g2_ref[...]
    logits = _dot_exact(h, wr_ref[...]) + br_ref[...]
    comb = _route(logits)
    comb_ref[...] = comb
    if rows_for_dispatch:
        h_ref[:, 0:D] = h
        h_ref[:, D:D + ROUTE_W] = comb
    else:
        h_ref[...] = h.astype(BF16)


def _back(x, proj, y_ssm, y_swa, y_mem, ws, wa, wm, wo, g2, wr, br, tm, rows_for_dispatch):
    t = x.shape[0]
    const = lambda i: (0, 0)
    hw = D + ROUTE_W if rows_for_dispatch else D
    return pl.pallas_call(
        functools.partial(_back_kernel, rows_for_dispatch=rows_for_dispatch),
        grid=(t // tm,),
        in_specs=[
            pl.BlockSpec((tm, D), lambda i: (i, 0)),
            pl.BlockSpec((tm, 3 * D), lambda i: (i, 0)),
            pl.BlockSpec((tm, SSM_INNER), lambda i: (i, 0)),
            pl.BlockSpec((tm, D), lambda i: (i, 0)),
            pl.BlockSpec((tm, D), lambda i: (i, 0)),
            pl.BlockSpec((SSM_INNER, D), const),
            pl.BlockSpec((D, D), const),
            pl.BlockSpec((D, D), const),
            pl.BlockSpec((D, D), const),
            pl.BlockSpec((1, D), const),
            pl.BlockSpec((D, ROUTE_W), const),
            pl.BlockSpec((1, ROUTE_W), const),
        ],
        out_specs=[
            pl.BlockSpec((tm, D), lambda i: (i, 0)),
            pl.BlockSpec((tm, hw), lambda i: (i, 0)),
            pl.BlockSpec((tm, ROUTE_W), lambda i: (i, 0)),
        ],
        out_shape=[
            jax.ShapeDtypeStruct((t, D), F32),
            jax.ShapeDtypeStruct((t, hw), F32 if rows_for_dispatch else BF16),
            jax.ShapeDtypeStruct((t, ROUTE_W), F32),
        ],
        compiler_params=_cparams(("parallel",)),
        name="back",
    )(x, proj, y_ssm, y_swa, y_mem, ws, wa, wm, wo, g2, wr, br)


MOE_TR = 512
RANK_TM = 1024
ROW_W = D + ROUTE_W


def _rank_kernel(comb_ref, pos_ref, cnt_ref, carry_ref, offs_ref):
    ph = pl.program_id(0)
    i = pl.program_id(1)
    tm = comb_ref.shape[0]
    lane = lax.broadcasted_iota(jnp.int32, (tm, LANE), 1)
    gid = jnp.sum(jnp.where(lane == 0, comb_ref[...], 0.0), axis=-1, keepdims=True)
    onehot = (lane.astype(F32) == gid).astype(F32)
    colsum = jnp.sum(onehot, axis=0, keepdims=True)

    @pl.when((ph == 0) & (i == 0))
    def _():
        cnt_ref[...] = jnp.zeros_like(cnt_ref)

    @pl.when(ph == 0)
    def _():
        cnt_ref[...] += colsum

    @pl.when((ph == 1) & (i == 0))
    def _():
        padded = jnp.ceil(cnt_ref[...] * (1.0 / MOE_TR)) * MOE_TR
        r = lax.broadcasted_iota(jnp.int32, (LANE, LANE), 0)
        c = lax.broadcasted_iota(jnp.int32, (LANE, LANE), 1)
        offs_ref[...] = _dot_exact(padded, (r < c).astype(F32))
        carry_ref[...] = jnp.zeros_like(carry_ref)

    @pl.when(ph == 1)
    def _():
        rr = lax.broadcasted_iota(jnp.int32, (tm, tm), 0)
        cc = lax.broadcasted_iota(jnp.int32, (tm, tm), 1)
        before = _dot((cc < rr).astype(BF16), onehot.astype(BF16))
        slot = onehot * (offs_ref[...] + carry_ref[...] + before)
        pos = lax.dot_general(jnp.ones((8, LANE), F32), slot, (((1,), (1,)), ((), ())),
                              preferred_element_type=F32, precision=HIGHEST)
        pos_ref[0] = pos.astype(jnp.int32)
        carry_ref[...] += colsum


def _rank(comb):
    t = comb.shape[0]
    tm = RANK_TM
    nt = t // tm
    return pl.pallas_call(
        _rank_kernel,
        grid=(2, nt),
        in_specs=[pl.BlockSpec((tm, ROUTE_W), lambda p, i: (i, 0))],
        out_specs=[
            pl.BlockSpec((1, 8, tm), lambda p, i: (i * p, 0, 0)),
            pl.BlockSpec((1, LANE), lambda p, i: (0, 0)),
        ],
        out_shape=[
            jax.ShapeDtypeStruct((nt, 8, tm), jnp.int32),
            jax.ShapeDtypeStruct((1, LANE), F32),
        ],
        scratch_shapes=[pltpu.VMEM((1, LANE), F32), pltpu.VMEM((1, LANE), F32)],
        compiler_params=_cparams(("arbitrary", "arbitrary")),
        name="moe_rank",
    )(comb)


def _row_copy(src_hbm, dst_hbm, src_row, dst_row, sem):
    return pltpu.make_async_copy(src_hbm.at[pl.ds(src_row, 1)], dst_hbm.at[pl.ds(dst_row, 1)], sem)


MOE_OBUFS = 3


def _gmoe_kernel(tg_ref, nt_ref, src_ref, nxt_ref, prv_ref, fin_ref, rows_hbm, wg_ref, wu_ref, wd_ref, mo_in,
                 mo_hbm, xbuf, obuf, gsem, ssem):
    del mo_in
    i = pl.program_id(0)
    n_used = nt_ref[0]
    slot = i % 2
    oslot = i % MOE_OBUFS
    pslot = (i + MOE_OBUFS - 1) % MOE_OBUFS
    tr = MOE_TR

    def start_gather(idx_ref, s):
        for r in range(tr):
            _row_copy(rows_hbm, xbuf.at[s], idx_ref[0, r], r, gsem.at[s]).start()

    def wait_gather(s):
        pltpu.make_async_copy(rows_hbm.at[pl.ds(0, tr)], xbuf.at[s], gsem.at[s]).wait()

    def start_scatter(idx_ref, s):
        for r in range(tr):
            _row_copy(obuf.at[s], mo_hbm, r, idx_ref[0, r], ssem.at[s]).start()

    def wait_scatter(s):
        pltpu.make_async_copy(obuf.at[s], mo_hbm.at[pl.ds(0, tr)], ssem.at[s]).wait()

    @pl.when((i == 0) & (n_used > 0))
    def _():
        start_gather(src_ref, 0)
        obuf[MOE_OBUFS - 1] = jnp.zeros((tr, D), F32)

    @pl.when(i < n_used)
    def _():
        wait_gather(slot)

        @pl.when(i >= 2)
        def _():
            wait_scatter(oslot)

        start_gather(nxt_ref, 1 - slot)
        start_scatter(prv_ref, pslot)
        x = xbuf[slot]
        h = x[:, 0:D].astype(BF16)
        comb = x[:, D:D + ROUTE_W]
        lane = lax.broadcasted_iota(jnp.int32, comb.shape, 1)
        first = N_GROUPS + tg_ref[i] * EPG
        for e in range(EPG):
            cw = jnp.sum(jnp.where(lane == first + e, comb, 0.0), axis=-1, keepdims=True)
            act = _silu(_dot(h, wg_ref[e])) * _dot(h, wu_ref[e])
            part = _dot((act * cw).astype(BF16), wd_ref[e])
            if e == 0:
                obuf[oslot] = part
            else:
                obuf[oslot] += part

    @pl.when((i == pl.num_programs(0) - 1) & (n_used > 0))
    def _():
        last = (n_used + MOE_OBUFS - 1) % MOE_OBUFS
        start_scatter(fin_ref, last)
        wait_gather(n_used % 2)
        wait_scatter(last)
        wait_scatter((n_used + MOE_OBUFS - 2) % MOE_OBUFS)

        @pl.when(n_used >= 2)
        def _():
            wait_scatter(n_used % MOE_OBUFS)


def _gmoe(tile_group, n_used, src3, dst3, rows, wg, wu, wd, mo_init):
    n_tiles = src3.shape[0]
    tr = MOE_TR
    idx = lambda f: pl.BlockSpec((None, 1, tr), f, memory_space=pltpu.SMEM)
    grid_spec = pltpu.PrefetchScalarGridSpec(
        num_scalar_prefetch=2,
        grid=(n_tiles,),
        in_specs=[
            idx(lambda i, tg, nt: (i, 0, 0)),
            idx(lambda i, tg, nt: (jnp.minimum(i + 1, n_tiles - 1), 0, 0)),
            idx(lambda i, tg, nt: (i, 0, 0)),
            idx(lambda i, tg, nt: (nt[0], 0, 0)),
            pl.BlockSpec(memory_space=pl.ANY),
            pl.BlockSpec((None, EPG, D, D_FF), lambda i, tg, nt: (tg[i], 0, 0, 0)),
            pl.BlockSpec((None, EPG, D, D_FF), lambda i, tg, nt: (tg[i], 0, 0, 0)),
            pl.BlockSpec((None, EPG, D_FF, D), lambda i, tg, nt: (tg[i], 0, 0, 0)),
            pl.BlockSpec(memory_space=pl.ANY),
        ],
        out_specs=pl.BlockSpec(memory_space=pl.ANY),
        scratch_shapes=[
            pltpu.VMEM((2, tr, ROW_W), F32),
            pltpu.VMEM((MOE_OBUFS, tr, D), F32),
            pltpu.SemaphoreType.DMA((2,)),
            pltpu.SemaphoreType.DMA((MOE_OBUFS,)),
        ],
    )
    return pl.pallas_call(
        _gmoe_kernel,
        grid_spec=grid_spec,
        out_shape=jax.ShapeDtypeStruct(mo_init.shape, F32),
        input_output_aliases={10: 0},
        compiler_params=_cparams(("arbitrary",)),
        name="moe_grouped",
    )(tile_group, n_used, src3, src3, dst3, dst3, rows, wg, wu, wd, mo_init)


def _residual_kernel(x1_ref, mo_ref, y_ref):
    y_ref[...] = x1_ref[...] + mo_ref[...]


def _residual(x1, mo, tm):
    t = x1.shape[0]
    return pl.pallas_call(
        _residual_kernel,
        grid=(t // tm,),
        in_specs=[pl.BlockSpec((tm, D), lambda i: (i, 0)), pl.BlockSpec((tm, D), lambda i: (i, 0))],
        out_specs=pl.BlockSpec((tm, D), lambda i: (i, 0)),
        out_shape=jax.ShapeDtypeStruct((t, D), F32),
        compiler_params=_cparams(("parallel",)),
        name="moe_residual",
    )(x1, mo)


def _moe_routed(x1, rows, comb, wg, wu, wd):
    t = x1.shape[0]
    tr = MOE_TR
    n_slots = t + N_GROUPS * tr
    n_tiles = n_slots // tr
    pos3, cnt = _rank(comb)
    pos = pos3[:, 0, :].reshape(t)
    token = jnp.arange(t, dtype=jnp.int32)
    padded = (jnp.ceil(cnt[0, :N_GROUPS] / tr) * tr).astype(jnp.int32)
    ends = jnp.cumsum(padded)
    tile_start = jnp.arange(n_tiles, dtype=jnp.int32) * tr
    tile_group = jnp.minimum(jnp.sum(tile_start[:, None] >= ends[None, :], axis=1), N_GROUPS - 1).astype(jnp.int32)
    n_used = (ends[-1:] // tr).astype(jnp.int32)
    row_in_tile = jnp.arange(tr, dtype=jnp.int32)
    spare = t + (1 + tile_group)[:, None] * tr + row_in_tile[None, :]
    src = jnp.zeros((n_slots,), jnp.int32).at[pos].set(token, unique_indices=True)
    dst = spare.reshape(n_slots).at[pos].set(token, unique_indices=True)
    dst3 = jnp.concatenate([t + row_in_tile, dst]).reshape(n_tiles + 1, 1, tr)
    mo_init = jnp.zeros((t + (1 + N_GROUPS) * tr, D), F32)
    mo = _gmoe(tile_group, n_used, src.reshape(n_tiles, 1, tr), dst3, rows,
               wg.reshape(N_GROUPS, EPG, D, D_FF), wu.reshape(N_GROUPS, EPG, D, D_FF),
               wd.reshape(N_GROUPS, EPG, D_FF, D), mo_init)
    return _residual(x1, mo, _pick(t, (1024, 512, 256, 128)))


def _moe_kernel(x1_ref, h_ref, comb_ref, wg_ref, wu_ref, wd_ref, o_ref):
    e = pl.program_id(1)

    @pl.when(e == 0)
    def _():
        o_ref[...] = x1_ref[...]

    h = h_ref[...]
    lane = lax.broadcasted_iota(jnp.int32, comb_ref.shape, 1)
    cw = jnp.sum(jnp.where(lane == e + N_GROUPS, comb_ref[...], 0.0), axis=-1, keepdims=True)
    act = _silu(_dot(h, wg_ref[0])) * _dot(h, wu_ref[0])
    o_ref[...] += _dot((act * cw).astype(BF16), wd_ref[0])


def _moe(x1, h, comb, wg, wu, wd, tm):
    t = x1.shape[0]
    return pl.pallas_call(
        _moe_kernel,
        grid=(t // tm, N_EXPERTS),
        in_specs=[
            pl.BlockSpec((tm, D), lambda i, e: (i, 0)),
            pl.BlockSpec((tm, D), lambda i, e: (i, 0)),
            pl.BlockSpec((tm, ROUTE_W), lambda i, e: (i, 0)),
            pl.BlockSpec((1, D, D_FF), lambda i, e: (e, 0, 0)),
            pl.BlockSpec((1, D, D_FF), lambda i, e: (e, 0, 0)),
            pl.BlockSpec((1, D_FF, D), lambda i, e: (e, 0, 0)),
        ],
        out_specs=pl.BlockSpec((tm, D), lambda i, e: (i, 0)),
        out_shape=jax.ShapeDtypeStruct((t, D), F32),
        compiler_params=_cparams(("parallel", "arbitrary")),
        name="moe",
    )(x1, h, comb, wg, wu, wd)


def _sample_prep_kernel(xbc_ref, q_ref, k_ref, mq_ref, dt_ref, sc_ref, cw_ref, cb_ref, dtb_ref, a_ref, e_ref,
                        qg_ref, kg_ref, mqg_ref, cos_ref, slo_ref, shi_ref,
                        conv_ref, xs_ref, bc_ref, dtx_ref, dec_ref, qn_ref, kn_ref, mqn_ref):
    n = xbc_ref.shape[0]
    x_raw = xbc_ref[...].astype(F32)
    acc = x_raw * cw_ref[SSM_CONV - 1:SSM_CONV, :] + cb_ref[...]
    for j in range(SSM_CONV - 1):
        acc = acc + sc_ref[:, j * CONV_CH:(j + 1) * CONV_CH] * cw_ref[j:j + 1, :]
    conv_ref[:, 0:(SSM_CONV - 2) * CONV_CH] = sc_ref[:, CONV_CH:(SSM_CONV - 1) * CONV_CH]
    conv_ref[:, (SSM_CONV - 2) * CONV_CH:] = x_raw
    act = _silu(acc)
    xs = act[:, 0:SSM_INNER]
    xs_ref[...] = xs
    bc_ref[...] = act[:, SSM_INNER:]
    dt = _softplus(dt_ref[...] + dtb_ref[...])
    e = e_ref[...]
    dtx = _dot_exact(dt, e) * xs
    dec = _dot_exact(jnp.exp(dt * a_ref[...]), e)
    dtx_ref[...] = dtx.T
    dec_ref[...] = dec.T

    lane = lax.broadcasted_iota(jnp.int32, (n, LANE), 1)
    low_half = lane < ATT_HD
    cos_t = jnp.broadcast_to(cos_ref[...], (n, LANE))
    sin_lo = jnp.broadcast_to(slo_ref[...], (n, LANE))
    sin_hi = jnp.broadcast_to(shi_ref[...], (n, LANE))
    qf = q_ref[...].astype(F32)
    for j in range(D // LANE):
        sl = slice(j * LANE, (j + 1) * LANE)
        qn_ref[:, sl] = _norm_rope_tile(qf[:, sl], qg_ref[...], cos_t, sin_lo, sin_hi, low_half)
    kf = k_ref[...].astype(F32)
    for j in range(ATT_KVD // LANE):
        sl = slice(j * LANE, (j + 1) * LANE)
        kn_ref[:, sl] = _norm_rope_tile(kf[:, sl], kg_ref[...], cos_t, sin_lo, sin_hi, low_half)
    mq = mq_ref[...].astype(F32)
    for h in range(MEM_HEADS):
        sl = slice(h * MEM_HD, (h + 1) * MEM_HD)
        mh = mq[:, sl]
        ms = jnp.mean(mh * mh, axis=-1, keepdims=True)
        mqn_ref[:, sl] = mh * lax.rsqrt(ms + EPS) * mqg_ref[...]


def _sample_prep(proj, dt_raw, sc2d, cw, cb, dtb, a, e, qg2, kg2, mqg, cos_t, sin_lo, sin_hi):
    n = proj.shape[0]
    c2 = lambda i: (0, 0)
    full = lambda shape: pl.BlockSpec(shape, c2)
    return pl.pallas_call(
        _sample_prep_kernel,
        grid=(1,),
        in_specs=[
            pl.BlockSpec((n, CONV_CH), lambda i: (0, COL_XBC_BLK)),
            pl.BlockSpec((n, D), lambda i: (0, COL_Q_BLK)),
            pl.BlockSpec((n, ATT_KVD), lambda i: (0, COL_K_BLK)),
            pl.BlockSpec((n, D), lambda i: (0, COL_MQ_BLK)),
            full((n, LANE)),
            full((n, (SSM_CONV - 1) * CONV_CH)),
            full((SSM_CONV, CONV_CH)),
            full((1, CONV_CH)),
            full((1, LANE)),
            full((1, LANE)),
            full((LANE, SSM_INNER)),
            full((1, LANE)),
            full((1, LANE)),
            full((1, MEM_HD)),
            full((1, LANE)),
            full((1, LANE)),
            full((1, LANE)),
        ],
        out_specs=[
            full((n, (SSM_CONV - 1) * CONV_CH)),
            full((n, SSM_INNER)),
            full((n, 2 * SSM_GROUPS * SSM_STATE)),
            full((SSM_INNER, n)),
            full((SSM_INNER, n)),
            full((n, D)),
            full((n, ATT_KVD)),
            full((n, D)),
        ],
        out_shape=[
            jax.ShapeDtypeStruct((n, (SSM_CONV - 1) * CONV_CH), F32),
            jax.ShapeDtypeStruct((n, SSM_INNER), F32),
            jax.ShapeDtypeStruct((n, 2 * SSM_GROUPS * SSM_STATE), F32),
            jax.ShapeDtypeStruct((SSM_INNER, n), F32),
            jax.ShapeDtypeStruct((SSM_INNER, n), F32),
            jax.ShapeDtypeStruct((n, D), F32),
            jax.ShapeDtypeStruct((n, ATT_KVD), F32),
            jax.ShapeDtypeStruct((n, D), F32),
        ],
        compiler_params=_cparams(("arbitrary",)),
        name="sample_prep",
    )(proj, proj, proj, proj, dt_raw, sc2d, cw, cb, dtb, a, e, qg2, kg2, mqg, cos_t, sin_lo, sin_hi)


SSM_S_TILE = 8


def _sample_ssm_kernel(st_ref, dtx_ref, dec_ref, bc_ref, so_ref, yt_ref):
    i = pl.program_id(0)
    n = dtx_ref.shape[1]

    @pl.when(i == 0)
    def _():
        yt_ref[...] = jnp.zeros_like(yt_ref)

    gw = SSM_HPG * SSM_HEADDIM
    lane = lax.broadcasted_iota(jnp.int32, (gw, n), 1)
    nbc = SSM_GROUPS * SSM_STATE
    for s in range(SSM_S_TILE):
        sel = lane == (i * SSM_S_TILE + s)
        for g in range(SSM_GROUPS):
            rows = slice(g * gw, (g + 1) * gw)
            dtx_c = jnp.sum(jnp.where(sel, dtx_ref[rows, :], 0.0), axis=1, keepdims=True)
            dec_c = jnp.sum(jnp.where(sel, dec_ref[rows, :], 0.0), axis=1, keepdims=True)
            bm = bc_ref[s:s + 1, g * SSM_STATE:(g + 1) * SSM_STATE]
            cm = bc_ref[s:s + 1, nbc + g * SSM_STATE:nbc + (g + 1) * SSM_STATE]
            hn = st_ref[s, rows, :] * dec_c + dtx_c * bm
            so_ref[s, rows, :] = hn
            yc = jnp.sum(hn * cm, axis=1, keepdims=True)
            yt_ref[rows, :] = jnp.where(sel, yc, yt_ref[rows, :])


def _sample_ssm(state, dtx_t, dec_t, bc):
    n = state.shape[0]
    return pl.pallas_call(
        _sample_ssm_kernel,
        grid=(n // SSM_S_TILE,),
        in_specs=[
            pl.BlockSpec((SSM_S_TILE, SSM_INNER, SSM_STATE), lambda i: (i, 0, 0)),
            pl.BlockSpec((SSM_INNER, n), lambda i: (0, 0)),
            pl.BlockSpec((SSM_INNER, n), lambda i: (0, 0)),
            pl.BlockSpec((SSM_S_TILE, 2 * SSM_GROUPS * SSM_STATE), lambda i: (i, 0)),
        ],
        out_specs=[
            pl.BlockSpec((SSM_S_TILE, SSM_INNER, SSM_STATE), lambda i: (i, 0, 0)),
            pl.BlockSpec((SSM_INNER, n), lambda i: (0, 0)),
        ],
        out_shape=[
            jax.ShapeDtypeStruct((n, SSM_INNER, SSM_STATE), F32),
            jax.ShapeDtypeStruct((SSM_INNER, n), F32),
        ],
        compiler_params=_cparams(("arbitrary",)),
        name="sample_ssm",
    )(state, dtx_t, dec_t, bc)


def _sample_post_kernel(yt_ref, xs_ref, z_ref, dexp_ref, ng_ref, y_ref):
    y = yt_ref[...].T + dexp_ref[...] * xs_ref[...]
    yg = y * _silu(z_ref[...].astype(F32))
    ms = jnp.mean(yg * yg, axis=-1, keepdims=True)
    y_ref[...] = yg * lax.rsqrt(ms + EPS) * ng_ref[...]


def _sample_post(y_t, xs, proj, dexp, ng):
    n = xs.shape[0]
    c2 = lambda i: (0, 0)
    return pl.pallas_call(
        _sample_post_kernel,
        grid=(1,),
        in_specs=[
            pl.BlockSpec((SSM_INNER, n), c2),
            pl.BlockSpec((n, SSM_INNER), c2),
            pl.BlockSpec((n, SSM_INNER), lambda i: (0, COL_Z_BLK)),
            pl.BlockSpec((1, SSM_INNER), c2),
            pl.BlockSpec((1, SSM_INNER), c2),
        ],
        out_specs=pl.BlockSpec((n, SSM_INNER), c2),
        out_shape=jax.ShapeDtypeStruct((n, SSM_INNER), F32),
        compiler_params=_cparams(("arbitrary",)),
        name="sample_post",
    )(y_t, xs, proj, dexp, ng)


SWA_S_TILE = 8


def _sample_swa_kernel(q_ref, kn_ref, v_ref, ck_ref, cv_ref, sink_ref, y_ref, ko_ref, vo_ref):
    w = WINDOW
    scale = ATT_HD ** -0.5
    newest = lax.broadcasted_iota(jnp.int32, (ATT_HD, w), 1) == w - 1
    for s in range(SWA_S_TILE):
        for kv in range(ATT_KV):
            kt = jnp.where(newest, kn_ref[s, kv], pltpu.roll(ck_ref[s, kv], w - 1, axis=1))
            vt = jnp.where(newest, v_ref[s, kv], pltpu.roll(cv_ref[s, kv], w - 1, axis=1))
            ko_ref[s, kv] = kt
            vo_ref[s, kv] = vt
            sc = _dot(q_ref[s, kv].astype(BF16), kt.astype(BF16)) * scale
            snk = sink_ref[kv]
            m = jnp.maximum(jnp.max(sc, axis=-1, keepdims=True), snk)
            p = jnp.exp(sc - m)
            den = jnp.sum(p, axis=-1, keepdims=True) + jnp.exp(snk - m)
            y_ref[s, kv] = _dot_nt(p.astype(BF16), vt.astype(BF16)) / den


def _sample_swa(q4, kn4, v4, ck_t, cv_t, sink3, l):
    n = q4.shape[0]
    w = WINDOW
    st = SWA_S_TILE
    grp = ATT_HEADS // ATT_KV
    cache = pl.BlockSpec((None, st, ATT_KV, ATT_HD, w), lambda i: (l, i, 0, 0, 0))
    new = pl.BlockSpec((st, ATT_KV, ATT_HD, 1), lambda i: (i, 0, 0, 0))
    out = pl.BlockSpec((st, ATT_KV, ATT_HD, w), lambda i: (i, 0, 0, 0))
    return pl.pallas_call(
        _sample_swa_kernel,
        grid=(n // st,),
        in_specs=[
            pl.BlockSpec((st, ATT_KV, grp, ATT_HD), lambda i: (i, 0, 0, 0)),
            new, new, cache, cache,
            pl.BlockSpec((ATT_KV, grp, 1), lambda i: (0, 0, 0)),
        ],
        out_specs=[pl.BlockSpec((st, ATT_KV, grp, ATT_HD), lambda i: (i, 0, 0, 0)), out, out],
        out_shape=[
            jax.ShapeDtypeStruct((n, ATT_KV, grp, ATT_HD), F32),
            jax.ShapeDtypeStruct((n, ATT_KV, ATT_HD, w), F32),
            jax.ShapeDtypeStruct((n, ATT_KV, ATT_HD, w), F32),
        ],
        compiler_params=_cparams(("parallel",)),
        name="sample_swa",
    )(q4, kn4, v4, ck_t, cv_t, sink3)


MEM_S_TILE = 4


def _sample_mem_kernel(q_ref, k_ref, v_ref, y_ref):
    scale = MEM_HD ** -0.5
    for s in range(MEM_S_TILE):
        sc = jnp.sum(k_ref[s] * q_ref[s][None], axis=-1, keepdims=True) * scale
        m = jnp.max(sc, axis=0, keepdims=True)
        p = jnp.exp(sc - m)
        den = jnp.sum(p, axis=0, keepdims=True)
        o = jnp.sum(p * v_ref[s], axis=0, keepdims=True) / den
        y_ref[s] = o[0]


def _sample_mem(q3, ck, cv, l):
    n = q3.shape[0]
    mem_len = ck.shape[2]
    st = MEM_S_TILE
    cache = pl.BlockSpec((None, st, mem_len, MEM_HEADS, MEM_HD), lambda i: (l, i, 0, 0, 0))
    return pl.pallas_call(
        _sample_mem_kernel,
        grid=(n // st,),
        in_specs=[pl.BlockSpec((st, MEM_HEADS, MEM_HD), lambda i: (i, 0, 0)), cache, cache],
        out_specs=pl.BlockSpec((st, MEM_HEADS, MEM_HD), lambda i: (i, 0, 0)),
        out_shape=jax.ShapeDtypeStruct((n, MEM_HEADS, MEM_HD), F32),
        compiler_params=_cparams(("parallel",)),
        name="sample_mem",
    )(q3, ck, cv)


def _prep_weights(lw):
    w_in = lw['w_in']
    cols = [w_in[:, 0:OFF_Z], w_in[:, OFF_XBC:OFF_DT], w_in[:, OFF_Z:OFF_XBC], w_in[:, OFF_Q:OFF_K],
            w_in[:, OFF_MQ:OFF_MQ + D], w_in[:, OFF_K:OFF_V], w_in[:, OFF_V:OFF_MQ]]
    p = {}
    p['w_main'] = jnp.concatenate(cols, axis=1).astype(BF16)
    p['w_dt'] = jnp.pad(w_in[:, OFF_DT:OFF_Q], ((0, 0), (0, LANE - SSM_HEADS))).astype(BF16)
    p['norm1_g'] = lw['norm1_g'].reshape(1, D)
    p['cw'] = lw['ssm_conv_w']
    p['cb'] = lw['ssm_conv_b'].reshape(1, CONV_CH)
    pad_h = (0, LANE - SSM_HEADS)
    p['dtb'] = jnp.pad(lw['ssm_dt_bias'].astype(F32), pad_h).reshape(1, LANE)
    p['a'] = jnp.pad(-jnp.exp(lw['ssm_a_log'].astype(F32)), pad_h).reshape(1, LANE)
    p['dexp'] = jnp.repeat(lw['ssm_d'], SSM_HEADDIM).reshape(1, SSM_INNER)
    p['ssm_ng'] = lw['ssm_norm_g'].reshape(1, SSM_INNER)
    head_of_ch = jnp.arange(SSM_INNER) // SSM_HEADDIM
    p['e'] = (jnp.arange(LANE)[:, None] == head_of_ch[None, :]).astype(F32)
    p['qg2'] = jnp.tile(lw['att_q_norm_g'], 2).reshape(1, LANE)
    p['kg2'] = jnp.tile(lw['att_k_norm_g'], 2).reshape(1, LANE)
    p['sink_row'] = lw['att_sink'].astype(F32).reshape(1, ATT_HEADS)
    p['sink3'] = lw['att_sink'].astype(F32).reshape(ATT_KV, ATT_HEADS // ATT_KV, 1)
    p['mem_g'] = lw['mem_norm_g'].reshape(1, D)
    p['w_mem_kv'] = lw['w_mem_kv'].astype(BF16)
    p['mqg'] = lw['mem_q_norm_g'].reshape(1, MEM_HD)
    p['mkg'] = lw['mem_k_norm_g'].reshape(1, MEM_HD)
    p['ws'] = lw['w_br_ssm'].astype(BF16)
    p['wa'] = lw['w_br_swa'].astype(BF16)
    p['wm'] = lw['w_br_mem'].astype(BF16)
    p['wo'] = lw['w_out'].astype(BF16)
    p['g2'] = lw['norm2_g'].reshape(1, D)
    pad_r = ROUTE_W - N_GROUPS - N_EXPERTS
    p['wr'] = jnp.pad(jnp.concatenate([lw['w_router_group'], lw['w_router_expert']], axis=1).astype(F32),
                      ((0, 0), (0, pad_r)))
    p['br'] = jnp.pad(jnp.concatenate([lw['b_router_group'], lw['b_router_expert']]).astype(F32),
                      (0, pad_r)).reshape(1, ROUTE_W)
    p['wg'] = lw['w_exp_gate'].astype(BF16)
    p['wu'] = lw['w_exp_up'].astype(BF16)
    p['wd'] = lw['w_exp_down'].astype(BF16)
    return p


def _pick(n, prefs):
    for c in prefs:
        if n % c == 0:
            return c
    return n


def _tail(x, proj, y_ssm, y_swa, y_mem, p):
    t = x.shape[0]
    routed = t % RANK_TM == 0
    x1, h, comb = _back(x, proj, y_ssm, y_swa, y_mem, p['ws'], p['wa'], p['wm'], p['wo'], p['g2'],
                        p['wr'], p['br'], _pick(t, (256, 128)), routed)
    if routed:
        return _moe_routed(x1, h, comb, p['wg'], p['wu'], p['wd'])
    return _moe(x1, h, comb, p['wg'], p['wu'], p['wd'], _pick(t, (1024, 512, 256, 128)))


def _prompt_layer(x, mem, p):
    nb, seq, _ = x.shape
    t = nb * seq
    nc = seq // CHUNK
    xf = x.reshape(t, D)
    proj, dt_raw = _front(xf, p['norm1_g'], p['w_main'], p['w_dt'], _pick(t, (1024, 512, 256, 128)), 1536)
    y_ssm, conv_new, ssm_new = _ssd(proj, dt_raw, p['cw'], p['cb'], p['dtb'], p['a'], p['dexp'],
                                    p['ssm_ng'], p['e'], nb, nc)
    cos_t, sin_lo, sin_hi = _rope_tables(jnp.arange(seq))
    y_swa, k_new, v_new = _swa(proj, p['qg2'], p['kg2'], cos_t, sin_lo, sin_hi, p['sink_row'], nb, nc)
    mem_len = mem.shape[1]
    mk, mv = _memkv(mem.reshape(nb * mem_len, D), p['mem_g'], p['w_mem_kv'], p['mkg'],
                    _pick(nb * mem_len, (512, 256, 128)))
    y_mem = _memattn(proj, mk, mv, p['mqg'], nb, seq, mem_len, _pick(seq, (512, 256, 128)))
    y = _tail(xf, proj, y_ssm, y_swa, y_mem, p)
    return (y.reshape(nb, seq, D), conv_new,
            ssm_new.reshape(nb, SSM_HEADS, SSM_HEADDIM, SSM_STATE),
            k_new.reshape(nb, WINDOW, ATT_KV, ATT_HD), v_new.reshape(nb, WINDOW, ATT_KV, ATT_HD),
            mk.reshape(nb, mem_len, MEM_HEADS, MEM_HD), mv.reshape(nb, mem_len, MEM_HEADS, MEM_HD))


def _sample_layer(x, conv_st, ssm_st, swa_k, swa_v, mem_k, mem_v, l, p):
    n = x.shape[0]
    xf = x.reshape(n, D)
    proj, dt_raw = _front(xf, p['norm1_g'], p['w_main'], p['w_dt'], n, 1536)
    cos_t, sin_lo, sin_hi = _rope_tables(jnp.full((1,), PAST_LEN, jnp.int32))
    sc2d = conv_st.reshape(n, (SSM_CONV - 1) * CONV_CH)
    conv_new, xs, bc, dtx_t, dec_t, qn, kn, mqn = _sample_prep(
        proj, dt_raw, sc2d, p['cw'], p['cb'], p['dtb'], p['a'], p['e'], p['qg2'], p['kg2'], p['mqg'],
        cos_t, sin_lo, sin_hi)
    ssm_new, y_t = _sample_ssm(ssm_st.reshape(n, SSM_INNER, SSM_STATE), dtx_t, dec_t, bc)
    y_ssm = _sample_post(y_t, xs, proj, p['dexp'], p['ssm_ng'])
    grp = ATT_HEADS // ATT_KV
    v_raw = proj[:, COL_V_BLK * ATT_KVD:(COL_V_BLK + 1) * ATT_KVD].astype(F32)
    to_t = (0, 1, 3, 4, 2)
    y_swa, k_new, v_new = _sample_swa(
        qn.reshape(n, ATT_KV, grp, ATT_HD), kn.reshape(n, ATT_KV, ATT_HD, 1),
        v_raw.reshape(n, ATT_KV, ATT_HD, 1), swa_k.transpose(to_t), swa_v.transpose(to_t), p['sink3'], l)
    y_swa = y_swa.reshape(n, D)
    k_new = k_new.transpose(0, 3, 1, 2)
    v_new = v_new.transpose(0, 3, 1, 2)
    y_mem = _sample_mem(mqn.reshape(n, MEM_HEADS, MEM_HD), mem_k, mem_v, l)
    y = _tail(xf, proj, y_ssm, y_swa.astype(BF16), y_mem.reshape(n, D).astype(BF16), p)
    return (y.reshape(n, 1, D), conv_new.reshape(n, SSM_CONV - 1, CONV_CH),
            ssm_new.reshape(n, SSM_HEADS, SSM_HEADDIM, SSM_STATE), k_new, v_new)


def kernel(x_prompt, x_sample, state_conv, state_ssm, cache_swa_k, cache_swa_v, cache_mem_k, cache_mem_v,
           mem_prompt, norm1_g, w_in, ssm_conv_w, ssm_conv_b, ssm_dt_bias, ssm_a_log, ssm_d, ssm_norm_g,
           att_q_norm_g, att_k_norm_g, att_sink, mem_norm_g, w_mem_kv, mem_q_norm_g, mem_k_norm_g,
           w_br_ssm, w_br_swa, w_br_mem, w_out, norm2_g, w_router_group, b_router_group,
           w_router_expert, b_router_expert, w_exp_gate, w_exp_up, w_exp_down):
    weights = dict(norm1_g=norm1_g, w_in=w_in, ssm_conv_w=ssm_conv_w, ssm_conv_b=ssm_conv_b,
                   ssm_dt_bias=ssm_dt_bias, ssm_a_log=ssm_a_log, ssm_d=ssm_d, ssm_norm_g=ssm_norm_g,
                   att_q_norm_g=att_q_norm_g, att_k_norm_g=att_k_norm_g, att_sink=att_sink,
                   mem_norm_g=mem_norm_g, w_mem_kv=w_mem_kv, mem_q_norm_g=mem_q_norm_g,
                   mem_k_norm_g=mem_k_norm_g, w_br_ssm=w_br_ssm, w_br_swa=w_br_swa, w_br_mem=w_br_mem,
                   w_out=w_out, norm2_g=norm2_g, w_router_group=w_router_group,
                   b_router_group=b_router_group, w_router_expert=w_router_expert,
                   b_router_expert=b_router_expert, w_exp_gate=w_exp_gate, w_exp_up=w_exp_up,
                   w_exp_down=w_exp_down)
    depth = w_in.shape[0]
    xp, xs = x_prompt, x_sample
    outs = [[] for _ in range(10)]
    for l in range(depth):
        p = _prep_weights({k: v[l] for k, v in weights.items()})
        xp, c1, c2, c3, c4, c5, c6 = _prompt_layer(xp, mem_prompt, p)
        xs, d1, d2, d3, d4 = _sample_layer(xs, state_conv[l], state_ssm[l], cache_swa_k, cache_swa_v,
                                           cache_mem_k, cache_mem_v, l, p)
        for lst, val in zip(outs, (c1, c2, c3, c4, c5, c6, d1, d2, d3, d4)):
            lst.append(val)
    return (xp, xs) + tuple(jnp.stack(o) for o in outs)
```

```python
import functools
import math

import jax
import jax.numpy as jnp
from jax import lax
from jax.experimental import pallas as pl
from jax.experimental.pallas import tpu as pltpu

F32 = jnp.float32
BF16 = jnp.bfloat16
HIGHEST = lax.Precision.HIGHEST

D = 1024
SSM_INNER = 2048
SSM_HEADDIM = 64
SSM_HEADS = 32
SSM_GROUPS = 4
SSM_HPG = SSM_HEADS // SSM_GROUPS
SSM_STATE = 128
SSM_CONV = 4
CONV_CH = SSM_INNER + 2 * SSM_GROUPS * SSM_STATE
CHUNK = 128
ATT_HEADS = 16
ATT_KV = 4
ATT_HD = 64
ATT_KVD = ATT_KV * ATT_HD
WINDOW = 128
ROPE_THETA = 10000.0
MEM_HEADS = 4
MEM_HD = 256
N_EXPERTS = 32
N_GROUPS = 4
EPG = 8
D_FF = 256
EPS = 1e-6
PAST_LEN = 16384

OFF_Z = 3 * D
OFF_XBC = OFF_Z + SSM_INNER
OFF_DT = OFF_XBC + CONV_CH
OFF_Q = OFF_DT + SSM_HEADS
OFF_K = OFF_Q + D
OFF_V = OFF_K + ATT_KVD
OFF_MQ = OFF_V + ATT_KVD

N_MAIN = 3 * D + CONV_CH + SSM_INNER + D + D + 2 * ATT_KVD
COL_XBC_BLK = 1
COL_Z_BLK = 3
COL_Q_BLK = 8
COL_MQ_BLK = 9
COL_K_BLK = 40
COL_V_BLK = 41
LANE = 128
ROUTE_W = 128

VMEM_LIMIT = 56 * 1024 * 1024


def _cparams(sem):
    return pltpu.CompilerParams(dimension_semantics=sem, vmem_limit_bytes=VMEM_LIMIT)


def _sigmoid(x):
    return 1.0 / (1.0 + jnp.exp(-x))


def _silu(x):
    return x * _sigmoid(x)


def _softplus(x):
    return jnp.maximum(x, 0.0) + jnp.log1p(jnp.exp(-jnp.abs(x)))


def _dot(a, b):
    return jnp.dot(a, b, preferred_element_type=F32)


def _dot_nt(a, b):
    return lax.dot_general(a, b, (((1,), (1,)), ((), ())), preferred_element_type=F32)


def _dot_tn(a, b):
    return lax.dot_general(a, b, (((0,), (0,)), ((), ())), preferred_element_type=F32)


def _dot_exact(a, b):
    return jnp.dot(a, b, preferred_element_type=F32, precision=HIGHEST)


def _front_kernel(x_ref, g_ref, w_ref, wdt_ref, o_ref, dt_ref, hn_ref):
    @pl.when(pl.program_id(1) == 0)
    def _():
        x = x_ref[...]
        ms = jnp.mean(x * x, axis=-1, keepdims=True)
        hn = (x * lax.rsqrt(ms + EPS) * g_ref[...]).astype(BF16)
        hn_ref[...] = hn
        dt_ref[...] = _dot(hn, wdt_ref[...])

    o_ref[...] = _dot(hn_ref[...], w_ref[...]).astype(BF16)


def _front(x, g, w_main, w_dt, tm, tn):
    t = x.shape[0]
    return pl.pallas_call(
        _front_kernel,
        grid=(t // tm, N_MAIN // tn),
        in_specs=[
            pl.BlockSpec((tm, D), lambda i, j: (i, 0)),
            pl.BlockSpec((1, D), lambda i, j: (0, 0)),
            pl.BlockSpec((D, tn), lambda i, j: (0, j)),
            pl.BlockSpec((D, LANE), lambda i, j: (0, 0)),
        ],
        out_specs=[
            pl.BlockSpec((tm, tn), lambda i, j: (i, j)),
            pl.BlockSpec((tm, LANE), lambda i, j: (i, 0)),
        ],
        out_shape=[
            jax.ShapeDtypeStruct((t, N_MAIN), BF16),
            jax.ShapeDtypeStruct((t, LANE), F32),
        ],
        scratch_shapes=[pltpu.VMEM((tm, D), BF16)],
        compiler_params=_cparams(("parallel", "arbitrary")),
        name="front",
    )(x, g, w_main, w_dt)


SUBLANES = 8


def _ssd_kernel(xbc_ref, z_ref, dt_ref, cw_ref, cb_ref, dtb_ref, a_ref, dexp_ref, ng_ref,
                y_ref, conv_ref, ssm_ref, prev_ref, h_ref, yacc_ref):
    c = pl.program_id(1)

    @pl.when(c == 0)
    def _():
        prev_ref[...] = jnp.zeros_like(prev_ref)
        h_ref[...] = jnp.zeros_like(h_ref)

    x_raw = xbc_ref[...].astype(F32)
    prev = prev_ref[...]
    row8 = lax.broadcasted_iota(jnp.int32, (SUBLANES, CONV_CH), 0)
    acc = x_raw * cw_ref[SSM_CONV - 1:SSM_CONV, :] + cb_ref[...]
    for s in range(1, SSM_CONV):
        xr = pltpu.roll(x_raw, s, axis=0)
        head = jnp.where(row8 < s, pltpu.roll(prev, s, axis=0), xr[0:SUBLANES])
        shifted = jnp.concatenate([head, xr[SUBLANES:]], axis=0)
        acc = acc + shifted * cw_ref[SSM_CONV - 1 - s:SSM_CONV - s, :]
    prev_ref[...] = x_raw[CHUNK - SUBLANES:CHUNK]
    conv_ref[0] = x_raw[CHUNK - (SSM_CONV - 1):CHUNK]
    act = _silu(acc)

    dt = _softplus(dt_ref[...] + dtb_ref[...])
    da = dt * a_ref[...]
    row = lax.broadcasted_iota(jnp.int32, (CHUNK, CHUNK), 0)
    col = lax.broadcasted_iota(jnp.int32, (CHUNK, CHUNK), 1)
    causal = row >= col
    acum = _dot_exact(causal.astype(F32), da)
    acum_t = acum.T
    dt_t = dt.T
    exp_a = jnp.exp(acum)
    last = acum[CHUNK - 1:CHUNK, :]
    w_end = jnp.exp(last - acum) * dt
    cd = jnp.broadcast_to(jnp.exp(acum_t[:, CHUNK - 1:CHUNK]), (LANE, SSM_STATE))
    lane = lax.broadcasted_iota(jnp.int32, (CHUNK, LANE), 1)
    low_half = lane < SSM_HEADDIM

    def pair_cols(per_head, hd):
        return jnp.where(low_half, per_head[:, hd:hd + 1], per_head[:, hd + 1:hd + 2])

    xs_off = 0
    b_off = SSM_INNER
    c_off = SSM_INNER + SSM_GROUPS * SSM_STATE
    for g in range(SSM_GROUPS):
        bm = act[:, b_off + g * SSM_STATE:b_off + (g + 1) * SSM_STATE].astype(BF16)
        cm = act[:, c_off + g * SSM_STATE:c_off + (g + 1) * SSM_STATE].astype(BF16)
        cb = _dot_nt(cm, bm)
        gw = SSM_HPG * SSM_HEADDIM
        ch0 = g * gw
        h_prev = h_ref[ch0:ch0 + gw, :]
        y_off = _dot_nt(cm, h_prev.astype(BF16))
        xg = act[:, xs_off + ch0:xs_off + ch0 + gw]
        xw = []
        for pr in range(SSM_HPG // 2):
            hd0 = g * SSM_HPG + pr * 2
            xp32 = xg[:, pr * LANE:(pr + 1) * LANE]
            xpair = xp32.astype(BF16)
            yd = []
            for sub in range(2):
                hd = hd0 + sub
                seg = acum[:, hd:hd + 1] - acum_t[hd:hd + 1, :]
                decay = jnp.exp(jnp.where(causal, seg, -jnp.inf))
                wts = cb * decay * dt_t[hd:hd + 1, :]
                yd.append(_dot(wts.astype(BF16), xpair))
            cl = ch0 + pr * LANE
            y_pair = (jnp.where(low_half, yd[0], yd[1])
                      + y_off[:, pr * LANE:(pr + 1) * LANE] * pair_cols(exp_a, hd0)
                      + dexp_ref[:, cl:cl + LANE] * xp32)
            yacc_ref[:, cl:cl + LANE] = y_pair
            xw.append((xp32 * pair_cols(w_end, hd0)).astype(BF16))
        states = _dot_tn(jnp.concatenate(xw, axis=1), bm)
        for r in range(SSM_HPG):
            hd = g * SSM_HPG + r
            r0 = ch0 + r * SSM_HEADDIM
            h_ref[r0:r0 + SSM_HEADDIM, :] = (h_ref[r0:r0 + SSM_HEADDIM, :] * cd[hd:hd + 1, :]
                                             + states[r * SSM_HEADDIM:(r + 1) * SSM_HEADDIM, :])

    ssm_ref[0] = h_ref[...]
    yg = yacc_ref[...] * _silu(z_ref[...].astype(F32))
    ms = jnp.mean(yg * yg, axis=-1, keepdims=True)
    y_ref[...] = (yg * lax.rsqrt(ms + EPS) * ng_ref[...]).astype(BF16)


def _ssd(proj, dt_raw, cw, cb, dtb, a, dexp, ng, nb, nc):
    t = proj.shape[0]
    return pl.pallas_call(
        _ssd_kernel,
        grid=(nb, nc),
        in_specs=[
            pl.BlockSpec((CHUNK, CONV_CH), lambda b, c: (b * nc + c, COL_XBC_BLK)),
            pl.BlockSpec((CHUNK, SSM_INNER), lambda b, c: (b * nc + c, COL_Z_BLK)),
            pl.BlockSpec((CHUNK, LANE), lambda b, c: (b * nc + c, 0)),
            pl.BlockSpec((SSM_CONV, CONV_CH), lambda b, c: (0, 0)),
            pl.BlockSpec((1, CONV_CH), lambda b, c: (0, 0)),
            pl.BlockSpec((1, LANE), lambda b, c: (0, 0)),
            pl.BlockSpec((1, LANE), lambda b, c: (0, 0)),
            pl.BlockSpec((1, SSM_INNER), lambda b, c: (0, 0)),
            pl.BlockSpec((1, SSM_INNER), lambda b, c: (0, 0)),
        ],
        out_specs=[
            pl.BlockSpec((CHUNK, SSM_INNER), lambda b, c: (b * nc + c, 0)),
            pl.BlockSpec((1, SSM_CONV - 1, CONV_CH), lambda b, c: (b, 0, 0)),
            pl.BlockSpec((1, SSM_INNER, SSM_STATE), lambda b, c: (b, 0, 0)),
        ],
        out_shape=[
            jax.ShapeDtypeStruct((t, SSM_INNER), BF16),
            jax.ShapeDtypeStruct((nb, SSM_CONV - 1, CONV_CH), F32),
            jax.ShapeDtypeStruct((nb, SSM_INNER, SSM_STATE), F32),
        ],
        scratch_shapes=[
            pltpu.VMEM((SUBLANES, CONV_CH), F32),
            pltpu.VMEM((SSM_INNER, SSM_STATE), F32),
            pltpu.VMEM((CHUNK, SSM_INNER), F32),
        ],
        compiler_params=_cparams(("parallel", "arbitrary")),
        name="ssd_prompt",
    )(proj, proj, dt_raw, cw, cb, dtb, a, dexp, ng)


def _norm_rope_tile(xj, g2, cos_t, sin_lo, sin_hi, low_half):
    sq = xj * xj
    s_lo = jnp.sum(jnp.where(low_half, sq, 0.0), axis=-1, keepdims=True)
    s_hi = jnp.sum(jnp.where(low_half, 0.0, sq), axis=-1, keepdims=True)
    ms = jnp.where(low_half, s_lo, s_hi) * (1.0 / ATT_HD)
    xn = xj * lax.rsqrt(ms + EPS) * g2
    half = ATT_HD // 2
    return (xn * cos_t + pltpu.roll(xn, LANE - half, axis=1) * sin_lo
            + pltpu.roll(xn, half, axis=1) * sin_hi)


def _rope_tables(pos):
    half = ATT_HD // 2
    inv = ROPE_THETA ** (-jnp.arange(half, dtype=F32) / half)
    ang = pos.astype(F32)[:, None] * inv[None, :]
    cos = jnp.cos(ang)
    sin = jnp.sin(ang)
    zero = jnp.zeros_like(sin)
    cos_t = jnp.concatenate([cos, cos, cos, cos], axis=1)
    sin_lo = jnp.concatenate([-sin, zero, -sin, zero], axis=1)
    sin_hi = jnp.concatenate([zero, sin, zero, sin], axis=1)
    return cos_t, sin_lo, sin_hi


SWA_NSB = 2


def _swa_kernel(q_ref, k_ref, v_ref, qg_ref, kg_ref, cos_ref, slo_ref, shi_ref, sink_ref,
                y_ref, ko_ref, vo_ref, kd_ref, vd_ref, qs_ref):
    c = pl.program_id(1)
    w = WINDOW
    nsb = SWA_NSB
    grp = ATT_HEADS // ATT_KV
    rows = nsb * w
    lane = lax.broadcasted_iota(jnp.int32, (rows, LANE), 1)
    low_half = lane < ATT_HD
    low_half_w = lax.broadcasted_iota(jnp.int32, (w, LANE), 1) < ATT_HD
    cos_t = cos_ref[...]
    sin_lo = slo_ref[...]
    sin_hi = shi_ref[...]

    @pl.when(c == 0)
    def _():
        kd_ref[:, 0:w, :] = jnp.zeros((ATT_KV, w, LANE), BF16)
        vd_ref[:, 0:w, :] = jnp.zeros((ATT_KV, w, 2 * LANE), BF16)
        vd_ref[:, :, LANE:2 * LANE] = jnp.ones((ATT_KV, (nsb + 1) * w, LANE), BF16)

    kf = k_ref[...].astype(F32)
    vf = v_ref[...].astype(F32)
    for j in range(ATT_KVD // LANE):
        sl = slice(j * LANE, (j + 1) * LANE)
        kn = _norm_rope_tile(kf[:, sl], kg_ref[...], cos_t, sin_lo, sin_hi, low_half)
        ko_ref[0, :, sl] = kn[rows - w:rows]
        kr = pltpu.roll(kn, ATT_HD, axis=1)
        vj = vf[:, sl]
        vr = pltpu.roll(vj, ATT_HD, axis=1)
        kd_ref[2 * j, w:w + rows, :] = jnp.where(low_half, kn, kr).astype(BF16)
        kd_ref[2 * j + 1, w:w + rows, :] = jnp.where(low_half, kr, kn).astype(BF16)
        vd_ref[2 * j, w:w + rows, 0:LANE] = jnp.where(low_half, vj, vr).astype(BF16)
        vd_ref[2 * j + 1, w:w + rows, 0:LANE] = jnp.where(low_half, vr, vj).astype(BF16)
    vo_ref[0] = vf[rows - w:rows]

    qf = q_ref[...].astype(F32)
    for j in range(D // LANE):
        sl = slice(j * LANE, (j + 1) * LANE)
        qn = _norm_rope_tile(qf[:, sl], qg_ref[...], cos_t, sin_lo, sin_hi, low_half)
        for par in range(2):
            h = 2 * j + par
            r0 = (h % grp) * w
            in_half = low_half if par == 0 else jnp.logical_not(low_half)
            qh = jnp.where(in_half, qn, 0.0).astype(BF16)
            for sb in range(nsb):
                qs_ref[h // grp, sb, r0:r0 + w, :] = qh[sb * w:(sb + 1) * w]

    qi = lax.broadcasted_iota(jnp.int32, (grp * w, 2 * w), 0) & (w - 1)
    kj = lax.broadcasted_iota(jnp.int32, (grp * w, 2 * w), 1)
    in_cur = (kj >= w) & ((kj - w) <= qi)
    in_prev = (kj < w) & (kj > qi)
    scale = ATT_HD ** -0.5

    for sb in range(nsb):
        mask = in_cur | (in_prev & (c > 0) if sb == 0 else in_prev)
        for kv in range(ATT_KV):
            snk = jnp.concatenate(
                [jnp.broadcast_to(sink_ref[:, kv * grp + i:kv * grp + i + 1], (w, 1)) for i in range(grp)], axis=0)
            s = _dot_nt(qs_ref[kv, sb], kd_ref[kv, sb * w:(sb + 2) * w, :]) * scale
            s = jnp.where(mask, s, -jnp.inf)
            m = jnp.maximum(jnp.max(s, axis=-1, keepdims=True), snk)
            p = jnp.exp(s - m)
            den = jnp.sum(p, axis=-1, keepdims=True) + jnp.exp(snk - m)
            o = _dot(p.astype(BF16), vd_ref[kv, sb * w:(sb + 2) * w, 0:LANE]) * (1.0 / den)
            for a in range(grp // 2):
                col = kv * (grp // 2) + a
                y_ref[sb * w:(sb + 1) * w, col * LANE:(col + 1) * LANE] = jnp.where(
                    low_half_w, o[2 * a * w:(2 * a + 1) * w, :], o[(2 * a + 1) * w:(2 * a + 2) * w, :]
                ).astype(BF16)

    kd_ref[:, 0:w, :] = kd_ref[:, rows:rows + w, :]
    vd_ref[:, 0:w, 0:LANE] = vd_ref[:, rows:rows + w, 0:LANE]


def _swa(proj, qg2, kg2, cos_t, sin_lo, sin_hi, sink, nb, seq):
    t = proj.shape[0]
    w = WINDOW
    rows = SWA_NSB * w
    nc = seq // rows
    grp = ATT_HEADS // ATT_KV
    tab = pl.BlockSpec((rows, LANE), lambda b, c: (c, 0))
    return pl.pallas_call(
        _swa_kernel,
        grid=(nb, nc),
        in_specs=[
            pl.BlockSpec((rows, D), lambda b, c: (b * nc + c, COL_Q_BLK)),
            pl.BlockSpec((rows, ATT_KVD), lambda b, c: (b * nc + c, COL_K_BLK)),
            pl.BlockSpec((rows, ATT_KVD), lambda b, c: (b * nc + c, COL_V_BLK)),
            pl.BlockSpec((1, LANE), lambda b, c: (0, 0)),
            pl.BlockSpec((1, LANE), lambda b, c: (0, 0)),
            tab, tab, tab,
            pl.BlockSpec((1, ATT_HEADS), lambda b, c: (0, 0)),
        ],
        out_specs=[
            pl.BlockSpec((rows, D), lambda b, c: (b * nc + c, 0)),
            pl.BlockSpec((1, w, ATT_KVD), lambda b, c: (b, 0, 0)),
            pl.BlockSpec((1, w, ATT_KVD), lambda b, c: (b, 0, 0)),
        ],
        out_shape=[
            jax.ShapeDtypeStruct((t, D), BF16),
            jax.ShapeDtypeStruct((nb, w, ATT_KVD), F32),
            jax.ShapeDtypeStruct((nb, w, ATT_KVD), F32),
        ],
        scratch_shapes=[
            pltpu.VMEM((ATT_KV, rows + w, LANE), BF16),
            pltpu.VMEM((ATT_KV, rows + w, 2 * LANE), BF16),
            pltpu.VMEM((ATT_KV, SWA_NSB, grp * w, LANE), BF16),
        ],
        compiler_params=_cparams(("parallel", "arbitrary")),
        name="swa_prompt",
    )(proj, proj, proj, qg2, kg2, cos_t, sin_lo, sin_hi, sink)


def _memkv_kernel(m_ref, g_ref, w_ref, kg_ref, k_ref, v_ref):
    x = m_ref[...]
    ms = jnp.mean(x * x, axis=-1, keepdims=True)
    hn = (x * lax.rsqrt(ms + EPS) * g_ref[...]).astype(BF16)
    kv = _dot(hn, w_ref[...])
    for h in range(MEM_HEADS):
        kh = kv[:, h * MEM_HD:(h + 1) * MEM_HD]
        ms = jnp.mean(kh * kh, axis=-1, keepdims=True)
        k_ref[:, h * MEM_HD:(h + 1) * MEM_HD] = kh * lax.rsqrt(ms + EPS) * kg_ref[...]
    v_ref[...] = kv[:, D:]


def _memkv(mem, g, w, kg, tm):
    t = mem.shape[0]
    return pl.pallas_call(
        _memkv_kernel,
        grid=(t // tm,),
        in_specs=[
            pl.BlockSpec((tm, D), lambda i: (i, 0)),
            pl.BlockSpec((1, D), lambda i: (0, 0)),
            pl.BlockSpec((D, 2 * D), lambda i: (0, 0)),
            pl.BlockSpec((1, MEM_HD), lambda i: (0, 0)),
        ],
        out_specs=[
            pl.BlockSpec((tm, D), lambda i: (i, 0)),
            pl.BlockSpec((tm, D), lambda i: (i, 0)),
        ],
        out_shape=[jax.ShapeDtypeStruct((t, D), F32)] * 2,
        compiler_params=_cparams(("parallel",)),
        name="mem_kv",
    )(mem, g, w, kg)


def _memattn_kernel(q_ref, k_ref, v_ref, qg_ref, y_ref):
    q = q_ref[...].astype(F32)
    scale = MEM_HD ** -0.5
    for h in range(MEM_HEADS):
        sl = slice(h * MEM_HD, (h + 1) * MEM_HD)
        qh = q[:, sl]
        ms = jnp.mean(qh * qh, axis=-1, keepdims=True)
        qn = (qh * lax.rsqrt(ms + EPS) * qg_ref[...]).astype(BF16)
        s = _dot_nt(qn, k_ref[:, sl].astype(BF16)) * scale
        m = jnp.max(s, axis=-1, keepdims=True)
        p = jnp.exp(s - m)
        den = jnp.sum(p, axis=-1, keepdims=True)
        o = _dot(p.astype(BF16), v_ref[:, sl].astype(BF16)) / den
        y_ref[:, sl] = o.astype(BF16)


def _memattn(proj, mk, mv, qg, nb, seq, mem_len, tm):
    t = proj.shape[0]
    nt = seq // tm
    return pl.pallas_call(
        _memattn_kernel,
        grid=(nb, nt),
        in_specs=[
            pl.BlockSpec((tm, D), lambda b, i: (b * nt + i, COL_MQ_BLK)),
            pl.BlockSpec((mem_len, D), lambda b, i: (b, 0)),
            pl.BlockSpec((mem_len, D), lambda b, i: (b, 0)),
            pl.BlockSpec((1, MEM_HD), lambda b, i: (0, 0)),
        ],
        out_specs=pl.BlockSpec((tm, D), lambda b, i: (b * nt + i, 0)),
        out_shape=jax.ShapeDtypeStruct((t, D), BF16),
        compiler_params=_cparams(("parallel", "arbitrary")),
        name="mem_attn_prompt",
    )(proj, mk, mv, qg)


def _route(logits):
    lane = lax.broadcasted_iota(jnp.int32, logits.shape, 1)
    lanef = lane.astype(F32)
    is_g = lane < N_GROUPS
    neg = -jnp.inf
    big = 1e9
    gl = jnp.where(is_g, logits, neg)
    gmax = jnp.max(gl, axis=-1, keepdims=True)
    gidx = jnp.min(jnp.where(is_g & (logits == gmax), lanef, big), axis=-1, keepdims=True)
    pg_top = 1.0 / jnp.sum(jnp.where(is_g, jnp.exp(logits - gmax), 0.0), axis=-1, keepdims=True)
    lo = N_GROUPS + gidx * EPG
    in_g = (lanef >= lo) & (lanef < lo + EPG)
    m1 = jnp.max(jnp.where(in_g, logits, neg), axis=-1, keepdims=True)
    i1 = jnp.min(jnp.where(in_g & (logits == m1), lanef, big), axis=-1, keepdims=True)
    rest = in_g & (lanef != i1)
    m2 = jnp.max(jnp.where(rest, logits, neg), axis=-1, keepdims=True)
    i2 = jnp.min(jnp.where(rest & (logits == m2), lanef, big), axis=-1, keepdims=True)
    r = jnp.exp(m2 - m1)
    w1 = pg_top / (1.0 + r)
    w2 = pg_top * r / (1.0 + r)
    comb = jnp.where(lanef == i1, w1, 0.0) + jnp.where(lanef == i2, w2, 0.0)
    return jnp.where(lane == 0, gidx, comb)


def _back_kernel(x_ref, gt_ref, ys_ref, ya_ref, ym_ref, ws_ref, wa_ref, wm_ref, wo_ref, g2_ref,
                 wr_ref, br_ref, x1_ref, h_ref, comb_ref, *, rows_for_dispatch):
    gt = gt_ref[...].astype(F32)
    merged = (_sigmoid(gt[:, 0:D]) * _dot(ys_ref[...].astype(BF16), ws_ref[...])
              + _sigmoid(gt[:, D:2 * D]) * _dot(ya_ref[...], wa_ref[...])
              + _sigmoid(gt[:, 2 * D:3 * D]) * _dot(ym_ref[...], wm_ref[...]))
    x1 = x_ref[...] + _dot(merged.astype(BF16), wo_ref[...])
    x1_ref[...] = x1
    ms = jnp.mean(x1 * x1, axis=-1, keepdims=True)
    h = x1 * lax.rsqrt(ms + EPS) * g2_ref[...]
    h_hi = h.astype(BF16)
    h_lo = (h - h_hi.astype(F32)).astype(BF16)
    wr = wr_ref[...]
    wr_hi = wr.astype(BF16)
    wr_lo = (wr - wr_hi.astype(F32)).astype(BF16)
    logits = (_dot(h_hi, wr_hi) + _dot(h_hi, wr_lo) + _dot(h_lo, wr_hi)) + br_ref[...]
    comb = _route(logits)
    comb_ref[...] = comb
    if rows_for_dispatch:
        h_ref[:, 0:D] = h
        h_ref[:, D:D + ROUTE_W] = comb
    else:
        h_ref[...] = h.astype(BF16)


def _back(x, proj, y_ssm, y_swa, y_mem, ws, wa, wm, wo, g2, wr, br, tm, rows_for_dispatch):
    t = x.shape[0]
    resident = lambda shape: pl.BlockSpec(shape, lambda i: (0, 0))
    hw = D + ROUTE_W if rows_for_dispatch else D
    return pl.pallas_call(
        functools.partial(_back_kernel, rows_for_dispatch=rows_for_dispatch),
        grid=(t // tm,),
        in_specs=[
            pl.BlockSpec((tm, D), lambda i: (i, 0)),
            pl.BlockSpec((tm, 3 * D), lambda i: (i, 0)),
            pl.BlockSpec((tm, SSM_INNER), lambda i: (i, 0)),
            pl.BlockSpec((tm, D), lambda i: (i, 0)),
            pl.BlockSpec((tm, D), lambda i: (i, 0)),
            resident((SSM_INNER, D)),
            resident((D, D)),
            resident((D, D)),
            resident((D, D)),
            resident((1, D)),
            resident((D, ROUTE_W)),
            resident((1, ROUTE_W)),
        ],
        out_specs=[
            pl.BlockSpec((tm, D), lambda i: (i, 0)),
            pl.BlockSpec((tm, hw), lambda i: (i, 0)),
            pl.BlockSpec((tm, ROUTE_W), lambda i: (i, 0)),
        ],
        out_shape=[
            jax.ShapeDtypeStruct((t, D), F32),
            jax.ShapeDtypeStruct((t, hw), F32 if rows_for_dispatch else BF16),
            jax.ShapeDtypeStruct((t, ROUTE_W), F32),
        ],
        compiler_params=_cparams(("parallel",)),
        name="back",
    )(x, proj, y_ssm, y_swa, y_mem, ws, wa, wm, wo, g2, wr, br)


MOE_TR = 512
RANK_TM = 1024
ROW_W = D + ROUTE_W


def _rank_kernel(comb_ref, pos_ref, cnt_ref, carry_ref, offs_ref):
    ph = pl.program_id(0)
    i = pl.program_id(1)
    tm = comb_ref.shape[0]
    lane = lax.broadcasted_iota(jnp.int32, (tm, LANE), 1)
    gid = jnp.sum(jnp.where(lane == 0, comb_ref[...], 0.0), axis=-1, keepdims=True)
    onehot = (lane.astype(F32) == gid).astype(F32)
    colsum = jnp.sum(onehot, axis=0, keepdims=True)

    @pl.when((ph == 0) & (i == 0))
    def _():
        cnt_ref[...] = jnp.zeros_like(cnt_ref)

    @pl.when(ph == 0)
    def _():
        cnt_ref[...] += colsum

    @pl.when((ph == 1) & (i == 0))
    def _():
        padded = jnp.ceil(cnt_ref[...] * (1.0 / MOE_TR)) * MOE_TR
        r = lax.broadcasted_iota(jnp.int32, (LANE, LANE), 0)
        c = lax.broadcasted_iota(jnp.int32, (LANE, LANE), 1)
        offs_ref[...] = _dot_exact(padded, (r < c).astype(F32))
        carry_ref[...] = jnp.zeros_like(carry_ref)

    @pl.when(ph == 1)
    def _():
        rr = lax.broadcasted_iota(jnp.int32, (tm, tm), 0)
        cc = lax.broadcasted_iota(jnp.int32, (tm, tm), 1)
        before = _dot((cc < rr).astype(BF16), onehot.astype(BF16))
        slot = onehot * (offs_ref[...] + carry_ref[...] + before)
        pos = lax.dot_general(jnp.ones((8, LANE), F32), slot, (((1,), (1,)), ((), ())),
                              preferred_element_type=F32, precision=HIGHEST)
        pos_ref[0] = pos.astype(jnp.int32)
        carry_ref[...] += colsum


def _rank(comb):
    t = comb.shape[0]
    tm = RANK_TM
    nt = t // tm
    return pl.pallas_call(
        _rank_kernel,
        grid=(2, nt),
        in_specs=[pl.BlockSpec((tm, ROUTE_W), lambda p, i: (i, 0))],
        out_specs=[
            pl.BlockSpec((1, 8, tm), lambda p, i: (i * p, 0, 0)),
            pl.BlockSpec((1, LANE), lambda p, i: (0, 0)),
        ],
        out_shape=[
            jax.ShapeDtypeStruct((nt, 8, tm), jnp.int32),
            jax.ShapeDtypeStruct((1, LANE), F32),
        ],
        scratch_shapes=[pltpu.VMEM((1, LANE), F32), pltpu.VMEM((1, LANE), F32)],
        compiler_params=_cparams(("arbitrary", "arbitrary")),
        name="moe_rank",
    )(comb)


def _row_copy(src_hbm, dst_hbm, src_row, dst_row, sem):
    return pltpu.make_async_copy(src_hbm.at[pl.ds(src_row, 1)], dst_hbm.at[pl.ds(dst_row, 1)], sem)


MOE_OBUFS = 3


def _gmoe_kernel(tg_ref, nt_ref, src_ref, nxt_ref, prv_ref, fin_ref, rows_hbm, wg_ref, wu_ref, wd_ref, mo_in,
                 mo_hbm, xbuf, obuf, gsem, ssem):
    del mo_in
    i = pl.program_id(0)
    n_used = nt_ref[0]
    slot = i % 2
    oslot = i % MOE_OBUFS
    pslot = (i + MOE_OBUFS - 1) % MOE_OBUFS
    tr = MOE_TR

    def start_gather(idx_ref, s):
        for r in range(tr):
            _row_copy(rows_hbm, xbuf.at[s], idx_ref[0, r], r, gsem.at[s]).start()

    def wait_gather(s):
        pltpu.make_async_copy(rows_hbm.at[pl.ds(0, tr)], xbuf.at[s], gsem.at[s]).wait()

    def start_scatter(idx_ref, s):
        for r in range(tr):
            _row_copy(obuf.at[s], mo_hbm, r, idx_ref[0, r], ssem.at[s]).start()

    def wait_scatter(s):
        pltpu.make_async_copy(obuf.at[s], mo_hbm.at[pl.ds(0, tr)], ssem.at[s]).wait()

    @pl.when((i == 0) & (n_used > 0))
    def _():
        start_gather(src_ref, 0)
        obuf[MOE_OBUFS - 1] = jnp.zeros((tr, D), F32)

    @pl.when(i < n_used)
    def _():
        wait_gather(slot)

        @pl.when(i >= 2)
        def _():
            wait_scatter(oslot)

        start_gather(nxt_ref, 1 - slot)
        start_scatter(prv_ref, pslot)
        x = xbuf[slot]
        h = x[:, 0:D].astype(BF16)
        comb = x[:, D:D + ROUTE_W]
        lane = lax.broadcasted_iota(jnp.int32, comb.shape, 1)
        first = N_GROUPS + tg_ref[i] * EPG
        for e in range(EPG):
            cw = jnp.sum(jnp.where(lane == first + e, comb, 0.0), axis=-1, keepdims=True)
            act = _silu(_dot(h, wg_ref[e])) * _dot(h, wu_ref[e])
            part = _dot((act * cw).astype(BF16), wd_ref[e])
            if e == 0:
                obuf[oslot] = part
            else:
                obuf[oslot] += part

    @pl.when((i == pl.num_programs(0) - 1) & (n_used > 0))
    def _():
        last = (n_used + MOE_OBUFS - 1) % MOE_OBUFS
        start_scatter(fin_ref, last)
        wait_gather(n_used % 2)
        wait_scatter(last)
        wait_scatter((n_used + MOE_OBUFS - 2) % MOE_OBUFS)

        @pl.when(n_used >= 2)
        def _():
            wait_scatter(n_used % MOE_OBUFS)


def _gmoe(tile_group, n_used, src3, dst3, rows, wg, wu, wd, mo_init):
    n_tiles = src3.shape[0]
    tr = MOE_TR
    idx = lambda f: pl.BlockSpec((None, 1, tr), f, memory_space=pltpu.SMEM)
    grid_spec = pltpu.PrefetchScalarGridSpec(
        num_scalar_prefetch=2,
        grid=(n_tiles,),
        in_specs=[
            idx(lambda i, tg, nt: (i, 0, 0)),
            idx(lambda i, tg, nt: (jnp.minimum(i + 1, n_tiles - 1), 0, 0)),
            idx(lambda i, tg, nt: (i, 0, 0)),
            idx(lambda i, tg, nt: (nt[0], 0, 0)),
            pl.BlockSpec(memory_space=pl.ANY),
            pl.BlockSpec((None, EPG, D, D_FF), lambda i, tg, nt: (tg[i], 0, 0, 0)),
            pl.BlockSpec((None, EPG, D, D_FF), lambda i, tg, nt: (tg[i], 0, 0, 0)),
            pl.BlockSpec((None, EPG, D_FF, D), lambda i, tg, nt: (tg[i], 0, 0, 0)),
            pl.BlockSpec(memory_space=pl.ANY),
        ],
        out_specs=pl.BlockSpec(memory_space=pl.ANY),
        scratch_shapes=[
            pltpu.VMEM((2, tr, ROW_W), F32),
            pltpu.VMEM((MOE_OBUFS, tr, D), F32),
            pltpu.SemaphoreType.DMA((2,)),
            pltpu.SemaphoreType.DMA((MOE_OBUFS,)),
        ],
    )
    return pl.pallas_call(
        _gmoe_kernel,
        grid_spec=grid_spec,
        out_shape=jax.ShapeDtypeStruct(mo_init.shape, F32),
        input_output_aliases={10: 0},
        compiler_params=_cparams(("arbitrary",)),
        name="moe_grouped",
    )(tile_group, n_used, src3, src3, dst3, dst3, rows, wg, wu, wd, mo_init)


def _residual_kernel(x1_ref, mo_ref, y_ref):
    y_ref[...] = x1_ref[...] + mo_ref[...]


def _residual(x1, mo, tm):
    t = x1.shape[0]
    return pl.pallas_call(
        _residual_kernel,
        grid=(t // tm,),
        in_specs=[pl.BlockSpec((tm, D), lambda i: (i, 0)), pl.BlockSpec((tm, D), lambda i: (i, 0))],
        out_specs=pl.BlockSpec((tm, D), lambda i: (i, 0)),
        out_shape=jax.ShapeDtypeStruct((t, D), F32),
        compiler_params=_cparams(("parallel",)),
        name="moe_residual",
    )(x1, mo)


def _moe_routed(x1, rows, comb, wg, wu, wd):
    t = x1.shape[0]
    tr = MOE_TR
    n_slots = t + N_GROUPS * tr
    n_tiles = n_slots // tr
    pos3, cnt = _rank(comb)
    pos = pos3[:, 0, :].reshape(t)
    token = jnp.arange(t, dtype=jnp.int32)
    padded = (jnp.ceil(cnt[0, :N_GROUPS] / tr) * tr).astype(jnp.int32)
    ends = jnp.cumsum(padded)
    tile_start = jnp.arange(n_tiles, dtype=jnp.int32) * tr
    tile_group = jnp.minimum(jnp.sum(tile_start[:, None] >= ends[None, :], axis=1), N_GROUPS - 1).astype(jnp.int32)
    n_used = (ends[-1:] // tr).astype(jnp.int32)
    row_in_tile = jnp.arange(tr, dtype=jnp.int32)
    spare = t + (1 + tile_group)[:, None] * tr + row_in_tile[None, :]
    slot_token = jnp.full((n_slots,), -1, jnp.int32).at[pos].set(token, unique_indices=True)
    src = jnp.maximum(slot_token, 0)
    dst = jnp.where(slot_token >= 0, slot_token, spare.reshape(n_slots))
    dst3 = jnp.concatenate([t + row_in_tile, dst]).reshape(n_tiles + 1, 1, tr)
    mo_init = jnp.zeros((t + (1 + N_GROUPS) * tr, D), F32)
    mo = _gmoe(tile_group, n_used, src.reshape(n_tiles, 1, tr), dst3, rows,
               wg.reshape(N_GROUPS, EPG, D, D_FF), wu.reshape(N_GROUPS, EPG, D, D_FF),
               wd.reshape(N_GROUPS, EPG, D_FF, D), mo_init)
    return _residual(x1, mo, _pick(t, (1024, 512, 256, 128)))


def _moe_kernel(x1_ref, h_ref, comb_ref, wg_ref, wu_ref, wd_ref, o_ref):
    e = pl.program_id(1)

    @pl.when(e == 0)
    def _():
        o_ref[...] = x1_ref[...]

    h = h_ref[...]
    lane = lax.broadcasted_iota(jnp.int32, comb_ref.shape, 1)
    cw = jnp.sum(jnp.where(lane == e + N_GROUPS, comb_ref[...], 0.0), axis=-1, keepdims=True)
    act = _silu(_dot(h, wg_ref[0])) * _dot(h, wu_ref[0])
    o_ref[...] += _dot((act * cw).astype(BF16), wd_ref[0])


def _moe(x1, h, comb, wg, wu, wd, tm):
    t = x1.shape[0]
    return pl.pallas_call(
        _moe_kernel,
        grid=(t // tm, N_EXPERTS),
        in_specs=[
            pl.BlockSpec((tm, D), lambda i, e: (i, 0)),
            pl.BlockSpec((tm, D), lambda i, e: (i, 0)),
            pl.BlockSpec((tm, ROUTE_W), lambda i, e: (i, 0)),
            pl.BlockSpec((1, D, D_FF), lambda i, e: (e, 0, 0)),
            pl.BlockSpec((1, D, D_FF), lambda i, e: (e, 0, 0)),
            pl.BlockSpec((1, D_FF, D), lambda i, e: (e, 0, 0)),
        ],
        out_specs=pl.BlockSpec((tm, D), lambda i, e: (i, 0)),
        out_shape=jax.ShapeDtypeStruct((t, D), F32),
        compiler_params=_cparams(("parallel", "arbitrary")),
        name="moe",
    )(x1, h, comb, wg, wu, wd)


def _sample_prep_kernel(xbc_ref, q_ref, k_ref, mq_ref, dt_ref, sc_ref, cw_ref, cb_ref, dtb_ref, a_ref, e_ref,
                        qg_ref, kg_ref, mqg_ref, cos_ref, slo_ref, shi_ref,
                        conv_ref, xs_ref, bc_ref, dtx_ref, dec_ref, qn_ref, kn_ref, mqn_ref):
    n = xbc_ref.shape[0]
    x_raw = xbc_ref[...].astype(F32)
    acc = x_raw * cw_ref[SSM_CONV - 1:SSM_CONV, :] + cb_ref[...]
    for j in range(SSM_CONV - 1):
        acc = acc + sc_ref[:, j * CONV_CH:(j + 1) * CONV_CH] * cw_ref[j:j + 1, :]
    conv_ref[:, 0:(SSM_CONV - 2) * CONV_CH] = sc_ref[:, CONV_CH:(SSM_CONV - 1) * CONV_CH]
    conv_ref[:, (SSM_CONV - 2) * CONV_CH:] = x_raw
    act = _silu(acc)
    xs = act[:, 0:SSM_INNER]
    xs_ref[...] = xs
    bc_ref[...] = act[:, SSM_INNER:]
    dt = _softplus(dt_ref[...] + dtb_ref[...])
    e = e_ref[...]
    dtx = _dot_exact(dt, e) * xs
    dec = _dot_exact(jnp.exp(dt * a_ref[...]), e)
    dtx_ref[...] = dtx.T
    dec_ref[...] = dec.T

    lane = lax.broadcasted_iota(jnp.int32, (n, LANE), 1)
    low_half = lane < ATT_HD
    cos_t = jnp.broadcast_to(cos_ref[...], (n, LANE))
    sin_lo = jnp.broadcast_to(slo_ref[...], (n, LANE))
    sin_hi = jnp.broadcast_to(shi_ref[...], (n, LANE))
    qf = q_ref[...].astype(F32)
    for j in range(D // LANE):
        sl = slice(j * LANE, (j + 1) * LANE)
        qn_ref[:, sl] = _norm_rope_tile(qf[:, sl], qg_ref[...], cos_t, sin_lo, sin_hi, low_half)
    kf = k_ref[...].astype(F32)
    for j in range(ATT_KVD // LANE):
        sl = slice(j * LANE, (j + 1) * LANE)
        kn_ref[:, sl] = _norm_rope_tile(kf[:, sl], kg_ref[...], cos_t, sin_lo, sin_hi, low_half)
    mq = mq_ref[...].astype(F32)
    for h in range(MEM_HEADS):
        sl = slice(h * MEM_HD, (h + 1) * MEM_HD)
        mh = mq[:, sl]
        ms = jnp.mean(mh * mh, axis=-1, keepdims=True)
        mqn_ref[:, sl] = mh * lax.rsqrt(ms + EPS) * mqg_ref[...]


def _sample_prep(proj, dt_raw, sc2d, cw, cb, dtb, a, e, qg2, kg2, mqg, cos_t, sin_lo, sin_hi):
    n = proj.shape[0]
    c2 = lambda i: (0, 0)
    full = lambda shape: pl.BlockSpec(shape, c2)
    return pl.pallas_call(
        _sample_prep_kernel,
        grid=(1,),
        in_specs=[
            pl.BlockSpec((n, CONV_CH), lambda i: (0, COL_XBC_BLK)),
            pl.BlockSpec((n, D), lambda i: (0, COL_Q_BLK)),
            pl.BlockSpec((n, ATT_KVD), lambda i: (0, COL_K_BLK)),
            pl.BlockSpec((n, D), lambda i: (0, COL_MQ_BLK)),
            full((n, LANE)),
            full((n, (SSM_CONV - 1) * CONV_CH)),
            full((SSM_CONV, CONV_CH)),
            full((1, CONV_CH)),
            full((1, LANE)),
            full((1, LANE)),
            full((LANE, SSM_INNER)),
            full((1, LANE)),
            full((1, LANE)),
            full((1, MEM_HD)),
            full((1, LANE)),
            full((1, LANE)),
            full((1, LANE)),
        ],
        out_specs=[
            full((n, (SSM_CONV - 1) * CONV_CH)),
            full((n, SSM_INNER)),
            full((n, 2 * SSM_GROUPS * SSM_STATE)),
            full((SSM_INNER, n)),
            full((SSM_INNER, n)),
            full((n, D)),
            full((n, ATT_KVD)),
            full((n, D)),
        ],
        out_shape=[
            jax.ShapeDtypeStruct((n, (SSM_CONV - 1) * CONV_CH), F32),
            jax.ShapeDtypeStruct((n, SSM_INNER), F32),
            jax.ShapeDtypeStruct((n, 2 * SSM_GROUPS * SSM_STATE), F32),
            jax.ShapeDtypeStruct((SSM_INNER, n), F32),
            jax.ShapeDtypeStruct((SSM_INNER, n), F32),
            jax.ShapeDtypeStruct((n, D), F32),
            jax.ShapeDtypeStruct((n, ATT_KVD), F32),
            jax.ShapeDtypeStruct((n, D), F32),
        ],
        compiler_params=_cparams(("arbitrary",)),
        name="sample_prep",
    )(proj, proj, proj, proj, dt_raw, sc2d, cw, cb, dtb, a, e, qg2, kg2, mqg, cos_t, sin_lo, sin_hi)


SSM_S_TILE = 8


def _sample_ssm_kernel(st_ref, dtx_ref, dec_ref, bc_ref, so_ref, yt_ref):
    i = pl.program_id(0)
    n = dtx_ref.shape[1]

    @pl.when(i == 0)
    def _():
        yt_ref[...] = jnp.zeros_like(yt_ref)

    gw = SSM_HPG * SSM_HEADDIM
    lane = lax.broadcasted_iota(jnp.int32, (gw, n), 1)
    nbc = SSM_GROUPS * SSM_STATE
    for s in range(SSM_S_TILE):
        sel = lane == (i * SSM_S_TILE + s)
        for g in range(SSM_GROUPS):
            rows = slice(g * gw, (g + 1) * gw)
            dtx_c = jnp.sum(jnp.where(sel, dtx_ref[rows, :], 0.0), axis=1, keepdims=True)
            dec_c = jnp.sum(jnp.where(sel, dec_ref[rows, :], 0.0), axis=1, keepdims=True)
            bm = bc_ref[s:s + 1, g * SSM_STATE:(g + 1) * SSM_STATE]
            cm = bc_ref[s:s + 1, nbc + g * SSM_STATE:nbc + (g + 1) * SSM_STATE]
            hn = st_ref[s, rows, :] * dec_c + dtx_c * bm
            so_ref[s, rows, :] = hn
            yc = jnp.sum(hn * cm, axis=1, keepdims=True)
            yt_ref[rows, :] = jnp.where(sel, yc, yt_ref[rows, :])


def _sample_ssm(state, dtx_t, dec_t, bc):
    n = state.shape[0]
    return pl.pallas_call(
        _sample_ssm_kernel,
        grid=(n // SSM_S_TILE,),
        in_specs=[
            pl.BlockSpec((SSM_S_TILE, SSM_INNER, SSM_STATE), lambda i: (i, 0, 0)),
            pl.BlockSpec((SSM_INNER, n), lambda i: (0, 0)),
            pl.BlockSpec((SSM_INNER, n), lambda i: (0, 0)),
            pl.BlockSpec((SSM_S_TILE, 2 * SSM_GROUPS * SSM_STATE), lambda i: (i, 0)),
        ],
        out_specs=[
            pl.BlockSpec((SSM_S_TILE, SSM_INNER, SSM_STATE), lambda i: (i, 0, 0)),
            pl.BlockSpec((SSM_INNER, n), lambda i: (0, 0)),
        ],
        out_shape=[
            jax.ShapeDtypeStruct((n, SSM_INNER, SSM_STATE), F32),
            jax.ShapeDtypeStruct((SSM_INNER, n), F32),
        ],
        compiler_params=_cparams(("arbitrary",)),
        name="sample_ssm",
    )(state, dtx_t, dec_t, bc)


def _sample_post_kernel(yt_ref, xs_ref, z_ref, dexp_ref, ng_ref, y_ref):
    y = yt_ref[...].T + dexp_ref[...] * xs_ref[...]
    yg = y * _silu(z_ref[...].astype(F32))
    ms = jnp.mean(yg * yg, axis=-1, keepdims=True)
    y_ref[...] = yg * lax.rsqrt(ms + EPS) * ng_ref[...]


def _sample_post(y_t, xs, proj, dexp, ng):
    n = xs.shape[0]
    c2 = lambda i: (0, 0)
    return pl.pallas_call(
        _sample_post_kernel,
        grid=(1,),
        in_specs=[
            pl.BlockSpec((SSM_INNER, n), c2),
            pl.BlockSpec((n, SSM_INNER), c2),
            pl.BlockSpec((n, SSM_INNER), lambda i: (0, COL_Z_BLK)),
            pl.BlockSpec((1, SSM_INNER), c2),
            pl.BlockSpec((1, SSM_INNER), c2),
        ],
        out_specs=pl.BlockSpec((n, SSM_INNER), c2),
        out_shape=jax.ShapeDtypeStruct((n, SSM_INNER), F32),
        compiler_params=_cparams(("arbitrary",)),
        name="sample_post",
    )(y_t, xs, proj, dexp, ng)


SWA_S_TILE = 8


def _sample_swa_kernel(q_ref, kn_ref, v_ref, ck_ref, cv_ref, sink_ref, y_ref, ko_ref, vo_ref):
    w = WINDOW
    scale = ATT_HD ** -0.5
    newest = lax.broadcasted_iota(jnp.int32, (ATT_HD, w), 1) == w - 1
    for s in range(SWA_S_TILE):
        for kv in range(ATT_KV):
            kt = jnp.where(newest, kn_ref[s, kv], pltpu.roll(ck_ref[s, kv], w - 1, axis=1))
            vt = jnp.where(newest, v_ref[s, kv], pltpu.roll(cv_ref[s, kv], w - 1, axis=1))
            ko_ref[s, kv] = kt
            vo_ref[s, kv] = vt
            sc = _dot(q_ref[s, kv].astype(BF16), kt.astype(BF16)) * scale
            snk = sink_ref[kv]
            m = jnp.maximum(jnp.max(sc, axis=-1, keepdims=True), snk)
            p = jnp.exp(sc - m)
            den = jnp.sum(p, axis=-1, keepdims=True) + jnp.exp(snk - m)
            y_ref[s, kv] = _dot_nt(p.astype(BF16), vt.astype(BF16)) / den


def _sample_swa(q4, kn4, v4, ck_t, cv_t, sink3, l):
    n = q4.shape[0]
    w = WINDOW
    st = SWA_S_TILE
    grp = ATT_HEADS // ATT_KV
    cache = pl.BlockSpec((None, st, ATT_KV, ATT_HD, w), lambda i: (l, i, 0, 0, 0))
    new = pl.BlockSpec((st, ATT_KV, ATT_HD, 1), lambda i: (i, 0, 0, 0))
    out = pl.BlockSpec((st, ATT_KV, ATT_HD, w), lambda i: (i, 0, 0, 0))
    return pl.pallas_call(
        _sample_swa_kernel,
        grid=(n // st,),
        in_specs=[
            pl.BlockSpec((st, ATT_KV, grp, ATT_HD), lambda i: (i, 0, 0, 0)),
            new, new, cache, cache,
            pl.BlockSpec((ATT_KV, grp, 1), lambda i: (0, 0, 0)),
        ],
        out_specs=[pl.BlockSpec((st, ATT_KV, grp, ATT_HD), lambda i: (i, 0, 0, 0)), out, out],
        out_shape=[
            jax.ShapeDtypeStruct((n, ATT_KV, grp, ATT_HD), F32),
            jax.ShapeDtypeStruct((n, ATT_KV, ATT_HD, w), F32),
            jax.ShapeDtypeStruct((n, ATT_KV, ATT_HD, w), F32),
        ],
        compiler_params=_cparams(("parallel",)),
        name="sample_swa",
    )(q4, kn4, v4, ck_t, cv_t, sink3)


MEM_S_TILE = 4


def _sample_mem_kernel(q_ref, k_ref, v_ref, y_ref):
    scale = MEM_HD ** -0.5
    for s in range(MEM_S_TILE):
        sc = jnp.sum(k_ref[s] * q_ref[s][None], axis=-1, keepdims=True) * scale
        m = jnp.max(sc, axis=0, keepdims=True)
        p = jnp.exp(sc - m)
        den = jnp.sum(p, axis=0, keepdims=True)
        o = jnp.sum(p * v_ref[s], axis=0, keepdims=True) / den
        y_ref[s] = o[0]


def _sample_mem(q3, ck, cv, l):
    n = q3.shape[0]
    mem_len = ck.shape[2]
    st = MEM_S_TILE
    cache = pl.BlockSpec((None, st, mem_len, MEM_HEADS, MEM_HD), lambda i: (l, i, 0, 0, 0))
    return pl.pallas_call(
        _sample_mem_kernel,
        grid=(n // st,),
        in_specs=[pl.BlockSpec((st, MEM_HEADS, MEM_HD), lambda i: (i, 0, 0)), cache, cache],
        out_specs=pl.BlockSpec((st, MEM_HEADS, MEM_HD), lambda i: (i, 0, 0)),
        out_shape=jax.ShapeDtypeStruct((n, MEM_HEADS, MEM_HD), F32),
        compiler_params=_cparams(("parallel",)),
        name="sample_mem",
    )(q3, ck, cv)


def _prep_weights(lw):
    w_in = lw['w_in']
    cols = [w_in[:, 0:OFF_Z], w_in[:, OFF_XBC:OFF_DT], w_in[:, OFF_Z:OFF_XBC], w_in[:, OFF_Q:OFF_K],
            w_in[:, OFF_MQ:OFF_MQ + D], w_in[:, OFF_K:OFF_V], w_in[:, OFF_V:OFF_MQ]]
    p = {}
    p['w_main'] = jnp.concatenate(cols, axis=1).astype(BF16)
    p['w_dt'] = jnp.pad(w_in[:, OFF_DT:OFF_Q], ((0, 0), (0, LANE - SSM_HEADS))).astype(BF16)
    p['norm1_g'] = lw['norm1_g'].reshape(1, D)
    p['cw'] = lw['ssm_conv_w']
    p['cb'] = lw['ssm_conv_b'].reshape(1, CONV_CH)
    pad_h = (0, LANE - SSM_HEADS)
    p['dtb'] = jnp.pad(lw['ssm_dt_bias'].astype(F32), pad_h).reshape(1, LANE)
    p['a'] = jnp.pad(-jnp.exp(lw['ssm_a_log'].astype(F32)), pad_h).reshape(1, LANE)
    p['dexp'] = jnp.repeat(lw['ssm_d'], SSM_HEADDIM).reshape(1, SSM_INNER)
    p['ssm_ng'] = lw['ssm_norm_g'].reshape(1, SSM_INNER)
    head_of_ch = jnp.arange(SSM_INNER) // SSM_HEADDIM
    p['e'] = (jnp.arange(LANE)[:, None] == head_of_ch[None, :]).astype(F32)
    p['qg2'] = jnp.tile(lw['att_q_norm_g'], 2).reshape(1, LANE)
    p['kg2'] = jnp.tile(lw['att_k_norm_g'], 2).reshape(1, LANE)
    p['sink_row'] = lw['att_sink'].astype(F32).reshape(1, ATT_HEADS)
    p['sink3'] = lw['att_sink'].astype(F32).reshape(ATT_KV, ATT_HEADS // ATT_KV, 1)
    p['mem_g'] = lw['mem_norm_g'].reshape(1, D)
    p['w_mem_kv'] = lw['w_mem_kv'].astype(BF16)
    p['mqg'] = lw['mem_q_norm_g'].reshape(1, MEM_HD)
    p['mkg'] = lw['mem_k_norm_g'].reshape(1, MEM_HD)
    p['ws'] = lw['w_br_ssm'].astype(BF16)
    p['wa'] = lw['w_br_swa'].astype(BF16)
    p['wm'] = lw['w_br_mem'].astype(BF16)
    p['wo'] = lw['w_out'].astype(BF16)
    p['g2'] = lw['norm2_g'].reshape(1, D)
    pad_r = ROUTE_W - N_GROUPS - N_EXPERTS
    p['wr'] = jnp.pad(jnp.concatenate([lw['w_router_group'], lw['w_router_expert']], axis=1).astype(F32),
                      ((0, 0), (0, pad_r)))
    p['br'] = jnp.pad(jnp.concatenate([lw['b_router_group'], lw['b_router_expert']]).astype(F32),
                      (0, pad_r)).reshape(1, ROUTE_W)
    p['wg'] = lw['w_exp_gate'].astype(BF16)
    p['wu'] = lw['w_exp_up'].astype(BF16)
    p['wd'] = lw['w_exp_down'].astype(BF16)
    return p


def _pick(n, prefs):
    for c in prefs:
        if n % c == 0:
            return c
    return n


def _tail(x, proj, y_ssm, y_swa, y_mem, p):
    t = x.shape[0]
    routed = t % RANK_TM == 0
    x1, h, comb = _back(x, proj, y_ssm, y_swa, y_mem, p['ws'], p['wa'], p['wm'], p['wo'], p['g2'],
                        p['wr'], p['br'], _pick(t, (256, 128)), routed)
    if routed:
        return _moe_routed(x1, h, comb, p['wg'], p['wu'], p['wd'])
    return _moe(x1, h, comb, p['wg'], p['wu'], p['wd'], _pick(t, (1024, 512, 256, 128)))


def _prompt_layer(x, mem, p):
    nb, seq, _ = x.shape
    t = nb * seq
    nc = seq // CHUNK
    xf = x.reshape(t, D)
    proj, dt_raw = _front(xf, p['norm1_g'], p['w_main'], p['w_dt'], _pick(t, (1024, 512, 256, 128)), 1536)
    y_ssm, conv_new, ssm_new = _ssd(proj, dt_raw, p['cw'], p['cb'], p['dtb'], p['a'], p['dexp'],
                                    p['ssm_ng'], nb, nc)
    cos_t, sin_lo, sin_hi = _rope_tables(jnp.arange(seq))
    y_swa, k_new, v_new = _swa(proj, p['qg2'], p['kg2'], cos_t, sin_lo, sin_hi, p['sink_row'], nb, seq)
    mem_len = mem.shape[1]
    mk, mv = _memkv(mem.reshape(nb * mem_len, D), p['mem_g'], p['w_mem_kv'], p['mkg'],
                    _pick(nb * mem_len, (512, 256, 128)))
    y_mem = _memattn(proj, mk, mv, p['mqg'], nb, seq, mem_len, _pick(seq, (512, 256, 128)))
    y = _tail(xf, proj, y_ssm, y_swa, y_mem, p)
    return (y.reshape(nb, seq, D), conv_new,
            ssm_new.reshape(nb, SSM_HEADS, SSM_HEADDIM, SSM_STATE),
            k_new.reshape(nb, WINDOW, ATT_KV, ATT_HD), v_new.reshape(nb, WINDOW, ATT_KV, ATT_HD),
            mk.reshape(nb, mem_len, MEM_HEADS, MEM_HD), mv.reshape(nb, mem_len, MEM_HEADS, MEM_HD))


def _sample_layer(x, conv_st, ssm_st, swa_k, swa_v, mem_k, mem_v, l, p):
    n = x.shape[0]
    xf = x.reshape(n, D)
    proj, dt_raw = _front(xf, p['norm1_g'], p['w_main'], p['w_dt'], n, 1536)
    cos_t, sin_lo, sin_hi = _rope_tables(jnp.full((1,), PAST_LEN, jnp.int32))
    sc2d = conv_st.reshape(n, (SSM_CONV - 1) * CONV_CH)
    conv_new, xs, bc, dtx_t, dec_t, qn, kn, mqn = _sample_prep(
        proj, dt_raw, sc2d, p['cw'], p['cb'], p['dtb'], p['a'], p['e'], p['qg2'], p['kg2'], p['mqg'],
        cos_t, sin_lo, sin_hi)
    ssm_new, y_t = _sample_ssm(ssm_st.reshape(n, SSM_INNER, SSM_STATE), dtx_t, dec_t, bc)
    y_ssm = _sample_post(y_t, xs, proj, p['dexp'], p['ssm_ng'])
    grp = ATT_HEADS // ATT_KV
    v_raw = proj[:, COL_V_BLK * ATT_KVD:(COL_V_BLK + 1) * ATT_KVD].astype(F32)
    to_t = (0, 1, 3, 4, 2)
    y_swa, k_new, v_new = _sample_swa(
        qn.reshape(n, ATT_KV, grp, ATT_HD), kn.reshape(n, ATT_KV, ATT_HD, 1),
        v_raw.reshape(n, ATT_KV, ATT_HD, 1), swa_k.transpose(to_t), swa_v.transpose(to_t), p['sink3'], l)
    y_swa = y_swa.reshape(n, D)
    k_new = k_new.transpose(0, 3, 1, 2)
    v_new = v_new.transpose(0, 3, 1, 2)
    y_mem = _sample_mem(mqn.reshape(n, MEM_HEADS, MEM_HD), mem_k, mem_v, l)
    y = _tail(xf, proj, y_ssm, y_swa.astype(BF16), y_mem.reshape(n, D).astype(BF16), p)
    return (y.reshape(n, 1, D), conv_new.reshape(n, SSM_CONV - 1, CONV_CH),
            ssm_new.reshape(n, SSM_HEADS, SSM_HEADDIM, SSM_STATE), k_new, v_new)


def kernel(x_prompt, x_sample, state_conv, state_ssm, cache_swa_k, cache_swa_v, cache_mem_k, cache_mem_v,
           mem_prompt, norm1_g, w_in, ssm_conv_w, ssm_conv_b, ssm_dt_bias, ssm_a_log, ssm_d, ssm_norm_g,
           att_q_norm_g, att_k_norm_g, att_sink, mem_norm_g, w_mem_kv, mem_q_norm_g, mem_k_norm_g,
           w_br_ssm, w_br_swa, w_br_mem, w_out, norm2_g, w_router_group, b_router_group,
           w_router_expert, b_router_expert, w_exp_gate, w_exp_up, w_exp_down):
    weights = dict(norm1_g=norm1_g, w_in=w_in, ssm_conv_w=ssm_conv_w, ssm_conv_b=ssm_conv_b,
                   ssm_dt_bias=ssm_dt_bias, ssm_a_log=ssm_a_log, ssm_d=ssm_d, ssm_norm_g=ssm_norm_g,
                   att_q_norm_g=att_q_norm_g, att_k_norm_g=att_k_norm_g, att_sink=att_sink,
                   mem_norm_g=mem_norm_g, w_mem_kv=w_mem_kv, mem_q_norm_g=mem_q_norm_g,
                   mem_k_norm_g=mem_k_norm_g, w_br_ssm=w_br_ssm, w_br_swa=w_br_swa, w_br_mem=w_br_mem,
                   w_out=w_out, norm2_g=norm2_g, w_router_group=w_router_group,
                   b_router_group=b_router_group, w_router_expert=w_router_expert,
                   b_router_expert=b_router_expert, w_exp_gate=w_exp_gate, w_exp_up=w_exp_up,
                   w_exp_down=w_exp_down)
    depth = w_in.shape[0]
    xp, xs = x_prompt, x_sample
    outs = [[] for _ in range(10)]
    for l in range(depth):
        p = _prep_weights({k: v[l] for k, v in weights.items()})
        xp, c1, c2, c3, c4, c5, c6 = _prompt_layer(xp, mem_prompt, p)
        xs, d1, d2, d3, d4 = _sample_layer(xs, state_conv[l], state_ssm[l], cache_swa_k, cache_swa_v,
                                           cache_mem_k, cache_mem_v, l, p)
        for lst, val in zip(outs, (c1, c2, c3, c4, c5, c6, d1, d2, d3, d4)):
            lst.append(val)
    return (xp, xs) + tuple(jnp.stack(o) for o in outs)
```

```python
import functools
import math

import jax
import jax.numpy as jnp
from jax import lax
from jax.experimental import pallas as pl
from jax.experimental.pallas import tpu as pltpu

F32 = jnp.float32
BF16 = jnp.bfloat16
HIGHEST = lax.Precision.HIGHEST

D = 1024
SSM_INNER = 2048
SSM_HEADDIM = 64
SSM_HEADS = 32
SSM_GROUPS = 4
SSM_HPG = SSM_HEADS // SSM_GROUPS
SSM_STATE = 128
SSM_CONV = 4
CONV_CH = SSM_INNER + 2 * SSM_GROUPS * SSM_STATE
CHUNK = 128
ATT_HEADS = 16
ATT_KV = 4
ATT_HD = 64
ATT_KVD = ATT_KV * ATT_HD
WINDOW = 128
ROPE_THETA = 10000.0
MEM_HEADS = 4
MEM_HD = 256
N_EXPERTS = 32
N_GROUPS = 4
EPG = 8
D_FF = 256
EPS = 1e-6
PAST_LEN = 16384

OFF_Z = 3 * D
OFF_XBC = OFF_Z + SSM_INNER
OFF_DT = OFF_XBC + CONV_CH
OFF_Q = OFF_DT + SSM_HEADS
OFF_K = OFF_Q + D
OFF_V = OFF_K + ATT_KVD
OFF_MQ = OFF_V + ATT_KVD

N_MAIN = 3 * D + CONV_CH + SSM_INNER + D + D + 2 * ATT_KVD
COL_XBC_BLK = 1
COL_Z_BLK = 3
COL_Q_BLK = 8
COL_MQ_BLK = 9
COL_K_BLK = 40
COL_V_BLK = 41
LANE = 128
ROUTE_W = 128

VMEM_LIMIT = 56 * 1024 * 1024


def _cparams(sem):
    return pltpu.CompilerParams(dimension_semantics=sem, vmem_limit_bytes=VMEM_LIMIT)


def _sigmoid(x):
    return 1.0 / (1.0 + jnp.exp(-x))


def _silu(x):
    return x * _sigmoid(x)


def _softplus(x):
    return jnp.maximum(x, 0.0) + jnp.log1p(jnp.exp(-jnp.abs(x)))


def _dot(a, b):
    return jnp.dot(a, b, preferred_element_type=F32)


def _dot_nt(a, b):
    return lax.dot_general(a, b, (((1,), (1,)), ((), ())), preferred_element_type=F32)


def _dot_tn(a, b):
    return lax.dot_general(a, b, (((0,), (0,)), ((), ())), preferred_element_type=F32)


def _dot_exact(a, b):
    return jnp.dot(a, b, preferred_element_type=F32, precision=HIGHEST)


def _front_kernel(x_ref, g_ref, w_ref, wdt_ref, o_ref, dt_ref, hn_ref):
    @pl.when(pl.program_id(1) == 0)
    def _():
        x = x_ref[...]
        ms = jnp.mean(x * x, axis=-1, keepdims=True)
        hn = (x * lax.rsqrt(ms + EPS) * g_ref[...]).astype(BF16)
        hn_ref[...] = hn
        dt_ref[...] = _dot(hn, wdt_ref[...])

    o_ref[...] = _dot(hn_ref[...], w_ref[...]).astype(BF16)


def _front(x, g, w_main, w_dt, tm, tn):
    t = x.shape[0]
    return pl.pallas_call(
        _front_kernel,
        grid=(t // tm, N_MAIN // tn),
        in_specs=[
            pl.BlockSpec((tm, D), lambda i, j: (i, 0)),
            pl.BlockSpec((1, D), lambda i, j: (0, 0)),
            pl.BlockSpec((D, tn), lambda i, j: (0, j)),
            pl.BlockSpec((D, LANE), lambda i, j: (0, 0)),
        ],
        out_specs=[
            pl.BlockSpec((tm, tn), lambda i, j: (i, j)),
            pl.BlockSpec((tm, LANE), lambda i, j: (i, 0)),
        ],
        out_shape=[
            jax.ShapeDtypeStruct((t, N_MAIN), BF16),
            jax.ShapeDtypeStruct((t, LANE), F32),
        ],
        scratch_shapes=[pltpu.VMEM((tm, D), BF16)],
        compiler_params=_cparams(("parallel", "arbitrary")),
        name="front",
    )(x, g, w_main, w_dt)


SUBLANES = 8


def _ssd_kernel(xbc_ref, z_ref, dt_ref, cw_ref, cb_ref, dtb_ref, a_ref, dexp_ref, ng_ref,
                y_ref, conv_ref, ssm_ref, prev_ref, h_ref, yacc_ref):
    c = pl.program_id(1)

    @pl.when(c == 0)
    def _():
        prev_ref[...] = jnp.zeros_like(prev_ref)
        h_ref[...] = jnp.zeros_like(h_ref)

    x_raw = xbc_ref[...].astype(F32)
    prev = prev_ref[...]
    row8 = lax.broadcasted_iota(jnp.int32, (SUBLANES, CONV_CH), 0)
    acc = x_raw * cw_ref[SSM_CONV - 1:SSM_CONV, :] + cb_ref[...]
    for s in range(1, SSM_CONV):
        xr = pltpu.roll(x_raw, s, axis=0)
        head = jnp.where(row8 < s, pltpu.roll(prev, s, axis=0), xr[0:SUBLANES])
        shifted = jnp.concatenate([head, xr[SUBLANES:]], axis=0)
        acc = acc + shifted * cw_ref[SSM_CONV - 1 - s:SSM_CONV - s, :]
    prev_ref[...] = x_raw[CHUNK - SUBLANES:CHUNK]
    conv_ref[0] = x_raw[CHUNK - (SSM_CONV - 1):CHUNK]
    act = _silu(acc)

    dt = _softplus(dt_ref[...] + dtb_ref[...])
    da = dt * a_ref[...]
    row = lax.broadcasted_iota(jnp.int32, (CHUNK, CHUNK), 0)
    col = lax.broadcasted_iota(jnp.int32, (CHUNK, CHUNK), 1)
    causal = row >= col
    acum = _dot_exact(causal.astype(F32), da)
    acum_t = acum.T
    dt_t = dt.T
    exp_a = jnp.exp(acum)
    last = acum[CHUNK - 1:CHUNK, :]
    w_end = jnp.exp(last - acum) * dt
    cd = jnp.broadcast_to(jnp.exp(acum_t[:, CHUNK - 1:CHUNK]), (LANE, SSM_STATE))
    lane = lax.broadcasted_iota(jnp.int32, (CHUNK, LANE), 1)
    low_half = lane < SSM_HEADDIM

    def pair_cols(per_head, hd):
        return jnp.where(low_half, per_head[:, hd:hd + 1], per_head[:, hd + 1:hd + 2])

    xs_off = 0
    b_off = SSM_INNER
    c_off = SSM_INNER + SSM_GROUPS * SSM_STATE
    for g in range(SSM_GROUPS):
        bm = act[:, b_off + g * SSM_STATE:b_off + (g + 1) * SSM_STATE].astype(BF16)
        cm = act[:, c_off + g * SSM_STATE:c_off + (g + 1) * SSM_STATE].astype(BF16)
        cb = _dot_nt(cm, bm)
        gw = SSM_HPG * SSM_HEADDIM
        ch0 = g * gw
        h_prev = h_ref[ch0:ch0 + gw, :]
        y_off = _dot_nt(cm, h_prev.astype(BF16))
        xg = act[:, xs_off + ch0:xs_off + ch0 + gw]
        xw = []
        for pr in range(SSM_HPG // 2):
            hd0 = g * SSM_HPG + pr * 2
            xp32 = xg[:, pr * LANE:(pr + 1) * LANE]
            xpair = xp32.astype(BF16)
            yd = []
            for sub in range(2):
                hd = hd0 + sub
                seg = acum[:, hd:hd + 1] - acum_t[hd:hd + 1, :]
                decay = jnp.exp(jnp.where(causal, seg, -jnp.inf))
                wts = cb * decay * dt_t[hd:hd + 1, :]
                yd.append(_dot(wts.astype(BF16), xpair))
            cl = ch0 + pr * LANE
            y_pair = (jnp.where(low_half, yd[0], yd[1])
                      + y_off[:, pr * LANE:(pr + 1) * LANE] * pair_cols(exp_a, hd0)
                      + dexp_ref[:, cl:cl + LANE] * xp32)
            yacc_ref[:, cl:cl + LANE] = y_pair
            xw.append((xp32 * pair_cols(w_end, hd0)).astype(BF16))
        states = _dot_tn(jnp.concatenate(xw, axis=1), bm)
        for r in range(SSM_HPG):
            hd = g * SSM_HPG + r
            r0 = ch0 + r * SSM_HEADDIM
            h_ref[r0:r0 + SSM_HEADDIM, :] = (h_ref[r0:r0 + SSM_HEADDIM, :] * cd[hd:hd + 1, :]
                                             + states[r * SSM_HEADDIM:(r + 1) * SSM_HEADDIM, :])

    ssm_ref[0] = h_ref[...]
    yg = yacc_ref[...] * _silu(z_ref[...].astype(F32))
    ms = jnp.mean(yg * yg, axis=-1, keepdims=True)
    y_ref[...] = (yg * lax.rsqrt(ms + EPS) * ng_ref[...]).astype(BF16)


def _ssd(proj, dt_raw, cw, cb, dtb, a, dexp, ng, nb, nc):
    t = proj.shape[0]
    return pl.pallas_call(
        _ssd_kernel,
        grid=(nb, nc),
        in_specs=[
            pl.BlockSpec((CHUNK, CONV_CH), lambda b, c: (b * nc + c, COL_XBC_BLK)),
            pl.BlockSpec((CHUNK, SSM_INNER), lambda b, c: (b * nc + c, COL_Z_BLK)),
            pl.BlockSpec((CHUNK, LANE), lambda b, c: (b * nc + c, 0)),
            pl.BlockSpec((SSM_CONV, CONV_CH), lambda b, c: (0, 0)),
            pl.BlockSpec((1, CONV_CH), lambda b, c: (0, 0)),
            pl.BlockSpec((1, LANE), lambda b, c: (0, 0)),
            pl.BlockSpec((1, LANE), lambda b, c: (0, 0)),
            pl.BlockSpec((1, SSM_INNER), lambda b, c: (0, 0)),
            pl.BlockSpec((1, SSM_INNER), lambda b, c: (0, 0)),
        ],
        out_specs=[
            pl.BlockSpec((CHUNK, SSM_INNER), lambda b, c: (b * nc + c, 0)),
            pl.BlockSpec((1, SSM_CONV - 1, CONV_CH), lambda b, c: (b, 0, 0)),
            pl.BlockSpec((1, SSM_INNER, SSM_STATE), lambda b, c: (b, 0, 0)),
        ],
        out_shape=[
            jax.ShapeDtypeStruct((t, SSM_INNER), BF16),
            jax.ShapeDtypeStruct((nb, SSM_CONV - 1, CONV_CH), F32),
            jax.ShapeDtypeStruct((nb, SSM_INNER, SSM_STATE), F32),
        ],
        scratch_shapes=[
            pltpu.VMEM((SUBLANES, CONV_CH), F32),
            pltpu.VMEM((SSM_INNER, SSM_STATE), F32),
            pltpu.VMEM((CHUNK, SSM_INNER), F32),
        ],
        compiler_params=_cparams(("parallel", "arbitrary")),
        name="ssd_prompt",
    )(proj, proj, dt_raw, cw, cb, dtb, a, dexp, ng)


def _norm_rope_tile(xj, g2, cos_t, sin_lo, sin_hi, low_half):
    sq = xj * xj
    s_lo = jnp.sum(jnp.where(low_half, sq, 0.0), axis=-1, keepdims=True)
    s_hi = jnp.sum(jnp.where(low_half, 0.0, sq), axis=-1, keepdims=True)
    ms = jnp.where(low_half, s_lo, s_hi) * (1.0 / ATT_HD)
    xn = xj * lax.rsqrt(ms + EPS) * g2
    half = ATT_HD // 2
    return (xn * cos_t + pltpu.roll(xn, LANE - half, axis=1) * sin_lo
            + pltpu.roll(xn, half, axis=1) * sin_hi)


def _rope_tables(pos):
    half = ATT_HD // 2
    inv = ROPE_THETA ** (-jnp.arange(half, dtype=F32) / half)
    ang = pos.astype(F32)[:, None] * inv[None, :]
    cos = jnp.cos(ang)
    sin = jnp.sin(ang)
    zero = jnp.zeros_like(sin)
    cos_t = jnp.concatenate([cos, cos, cos, cos], axis=1)
    sin_lo = jnp.concatenate([-sin, zero, -sin, zero], axis=1)
    sin_hi = jnp.concatenate([zero, sin, zero, sin], axis=1)
    return cos_t, sin_lo, sin_hi


SWA_NSB = 2


def _swa_kernel(q_ref, k_ref, v_ref, qg_ref, kg_ref, cos_ref, slo_ref, shi_ref, sink_ref,
                y_ref, ko_ref, vo_ref, kd_ref, vd_ref, qs_ref):
    c = pl.program_id(1)
    w = WINDOW
    nsb = SWA_NSB
    grp = ATT_HEADS // ATT_KV
    rows = nsb * w
    lane = lax.broadcasted_iota(jnp.int32, (rows, LANE), 1)
    low_half = lane < ATT_HD
    low_half_w = lax.broadcasted_iota(jnp.int32, (w, LANE), 1) < ATT_HD
    cos_t = cos_ref[...]
    sin_lo = slo_ref[...]
    sin_hi = shi_ref[...]

    @pl.when(c == 0)
    def _():
        kd_ref[:, 0:w, :] = jnp.zeros((ATT_KV, w, LANE), BF16)
        vd_ref[:, 0:w, :] = jnp.zeros((ATT_KV, w, 2 * LANE), BF16)
        vd_ref[:, :, LANE:2 * LANE] = jnp.ones((ATT_KV, (nsb + 1) * w, LANE), BF16)

    kf = k_ref[...].astype(F32)
    vf = v_ref[...].astype(F32)
    for j in range(ATT_KVD // LANE):
        sl = slice(j * LANE, (j + 1) * LANE)
        kn = _norm_rope_tile(kf[:, sl], kg_ref[...], cos_t, sin_lo, sin_hi, low_half)
        ko_ref[0, :, sl] = kn[rows - w:rows]
        kr = pltpu.roll(kn, ATT_HD, axis=1)
        vj = vf[:, sl]
        vr = pltpu.roll(vj, ATT_HD, axis=1)
        kd_ref[2 * j, w:w + rows, :] = jnp.where(low_half, kn, kr).astype(BF16)
        kd_ref[2 * j + 1, w:w + rows, :] = jnp.where(low_half, kr, kn).astype(BF16)
        vd_ref[2 * j, w:w + rows, 0:LANE] = jnp.where(low_half, vj, vr).astype(BF16)
        vd_ref[2 * j + 1, w:w + rows, 0:LANE] = jnp.where(low_half, vr, vj).astype(BF16)
    vo_ref[0] = vf[rows - w:rows]

    qf = q_ref[...].astype(F32)
    for j in range(D // LANE):
        sl = slice(j * LANE, (j + 1) * LANE)
        qn = _norm_rope_tile(qf[:, sl], qg_ref[...], cos_t, sin_lo, sin_hi, low_half)
        for par in range(2):
            h = 2 * j + par
            r0 = (h % grp) * w
            in_half = low_half if par == 0 else jnp.logical_not(low_half)
            qh = jnp.where(in_half, qn, 0.0).astype(BF16)
            for sb in range(nsb):
                qs_ref[h // grp, sb, r0:r0 + w, :] = qh[sb * w:(sb + 1) * w]

    qi = lax.broadcasted_iota(jnp.int32, (grp * w, 2 * w), 0) & (w - 1)
    kj = lax.broadcasted_iota(jnp.int32, (grp * w, 2 * w), 1)
    in_cur = (kj >= w) & ((kj - w) <= qi)
    in_prev = (kj < w) & (kj > qi)
    scale = ATT_HD ** -0.5

    for sb in range(nsb):
        mask = in_cur | (in_prev & (c > 0) if sb == 0 else in_prev)
        for kv in range(ATT_KV):
            snk = jnp.concatenate(
                [jnp.broadcast_to(sink_ref[:, kv * grp + i:kv * grp + i + 1], (w, 1)) for i in range(grp)], axis=0)
            s = _dot_nt(qs_ref[kv, sb], kd_ref[kv, sb * w:(sb + 2) * w, :]) * scale
            s = jnp.where(mask, s, -jnp.inf)
            m = jnp.maximum(jnp.max(s, axis=-1, keepdims=True), snk)
            p = jnp.exp(s - m)
            den = jnp.sum(p, axis=-1, keepdims=True) + jnp.exp(snk - m)
            o = _dot(p.astype(BF16), vd_ref[kv, sb * w:(sb + 2) * w, 0:LANE]) * (1.0 / den)
            for a in range(grp // 2):
                col = kv * (grp // 2) + a
                y_ref[sb * w:(sb + 1) * w, col * LANE:(col + 1) * LANE] = jnp.where(
                    low_half_w, o[2 * a * w:(2 * a + 1) * w, :], o[(2 * a + 1) * w:(2 * a + 2) * w, :]
                ).astype(BF16)

    kd_ref[:, 0:w, :] = kd_ref[:, rows:rows + w, :]
    vd_ref[:, 0:w, 0:LANE] = vd_ref[:, rows:rows + w, 0:LANE]


def _swa(proj, qg2, kg2, cos_t, sin_lo, sin_hi, sink, nb, seq):
    t = proj.shape[0]
    w = WINDOW
    rows = SWA_NSB * w
    nc = seq // rows
    grp = ATT_HEADS // ATT_KV
    tab = pl.BlockSpec((rows, LANE), lambda b, c: (c, 0))
    return pl.pallas_call(
        _swa_kernel,
        grid=(nb, nc),
        in_specs=[
            pl.BlockSpec((rows, D), lambda b, c: (b * nc + c, COL_Q_BLK)),
            pl.BlockSpec((rows, ATT_KVD), lambda b, c: (b * nc + c, COL_K_BLK)),
            pl.BlockSpec((rows, ATT_KVD), lambda b, c: (b * nc + c, COL_V_BLK)),
            pl.BlockSpec((1, LANE), lambda b, c: (0, 0)),
            pl.BlockSpec((1, LANE), lambda b, c: (0, 0)),
            tab, tab, tab,
            pl.BlockSpec((1, ATT_HEADS), lambda b, c: (0, 0)),
        ],
        out_specs=[
            pl.BlockSpec((rows, D), lambda b, c: (b * nc + c, 0)),
            pl.BlockSpec((1, w, ATT_KVD), lambda b, c: (b, 0, 0)),
            pl.BlockSpec((1, w, ATT_KVD), lambda b, c: (b, 0, 0)),
        ],
        out_shape=[
            jax.ShapeDtypeStruct((t, D), BF16),
            jax.ShapeDtypeStruct((nb, w, ATT_KVD), F32),
            jax.ShapeDtypeStruct((nb, w, ATT_KVD), F32),
        ],
        scratch_shapes=[
            pltpu.VMEM((ATT_KV, rows + w, LANE), BF16),
            pltpu.VMEM((ATT_KV, rows + w, 2 * LANE), BF16),
            pltpu.VMEM((ATT_KV, SWA_NSB, grp * w, LANE), BF16),
        ],
        compiler_params=_cparams(("parallel", "arbitrary")),
        name="swa_prompt",
    )(proj, proj, proj, qg2, kg2, cos_t, sin_lo, sin_hi, sink)


def _memkv_kernel(m_ref, g_ref, w_ref, kg_ref, k_ref, v_ref):
    x = m_ref[...]
    ms = jnp.mean(x * x, axis=-1, keepdims=True)
    hn = (x * lax.rsqrt(ms + EPS) * g_ref[...]).astype(BF16)
    kv = _dot(hn, w_ref[...])
    for h in range(MEM_HEADS):
        kh = kv[:, h * MEM_HD:(h + 1) * MEM_HD]
        ms = jnp.mean(kh * kh, axis=-1, keepdims=True)
        k_ref[:, h * MEM_HD:(h + 1) * MEM_HD] = kh * lax.rsqrt(ms + EPS) * kg_ref[...]
    v_ref[...] = kv[:, D:]


def _memkv(mem, g, w, kg, tm):
    t = mem.shape[0]
    return pl.pallas_call(
        _memkv_kernel,
        grid=(t // tm,),
        in_specs=[
            pl.BlockSpec((tm, D), lambda i: (i, 0)),
            pl.BlockSpec((1, D), lambda i: (0, 0)),
            pl.BlockSpec((D, 2 * D), lambda i: (0, 0)),
            pl.BlockSpec((1, MEM_HD), lambda i: (0, 0)),
        ],
        out_specs=[
            pl.BlockSpec((tm, D), lambda i: (i, 0)),
            pl.BlockSpec((tm, D), lambda i: (i, 0)),
        ],
        out_shape=[jax.ShapeDtypeStruct((t, D), F32)] * 2,
        compiler_params=_cparams(("parallel",)),
        name="mem_kv",
    )(mem, g, w, kg)


def _memattn_kernel(q_ref, k_ref, v_ref, qg_ref, y_ref):
    q = q_ref[...].astype(F32)
    scale = MEM_HD ** -0.5
    for h in range(MEM_HEADS):
        sl = slice(h * MEM_HD, (h + 1) * MEM_HD)
        qh = q[:, sl]
        ms = jnp.mean(qh * qh, axis=-1, keepdims=True)
        qn = (qh * lax.rsqrt(ms + EPS) * qg_ref[...]).astype(BF16)
        s = _dot_nt(qn, k_ref[:, sl].astype(BF16)) * scale
        m = jnp.max(s, axis=-1, keepdims=True)
        p = jnp.exp(s - m)
        den = jnp.sum(p, axis=-1, keepdims=True)
        o = _dot(p.astype(BF16), v_ref[:, sl].astype(BF16)) / den
        y_ref[:, sl] = o.astype(BF16)


def _memattn(proj, mk, mv, qg, nb, seq, mem_len, tm):
    t = proj.shape[0]
    nt = seq // tm
    return pl.pallas_call(
        _memattn_kernel,
        grid=(nb, nt),
        in_specs=[
            pl.BlockSpec((tm, D), lambda b, i: (b * nt + i, COL_MQ_BLK)),
            pl.BlockSpec((mem_len, D), lambda b, i: (b, 0)),
            pl.BlockSpec((mem_len, D), lambda b, i: (b, 0)),
            pl.BlockSpec((1, MEM_HD), lambda b, i: (0, 0)),
        ],
        out_specs=pl.BlockSpec((tm, D), lambda b, i: (b * nt + i, 0)),
        out_shape=jax.ShapeDtypeStruct((t, D), BF16),
        compiler_params=_cparams(("parallel", "arbitrary")),
        name="mem_attn_prompt",
    )(proj, mk, mv, qg)


def _route(logits):
    lane = lax.broadcasted_iota(jnp.int32, logits.shape, 1)
    lanef = lane.astype(F32)
    is_g = lane < N_GROUPS
    neg = -jnp.inf
    big = 1e9
    gl = jnp.where(is_g, logits, neg)
    gmax = jnp.max(gl, axis=-1, keepdims=True)
    gidx = jnp.min(jnp.where(is_g & (logits == gmax), lanef, big), axis=-1, keepdims=True)
    pg_top = 1.0 / jnp.sum(jnp.where(is_g, jnp.exp(logits - gmax), 0.0), axis=-1, keepdims=True)
    lo = N_GROUPS + gidx * EPG
    in_g = (lanef >= lo) & (lanef < lo + EPG)
    m1 = jnp.max(jnp.where(in_g, logits, neg), axis=-1, keepdims=True)
    i1 = jnp.min(jnp.where(in_g & (logits == m1), lanef, big), axis=-1, keepdims=True)
    rest = in_g & (lanef != i1)
    m2 = jnp.max(jnp.where(rest, logits, neg), axis=-1, keepdims=True)
    i2 = jnp.min(jnp.where(rest & (logits == m2), lanef, big), axis=-1, keepdims=True)
    r = jnp.exp(m2 - m1)
    w1 = pg_top / (1.0 + r)
    w2 = pg_top * r / (1.0 + r)
    comb = jnp.where(lanef == i1, w1, 0.0) + jnp.where(lanef == i2, w2, 0.0)
    a = jnp.minimum(i1, i2) - lo
    b = jnp.maximum(i1, i2) - lo
    bucket = gidx * PAIRS_PER_GROUP + a * (2 * EPG - 1 - a) * 0.5 + (b - a - 1.0)
    return jnp.where(lane == 0, bucket, comb)


def _back_kernel(x_ref, gt_ref, ys_ref, ya_ref, ym_ref, ws_ref, wa_ref, wm_ref, wo_ref, g2_ref,
                 wr_ref, br_ref, x1_ref, h_ref, comb_ref, *, rows_for_dispatch):
    gt = gt_ref[...].astype(F32)
    merged = (_sigmoid(gt[:, 0:D]) * _dot(ys_ref[...].astype(BF16), ws_ref[...])
              + _sigmoid(gt[:, D:2 * D]) * _dot(ya_ref[...], wa_ref[...])
              + _sigmoid(gt[:, 2 * D:3 * D]) * _dot(ym_ref[...], wm_ref[...]))
    x1 = x_ref[...] + _dot(merged.astype(BF16), wo_ref[...])
    x1_ref[...] = x1
    ms = jnp.mean(x1 * x1, axis=-1, keepdims=True)
    h = x1 * lax.rsqrt(ms + EPS) * g2_ref[...]
    h_hi = h.astype(BF16)
    h_lo = (h - h_hi.astype(F32)).astype(BF16)
    wr = wr_ref[...]
    wr_hi = wr.astype(BF16)
    wr_lo = (wr - wr_hi.astype(F32)).astype(BF16)
    logits = (_dot(h_hi, wr_hi) + _dot(h_hi, wr_lo) + _dot(h_lo, wr_hi)) + br_ref[...]
    comb = _route(logits)
    comb_ref[...] = comb
    if rows_for_dispatch:
        h_ref[:, 0:D] = h
        h_ref[:, D:D + ROUTE_W] = comb
    else:
        h_ref[...] = h.astype(BF16)


def _back(x, proj, y_ssm, y_swa, y_mem, ws, wa, wm, wo, g2, wr, br, tm, rows_for_dispatch):
    t = x.shape[0]
    resident = lambda shape: pl.BlockSpec(shape, lambda i: (0, 0), pipeline_mode=pl.Buffered(1))
    hw = D + ROUTE_W if rows_for_dispatch else D
    return pl.pallas_call(
        functools.partial(_back_kernel, rows_for_dispatch=rows_for_dispatch),
        grid=(t // tm,),
        in_specs=[
            pl.BlockSpec((tm, D), lambda i: (i, 0)),
            pl.BlockSpec((tm, 3 * D), lambda i: (i, 0)),
            pl.BlockSpec((tm, SSM_INNER), lambda i: (i, 0)),
            pl.BlockSpec((tm, D), lambda i: (i, 0)),
            pl.BlockSpec((tm, D), lambda i: (i, 0)),
            resident((SSM_INNER, D)),
            resident((D, D)),
            resident((D, D)),
            resident((D, D)),
            resident((1, D)),
            resident((D, ROUTE_W)),
            resident((1, ROUTE_W)),
        ],
        out_specs=[
            pl.BlockSpec((tm, D), lambda i: (i, 0)),
            pl.BlockSpec((tm, hw), lambda i: (i, 0)),
            pl.BlockSpec((tm, ROUTE_W), lambda i: (i, 0)),
        ],
        out_shape=[
            jax.ShapeDtypeStruct((t, D), F32),
            jax.ShapeDtypeStruct((t, hw), F32 if rows_for_dispatch else BF16),
            jax.ShapeDtypeStruct((t, ROUTE_W), F32),
        ],
        compiler_params=_cparams(("parallel",)),
        name="back",
    )(x, proj, y_ssm, y_swa, y_mem, ws, wa, wm, wo, g2, wr, br)


PAIRS_PER_GROUP = EPG * (EPG - 1) // 2
N_BUCKETS = N_GROUPS * PAIRS_PER_GROUP
MOE_TR = 256
RANK_TM = 1024
COMBINE_TM = 512
ROW_W = D + ROUTE_W


def _bucket_experts():
    first, second = [], []
    for g in range(N_GROUPS):
        for a in range(EPG):
            for b in range(a + 1, EPG):
                first.append(g * EPG + a)
                second.append(g * EPG + b)
    return jnp.array(first, jnp.int32), jnp.array(second, jnp.int32)


def _rank_kernel(comb_ref, pos_ref, cnt_ref, carry_ref, offs_ref):
    ph = pl.program_id(0)
    i = pl.program_id(1)
    tm = comb_ref.shape[0]
    lane = lax.broadcasted_iota(jnp.int32, (tm, LANE), 1)
    gid = jnp.sum(jnp.where(lane == 0, comb_ref[...], 0.0), axis=-1, keepdims=True)
    onehot = (lane.astype(F32) == gid).astype(F32)
    colsum = jnp.sum(onehot, axis=0, keepdims=True)

    @pl.when((ph == 0) & (i == 0))
    def _():
        cnt_ref[...] = jnp.zeros_like(cnt_ref)

    @pl.when(ph == 0)
    def _():
        cnt_ref[...] += colsum

    @pl.when((ph == 1) & (i == 0))
    def _():
        padded = jnp.ceil(cnt_ref[...] * (1.0 / MOE_TR)) * MOE_TR
        r = lax.broadcasted_iota(jnp.int32, (LANE, LANE), 0)
        c = lax.broadcasted_iota(jnp.int32, (LANE, LANE), 1)
        offs_ref[...] = _dot_exact(padded, (r < c).astype(F32))
        carry_ref[...] = jnp.zeros_like(carry_ref)

    @pl.when(ph == 1)
    def _():
        rr = lax.broadcasted_iota(jnp.int32, (tm, tm), 0)
        cc = lax.broadcasted_iota(jnp.int32, (tm, tm), 1)
        before = _dot((cc < rr).astype(BF16), onehot.astype(BF16))
        slot = onehot * (offs_ref[...] + carry_ref[...] + before)
        pos = lax.dot_general(jnp.ones((8, LANE), F32), slot, (((1,), (1,)), ((), ())),
                              preferred_element_type=F32, precision=HIGHEST)
        pos_ref[0] = pos.astype(jnp.int32)
        carry_ref[...] += colsum


def _rank(comb):
    t = comb.shape[0]
    tm = RANK_TM
    nt = t // tm
    return pl.pallas_call(
        _rank_kernel,
        grid=(2, nt),
        in_specs=[pl.BlockSpec((tm, ROUTE_W), lambda p, i: (i, 0))],
        out_specs=[
            pl.BlockSpec((1, 8, tm), lambda p, i: (i * p, 0, 0)),
            pl.BlockSpec((1, LANE), lambda p, i: (0, 0)),
        ],
        out_shape=[
            jax.ShapeDtypeStruct((nt, 8, tm), jnp.int32),
            jax.ShapeDtypeStruct((1, LANE), F32),
        ],
        scratch_shapes=[pltpu.VMEM((1, LANE), F32), pltpu.VMEM((1, LANE), F32)],
        compiler_params=_cparams(("arbitrary", "arbitrary")),
        name="moe_rank",
    )(comb)


def _row_copy(src_hbm, dst_hbm, src_row, dst_row, sem):
    return pltpu.make_async_copy(src_hbm.at[pl.ds(src_row, 1)], dst_hbm.at[pl.ds(dst_row, 1)], sem)


PAD_BITS = MOE_TR.bit_length() - 1


def _dispatch_kernel(ps_ref, pc_ref, nt_ref, pos_ref, rows_ref, xs_hbm, zbuf, sem, zsem):
    i = pl.program_id(0)
    tm = rows_ref.shape[0]
    for r in range(tm):
        _row_copy(rows_ref, xs_hbm, r, pos_ref[0, r], sem.at[0]).start()
    pltpu.make_async_copy(rows_ref, xs_hbm.at[pl.ds(0, tm)], sem.at[0]).wait()

    @pl.when(i == pl.num_programs(0) - 1)
    def _():
        zbuf[...] = jnp.zeros_like(zbuf)

        def runs(b, fn):
            first = ps_ref[b]
            count = pc_ref[b]
            head = jnp.minimum((-first) & (SUBLANES - 1), count)
            for j in range(SUBLANES - 1):
                @pl.when(j < head)
                def _():
                    fn(pltpu.make_async_copy(zbuf.at[pl.ds(0, 1)], xs_hbm.at[pl.ds(first + j, 1)], zsem.at[0]))
            rest = count - head
            for k in range(SUBLANES.bit_length() - 1, PAD_BITS):
                size = 1 << k
                start = pl.multiple_of(first + head + ((rest >> (k + 1)) << (k + 1)), SUBLANES)

                @pl.when((rest & size) != 0)
                def _():
                    fn(pltpu.make_async_copy(zbuf.at[pl.ds(0, size)], xs_hbm.at[pl.ds(start, size)], zsem.at[0]))

        def issue(b, carry):
            runs(b, lambda cp: cp.start())
            return carry

        def drain(b, carry):
            runs(b, lambda cp: cp.wait())
            return carry

        lax.fori_loop(0, N_BUCKETS, issue, 0)
        lax.fori_loop(0, N_BUCKETS, drain, 0)

        def tile_copy(j):
            return pltpu.make_async_copy(zbuf, xs_hbm.at[pl.ds(pl.multiple_of(j * MOE_TR, MOE_TR), MOE_TR)],
                                         zsem.at[0])

        n_tiles = xs_hbm.shape[0] // MOE_TR
        lax.fori_loop(nt_ref[0], n_tiles, lambda j, c: (tile_copy(j).start(), c)[1], 0)
        lax.fori_loop(nt_ref[0], n_tiles, lambda j, c: (tile_copy(j).wait(), c)[1], 0)


def _dispatch(pad_start, pad_count, n_used, pos3, rows, n_slots):
    t = rows.shape[0]
    tm = RANK_TM
    grid_spec = pltpu.PrefetchScalarGridSpec(
        num_scalar_prefetch=3,
        grid=(t // tm,),
        in_specs=[
            pl.BlockSpec((None, 1, tm), lambda i, ps, pc, nt: (i, 0, 0), memory_space=pltpu.SMEM),
            pl.BlockSpec((tm, ROW_W), lambda i, ps, pc, nt: (i, 0)),
        ],
        out_specs=pl.BlockSpec(memory_space=pl.ANY),
        scratch_shapes=[
            pltpu.VMEM((MOE_TR, ROW_W), F32),
            pltpu.SemaphoreType.DMA((1,)),
            pltpu.SemaphoreType.DMA((1,)),
        ],
    )
    return pl.pallas_call(
        _dispatch_kernel,
        grid_spec=grid_spec,
        out_shape=jax.ShapeDtypeStruct((n_slots, ROW_W), F32),
        compiler_params=_cparams(("arbitrary",)),
        name="moe_dispatch",
    )(pad_start, pad_count, n_used, pos3, rows)


def _gmoe_kernel(ea_ref, eb_ref, nt_ref, xs_ref, wga_ref, wua_ref, wda_ref, wgb_ref, wub_ref, wdb_ref, ys_ref):
    i = pl.program_id(0)

    @pl.when(i < nt_ref[0])
    def _():
        x = xs_ref[...]
        h = x[:, 0:D].astype(BF16)
        comb = x[:, D:D + ROUTE_W]
        lane = lax.broadcasted_iota(jnp.int32, comb.shape, 1)
        acc = None
        for e_ref, wg_ref, wu_ref, wd_ref in ((ea_ref, wga_ref, wua_ref, wda_ref), (eb_ref, wgb_ref, wub_ref, wdb_ref)):
            cw = jnp.sum(jnp.where(lane == N_GROUPS + e_ref[i], comb, 0.0), axis=-1, keepdims=True)
            act = _silu(_dot(h, wg_ref[...])) * _dot(h, wu_ref[...])
            part = _dot((act * cw).astype(BF16), wd_ref[...])
            acc = part if acc is None else acc + part
        ys_ref[...] = acc

    @pl.when(i >= nt_ref[0])
    def _():
        ys_ref[...] = jnp.zeros_like(ys_ref)


def _gmoe(exp_a, exp_b, n_used, xs, wg, wu, wd):
    n_tiles = xs.shape[0] // MOE_TR
    tr = MOE_TR
    wspec = lambda shape, which: pl.BlockSpec(
        (None,) + shape, lambda i, ea, eb, nt: ((ea, eb)[which][i], 0, 0))
    grid_spec = pltpu.PrefetchScalarGridSpec(
        num_scalar_prefetch=3,
        grid=(n_tiles,),
        in_specs=[
            pl.BlockSpec((tr, ROW_W), lambda i, ea, eb, nt: (jnp.where(i < nt[0], i, 0), 0)),
            wspec((D, D_FF), 0), wspec((D, D_FF), 0), wspec((D_FF, D), 0),
            wspec((D, D_FF), 1), wspec((D, D_FF), 1), wspec((D_FF, D), 1),
        ],
        out_specs=pl.BlockSpec((tr, D), lambda i, ea, eb, nt: (i, 0)),
    )
    return pl.pallas_call(
        _gmoe_kernel,
        grid_spec=grid_spec,
        out_shape=jax.ShapeDtypeStruct((n_tiles * tr, D), F32),
        compiler_params=_cparams(("arbitrary",)),
        name="moe_grouped",
    )(exp_a, exp_b, n_used, xs, wg, wu, wd, wg, wu, wd)


def _combine_kernel(cur_ref, nxt_ref, x1_ref, ys_hbm, y_ref, buf, sem):
    i = pl.program_id(0)
    slot = i % 2
    tm = x1_ref.shape[0]

    def start_gather(idx_ref, s):
        for r in range(tm):
            _row_copy(ys_hbm, buf.at[s], idx_ref[0, r], r, sem.at[s]).start()

    @pl.when(i == 0)
    def _():
        start_gather(cur_ref, 0)

    @pl.when(i + 1 < pl.num_programs(0))
    def _():
        start_gather(nxt_ref, 1 - slot)

    pltpu.make_async_copy(ys_hbm.at[pl.ds(0, tm)], buf.at[slot], sem.at[slot]).wait()
    y_ref[...] = x1_ref[...] + buf[slot]


def _combine(pos3, x1, ys):
    t = x1.shape[0]
    tm = pos3.shape[2]
    nt = t // tm
    idx = lambda f: pl.BlockSpec((None, 1, tm), f, memory_space=pltpu.SMEM)
    return pl.pallas_call(
        _combine_kernel,
        grid=(nt,),
        in_specs=[
            idx(lambda i: (i, 0, 0)),
            idx(lambda i: (jnp.minimum(i + 1, nt - 1), 0, 0)),
            pl.BlockSpec((tm, D), lambda i: (i, 0)),
            pl.BlockSpec(memory_space=pl.ANY),
        ],
        out_specs=pl.BlockSpec((tm, D), lambda i: (i, 0)),
        out_shape=jax.ShapeDtypeStruct((t, D), F32),
        scratch_shapes=[pltpu.VMEM((2, tm, D), F32), pltpu.SemaphoreType.DMA((2,))],
        compiler_params=_cparams(("arbitrary",)),
        name="moe_combine",
    )(pos3, pos3, x1, ys)


def _moe_routed(x1, rows, comb, wg, wu, wd):
    t = x1.shape[0]
    tr = MOE_TR
    n_tiles = (t + N_BUCKETS * (tr - 1)) // tr
    pos3, cnt = _rank(comb)
    pos = pos3[:, 0, :]
    count = cnt[0, :N_BUCKETS].astype(jnp.int32)
    padded = (count + tr - 1) // tr * tr
    ends = jnp.cumsum(padded)
    pad_start = ends - padded + count
    pad_count = padded - count
    n_used = (ends[-1:] // tr).astype(jnp.int32)
    tile_start = jnp.arange(n_tiles, dtype=jnp.int32) * tr
    tile_bucket = jnp.minimum(jnp.sum(tile_start[:, None] >= ends[None, :], axis=1), N_BUCKETS - 1)
    first, second = _bucket_experts()
    xs = _dispatch(pad_start, pad_count, n_used, pos.reshape(t // RANK_TM, 1, RANK_TM), rows, n_tiles * tr)
    ys = _gmoe(first[tile_bucket], second[tile_bucket], n_used, xs, wg, wu, wd)
    return _combine(pos.reshape(t // COMBINE_TM, 1, COMBINE_TM), x1, ys)


def _moe_kernel(x1_ref, h_ref, comb_ref, wg_ref, wu_ref, wd_ref, o_ref):
    e = pl.program_id(1)

    @pl.when(e == 0)
    def _():
        o_ref[...] = x1_ref[...]

    h = h_ref[...]
    lane = lax.broadcasted_iota(jnp.int32, comb_ref.shape, 1)
    cw = jnp.sum(jnp.where(lane == e + N_GROUPS, comb_ref[...], 0.0), axis=-1, keepdims=True)
    act = _silu(_dot(h, wg_ref[0])) * _dot(h, wu_ref[0])
    o_ref[...] += _dot((act * cw).astype(BF16), wd_ref[0])


def _moe(x1, h, comb, wg, wu, wd, tm):
    t = x1.shape[0]
    return pl.pallas_call(
        _moe_kernel,
        grid=(t // tm, N_EXPERTS),
        in_specs=[
            pl.BlockSpec((tm, D), lambda i, e: (i, 0)),
            pl.BlockSpec((tm, D), lambda i, e: (i, 0)),
            pl.BlockSpec((tm, ROUTE_W), lambda i, e: (i, 0)),
            pl.BlockSpec((1, D, D_FF), lambda i, e: (e, 0, 0)),
            pl.BlockSpec((1, D, D_FF), lambda i, e: (e, 0, 0)),
            pl.BlockSpec((1, D_FF, D), lambda i, e: (e, 0, 0)),
        ],
        out_specs=pl.BlockSpec((tm, D), lambda i, e: (i, 0)),
        out_shape=jax.ShapeDtypeStruct((t, D), F32),
        compiler_params=_cparams(("parallel", "arbitrary")),
        name="moe",
    )(x1, h, comb, wg, wu, wd)


def _sample_prep_kernel(xbc_ref, q_ref, k_ref, mq_ref, dt_ref, sc_ref, cw_ref, cb_ref, dtb_ref, a_ref, e_ref,
                        qg_ref, kg_ref, mqg_ref, cos_ref, slo_ref, shi_ref,
                        conv_ref, xs_ref, bc_ref, dtx_ref, dec_ref, qn_ref, kn_ref, mqn_ref):
    n = xbc_ref.shape[0]
    x_raw = xbc_ref[...].astype(F32)
    acc = x_raw * cw_ref[SSM_CONV - 1:SSM_CONV, :] + cb_ref[...]
    for j in range(SSM_CONV - 1):
        acc = acc + sc_ref[:, j * CONV_CH:(j + 1) * CONV_CH] * cw_ref[j:j + 1, :]
    conv_ref[:, 0:(SSM_CONV - 2) * CONV_CH] = sc_ref[:, CONV_CH:(SSM_CONV - 1) * CONV_CH]
    conv_ref[:, (SSM_CONV - 2) * CONV_CH:] = x_raw
    act = _silu(acc)
    xs = act[:, 0:SSM_INNER]
    xs_ref[...] = xs
    bc_ref[...] = act[:, SSM_INNER:]
    dt = _softplus(dt_ref[...] + dtb_ref[...])
    e = e_ref[...]
    dtx = _dot_exact(dt, e) * xs
    dec = _dot_exact(jnp.exp(dt * a_ref[...]), e)
    dtx_ref[...] = dtx.T
    dec_ref[...] = dec.T

    lane = lax.broadcasted_iota(jnp.int32, (n, LANE), 1)
    low_half = lane < ATT_HD
    cos_t = jnp.broadcast_to(cos_ref[...], (n, LANE))
    sin_lo = jnp.broadcast_to(slo_ref[...], (n, LANE))
    sin_hi = jnp.broadcast_to(shi_ref[...], (n, LANE))
    qf = q_ref[...].astype(F32)
    for j in range(D // LANE):
        sl = slice(j * LANE, (j + 1) * LANE)
        qn_ref[:, sl] = _norm_rope_tile(qf[:, sl], qg_ref[...], cos_t, sin_lo, sin_hi, low_half)
    kf = k_ref[...].astype(F32)
    for j in range(ATT_KVD // LANE):
        sl = slice(j * LANE, (j + 1) * LANE)
        kn_ref[:, sl] = _norm_rope_tile(kf[:, sl], kg_ref[...], cos_t, sin_lo, sin_hi, low_half)
    mq = mq_ref[...].astype(F32)
    for h in range(MEM_HEADS):
        sl = slice(h * MEM_HD, (h + 1) * MEM_HD)
        mh = mq[:, sl]
        ms = jnp.mean(mh * mh, axis=-1, keepdims=True)
        mqn_ref[:, sl] = mh * lax.rsqrt(ms + EPS) * mqg_ref[...]


def _sample_prep(proj, dt_raw, sc2d, cw, cb, dtb, a, e, qg2, kg2, mqg, cos_t, sin_lo, sin_hi):
    n = proj.shape[0]
    c2 = lambda i: (0, 0)
    full = lambda shape: pl.BlockSpec(shape, c2)
    return pl.pallas_call(
        _sample_prep_kernel,
        grid=(1,),
        in_specs=[
            pl.BlockSpec((n, CONV_CH), lambda i: (0, COL_XBC_BLK)),
            pl.BlockSpec((n, D), lambda i: (0, COL_Q_BLK)),
            pl.BlockSpec((n, ATT_KVD), lambda i: (0, COL_K_BLK)),
            pl.BlockSpec((n, D), lambda i: (0, COL_MQ_BLK)),
            full((n, LANE)),
            full((n, (SSM_CONV - 1) * CONV_CH)),
            full((SSM_CONV, CONV_CH)),
            full((1, CONV_CH)),
            full((1, LANE)),
            full((1, LANE)),
            full((LANE, SSM_INNER)),
            full((1, LANE)),
            full((1, LANE)),
            full((1, MEM_HD)),
            full((1, LANE)),
            full((1, LANE)),
            full((1, LANE)),
        ],
        out_specs=[
            full((n, (SSM_CONV - 1) * CONV_CH)),
            full((n, SSM_INNER)),
            full((n, 2 * SSM_GROUPS * SSM_STATE)),
            full((SSM_INNER, n)),
            full((SSM_INNER, n)),
            full((n, D)),
            full((n, ATT_KVD)),
            full((n, D)),
        ],
        out_shape=[
            jax.ShapeDtypeStruct((n, (SSM_CONV - 1) * CONV_CH), F32),
            jax.ShapeDtypeStruct((n, SSM_INNER), F32),
            jax.ShapeDtypeStruct((n, 2 * SSM_GROUPS * SSM_STATE), F32),
            jax.ShapeDtypeStruct((SSM_INNER, n), F32),
            jax.ShapeDtypeStruct((SSM_INNER, n), F32),
            jax.ShapeDtypeStruct((n, D), F32),
            jax.ShapeDtypeStruct((n, ATT_KVD), F32),
            jax.ShapeDtypeStruct((n, D), F32),
        ],
        compiler_params=_cparams(("arbitrary",)),
        name="sample_prep",
    )(proj, proj, proj, proj, dt_raw, sc2d, cw, cb, dtb, a, e, qg2, kg2, mqg, cos_t, sin_lo, sin_hi)


SSM_S_TILE = 8


def _sample_ssm_kernel(st_ref, dtx_ref, dec_ref, bc_ref, so_ref, yt_ref):
    i = pl.program_id(0)
    n = dtx_ref.shape[1]

    @pl.when(i == 0)
    def _():
        yt_ref[...] = jnp.zeros_like(yt_ref)

    gw = SSM_HPG * SSM_HEADDIM
    lane = lax.broadcasted_iota(jnp.int32, (gw, n), 1)
    nbc = SSM_GROUPS * SSM_STATE
    for s in range(SSM_S_TILE):
        sel = lane == (i * SSM_S_TILE + s)
        for g in range(SSM_GROUPS):
            rows = slice(g * gw, (g + 1) * gw)
            dtx_c = jnp.sum(jnp.where(sel, dtx_ref[rows, :], 0.0), axis=1, keepdims=True)
            dec_c = jnp.sum(jnp.where(sel, dec_ref[rows, :], 0.0), axis=1, keepdims=True)
            bm = bc_ref[s:s + 1, g * SSM_STATE:(g + 1) * SSM_STATE]
            cm = bc_ref[s:s + 1, nbc + g * SSM_STATE:nbc + (g + 1) * SSM_STATE]
            hn = st_ref[s, rows, :] * dec_c + dtx_c * bm
            so_ref[s, rows, :] = hn
            yc = jnp.sum(hn * cm, axis=1, keepdims=True)
            yt_ref[rows, :] = jnp.where(sel, yc, yt_ref[rows, :])


def _sample_ssm(state, dtx_t, dec_t, bc):
    n = state.shape[0]
    return pl.pallas_call(
        _sample_ssm_kernel,
        grid=(n // SSM_S_TILE,),
        in_specs=[
            pl.BlockSpec((SSM_S_TILE, SSM_INNER, SSM_STATE), lambda i: (i, 0, 0)),
            pl.BlockSpec((SSM_INNER, n), lambda i: (0, 0)),
            pl.BlockSpec((SSM_INNER, n), lambda i: (0, 0)),
            pl.BlockSpec((SSM_S_TILE, 2 * SSM_GROUPS * SSM_STATE), lambda i: (i, 0)),
        ],
        out_specs=[
            pl.BlockSpec((SSM_S_TILE, SSM_INNER, SSM_STATE), lambda i: (i, 0, 0)),
            pl.BlockSpec((SSM_INNER, n), lambda i: (0, 0)),
        ],
        out_shape=[
            jax.ShapeDtypeStruct((n, SSM_INNER, SSM_STATE), F32),
            jax.ShapeDtypeStruct((SSM_INNER, n), F32),
        ],
        compiler_params=_cparams(("arbitrary",)),
        name="sample_ssm",
    )(state, dtx_t, dec_t, bc)


def _sample_post_kernel(yt_ref, xs_ref, z_ref, dexp_ref, ng_ref, y_ref):
    y = yt_ref[...].T + dexp_ref[...] * xs_ref[...]
    yg = y * _silu(z_ref[...].astype(F32))
    ms = jnp.mean(yg * yg, axis=-1, keepdims=True)
    y_ref[...] = yg * lax.rsqrt(ms + EPS) * ng_ref[...]


def _sample_post(y_t, xs, proj, dexp, ng):
    n = xs.shape[0]
    c2 = lambda i: (0, 0)
    return pl.pallas_call(
        _sample_post_kernel,
        grid=(1,),
        in_specs=[
            pl.BlockSpec((SSM_INNER, n), c2),
            pl.BlockSpec((n, SSM_INNER), c2),
            pl.BlockSpec((n, SSM_INNER), lambda i: (0, COL_Z_BLK)),
            pl.BlockSpec((1, SSM_INNER), c2),
            pl.BlockSpec((1, SSM_INNER), c2),
        ],
        out_specs=pl.BlockSpec((n, SSM_INNER), c2),
        out_shape=jax.ShapeDtypeStruct((n, SSM_INNER), F32),
        compiler_params=_cparams(("arbitrary",)),
        name="sample_post",
    )(y_t, xs, proj, dexp, ng)


SWA_S_TILE = 8


def _sample_swa_kernel(q_ref, kn_ref, v_ref, ck_ref, cv_ref, sink_ref, y_ref, ko_ref, vo_ref):
    w = WINDOW
    scale = ATT_HD ** -0.5
    newest = lax.broadcasted_iota(jnp.int32, (ATT_HD, w), 1) == w - 1
    for s in range(SWA_S_TILE):
        for kv in range(ATT_KV):
            kt = jnp.where(newest, kn_ref[s, kv], pltpu.roll(ck_ref[s, kv], w - 1, axis=1))
            vt = jnp.where(newest, v_ref[s, kv], pltpu.roll(cv_ref[s, kv], w - 1, axis=1))
            ko_ref[s, kv] = kt
            vo_ref[s, kv] = vt
            sc = _dot(q_ref[s, kv].astype(BF16), kt.astype(BF16)) * scale
            snk = sink_ref[kv]
            m = jnp.maximum(jnp.max(sc, axis=-1, keepdims=True), snk)
            p = jnp.exp(sc - m)
            den = jnp.sum(p, axis=-1, keepdims=True) + jnp.exp(snk - m)
            y_ref[s, kv] = _dot_nt(p.astype(BF16), vt.astype(BF16)) / den


def _sample_swa(q4, kn4, v4, ck_t, cv_t, sink3, l):
    n = q4.shape[0]
    w = WINDOW
    st = SWA_S_TILE
    grp = ATT_HEADS // ATT_KV
    cache = pl.BlockSpec((None, st, ATT_KV, ATT_HD, w), lambda i: (l, i, 0, 0, 0))
    new = pl.BlockSpec((st, ATT_KV, ATT_HD, 1), lambda i: (i, 0, 0, 0))
    out = pl.BlockSpec((st, ATT_KV, ATT_HD, w), lambda i: (i, 0, 0, 0))
    return pl.pallas_call(
        _sample_swa_kernel,
        grid=(n // st,),
        in_specs=[
            pl.BlockSpec((st, ATT_KV, grp, ATT_HD), lambda i: (i, 0, 0, 0)),
            new, new, cache, cache,
            pl.BlockSpec((ATT_KV, grp, 1), lambda i: (0, 0, 0)),
        ],
        out_specs=[pl.BlockSpec((st, ATT_KV, grp, ATT_HD), lambda i: (i, 0, 0, 0)), out, out],
        out_shape=[
            jax.ShapeDtypeStruct((n, ATT_KV, grp, ATT_HD), F32),
            jax.ShapeDtypeStruct((n, ATT_KV, ATT_HD, w), F32),
            jax.ShapeDtypeStruct((n, ATT_KV, ATT_HD, w), F32),
        ],
        compiler_params=_cparams(("parallel",)),
        name="sample_swa",
    )(q4, kn4, v4, ck_t, cv_t, sink3)


MEM_S_TILE = 4


def _sample_mem_kernel(q_ref, k_ref, v_ref, y_ref):
    scale = MEM_HD ** -0.5
    for s in range(MEM_S_TILE):
        sc = jnp.sum(k_ref[s] * q_ref[s][None], axis=-1, keepdims=True) * scale
        m = jnp.max(sc, axis=0, keepdims=True)
        p = jnp.exp(sc - m)
        den = jnp.sum(p, axis=0, keepdims=True)
        o = jnp.sum(p * v_ref[s], axis=0, keepdims=True) / den
        y_ref[s] = o[0]


def _sample_mem(q3, ck, cv, l):
    n = q3.shape[0]
    mem_len = ck.shape[2]
    st = MEM_S_TILE
    cache = pl.BlockSpec((None, st, mem_len, MEM_HEADS, MEM_HD), lambda i: (l, i, 0, 0, 0))
    return pl.pallas_call(
        _sample_mem_kernel,
        grid=(n // st,),
        in_specs=[pl.BlockSpec((st, MEM_HEADS, MEM_HD), lambda i: (i, 0, 0)), cache, cache],
        out_specs=pl.BlockSpec((st, MEM_HEADS, MEM_HD), lambda i: (i, 0, 0)),
        out_shape=jax.ShapeDtypeStruct((n, MEM_HEADS, MEM_HD), F32),
        compiler_params=_cparams(("parallel",)),
        name="sample_mem",
    )(q3, ck, cv)


def _prep_weights(lw):
    w_in = lw['w_in']
    cols = [w_in[:, 0:OFF_Z], w_in[:, OFF_XBC:OFF_DT], w_in[:, OFF_Z:OFF_XBC], w_in[:, OFF_Q:OFF_K],
            w_in[:, OFF_MQ:OFF_MQ + D], w_in[:, OFF_K:OFF_V], w_in[:, OFF_V:OFF_MQ]]
    p = {}
    p['w_main'] = jnp.concatenate(cols, axis=1).astype(BF16)
    p['w_dt'] = jnp.pad(w_in[:, OFF_DT:OFF_Q], ((0, 0), (0, LANE - SSM_HEADS))).astype(BF16)
    p['norm1_g'] = lw['norm1_g'].reshape(1, D)
    p['cw'] = lw['ssm_conv_w']
    p['cb'] = lw['ssm_conv_b'].reshape(1, CONV_CH)
    pad_h = (0, LANE - SSM_HEADS)
    p['dtb'] = jnp.pad(lw['ssm_dt_bias'].astype(F32), pad_h).reshape(1, LANE)
    p['a'] = jnp.pad(-jnp.exp(lw['ssm_a_log'].astype(F32)), pad_h).reshape(1, LANE)
    p['dexp'] = jnp.repeat(lw['ssm_d'], SSM_HEADDIM).reshape(1, SSM_INNER)
    p['ssm_ng'] = lw['ssm_norm_g'].reshape(1, SSM_INNER)
    head_of_ch = jnp.arange(SSM_INNER) // SSM_HEADDIM
    p['e'] = (jnp.arange(LANE)[:, None] == head_of_ch[None, :]).astype(F32)
    p['qg2'] = jnp.tile(lw['att_q_norm_g'], 2).reshape(1, LANE)
    p['kg2'] = jnp.tile(lw['att_k_norm_g'], 2).reshape(1, LANE)
    p['sink_row'] = lw['att_sink'].astype(F32).reshape(1, ATT_HEADS)
    p['sink3'] = lw['att_sink'].astype(F32).reshape(ATT_KV, ATT_HEADS // ATT_KV, 1)
    p['mem_g'] = lw['mem_norm_g'].reshape(1, D)
    p['w_mem_kv'] = lw['w_mem_kv'].astype(BF16)
    p['mqg'] = lw['mem_q_norm_g'].reshape(1, MEM_HD)
    p['mkg'] = lw['mem_k_norm_g'].reshape(1, MEM_HD)
    p['ws'] = lw['w_br_ssm'].astype(BF16)
    p['wa'] = lw['w_br_swa'].astype(BF16)
    p['wm'] = lw['w_br_mem'].astype(BF16)
    p['wo'] = lw['w_out'].astype(BF16)
    p['g2'] = lw['norm2_g'].reshape(1, D)
    pad_r = ROUTE_W - N_GROUPS - N_EXPERTS
    p['wr'] = jnp.pad(jnp.concatenate([lw['w_router_group'], lw['w_router_expert']], axis=1).astype(F32),
                      ((0, 0), (0, pad_r)))
    p['br'] = jnp.pad(jnp.concatenate([lw['b_router_group'], lw['b_router_expert']]).astype(F32),
                      (0, pad_r)).reshape(1, ROUTE_W)
    p['wg'] = lw['w_exp_gate'].astype(BF16)
    p['wu'] = lw['w_exp_up'].astype(BF16)
    p['wd'] = lw['w_exp_down'].astype(BF16)
    return p


def _pick(n, prefs):
    for c in prefs:
        if n % c == 0:
            return c
    return n


def _tail(x, proj, y_ssm, y_swa, y_mem, p):
    t = x.shape[0]
    routed = t % RANK_TM == 0
    x1, h, comb = _back(x, proj, y_ssm, y_swa, y_mem, p['ws'], p['wa'], p['wm'], p['wo'], p['g2'],
                        p['wr'], p['br'], _pick(t, (512, 256, 128)), routed)
    if routed:
        return _moe_routed(x1, h, comb, p['wg'], p['wu'], p['wd'])
    return _moe(x1, h, comb, p['wg'], p['wu'], p['wd'], _pick(t, (1024, 512, 256, 128)))


def _prompt_layer(x, mem, p):
    nb, seq, _ = x.shape
    t = nb * seq
    nc = seq // CHUNK
    xf = x.reshape(t, D)
    proj, dt_raw = _front(xf, p['norm1_g'], p['w_main'], p['w_dt'], _pick(t, (1024, 512, 256, 128)), 1536)
    y_ssm, conv_new, ssm_new = _ssd(proj, dt_raw, p['cw'], p['cb'], p['dtb'], p['a'], p['dexp'],
                                    p['ssm_ng'], nb, nc)
    cos_t, sin_lo, sin_hi = _rope_tables(jnp.arange(seq))
    y_swa, k_new, v_new = _swa(proj, p['qg2'], p['kg2'], cos_t, sin_lo, sin_hi, p['sink_row'], nb, seq)
    mem_len = mem.shape[1]
    mk, mv = _memkv(mem.reshape(nb * mem_len, D), p['mem_g'], p['w_mem_kv'], p['mkg'],
                    _pick(nb * mem_len, (512, 256, 128)))
    y_mem = _memattn(proj, mk, mv, p['mqg'], nb, seq, mem_len, _pick(seq, (512, 256, 128)))
    y = _tail(xf, proj, y_ssm, y_swa, y_mem, p)
    return (y.reshape(nb, seq, D), conv_new,
            ssm_new.reshape(nb, SSM_HEADS, SSM_HEADDIM, SSM_STATE),
            k_new.reshape(nb, WINDOW, ATT_KV, ATT_HD), v_new.reshape(nb, WINDOW, ATT_KV, ATT_HD),
            mk.reshape(nb, mem_len, MEM_HEADS, MEM_HD), mv.reshape(nb, mem_len, MEM_HEADS, MEM_HD))


def _sample_layer(x, conv_st, ssm_st, swa_k, swa_v, mem_k, mem_v, l, p):
    n = x.shape[0]
    xf = x.reshape(n, D)
    proj, dt_raw = _front(xf, p['norm1_g'], p['w_main'], p['w_dt'], n, 1536)
    cos_t, sin_lo, sin_hi = _rope_tables(jnp.full((1,), PAST_LEN, jnp.int32))
    sc2d = conv_st.reshape(n, (SSM_CONV - 1) * CONV_CH)
    conv_new, xs, bc, dtx_t, dec_t, qn, kn, mqn = _sample_prep(
        proj, dt_raw, sc2d, p['cw'], p['cb'], p['dtb'], p['a'], p['e'], p['qg2'], p['kg2'], p['mqg'],
        cos_t, sin_lo, sin_hi)
    ssm_new, y_t = _sample_ssm(ssm_st.reshape(n, SSM_INNER, SSM_STATE), dtx_t, dec_t, bc)
    y_ssm = _sample_post(y_t, xs, proj, p['dexp'], p['ssm_ng'])
    grp = ATT_HEADS // ATT_KV
    v_raw = proj[:, COL_V_BLK * ATT_KVD:(COL_V_BLK + 1) * ATT_KVD].astype(F32)
    to_t = (0, 1, 3, 4, 2)
    y_swa, k_new, v_new = _sample_swa(
        qn.reshape(n, ATT_KV, grp, ATT_HD), kn.reshape(n, ATT_KV, ATT_HD, 1),
        v_raw.reshape(n, ATT_KV, ATT_HD, 1), swa_k.transpose(to_t), swa_v.transpose(to_t), p['sink3'], l)
    y_swa = y_swa.reshape(n, D)
    k_new = k_new.transpose(0, 3, 1, 2)
    v_new = v_new.transpose(0, 3, 1, 2)
    y_mem = _sample_mem(mqn.reshape(n, MEM_HEADS, MEM_HD), mem_k, mem_v, l)
    y = _tail(xf, proj, y_ssm, y_swa.astype(BF16), y_mem.reshape(n, D).astype(BF16), p)
    return (y.reshape(n, 1, D), conv_new.reshape(n, SSM_CONV - 1, CONV_CH),
            ssm_new.reshape(n, SSM_HEADS, SSM_HEADDIM, SSM_STATE), k_new, v_new)


def kernel(x_prompt, x_sample, state_conv, state_ssm, cache_swa_k, cache_swa_v, cache_mem_k, cache_mem_v,
           mem_prompt, norm1_g, w_in, ssm_conv_w, ssm_conv_b, ssm_dt_bias, ssm_a_log, ssm_d, ssm_norm_g,
           att_q_norm_g, att_k_norm_g, att_sink, mem_norm_g, w_mem_kv, mem_q_norm_g, mem_k_norm_g,
           w_br_ssm, w_br_swa, w_br_mem, w_out, norm2_g, w_router_group, b_router_group,
           w_router_expert, b_router_expert, w_exp_gate, w_exp_up, w_exp_down):
    weights = dict(norm1_g=norm1_g, w_in=w_in, ssm_conv_w=ssm_conv_w, ssm_conv_b=ssm_conv_b,
                   ssm_dt_bias=ssm_dt_bias, ssm_a_log=ssm_a_log, ssm_d=ssm_d, ssm_norm_g=ssm_norm_g,
                   att_q_norm_g=att_q_norm_g, att_k_norm_g=att_k_norm_g, att_sink=att_sink,
                   mem_norm_g=mem_norm_g, w_mem_kv=w_mem_kv, mem_q_norm_g=mem_q_norm_g,
                   mem_k_norm_g=mem_k_norm_g, w_br_ssm=w_br_ssm, w_br_swa=w_br_swa, w_br_mem=w_br_mem,
                   w_out=w_out, norm2_g=norm2_g, w_router_group=w_router_group,
                   b_router_group=b_router_group, w_router_expert=w_router_expert,
                   b_router_expert=b_router_expert, w_exp_gate=w_exp_gate, w_exp_up=w_exp_up,
                   w_exp_down=w_exp_down)
    depth = w_in.shape[0]
    xp, xs = x_prompt, x_sample
    outs = [[] for _ in range(10)]
    for l in range(depth):
        p = _prep_weights({k: v[l] for k, v in weights.items()})
        xp, c1, c2, c3, c4, c5, c6 = _prompt_layer(xp, mem_prompt, p)
        xs, d1, d2, d3, d4 = _sample_layer(xs, state_conv[l], state_ssm[l], cache_swa_k, cache_swa_v,
                                           cache_mem_k, cache_mem_v, l, p)
        for lst, val in zip(outs, (c1, c2, c3, c4, c5, c6, d1, d2, d3, d4)):
            lst.append(val)
    return (xp, xs) + tuple(jnp.stack(o) for o in outs)
```

```python
import functools
import math

import jax
import jax.numpy as jnp
from jax import lax
from jax.experimental import pallas as pl
from jax.experimental.pallas import tpu as pltpu

F32 = jnp.float32
BF16 = jnp.bfloat16
HIGHEST = lax.Precision.HIGHEST

D = 1024
SSM_INNER = 2048
SSM_HEADDIM = 64
SSM_HEADS = 32
SSM_GROUPS = 4
SSM_HPG = SSM_HEADS // SSM_GROUPS
SSM_STATE = 128
SSM_CONV = 4
CONV_CH = SSM_INNER + 2 * SSM_GROUPS * SSM_STATE
CHUNK = 128
ATT_HEADS = 16
ATT_KV = 4
ATT_HD = 64
ATT_KVD = ATT_KV * ATT_HD
WINDOW = 128
ROPE_THETA = 10000.0
MEM_HEADS = 4
MEM_HD = 256
N_EXPERTS = 32
N_GROUPS = 4
EPG = 8
D_FF = 256
EPS = 1e-6
PAST_LEN = 16384

OFF_Z = 3 * D
OFF_XBC = OFF_Z + SSM_INNER
OFF_DT = OFF_XBC + CONV_CH
OFF_Q = OFF_DT + SSM_HEADS
OFF_K = OFF_Q + D
OFF_V = OFF_K + ATT_KVD
OFF_MQ = OFF_V + ATT_KVD

N_MAIN = 3 * D + CONV_CH + SSM_INNER + D + D + 2 * ATT_KVD
COL_XBC_BLK = 1
COL_Z_BLK = 3
COL_Q_BLK = 8
COL_MQ_BLK = 9
COL_K_BLK = 40
COL_V_BLK = 41
LANE = 128
ROUTE_W = 128

VMEM_LIMIT = 56 * 1024 * 1024


def _cparams(sem):
    return pltpu.CompilerParams(dimension_semantics=sem, vmem_limit_bytes=VMEM_LIMIT)


def _sigmoid(x):
    return 1.0 / (1.0 + jnp.exp(-x))


def _silu(x):
    return x * _sigmoid(x)


def _softplus(x):
    return jnp.maximum(x, 0.0) + jnp.log1p(jnp.exp(-jnp.abs(x)))


def _dot(a, b):
    return jnp.dot(a, b, preferred_element_type=F32)


def _dot_nt(a, b):
    return lax.dot_general(a, b, (((1,), (1,)), ((), ())), preferred_element_type=F32)


def _dot_tn(a, b):
    return lax.dot_general(a, b, (((0,), (0,)), ((), ())), preferred_element_type=F32)


def _dot_exact(a, b):
    return jnp.dot(a, b, preferred_element_type=F32, precision=HIGHEST)


def _front_kernel(x_ref, g_ref, w_ref, wdt_ref, o_ref, dt_ref, hn_ref):
    @pl.when(pl.program_id(1) == 0)
    def _():
        x = x_ref[...]
        ms = jnp.mean(x * x, axis=-1, keepdims=True)
        hn = (x * lax.rsqrt(ms + EPS) * g_ref[...]).astype(BF16)
        hn_ref[...] = hn
        dt_ref[...] = _dot(hn, wdt_ref[...])

    o_ref[...] = _dot(hn_ref[...], w_ref[...]).astype(BF16)


def _front(x, g, w_main, w_dt, tm, tn):
    t = x.shape[0]
    return pl.pallas_call(
        _front_kernel,
        grid=(t // tm, N_MAIN // tn),
        in_specs=[
            pl.BlockSpec((tm, D), lambda i, j: (i, 0)),
            pl.BlockSpec((1, D), lambda i, j: (0, 0)),
            pl.BlockSpec((D, tn), lambda i, j: (0, j)),
            pl.BlockSpec((D, LANE), lambda i, j: (0, 0)),
        ],
        out_specs=[
            pl.BlockSpec((tm, tn), lambda i, j: (i, j)),
            pl.BlockSpec((tm, LANE), lambda i, j: (i, 0)),
        ],
        out_shape=[
            jax.ShapeDtypeStruct((t, N_MAIN), BF16),
            jax.ShapeDtypeStruct((t, LANE), F32),
        ],
        scratch_shapes=[pltpu.VMEM((tm, D), BF16)],
        compiler_params=_cparams(("parallel", "arbitrary")),
        name="front",
    )(x, g, w_main, w_dt)


SUBLANES = 8


def _ssd_kernel(xbc_ref, z_ref, dt_ref, cw_ref, cb_ref, dtb_ref, a_ref, dexp_ref, ng_ref,
                y_ref, conv_ref, ssm_ref, prev_ref, h_ref, yacc_ref):
    c = pl.program_id(1)

    @pl.when(c == 0)
    def _():
        prev_ref[...] = jnp.zeros_like(prev_ref)
        h_ref[...] = jnp.zeros_like(h_ref)

    x_raw = xbc_ref[...].astype(F32)
    prev = prev_ref[...]
    row8 = lax.broadcasted_iota(jnp.int32, (SUBLANES, CONV_CH), 0)
    acc = x_raw * cw_ref[SSM_CONV - 1:SSM_CONV, :] + cb_ref[...]
    for s in range(1, SSM_CONV):
        xr = pltpu.roll(x_raw, s, axis=0)
        head = jnp.where(row8 < s, pltpu.roll(prev, s, axis=0), xr[0:SUBLANES])
        shifted = jnp.concatenate([head, xr[SUBLANES:]], axis=0)
        acc = acc + shifted * cw_ref[SSM_CONV - 1 - s:SSM_CONV - s, :]
    prev_ref[...] = x_raw[CHUNK - SUBLANES:CHUNK]
    conv_ref[0] = x_raw[CHUNK - (SSM_CONV - 1):CHUNK]
    act = _silu(acc)

    dt = _softplus(dt_ref[...] + dtb_ref[...])
    da = dt * a_ref[...]
    row = lax.broadcasted_iota(jnp.int32, (CHUNK, CHUNK), 0)
    col = lax.broadcasted_iota(jnp.int32, (CHUNK, CHUNK), 1)
    causal = row >= col
    acum = _dot_exact(causal.astype(F32), da)
    acum_t = acum.T
    dt_t = dt.T
    exp_a = jnp.exp(acum)
    last = acum[CHUNK - 1:CHUNK, :]
    w_end = jnp.exp(last - acum) * dt
    cd = jnp.broadcast_to(jnp.exp(acum_t[:, CHUNK - 1:CHUNK]), (LANE, SSM_STATE))
    lane = lax.broadcasted_iota(jnp.int32, (CHUNK, LANE), 1)
    low_half = lane < SSM_HEADDIM

    def pair_cols(per_head, hd):
        return jnp.where(low_half, per_head[:, hd:hd + 1], per_head[:, hd + 1:hd + 2])

    xs_off = 0
    b_off = SSM_INNER
    c_off = SSM_INNER + SSM_GROUPS * SSM_STATE
    for g in range(SSM_GROUPS):
        bm = act[:, b_off + g * SSM_STATE:b_off + (g + 1) * SSM_STATE].astype(BF16)
        cm = act[:, c_off + g * SSM_STATE:c_off + (g + 1) * SSM_STATE].astype(BF16)
        cb = _dot_nt(cm, bm)
        gw = SSM_HPG * SSM_HEADDIM
        ch0 = g * gw
        h_prev = h_ref[ch0:ch0 + gw, :]
        y_off = _dot_nt(cm, h_prev.astype(BF16))
        xg = act[:, xs_off + ch0:xs_off + ch0 + gw]
        xw = []
        for pr in range(SSM_HPG // 2):
            hd0 = g * SSM_HPG + pr * 2
            xp32 = xg[:, pr * LANE:(pr + 1) * LANE]
            xpair = xp32.astype(BF16)
            yd = []
            for sub in range(2):
                hd = hd0 + sub
                seg = acum[:, hd:hd + 1] - acum_t[hd:hd + 1, :]
                decay = jnp.exp(jnp.where(causal, seg, -jnp.inf))
                wts = cb * decay * dt_t[hd:hd + 1, :]
                yd.append(_dot(wts.astype(BF16), xpair))
            cl = ch0 + pr * LANE
            y_pair = (jnp.where(low_half, yd[0], yd[1])
                      + y_off[:, pr * LANE:(pr + 1) * LANE] * pair_cols(exp_a, hd0)
                      + dexp_ref[:, cl:cl + LANE] * xp32)
            yacc_ref[:, cl:cl + LANE] = y_pair
            xw.append((xp32 * pair_cols(w_end, hd0)).astype(BF16))
        states = _dot_tn(jnp.concatenate(xw, axis=1), bm)
        for r in range(SSM_HPG):
            hd = g * SSM_HPG + r
            r0 = ch0 + r * SSM_HEADDIM
            h_ref[r0:r0 + SSM_HEADDIM, :] = (h_ref[r0:r0 + SSM_HEADDIM, :] * cd[hd:hd + 1, :]
                                             + states[r * SSM_HEADDIM:(r + 1) * SSM_HEADDIM, :])

    ssm_ref[0] = h_ref[...]
    yg = yacc_ref[...] * _silu(z_ref[...].astype(F32))
    ms = jnp.mean(yg * yg, axis=-1, keepdims=True)
    y_ref[...] = (yg * lax.rsqrt(ms + EPS) * ng_ref[...]).astype(BF16)


def _ssd(proj, dt_raw, cw, cb, dtb, a, dexp, ng, nb, nc):
    t = proj.shape[0]
    return pl.pallas_call(
        _ssd_kernel,
        grid=(nb, nc),
        in_specs=[
            pl.BlockSpec((CHUNK, CONV_CH), lambda b, c: (b * nc + c, COL_XBC_BLK)),
            pl.BlockSpec((CHUNK, SSM_INNER), lambda b, c: (b * nc + c, COL_Z_BLK)),
            pl.BlockSpec((CHUNK, LANE), lambda b, c: (b * nc + c, 0)),
            pl.BlockSpec((SSM_CONV, CONV_CH), lambda b, c: (0, 0)),
            pl.BlockSpec((1, CONV_CH), lambda b, c: (0, 0)),
            pl.BlockSpec((1, LANE), lambda b, c: (0, 0)),
            pl.BlockSpec((1, LANE), lambda b, c: (0, 0)),
            pl.BlockSpec((1, SSM_INNER), lambda b, c: (0, 0)),
            pl.BlockSpec((1, SSM_INNER), lambda b, c: (0, 0)),
        ],
        out_specs=[
            pl.BlockSpec((CHUNK, SSM_INNER), lambda b, c: (b * nc + c, 0)),
            pl.BlockSpec((1, SSM_CONV - 1, CONV_CH), lambda b, c: (b, 0, 0)),
            pl.BlockSpec((1, SSM_INNER, SSM_STATE), lambda b, c: (b, 0, 0)),
        ],
        out_shape=[
            jax.ShapeDtypeStruct((t, SSM_INNER), BF16),
            jax.ShapeDtypeStruct((nb, SSM_CONV - 1, CONV_CH), F32),
            jax.ShapeDtypeStruct((nb, SSM_INNER, SSM_STATE), F32),
        ],
        scratch_shapes=[
            pltpu.VMEM((SUBLANES, CONV_CH), F32),
            pltpu.VMEM((SSM_INNER, SSM_STATE), F32),
            pltpu.VMEM((CHUNK, SSM_INNER), F32),
        ],
        compiler_params=_cparams(("parallel", "arbitrary")),
        name="ssd_prompt",
    )(proj, proj, dt_raw, cw, cb, dtb, a, dexp, ng)


def _norm_rope_tile(xj, g2, cos_t, sin_lo, sin_hi, low_half):
    sq = xj * xj
    s_lo = jnp.sum(jnp.where(low_half, sq, 0.0), axis=-1, keepdims=True)
    s_hi = jnp.sum(jnp.where(low_half, 0.0, sq), axis=-1, keepdims=True)
    ms = jnp.where(low_half, s_lo, s_hi) * (1.0 / ATT_HD)
    xn = xj * lax.rsqrt(ms + EPS) * g2
    half = ATT_HD // 2
    return (xn * cos_t + pltpu.roll(xn, LANE - half, axis=1) * sin_lo
            + pltpu.roll(xn, half, axis=1) * sin_hi)


def _rope_tables(pos):
    half = ATT_HD // 2
    inv = ROPE_THETA ** (-jnp.arange(half, dtype=F32) / half)
    ang = pos.astype(F32)[:, None] * inv[None, :]
    cos = jnp.cos(ang)
    sin = jnp.sin(ang)
    zero = jnp.zeros_like(sin)
    cos_t = jnp.concatenate([cos, cos, cos, cos], axis=1)
    sin_lo = jnp.concatenate([-sin, zero, -sin, zero], axis=1)
    sin_hi = jnp.concatenate([zero, sin, zero, sin], axis=1)
    return cos_t, sin_lo, sin_hi


SWA_NSB = 2


def _swa_kernel(q_ref, k_ref, v_ref, qg_ref, kg_ref, cos_ref, slo_ref, shi_ref, sink_ref,
                y_ref, ko_ref, vo_ref, kd_ref, vd_ref, qs_ref):
    c = pl.program_id(1)
    w = WINDOW
    nsb = SWA_NSB
    grp = ATT_HEADS // ATT_KV
    rows = nsb * w
    lane = lax.broadcasted_iota(jnp.int32, (rows, LANE), 1)
    low_half = lane < ATT_HD
    low_half_w = lax.broadcasted_iota(jnp.int32, (w, LANE), 1) < ATT_HD
    cos_t = cos_ref[...]
    sin_lo = slo_ref[...]
    sin_hi = shi_ref[...]

    @pl.when(c == 0)
    def _():
        kd_ref[:, 0:w, :] = jnp.zeros((ATT_KV, w, LANE), BF16)
        vd_ref[:, 0:w, :] = jnp.zeros((ATT_KV, w, 2 * LANE), BF16)
        vd_ref[:, :, LANE:2 * LANE] = jnp.ones((ATT_KV, (nsb + 1) * w, LANE), BF16)

    kf = k_ref[...].astype(F32)
    vf = v_ref[...].astype(F32)
    for j in range(ATT_KVD // LANE):
        sl = slice(j * LANE, (j + 1) * LANE)
        kn = _norm_rope_tile(kf[:, sl], kg_ref[...], cos_t, sin_lo, sin_hi, low_half)
        ko_ref[0, :, sl] = kn[rows - w:rows]
        kr = pltpu.roll(kn, ATT_HD, axis=1)
        vj = vf[:, sl]
        vr = pltpu.roll(vj, ATT_HD, axis=1)
        kd_ref[2 * j, w:w + rows, :] = jnp.where(low_half, kn, kr).astype(BF16)
        kd_ref[2 * j + 1, w:w + rows, :] = jnp.where(low_half, kr, kn).astype(BF16)
        vd_ref[2 * j, w:w + rows, 0:LANE] = jnp.where(low_half, vj, vr).astype(BF16)
        vd_ref[2 * j + 1, w:w + rows, 0:LANE] = jnp.where(low_half, vr, vj).astype(BF16)
    vo_ref[0] = vf[rows - w:rows]

    qf = q_ref[...].astype(F32)
    for j in range(D // LANE):
        sl = slice(j * LANE, (j + 1) * LANE)
        qn = _norm_rope_tile(qf[:, sl], qg_ref[...] * (ATT_HD ** -0.5), cos_t, sin_lo, sin_hi, low_half)
        for par in range(2):
            h = 2 * j + par
            r0 = (h % grp) * w
            in_half = low_half if par == 0 else jnp.logical_not(low_half)
            qh = jnp.where(in_half, qn, 0.0).astype(BF16)
            for sb in range(nsb):
                qs_ref[h // grp, sb, r0:r0 + w, :] = qh[sb * w:(sb + 1) * w]

    qi = lax.broadcasted_iota(jnp.int32, (grp * w, 2 * w), 0) & (w - 1)
    kj = lax.broadcasted_iota(jnp.int32, (grp * w, 2 * w), 1)
    in_cur = (kj >= w) & ((kj - w) <= qi)
    in_prev = (kj < w) & (kj > qi)

    for sb in range(nsb):
        mask = in_cur | (in_prev & (c > 0) if sb == 0 else in_prev)
        for kv in range(ATT_KV):
            snk = jnp.concatenate(
                [jnp.broadcast_to(sink_ref[:, kv * grp + i:kv * grp + i + 1], (w, 1)) for i in range(grp)], axis=0)
            s = _dot_nt(qs_ref[kv, sb], kd_ref[kv, sb * w:(sb + 2) * w, :])
            s = jnp.where(mask, s, -jnp.inf)
            m = jnp.maximum(jnp.max(s, axis=-1, keepdims=True), snk)
            p = jnp.exp(s - m)
            den = jnp.sum(p, axis=-1, keepdims=True) + jnp.exp(snk - m)
            o = _dot(p.astype(BF16), vd_ref[kv, sb * w:(sb + 2) * w, 0:LANE]) * (1.0 / den)
            for a in range(grp // 2):
                col = kv * (grp // 2) + a
                y_ref[sb * w:(sb + 1) * w, col * LANE:(col + 1) * LANE] = jnp.where(
                    low_half_w, o[2 * a * w:(2 * a + 1) * w, :], o[(2 * a + 1) * w:(2 * a + 2) * w, :]
                ).astype(BF16)

    kd_ref[:, 0:w, :] = kd_ref[:, rows:rows + w, :]
    vd_ref[:, 0:w, 0:LANE] = vd_ref[:, rows:rows + w, 0:LANE]


def _swa(proj, qg2, kg2, cos_t, sin_lo, sin_hi, sink, nb, seq):
    t = proj.shape[0]
    w = WINDOW
    rows = SWA_NSB * w
    nc = seq // rows
    grp = ATT_HEADS // ATT_KV
    tab = pl.BlockSpec((rows, LANE), lambda b, c: (c, 0))
    return pl.pallas_call(
        _swa_kernel,
        grid=(nb, nc),
        in_specs=[
            pl.BlockSpec((rows, D), lambda b, c: (b * nc + c, COL_Q_BLK)),
            pl.BlockSpec((rows, ATT_KVD), lambda b, c: (b * nc + c, COL_K_BLK)),
            pl.BlockSpec((rows, ATT_KVD), lambda b, c: (b * nc + c, COL_V_BLK)),
            pl.BlockSpec((1, LANE), lambda b, c: (0, 0)),
            pl.BlockSpec((1, LANE), lambda b, c: (0, 0)),
            tab, tab, tab,
            pl.BlockSpec((1, ATT_HEADS), lambda b, c: (0, 0)),
        ],
        out_specs=[
            pl.BlockSpec((rows, D), lambda b, c: (b * nc + c, 0)),
            pl.BlockSpec((1, w, ATT_KVD), lambda b, c: (b, 0, 0)),
            pl.BlockSpec((1, w, ATT_KVD), lambda b, c: (b, 0, 0)),
        ],
        out_shape=[
            jax.ShapeDtypeStruct((t, D), BF16),
            jax.ShapeDtypeStruct((nb, w, ATT_KVD), F32),
            jax.ShapeDtypeStruct((nb, w, ATT_KVD), F32),
        ],
        scratch_shapes=[
            pltpu.VMEM((ATT_KV, rows + w, LANE), BF16),
            pltpu.VMEM((ATT_KV, rows + w, 2 * LANE), BF16),
            pltpu.VMEM((ATT_KV, SWA_NSB, grp * w, LANE), BF16),
        ],
        compiler_params=_cparams(("parallel", "arbitrary")),
        name="swa_prompt",
    )(proj, proj, proj, qg2, kg2, cos_t, sin_lo, sin_hi, sink)


def _memkv_kernel(m_ref, g_ref, w_ref, kg_ref, k_ref, v_ref):
    x = m_ref[...]
    ms = jnp.mean(x * x, axis=-1, keepdims=True)
    hn = (x * lax.rsqrt(ms + EPS) * g_ref[...]).astype(BF16)
    kv = _dot(hn, w_ref[...])
    for h in range(MEM_HEADS):
        kh = kv[:, h * MEM_HD:(h + 1) * MEM_HD]
        ms = jnp.mean(kh * kh, axis=-1, keepdims=True)
        k_ref[:, h * MEM_HD:(h + 1) * MEM_HD] = kh * lax.rsqrt(ms + EPS) * kg_ref[...]
    v_ref[...] = kv[:, D:]


def _memkv(mem, g, w, kg, tm):
    t = mem.shape[0]
    return pl.pallas_call(
        _memkv_kernel,
        grid=(t // tm,),
        in_specs=[
            pl.BlockSpec((tm, D), lambda i: (i, 0)),
            pl.BlockSpec((1, D), lambda i: (0, 0)),
            pl.BlockSpec((D, 2 * D), lambda i: (0, 0)),
            pl.BlockSpec((1, MEM_HD), lambda i: (0, 0)),
        ],
        out_specs=[
            pl.BlockSpec((tm, D), lambda i: (i, 0)),
            pl.BlockSpec((tm, D), lambda i: (i, 0)),
        ],
        out_shape=[jax.ShapeDtypeStruct((t, D), F32)] * 2,
        compiler_params=_cparams(("parallel",)),
        name="mem_kv",
    )(mem, g, w, kg)


def _memattn_kernel(q_ref, k_ref, v_ref, qg_ref, y_ref):
    q = q_ref[...].astype(F32)
    scale = MEM_HD ** -0.5
    for h in range(MEM_HEADS):
        sl = slice(h * MEM_HD, (h + 1) * MEM_HD)
        qh = q[:, sl]
        ms = jnp.mean(qh * qh, axis=-1, keepdims=True)
        qn = (qh * lax.rsqrt(ms + EPS) * qg_ref[...]).astype(BF16)
        s = _dot_nt(qn, k_ref[:, sl].astype(BF16)) * scale
        m = jnp.max(s, axis=-1, keepdims=True)
        p = jnp.exp(s - m)
        den = jnp.sum(p, axis=-1, keepdims=True)
        o = _dot(p.astype(BF16), v_ref[:, sl].astype(BF16)) / den
        y_ref[:, sl] = o.astype(BF16)


def _memattn(proj, mk, mv, qg, nb, seq, mem_len, tm):
    t = proj.shape[0]
    nt = seq // tm
    return pl.pallas_call(
        _memattn_kernel,
        grid=(nb, nt),
        in_specs=[
            pl.BlockSpec((tm, D), lambda b, i: (b * nt + i, COL_MQ_BLK)),
            pl.BlockSpec((mem_len, D), lambda b, i: (b, 0)),
            pl.BlockSpec((mem_len, D), lambda b, i: (b, 0)),
            pl.BlockSpec((1, MEM_HD), lambda b, i: (0, 0)),
        ],
        out_specs=pl.BlockSpec((tm, D), lambda b, i: (b * nt + i, 0)),
        out_shape=jax.ShapeDtypeStruct((t, D), BF16),
        compiler_params=_cparams(("parallel", "arbitrary")),
        name="mem_attn_prompt",
    )(proj, mk, mv, qg)


def _route(logits):
    lane = lax.broadcasted_iota(jnp.int32, logits.shape, 1)
    lanef = lane.astype(F32)
    is_g = lane < N_GROUPS
    neg = -jnp.inf
    big = 1e9
    gl = jnp.where(is_g, logits, neg)
    gmax = jnp.max(gl, axis=-1, keepdims=True)
    gidx = jnp.min(jnp.where(is_g & (logits == gmax), lanef, big), axis=-1, keepdims=True)
    pg_top = 1.0 / jnp.sum(jnp.where(is_g, jnp.exp(logits - gmax), 0.0), axis=-1, keepdims=True)
    lo = N_GROUPS + gidx * EPG
    in_g = (lanef >= lo) & (lanef < lo + EPG)
    m1 = jnp.max(jnp.where(in_g, logits, neg), axis=-1, keepdims=True)
    i1 = jnp.min(jnp.where(in_g & (logits == m1), lanef, big), axis=-1, keepdims=True)
    rest = in_g & (lanef != i1)
    m2 = jnp.max(jnp.where(rest, logits, neg), axis=-1, keepdims=True)
    i2 = jnp.min(jnp.where(rest & (logits == m2), lanef, big), axis=-1, keepdims=True)
    r = jnp.exp(m2 - m1)
    w1 = pg_top / (1.0 + r)
    w2 = pg_top * r / (1.0 + r)
    comb = jnp.where(lanef == i1, w1, 0.0) + jnp.where(lanef == i2, w2, 0.0)
    a = jnp.minimum(i1, i2) - lo
    b = jnp.maximum(i1, i2) - lo
    bucket = gidx * PAIRS_PER_GROUP + a * (2 * EPG - 1 - a) * 0.5 + (b - a - 1.0)
    return jnp.where(lane == 0, bucket, comb)


def _back_kernel(x_ref, gt_ref, ys_ref, ya_ref, ym_ref, ws_ref, wa_ref, wm_ref, wo_ref, g2_ref,
                 wr_ref, br_ref, x1_ref, h_ref, comb_ref, *, rows_for_dispatch):
    gt = gt_ref[...].astype(F32)
    merged = (_sigmoid(gt[:, 0:D]) * _dot(ys_ref[...].astype(BF16), ws_ref[...])
              + _sigmoid(gt[:, D:2 * D]) * _dot(ya_ref[...], wa_ref[...])
              + _sigmoid(gt[:, 2 * D:3 * D]) * _dot(ym_ref[...], wm_ref[...]))
    x1 = x_ref[...] + _dot(merged.astype(BF16), wo_ref[...])
    x1_ref[...] = x1
    ms = jnp.mean(x1 * x1, axis=-1, keepdims=True)
    h = x1 * lax.rsqrt(ms + EPS) * g2_ref[...]
    h_hi = h.astype(BF16)
    h_lo = (h - h_hi.astype(F32)).astype(BF16)
    wr = wr_ref[...]
    wr_hi = wr.astype(BF16)
    wr_lo = (wr - wr_hi.astype(F32)).astype(BF16)
    logits = (_dot(h_hi, wr_hi) + _dot(h_hi, wr_lo) + _dot(h_lo, wr_hi)) + br_ref[...]
    comb = _route(logits)
    comb_ref[...] = comb
    if rows_for_dispatch:
        h_ref[:, 0:D] = h
        h_ref[:, D:D + ROUTE_W] = comb
    else:
        h_ref[...] = h.astype(BF16)


def _back(x, proj, y_ssm, y_swa, y_mem, ws, wa, wm, wo, g2, wr, br, tm, rows_for_dispatch):
    t = x.shape[0]
    resident = lambda shape: pl.BlockSpec(shape, lambda i: (0, 0), pipeline_mode=pl.Buffered(1))
    hw = D + ROUTE_W if rows_for_dispatch else D
    return pl.pallas_call(
        functools.partial(_back_kernel, rows_for_dispatch=rows_for_dispatch),
        grid=(t // tm,),
        in_specs=[
            pl.BlockSpec((tm, D), lambda i: (i, 0)),
            pl.BlockSpec((tm, 3 * D), lambda i: (i, 0)),
            pl.BlockSpec((tm, SSM_INNER), lambda i: (i, 0)),
            pl.BlockSpec((tm, D), lambda i: (i, 0)),
            pl.BlockSpec((tm, D), lambda i: (i, 0)),
            resident((SSM_INNER, D)),
            resident((D, D)),
            resident((D, D)),
            resident((D, D)),
            resident((1, D)),
            resident((D, ROUTE_W)),
            resident((1, ROUTE_W)),
        ],
        out_specs=[
            pl.BlockSpec((tm, D), lambda i: (i, 0)),
            pl.BlockSpec((tm, hw), lambda i: (i, 0)),
            pl.BlockSpec((tm, ROUTE_W), lambda i: (i, 0)),
        ],
        out_shape=[
            jax.ShapeDtypeStruct((t, D), F32),
            jax.ShapeDtypeStruct((t, hw), F32 if rows_for_dispatch else BF16),
            jax.ShapeDtypeStruct((t, ROUTE_W), F32),
        ],
        compiler_params=_cparams(("parallel",)),
        name="back",
    )(x, proj, y_ssm, y_swa, y_mem, ws, wa, wm, wo, g2, wr, br)


PAIRS_PER_GROUP = EPG * (EPG - 1) // 2
N_BUCKETS = N_GROUPS * PAIRS_PER_GROUP
MOE_TR = 256
RANK_TM = 1024
COMBINE_TM = 512
ROW_W = D + ROUTE_W


def _bucket_experts():
    first, second = [], []
    for g in range(N_GROUPS):
        for a in range(EPG):
            for b in range(a + 1, EPG):
                first.append(g * EPG + a)
                second.append(g * EPG + b)
    return jnp.array(first, jnp.int32), jnp.array(second, jnp.int32)


def _rank_kernel(comb_ref, pos_ref, cnt_ref, carry_ref, offs_ref):
    ph = pl.program_id(0)
    i = pl.program_id(1)
    tm = comb_ref.shape[0]
    lane = lax.broadcasted_iota(jnp.int32, (tm, LANE), 1)
    gid = jnp.sum(jnp.where(lane == 0, comb_ref[...], 0.0), axis=-1, keepdims=True)
    onehot = (lane.astype(F32) == gid).astype(F32)
    colsum = jnp.sum(onehot, axis=0, keepdims=True)

    @pl.when((ph == 0) & (i == 0))
    def _():
        cnt_ref[...] = jnp.zeros_like(cnt_ref)

    @pl.when(ph == 0)
    def _():
        cnt_ref[...] += colsum

    @pl.when((ph == 1) & (i == 0))
    def _():
        padded = jnp.ceil(cnt_ref[...] * (1.0 / MOE_TR)) * MOE_TR
        r = lax.broadcasted_iota(jnp.int32, (LANE, LANE), 0)
        c = lax.broadcasted_iota(jnp.int32, (LANE, LANE), 1)
        offs_ref[...] = _dot_exact(padded, (r < c).astype(F32))
        carry_ref[...] = jnp.zeros_like(carry_ref)

    @pl.when(ph == 1)
    def _():
        rr = lax.broadcasted_iota(jnp.int32, (tm, tm), 0)
        cc = lax.broadcasted_iota(jnp.int32, (tm, tm), 1)
        before = _dot((cc < rr).astype(BF16), onehot.astype(BF16))
        slot = onehot * (offs_ref[...] + carry_ref[...] + before)
        pos = lax.dot_general(jnp.ones((8, LANE), F32), slot, (((1,), (1,)), ((), ())),
                              preferred_element_type=F32, precision=HIGHEST)
        pos_ref[0] = pos.astype(jnp.int32)
        carry_ref[...] += colsum


def _rank(comb):
    t = comb.shape[0]
    tm = RANK_TM
    nt = t // tm
    return pl.pallas_call(
        _rank_kernel,
        grid=(2, nt),
        in_specs=[pl.BlockSpec((tm, ROUTE_W), lambda p, i: (i, 0))],
        out_specs=[
            pl.BlockSpec((1, 8, tm), lambda p, i: (i * p, 0, 0)),
            pl.BlockSpec((1, LANE), lambda p, i: (0, 0)),
        ],
        out_shape=[
            jax.ShapeDtypeStruct((nt, 8, tm), jnp.int32),
            jax.ShapeDtypeStruct((1, LANE), F32),
        ],
        scratch_shapes=[pltpu.VMEM((1, LANE), F32), pltpu.VMEM((1, LANE), F32)],
        compiler_params=_cparams(("arbitrary", "arbitrary")),
        name="moe_rank",
    )(comb)


def _row_copy(src_hbm, dst_hbm, src_row, dst_row, sem):
    return pltpu.make_async_copy(src_hbm.at[pl.ds(src_row, 1)], dst_hbm.at[pl.ds(dst_row, 1)], sem)


PAD_BITS = MOE_TR.bit_length() - 1


DISPATCH_BUFS = 3


def _dispatch_kernel(ps_ref, pc_ref, nt_ref, pos_ref, rows_hbm, xs_hbm, rbuf, zbuf, lsem, sem, zsem):
    i = pl.program_id(0)
    n_steps = pl.num_programs(0)
    tm = rbuf.shape[1]
    slot = i % 2
    cur = i % DISPATCH_BUFS

    def load(block, b):
        return pltpu.make_async_copy(rows_hbm.at[pl.ds(pl.multiple_of(block * tm, tm), tm)], rbuf.at[b], lsem.at[b])

    def wait_rows(b, s):
        pltpu.make_async_copy(rbuf.at[b], xs_hbm.at[pl.ds(0, tm)], sem.at[s]).wait()

    @pl.when(i == 0)
    def _():
        load(0, 0).start()

    @pl.when(i + 1 < n_steps)
    def _():
        load(i + 1, (i + 1) % DISPATCH_BUFS).start()

    load(i, cur).wait()
    for r in range(tm):
        _row_copy(rbuf.at[cur], xs_hbm, r, pos_ref[0, r], sem.at[slot]).start()

    @pl.when(i > 0)
    def _():
        wait_rows((i + DISPATCH_BUFS - 1) % DISPATCH_BUFS, 1 - slot)

    @pl.when(i == pl.num_programs(0) - 1)
    def _():
        wait_rows(cur, slot)
        zbuf[...] = jnp.zeros_like(zbuf)

        def runs(b, fn):
            first = ps_ref[b]
            count = pc_ref[b]
            head = jnp.minimum((-first) & (SUBLANES - 1), count)
            for j in range(SUBLANES - 1):
                @pl.when(j < head)
                def _():
                    fn(pltpu.make_async_copy(zbuf.at[pl.ds(0, 1)], xs_hbm.at[pl.ds(first + j, 1)], zsem.at[0]))
            rest = count - head
            for k in range(SUBLANES.bit_length() - 1, PAD_BITS):
                size = 1 << k
                start = pl.multiple_of(first + head + ((rest >> (k + 1)) << (k + 1)), SUBLANES)

                @pl.when((rest & size) != 0)
                def _():
                    fn(pltpu.make_async_copy(zbuf.at[pl.ds(0, size)], xs_hbm.at[pl.ds(start, size)], zsem.at[0]))

        def issue(b, carry):
            runs(b, lambda cp: cp.start())
            return carry

        def drain(b, carry):
            runs(b, lambda cp: cp.wait())
            return carry

        lax.fori_loop(0, N_BUCKETS, issue, 0)
        lax.fori_loop(0, N_BUCKETS, drain, 0)

        def tile_copy(j):
            return pltpu.make_async_copy(zbuf, xs_hbm.at[pl.ds(pl.multiple_of(j * MOE_TR, MOE_TR), MOE_TR)],
                                         zsem.at[0])

        n_tiles = xs_hbm.shape[0] // MOE_TR
        lax.fori_loop(nt_ref[0], n_tiles, lambda j, c: (tile_copy(j).start(), c)[1], 0)
        lax.fori_loop(nt_ref[0], n_tiles, lambda j, c: (tile_copy(j).wait(), c)[1], 0)


def _dispatch(pad_start, pad_count, n_used, pos3, rows, n_slots):
    t = rows.shape[0]
    tm = RANK_TM
    grid_spec = pltpu.PrefetchScalarGridSpec(
        num_scalar_prefetch=3,
        grid=(t // tm,),
        in_specs=[
            pl.BlockSpec((None, 1, tm), lambda i, ps, pc, nt: (i, 0, 0), memory_space=pltpu.SMEM),
            pl.BlockSpec(memory_space=pl.ANY),
        ],
        out_specs=pl.BlockSpec(memory_space=pl.ANY),
        scratch_shapes=[
            pltpu.VMEM((DISPATCH_BUFS, tm, ROW_W), F32),
            pltpu.VMEM((MOE_TR, ROW_W), F32),
            pltpu.SemaphoreType.DMA((DISPATCH_BUFS,)),
            pltpu.SemaphoreType.DMA((2,)),
            pltpu.SemaphoreType.DMA((1,)),
        ],
    )
    return pl.pallas_call(
        _dispatch_kernel,
        grid_spec=grid_spec,
        out_shape=jax.ShapeDtypeStruct((n_slots, ROW_W), F32),
        compiler_params=_cparams(("arbitrary",)),
        name="moe_dispatch",
    )(pad_start, pad_count, n_used, pos3, rows)


def _gmoe_kernel(ea_ref, eb_ref, nt_ref, xs_ref, wga_ref, wua_ref, wda_ref, wgb_ref, wub_ref, wdb_ref, ys_ref):
    i = pl.program_id(0)

    @pl.when(i < nt_ref[0])
    def _():
        x = xs_ref[...]
        h = x[:, 0:D].astype(BF16)
        comb = x[:, D:D + ROUTE_W]
        lane = lax.broadcasted_iota(jnp.int32, comb.shape, 1)
        acc = None
        for e_ref, wg_ref, wu_ref, wd_ref in ((ea_ref, wga_ref, wua_ref, wda_ref), (eb_ref, wgb_ref, wub_ref, wdb_ref)):
            cw = jnp.sum(jnp.where(lane == N_GROUPS + e_ref[i], comb, 0.0), axis=-1, keepdims=True)
            act = _silu(_dot(h, wg_ref[...])) * _dot(h, wu_ref[...])
            part = _dot((act * cw).astype(BF16), wd_ref[...])
            acc = part if acc is None else acc + part
        ys_ref[...] = acc

    @pl.when(i >= nt_ref[0])
    def _():
        ys_ref[...] = jnp.zeros_like(ys_ref)


def _gmoe(exp_a, exp_b, n_used, xs, wg, wu, wd):
    n_tiles = xs.shape[0] // MOE_TR
    tr = MOE_TR
    wspec = lambda shape, which: pl.BlockSpec(
        (None,) + shape, lambda i, ea, eb, nt: ((ea, eb)[which][i], 0, 0))
    grid_spec = pltpu.PrefetchScalarGridSpec(
        num_scalar_prefetch=3,
        grid=(n_tiles,),
        in_specs=[
            pl.BlockSpec((tr, ROW_W), lambda i, ea, eb, nt: (jnp.where(i < nt[0], i, 0), 0)),
            wspec((D, D_FF), 0), wspec((D, D_FF), 0), wspec((D_FF, D), 0),
            wspec((D, D_FF), 1), wspec((D, D_FF), 1), wspec((D_FF, D), 1),
        ],
        out_specs=pl.BlockSpec((tr, D), lambda i, ea, eb, nt: (i, 0)),
    )
    return pl.pallas_call(
        _gmoe_kernel,
        grid_spec=grid_spec,
        out_shape=jax.ShapeDtypeStruct((n_tiles * tr, D), F32),
        compiler_params=_cparams(("arbitrary",)),
        name="moe_grouped",
    )(exp_a, exp_b, n_used, xs, wg, wu, wd, wg, wu, wd)


def _combine_kernel(cur_ref, nxt_ref, x1_ref, ys_hbm, y_ref, buf, sem):
    i = pl.program_id(0)
    slot = i % 2
    tm = x1_ref.shape[0]

    def start_gather(idx_ref, s):
        for r in range(tm):
            _row_copy(ys_hbm, buf.at[s], idx_ref[0, r], r, sem.at[s]).start()

    @pl.when(i == 0)
    def _():
        start_gather(cur_ref, 0)

    @pl.when(i + 1 < pl.num_programs(0))
    def _():
        start_gather(nxt_ref, 1 - slot)

    pltpu.make_async_copy(ys_hbm.at[pl.ds(0, tm)], buf.at[slot], sem.at[slot]).wait()
    y_ref[...] = x1_ref[...] + buf[slot]


def _combine(pos3, x1, ys):
    t = x1.shape[0]
    tm = pos3.shape[2]
    nt = t // tm
    idx = lambda f: pl.BlockSpec((None, 1, tm), f, memory_space=pltpu.SMEM)
    return pl.pallas_call(
        _combine_kernel,
        grid=(nt,),
        in_specs=[
            idx(lambda i: (i, 0, 0)),
            idx(lambda i: (jnp.minimum(i + 1, nt - 1), 0, 0)),
            pl.BlockSpec((tm, D), lambda i: (i, 0)),
            pl.BlockSpec(memory_space=pl.ANY),
        ],
        out_specs=pl.BlockSpec((tm, D), lambda i: (i, 0)),
        out_shape=jax.ShapeDtypeStruct((t, D), F32),
        scratch_shapes=[pltpu.VMEM((2, tm, D), F32), pltpu.SemaphoreType.DMA((2,))],
        compiler_params=_cparams(("arbitrary",)),
        name="moe_combine",
    )(pos3, pos3, x1, ys)


def _moe_routed(x1, rows, comb, wg, wu, wd):
    t = x1.shape[0]
    tr = MOE_TR
    n_tiles = (t + N_BUCKETS * (tr - 1)) // tr
    pos3, cnt = _rank(comb)
    pos = pos3[:, 0, :]
    count = cnt[0, :N_BUCKETS].astype(jnp.int32)
    padded = (count + tr - 1) // tr * tr
    ends = jnp.cumsum(padded)
    pad_start = ends - padded + count
    pad_count = padded - count
    n_used = (ends[-1:] // tr).astype(jnp.int32)
    tile_start = jnp.arange(n_tiles, dtype=jnp.int32) * tr
    tile_bucket = jnp.minimum(jnp.sum(tile_start[:, None] >= ends[None, :], axis=1), N_BUCKETS - 1)
    first, second = _bucket_experts()
    xs = _dispatch(pad_start, pad_count, n_used, pos.reshape(t // RANK_TM, 1, RANK_TM), rows, n_tiles * tr)
    ys = _gmoe(first[tile_bucket], second[tile_bucket], n_used, xs, wg, wu, wd)
    return _combine(pos.reshape(t // COMBINE_TM, 1, COMBINE_TM), x1, ys)


def _moe_kernel(x1_ref, h_ref, comb_ref, wg_ref, wu_ref, wd_ref, o_ref):
    e = pl.program_id(1)

    @pl.when(e == 0)
    def _():
        o_ref[...] = x1_ref[...]

    h = h_ref[...]
    lane = lax.broadcasted_iota(jnp.int32, comb_ref.shape, 1)
    cw = jnp.sum(jnp.where(lane == e + N_GROUPS, comb_ref[...], 0.0), axis=-1, keepdims=True)
    act = _silu(_dot(h, wg_ref[0])) * _dot(h, wu_ref[0])
    o_ref[...] += _dot((act * cw).astype(BF16), wd_ref[0])


def _moe(x1, h, comb, wg, wu, wd, tm):
    t = x1.shape[0]
    return pl.pallas_call(
        _moe_kernel,
        grid=(t // tm, N_EXPERTS),
        in_specs=[
            pl.BlockSpec((tm, D), lambda i, e: (i, 0)),
            pl.BlockSpec((tm, D), lambda i, e: (i, 0)),
            pl.BlockSpec((tm, ROUTE_W), lambda i, e: (i, 0)),
            pl.BlockSpec((1, D, D_FF), lambda i, e: (e, 0, 0)),
            pl.BlockSpec((1, D, D_FF), lambda i, e: (e, 0, 0)),
            pl.BlockSpec((1, D_FF, D), lambda i, e: (e, 0, 0)),
        ],
        out_specs=pl.BlockSpec((tm, D), lambda i, e: (i, 0)),
        out_shape=jax.ShapeDtypeStruct((t, D), F32),
        compiler_params=_cparams(("parallel", "arbitrary")),
        name="moe",
    )(x1, h, comb, wg, wu, wd)


def _sample_prep_kernel(xbc_ref, q_ref, k_ref, mq_ref, dt_ref, sc_ref, cw_ref, cb_ref, dtb_ref, a_ref, e_ref,
                        qg_ref, kg_ref, mqg_ref, cos_ref, slo_ref, shi_ref,
                        conv_ref, xs_ref, bc_ref, dtx_ref, dec_ref, qn_ref, kn_ref, mqn_ref):
    n = xbc_ref.shape[0]
    x_raw = xbc_ref[...].astype(F32)
    acc = x_raw * cw_ref[SSM_CONV - 1:SSM_CONV, :] + cb_ref[...]
    for j in range(SSM_CONV - 1):
        acc = acc + sc_ref[:, j * CONV_CH:(j + 1) * CONV_CH] * cw_ref[j:j + 1, :]
    conv_ref[:, 0:(SSM_CONV - 2) * CONV_CH] = sc_ref[:, CONV_CH:(SSM_CONV - 1) * CONV_CH]
    conv_ref[:, (SSM_CONV - 2) * CONV_CH:] = x_raw
    act = _silu(acc)
    xs = act[:, 0:SSM_INNER]
    xs_ref[...] = xs
    bc_ref[...] = act[:, SSM_INNER:]
    dt = _softplus(dt_ref[...] + dtb_ref[...])
    e = e_ref[...]
    dtx = _dot_exact(dt, e) * xs
    dec = _dot_exact(jnp.exp(dt * a_ref[...]), e)
    dtx_ref[...] = dtx.T
    dec_ref[...] = dec.T

    lane = lax.broadcasted_iota(jnp.int32, (n, LANE), 1)
    low_half = lane < ATT_HD
    cos_t = jnp.broadcast_to(cos_ref[...], (n, LANE))
    sin_lo = jnp.broadcast_to(slo_ref[...], (n, LANE))
    sin_hi = jnp.broadcast_to(shi_ref[...], (n, LANE))
    qf = q_ref[...].astype(F32)
    for j in range(D // LANE):
        sl = slice(j * LANE, (j + 1) * LANE)
        qn_ref[:, sl] = _norm_rope_tile(qf[:, sl], qg_ref[...], cos_t, sin_lo, sin_hi, low_half)
    kf = k_ref[...].astype(F32)
    for j in range(ATT_KVD // LANE):
        sl = slice(j * LANE, (j + 1) * LANE)
        kn_ref[:, sl] = _norm_rope_tile(kf[:, sl], kg_ref[...], cos_t, sin_lo, sin_hi, low_half)
    mq = mq_ref[...].astype(F32)
    for h in range(MEM_HEADS):
        sl = slice(h * MEM_HD, (h + 1) * MEM_HD)
        mh = mq[:, sl]
        ms = jnp.mean(mh * mh, axis=-1, keepdims=True)
        mqn_ref[:, sl] = mh * lax.rsqrt(ms + EPS) * mqg_ref[...]


def _sample_prep(proj, dt_raw, sc2d, cw, cb, dtb, a, e, qg2, kg2, mqg, cos_t, sin_lo, sin_hi):
    n = proj.shape[0]
    c2 = lambda i: (0, 0)
    full = lambda shape: pl.BlockSpec(shape, c2)
    return pl.pallas_call(
        _sample_prep_kernel,
        grid=(1,),
        in_specs=[
            pl.BlockSpec((n, CONV_CH), lambda i: (0, COL_XBC_BLK)),
            pl.BlockSpec((n, D), lambda i: (0, COL_Q_BLK)),
            pl.BlockSpec((n, ATT_KVD), lambda i: (0, COL_K_BLK)),
            pl.BlockSpec((n, D), lambda i: (0, COL_MQ_BLK)),
            full((n, LANE)),
            full((n, (SSM_CONV - 1) * CONV_CH)),
            full((SSM_CONV, CONV_CH)),
            full((1, CONV_CH)),
            full((1, LANE)),
            full((1, LANE)),
            full((LANE, SSM_INNER)),
            full((1, LANE)),
            full((1, LANE)),
            full((1, MEM_HD)),
            full((1, LANE)),
            full((1, LANE)),
            full((1, LANE)),
        ],
        out_specs=[
            full((n, (SSM_CONV - 1) * CONV_CH)),
            full((n, SSM_INNER)),
            full((n, 2 * SSM_GROUPS * SSM_STATE)),
            full((SSM_INNER, n)),
            full((SSM_INNER, n)),
            full((n, D)),
            full((n, ATT_KVD)),
            full((n, D)),
        ],
        out_shape=[
            jax.ShapeDtypeStruct((n, (SSM_CONV - 1) * CONV_CH), F32),
            jax.ShapeDtypeStruct((n, SSM_INNER), F32),
            jax.ShapeDtypeStruct((n, 2 * SSM_GROUPS * SSM_STATE), F32),
            jax.ShapeDtypeStruct((SSM_INNER, n), F32),
            jax.ShapeDtypeStruct((SSM_INNER, n), F32),
            jax.ShapeDtypeStruct((n, D), F32),
            jax.ShapeDtypeStruct((n, ATT_KVD), F32),
            jax.ShapeDtypeStruct((n, D), F32),
        ],
        compiler_params=_cparams(("arbitrary",)),
        name="sample_prep",
    )(proj, proj, proj, proj, dt_raw, sc2d, cw, cb, dtb, a, e, qg2, kg2, mqg, cos_t, sin_lo, sin_hi)


SSM_S_TILE = 8


def _sample_ssm_kernel(st_ref, dtx_ref, dec_ref, bc_ref, so_ref, yt_ref):
    i = pl.program_id(0)
    n = dtx_ref.shape[1]

    @pl.when(i == 0)
    def _():
        yt_ref[...] = jnp.zeros_like(yt_ref)

    gw = SSM_HPG * SSM_HEADDIM
    lane = lax.broadcasted_iota(jnp.int32, (gw, n), 1)
    nbc = SSM_GROUPS * SSM_STATE
    for s in range(SSM_S_TILE):
        sel = lane == (i * SSM_S_TILE + s)
        for g in range(SSM_GROUPS):
            rows = slice(g * gw, (g + 1) * gw)
            dtx_c = jnp.sum(jnp.where(sel, dtx_ref[rows, :], 0.0), axis=1, keepdims=True)
            dec_c = jnp.sum(jnp.where(sel, dec_ref[rows, :], 0.0), axis=1, keepdims=True)
            bm = bc_ref[s:s + 1, g * SSM_STATE:(g + 1) * SSM_STATE]
            cm = bc_ref[s:s + 1, nbc + g * SSM_STATE:nbc + (g + 1) * SSM_STATE]
            hn = st_ref[s, rows, :] * dec_c + dtx_c * bm
            so_ref[s, rows, :] = hn
            yc = _dot_nt(hn.astype(BF16), jnp.broadcast_to(cm, (n, SSM_STATE)).astype(BF16))
            yt_ref[rows, :] = jnp.where(sel, yc, yt_ref[rows, :])


def _sample_ssm(state, dtx_t, dec_t, bc):
    n = state.shape[0]
    return pl.pallas_call(
        _sample_ssm_kernel,
        grid=(n // SSM_S_TILE,),
        in_specs=[
            pl.BlockSpec((SSM_S_TILE, SSM_INNER, SSM_STATE), lambda i: (i, 0, 0)),
            pl.BlockSpec((SSM_INNER, n), lambda i: (0, 0)),
            pl.BlockSpec((SSM_INNER, n), lambda i: (0, 0)),
            pl.BlockSpec((SSM_S_TILE, 2 * SSM_GROUPS * SSM_STATE), lambda i: (i, 0)),
        ],
        out_specs=[
            pl.BlockSpec((SSM_S_TILE, SSM_INNER, SSM_STATE), lambda i: (i, 0, 0)),
            pl.BlockSpec((SSM_INNER, n), lambda i: (0, 0)),
        ],
        out_shape=[
            jax.ShapeDtypeStruct((n, SSM_INNER, SSM_STATE), F32),
            jax.ShapeDtypeStruct((SSM_INNER, n), F32),
        ],
        compiler_params=_cparams(("arbitrary",)),
        name="sample_ssm",
    )(state, dtx_t, dec_t, bc)


def _sample_post_kernel(yt_ref, xs_ref, z_ref, dexp_ref, ng_ref, y_ref):
    y = yt_ref[...].T + dexp_ref[...] * xs_ref[...]
    yg = y * _silu(z_ref[...].astype(F32))
    ms = jnp.mean(yg * yg, axis=-1, keepdims=True)
    y_ref[...] = yg * lax.rsqrt(ms + EPS) * ng_ref[...]


def _sample_post(y_t, xs, proj, dexp, ng):
    n = xs.shape[0]
    c2 = lambda i: (0, 0)
    return pl.pallas_call(
        _sample_post_kernel,
        grid=(1,),
        in_specs=[
            pl.BlockSpec((SSM_INNER, n), c2),
            pl.BlockSpec((n, SSM_INNER), c2),
            pl.BlockSpec((n, SSM_INNER), lambda i: (0, COL_Z_BLK)),
            pl.BlockSpec((1, SSM_INNER), c2),
            pl.BlockSpec((1, SSM_INNER), c2),
        ],
        out_specs=pl.BlockSpec((n, SSM_INNER), c2),
        out_shape=jax.ShapeDtypeStruct((n, SSM_INNER), F32),
        compiler_params=_cparams(("arbitrary",)),
        name="sample_post",
    )(y_t, xs, proj, dexp, ng)


SWA_S_TILE = 8


def _sample_swa_kernel(q_ref, kn_ref, v_ref, ck_ref, cv_ref, sink_ref, y_ref, ko_ref, vo_ref, sc_ref):
    w = WINDOW
    scale = ATT_HD ** -0.5
    newest = lax.broadcasted_iota(jnp.int32, (ATT_HD, w), 1) == w - 1
    units = [(s, kv) for s in range(SWA_S_TILE) for kv in range(ATT_KV)]
    for u, (s, kv) in enumerate(units):
        kt = jnp.where(newest, kn_ref[s, kv], pltpu.roll(ck_ref[s, kv], w - 1, axis=1))
        vt = jnp.where(newest, v_ref[s, kv], pltpu.roll(cv_ref[s, kv], w - 1, axis=1))
        ko_ref[s, kv] = kt
        vo_ref[s, kv] = vt
        sc_ref[u * SUBLANES:(u + 1) * SUBLANES, :] = _dot(q_ref[s, kv].astype(BF16), kt.astype(BF16)) * scale
    sc = sc_ref[...]
    snk = jnp.concatenate([sink_ref[...]] * SWA_S_TILE, axis=0)
    m = jnp.maximum(jnp.max(sc, axis=-1, keepdims=True), snk)
    p = jnp.exp(sc - m)
    pn = p / (jnp.sum(p, axis=-1, keepdims=True) + jnp.exp(snk - m))
    for u, (s, kv) in enumerate(units):
        pu = pn[u * SUBLANES:(u + 1) * SUBLANES].astype(BF16)
        y_ref[s, kv] = _dot_nt(pu, vo_ref[s, kv].astype(BF16))


def _sample_swa(q4, kn4, v4, ck_t, cv_t, sink_col, l):
    n = q4.shape[0]
    w = WINDOW
    st = SWA_S_TILE
    cache = pl.BlockSpec((None, st, ATT_KV, ATT_HD, w), lambda i: (l, i, 0, 0, 0))
    new = pl.BlockSpec((st, ATT_KV, ATT_HD, 1), lambda i: (i, 0, 0, 0))
    out = pl.BlockSpec((st, ATT_KV, ATT_HD, w), lambda i: (i, 0, 0, 0))
    qspec = pl.BlockSpec((st, ATT_KV, SUBLANES, ATT_HD), lambda i: (i, 0, 0, 0))
    return pl.pallas_call(
        _sample_swa_kernel,
        grid=(n // st,),
        in_specs=[
            qspec, new, new, cache, cache,
            pl.BlockSpec((ATT_KV * SUBLANES, 1), lambda i: (0, 0)),
        ],
        out_specs=[qspec, out, out],
        out_shape=[
            jax.ShapeDtypeStruct((n, ATT_KV, SUBLANES, ATT_HD), F32),
            jax.ShapeDtypeStruct((n, ATT_KV, ATT_HD, w), F32),
            jax.ShapeDtypeStruct((n, ATT_KV, ATT_HD, w), F32),
        ],
        scratch_shapes=[pltpu.VMEM((st * ATT_KV * SUBLANES, w), F32)],
        compiler_params=_cparams(("parallel",)),
        name="sample_swa",
    )(q4, kn4, v4, ck_t, cv_t, sink_col)


MEM_S_TILE = 4


def _sample_mem_kernel(q_ref, k_ref, v_ref, y_ref):
    scale = MEM_HD ** -0.5
    for s in range(MEM_S_TILE):
        sc = jnp.sum(k_ref[s] * q_ref[s][None], axis=-1, keepdims=True) * scale
        m = jnp.max(sc, axis=0, keepdims=True)
        p = jnp.exp(sc - m)
        den = jnp.sum(p, axis=0, keepdims=True)
        o = jnp.sum(p * v_ref[s], axis=0, keepdims=True) / den
        y_ref[s] = o[0]


def _sample_mem(q3, ck, cv, l):
    n = q3.shape[0]
    mem_len = ck.shape[2]
    st = MEM_S_TILE
    cache = pl.BlockSpec((None, st, mem_len, MEM_HEADS, MEM_HD), lambda i: (l, i, 0, 0, 0))
    return pl.pallas_call(
        _sample_mem_kernel,
        grid=(n // st,),
        in_specs=[pl.BlockSpec((st, MEM_HEADS, MEM_HD), lambda i: (i, 0, 0)), cache, cache],
        out_specs=pl.BlockSpec((st, MEM_HEADS, MEM_HD), lambda i: (i, 0, 0)),
        out_shape=jax.ShapeDtypeStruct((n, MEM_HEADS, MEM_HD), F32),
        compiler_params=_cparams(("parallel",)),
        name="sample_mem",
    )(q3, ck, cv)


def _prep_weights(lw):
    w_in = lw['w_in']
    cols = [w_in[:, 0:OFF_Z], w_in[:, OFF_XBC:OFF_DT], w_in[:, OFF_Z:OFF_XBC], w_in[:, OFF_Q:OFF_K],
            w_in[:, OFF_MQ:OFF_MQ + D], w_in[:, OFF_K:OFF_V], w_in[:, OFF_V:OFF_MQ]]
    p = {}
    p['w_main'] = jnp.concatenate(cols, axis=1).astype(BF16)
    p['w_dt'] = jnp.pad(w_in[:, OFF_DT:OFF_Q], ((0, 0), (0, LANE - SSM_HEADS))).astype(BF16)
    p['norm1_g'] = lw['norm1_g'].reshape(1, D)
    p['cw'] = lw['ssm_conv_w']
    p['cb'] = lw['ssm_conv_b'].reshape(1, CONV_CH)
    pad_h = (0, LANE - SSM_HEADS)
    p['dtb'] = jnp.pad(lw['ssm_dt_bias'].astype(F32), pad_h).reshape(1, LANE)
    p['a'] = jnp.pad(-jnp.exp(lw['ssm_a_log'].astype(F32)), pad_h).reshape(1, LANE)
    p['dexp'] = jnp.repeat(lw['ssm_d'], SSM_HEADDIM).reshape(1, SSM_INNER)
    p['ssm_ng'] = lw['ssm_norm_g'].reshape(1, SSM_INNER)
    head_of_ch = jnp.arange(SSM_INNER) // SSM_HEADDIM
    p['e'] = (jnp.arange(LANE)[:, None] == head_of_ch[None, :]).astype(F32)
    p['qg2'] = jnp.tile(lw['att_q_norm_g'], 2).reshape(1, LANE)
    p['kg2'] = jnp.tile(lw['att_k_norm_g'], 2).reshape(1, LANE)
    p['sink_row'] = lw['att_sink'].astype(F32).reshape(1, ATT_HEADS)
    grp = ATT_HEADS // ATT_KV
    p['sink_col'] = jnp.pad(lw['att_sink'].astype(F32).reshape(ATT_KV, grp),
                            ((0, 0), (0, SUBLANES - grp))).reshape(ATT_KV * SUBLANES, 1)
    p['mem_g'] = lw['mem_norm_g'].reshape(1, D)
    p['w_mem_kv'] = lw['w_mem_kv'].astype(BF16)
    p['mqg'] = lw['mem_q_norm_g'].reshape(1, MEM_HD)
    p['mkg'] = lw['mem_k_norm_g'].reshape(1, MEM_HD)
    p['ws'] = lw['w_br_ssm'].astype(BF16)
    p['wa'] = lw['w_br_swa'].astype(BF16)
    p['wm'] = lw['w_br_mem'].astype(BF16)
    p['wo'] = lw['w_out'].astype(BF16)
    p['g2'] = lw['norm2_g'].reshape(1, D)
    pad_r = ROUTE_W - N_GROUPS - N_EXPERTS
    p['wr'] = jnp.pad(jnp.concatenate([lw['w_router_group'], lw['w_router_expert']], axis=1).astype(F32),
                      ((0, 0), (0, pad_r)))
    p['br'] = jnp.pad(jnp.concatenate([lw['b_router_group'], lw['b_router_expert']]).astype(F32),
                      (0, pad_r)).reshape(1, ROUTE_W)
    p['wg'] = lw['w_exp_gate'].astype(BF16)
    p['wu'] = lw['w_exp_up'].astype(BF16)
    p['wd'] = lw['w_exp_down'].astype(BF16)
    return p


def _pick(n, prefs):
    for c in prefs:
        if n % c == 0:
            return c
    return n


def _tail(x, proj, y_ssm, y_swa, y_mem, p):
    t = x.shape[0]
    routed = t % RANK_TM == 0
    x1, h, comb = _back(x, proj, y_ssm, y_swa, y_mem, p['ws'], p['wa'], p['wm'], p['wo'], p['g2'],
                        p['wr'], p['br'], _pick(t, (512, 256, 128)), routed)
    if routed:
        return _moe_routed(x1, h, comb, p['wg'], p['wu'], p['wd'])
    return _moe(x1, h, comb, p['wg'], p['wu'], p['wd'], _pick(t, (1024, 512, 256, 128)))


def _prompt_layer(x, mem, p):
    nb, seq, _ = x.shape
    t = nb * seq
    nc = seq // CHUNK
    xf = x.reshape(t, D)
    proj, dt_raw = _front(xf, p['norm1_g'], p['w_main'], p['w_dt'], _pick(t, (1024, 512, 256, 128)), 1536)
    y_ssm, conv_new, ssm_new = _ssd(proj, dt_raw, p['cw'], p['cb'], p['dtb'], p['a'], p['dexp'],
                                    p['ssm_ng'], nb, nc)
    cos_t, sin_lo, sin_hi = _rope_tables(jnp.arange(seq))
    y_swa, k_new, v_new = _swa(proj, p['qg2'], p['kg2'], cos_t, sin_lo, sin_hi, p['sink_row'], nb, seq)
    mem_len = mem.shape[1]
    mk, mv = _memkv(mem.reshape(nb * mem_len, D), p['mem_g'], p['w_mem_kv'], p['mkg'],
                    _pick(nb * mem_len, (512, 256, 128)))
    y_mem = _memattn(proj, mk, mv, p['mqg'], nb, seq, mem_len, _pick(seq, (512, 256, 128)))
    y = _tail(xf, proj, y_ssm, y_swa, y_mem, p)
    return (y.reshape(nb, seq, D), conv_new,
            ssm_new.reshape(nb, SSM_HEADS, SSM_HEADDIM, SSM_STATE),
            k_new.reshape(nb, WINDOW, ATT_KV, ATT_HD), v_new.reshape(nb, WINDOW, ATT_KV, ATT_HD),
            mk.reshape(nb, mem_len, MEM_HEADS, MEM_HD), mv.reshape(nb, mem_len, MEM_HEADS, MEM_HD))


def _sample_layer(x, conv_st, ssm_st, swa_k, swa_v, mem_k, mem_v, l, p):
    n = x.shape[0]
    xf = x.reshape(n, D)
    proj, dt_raw = _front(xf, p['norm1_g'], p['w_main'], p['w_dt'], n, 1536)
    cos_t, sin_lo, sin_hi = _rope_tables(jnp.full((1,), PAST_LEN, jnp.int32))
    sc2d = conv_st.reshape(n, (SSM_CONV - 1) * CONV_CH)
    conv_new, xs, bc, dtx_t, dec_t, qn, kn, mqn = _sample_prep(
        proj, dt_raw, sc2d, p['cw'], p['cb'], p['dtb'], p['a'], p['e'], p['qg2'], p['kg2'], p['mqg'],
        cos_t, sin_lo, sin_hi)
    ssm_new, y_t = _sample_ssm(ssm_st.reshape(n, SSM_INNER, SSM_STATE), dtx_t, dec_t, bc)
    y_ssm = _sample_post(y_t, xs, proj, p['dexp'], p['ssm_ng'])
    grp = ATT_HEADS // ATT_KV
    v_raw = proj[:, COL_V_BLK * ATT_KVD:(COL_V_BLK + 1) * ATT_KVD].astype(F32)
    to_t = (0, 1, 3, 4, 2)
    y_swa, k_new, v_new = _sample_swa(
        jnp.pad(qn.reshape(n, ATT_KV, grp, ATT_HD), ((0, 0), (0, 0), (0, SUBLANES - grp), (0, 0))),
        kn.reshape(n, ATT_KV, ATT_HD, 1), v_raw.reshape(n, ATT_KV, ATT_HD, 1),
        swa_k.transpose(to_t), swa_v.transpose(to_t), p['sink_col'], l)
    y_swa = y_swa[:, :, 0:grp, :].reshape(n, D)
    k_new = k_new.transpose(0, 3, 1, 2)
    v_new = v_new.transpose(0, 3, 1, 2)
    y_mem = _sample_mem(mqn.reshape(n, MEM_HEADS, MEM_HD), mem_k, mem_v, l)
    y = _tail(xf, proj, y_ssm, y_swa.astype(BF16), y_mem.reshape(n, D).astype(BF16), p)
    return (y.reshape(n, 1, D), conv_new.reshape(n, SSM_CONV - 1, CONV_CH),
            ssm_new.reshape(n, SSM_HEADS, SSM_HEADDIM, SSM_STATE), k_new, v_new)


def kernel(x_prompt, x_sample, state_conv, state_ssm, cache_swa_k, cache_swa_v, cache_mem_k, cache_mem_v,
           mem_prompt, norm1_g, w_in, ssm_conv_w, ssm_conv_b, ssm_dt_bias, ssm_a_log, ssm_d, ssm_norm_g,
           att_q_norm_g, att_k_norm_g, att_sink, mem_norm_g, w_mem_kv, mem_q_norm_g, mem_k_norm_g,
           w_br_ssm, w_br_swa, w_br_mem, w_out, norm2_g, w_router_group, b_router_group,
           w_router_expert, b_router_expert, w_exp_gate, w_exp_up, w_exp_down):
    weights = dict(norm1_g=norm1_g, w_in=w_in, ssm_conv_w=ssm_conv_w, ssm_conv_b=ssm_conv_b,
                   ssm_dt_bias=ssm_dt_bias, ssm_a_log=ssm_a_log, ssm_d=ssm_d, ssm_norm_g=ssm_norm_g,
                   att_q_norm_g=att_q_norm_g, att_k_norm_g=att_k_norm_g, att_sink=att_sink,
                   mem_norm_g=mem_norm_g, w_mem_kv=w_mem_kv, mem_q_norm_g=mem_q_norm_g,
                   mem_k_norm_g=mem_k_norm_g, w_br_ssm=w_br_ssm, w_br_swa=w_br_swa, w_br_mem=w_br_mem,
                   w_out=w_out, norm2_g=norm2_g, w_router_group=w_router_group,
                   b_router_group=b_router_group, w_router_expert=w_router_expert,
                   b_router_expert=b_router_expert, w_exp_gate=w_exp_gate, w_exp_up=w_exp_up,
                   w_exp_down=w_exp_down)
    depth = w_in.shape[0]
    xp, xs = x_prompt, x_sample
    outs = [[] for _ in range(10)]
    for l in range(depth):
        p = _prep_weights({k: v[l] for k, v in weights.items()})
        xp, c1, c2, c3, c4, c5, c6 = _prompt_layer(xp, mem_prompt, p)
        xs, d1, d2, d3, d4 = _sample_layer(xs, state_conv[l], state_ssm[l], cache_swa_k, cache_swa_v,
                                           cache_mem_k, cache_mem_v, l, p)
        for lst, val in zip(outs, (c1, c2, c3, c4, c5, c6, d1, d2, d3, d4)):
            lst.append(val)
    return (xp, xs) + tuple(jnp.stack(o) for o in outs)
```

```python
import functools
import math

import jax
import jax.numpy as jnp
from jax import lax
from jax.experimental import pallas as pl
from jax.experimental.pallas import tpu as pltpu

F32 = jnp.float32
BF16 = jnp.bfloat16
HIGHEST = lax.Precision.HIGHEST

D = 1024
SSM_INNER = 2048
SSM_HEADDIM = 64
SSM_HEADS = 32
SSM_GROUPS = 4
SSM_HPG = SSM_HEADS // SSM_GROUPS
SSM_STATE = 128
SSM_CONV = 4
CONV_CH = SSM_INNER + 2 * SSM_GROUPS * SSM_STATE
CHUNK = 128
ATT_HEADS = 16
ATT_KV = 4
ATT_HD = 64
ATT_KVD = ATT_KV * ATT_HD
WINDOW = 128
ROPE_THETA = 10000.0
MEM_HEADS = 4
MEM_HD = 256
N_EXPERTS = 32
N_GROUPS = 4
EPG = 8
D_FF = 256
EPS = 1e-6
PAST_LEN = 16384

OFF_Z = 3 * D
OFF_XBC = OFF_Z + SSM_INNER
OFF_DT = OFF_XBC + CONV_CH
OFF_Q = OFF_DT + SSM_HEADS
OFF_K = OFF_Q + D
OFF_V = OFF_K + ATT_KVD
OFF_MQ = OFF_V + ATT_KVD

N_MAIN = 3 * D + CONV_CH + SSM_INNER + D + D + 2 * ATT_KVD
COL_XBC_BLK = 1
COL_Z_BLK = 3
COL_Q_BLK = 8
COL_MQ_BLK = 9
COL_K_BLK = 40
COL_V_BLK = 41
LANE = 128
ROUTE_W = 128

VMEM_LIMIT = 56 * 1024 * 1024


def _cparams(sem):
    return pltpu.CompilerParams(dimension_semantics=sem, vmem_limit_bytes=VMEM_LIMIT)


def _sigmoid(x):
    return 1.0 / (1.0 + jnp.exp(-x))


def _silu(x):
    h = 0.5 * x
    return h + h * jnp.tanh(h)


def _softplus(x):
    return jnp.maximum(x, 0.0) + jnp.log(1.0 + jnp.exp(-jnp.abs(x)))


def _dot(a, b):
    return jnp.dot(a, b, preferred_element_type=F32)


def _dot_nt(a, b):
    return lax.dot_general(a, b, (((1,), (1,)), ((), ())), preferred_element_type=F32)


def _dot_tn(a, b):
    return lax.dot_general(a, b, (((0,), (0,)), ((), ())), preferred_element_type=F32)


def _dot_exact(a, b):
    return jnp.dot(a, b, preferred_element_type=F32, precision=HIGHEST)


def _front_kernel(x_ref, g_ref, w_ref, wdt_ref, o_ref, dt_ref, hn_ref):
    @pl.when(pl.program_id(1) == 0)
    def _():
        x = x_ref[...]
        ms = jnp.mean(x * x, axis=-1, keepdims=True)
        hn = (x * lax.rsqrt(ms + EPS) * g_ref[...]).astype(BF16)
        hn_ref[...] = hn
        dt_ref[...] = _dot(hn, wdt_ref[...])

    o_ref[...] = _dot(hn_ref[...], w_ref[...]).astype(BF16)


def _front(x, g, w_main, w_dt, tm, tn):
    t = x.shape[0]
    return pl.pallas_call(
        _front_kernel,
        grid=(t // tm, N_MAIN // tn),
        in_specs=[
            pl.BlockSpec((tm, D), lambda i, j: (i, 0)),
            pl.BlockSpec((1, D), lambda i, j: (0, 0)),
            pl.BlockSpec((D, tn), lambda i, j: (0, j)),
            pl.BlockSpec((D, LANE), lambda i, j: (0, 0)),
        ],
        out_specs=[
            pl.BlockSpec((tm, tn), lambda i, j: (i, j)),
            pl.BlockSpec((tm, LANE), lambda i, j: (i, 0)),
        ],
        out_shape=[
            jax.ShapeDtypeStruct((t, N_MAIN), BF16),
            jax.ShapeDtypeStruct((t, LANE), F32),
        ],
        scratch_shapes=[pltpu.VMEM((tm, D), BF16)],
        compiler_params=_cparams(("parallel", "arbitrary")),
        name="front",
    )(x, g, w_main, w_dt)


SUBLANES = 8


def _ssd_kernel(xbc_ref, z_ref, dt_ref, cw_ref, cb_ref, dtb_ref, a_ref, dexp_ref, ng_ref,
                y_ref, conv_ref, ssm_ref, prev_ref, h_ref, yacc_ref):
    c = pl.program_id(1)

    @pl.when(c == 0)
    def _():
        prev_ref[...] = jnp.zeros_like(prev_ref)
        h_ref[...] = jnp.zeros_like(h_ref)

    x_raw = xbc_ref[...].astype(F32)
    prev = prev_ref[...]
    row8 = lax.broadcasted_iota(jnp.int32, (SUBLANES, CONV_CH), 0)
    row = lax.broadcasted_iota(jnp.int32, (CHUNK, CHUNK), 0)
    col = lax.broadcasted_iota(jnp.int32, (CHUNK, CHUNK), 1)
    causal = row >= col
    acc = x_raw * cw_ref[SSM_CONV - 1:SSM_CONV, :] + cb_ref[...]
    for s in range(1, SSM_CONV):
        xr = pltpu.roll(x_raw, s, axis=0)
        head = jnp.where(row8 < s, pltpu.roll(prev, s, axis=0), xr[0:SUBLANES])
        shifted = jnp.concatenate([head, xr[SUBLANES:]], axis=0)
        acc = acc + shifted * cw_ref[SSM_CONV - 1 - s:SSM_CONV - s, :]
    prev_ref[...] = x_raw[CHUNK - SUBLANES:CHUNK]
    conv_ref[0] = x_raw[CHUNK - (SSM_CONV - 1):CHUNK]
    act = _silu(acc)

    dt = _softplus(dt_ref[...] + dtb_ref[...])
    da = dt * a_ref[...]
    acum = _dot_exact(causal.astype(F32), da)
    acum_t = acum.T
    dt_t = dt.T
    exp_a = jnp.exp(acum)
    last = acum[CHUNK - 1:CHUNK, :]
    w_end = jnp.exp(last - acum) * dt
    cd = jnp.broadcast_to(jnp.exp(acum_t[:, CHUNK - 1:CHUNK]), (LANE, SSM_STATE))
    lane = lax.broadcasted_iota(jnp.int32, (CHUNK, LANE), 1)
    low_half = lane < SSM_HEADDIM

    def pair_cols(per_head, hd):
        return jnp.where(low_half, per_head[:, hd:hd + 1], per_head[:, hd + 1:hd + 2])

    xs_off = 0
    b_off = SSM_INNER
    c_off = SSM_INNER + SSM_GROUPS * SSM_STATE
    for g in range(SSM_GROUPS):
        bm = act[:, b_off + g * SSM_STATE:b_off + (g + 1) * SSM_STATE].astype(BF16)
        cm = act[:, c_off + g * SSM_STATE:c_off + (g + 1) * SSM_STATE].astype(BF16)
        cb = _dot_nt(cm, bm)
        gw = SSM_HPG * SSM_HEADDIM
        ch0 = g * gw
        h_prev = h_ref[ch0:ch0 + gw, :]
        y_off = _dot_nt(cm, h_prev.astype(BF16))
        xg = act[:, xs_off + ch0:xs_off + ch0 + gw]
        xw = []
        for pr in range(SSM_HPG // 2):
            hd0 = g * SSM_HPG + pr * 2
            xp32 = xg[:, pr * LANE:(pr + 1) * LANE]
            xpair = xp32.astype(BF16)
            yd = []
            for sub in range(2):
                hd = hd0 + sub
                seg = acum[:, hd:hd + 1] - acum_t[hd:hd + 1, :]
                decay = jnp.exp(jnp.where(causal, seg, -jnp.inf))
                wts = cb * decay * dt_t[hd:hd + 1, :]
                yd.append(_dot(wts.astype(BF16), xpair))
            cl = ch0 + pr * LANE
            y_pair = (jnp.where(low_half, yd[0], yd[1])
                      + y_off[:, pr * LANE:(pr + 1) * LANE] * pair_cols(exp_a, hd0)
                      + dexp_ref[:, cl:cl + LANE] * xp32)
            yacc_ref[:, cl:cl + LANE] = y_pair
            xw.append((xp32 * pair_cols(w_end, hd0)).astype(BF16))
        states = _dot_tn(jnp.concatenate(xw, axis=1), bm)
        for r in range(SSM_HPG):
            hd = g * SSM_HPG + r
            r0 = ch0 + r * SSM_HEADDIM
            h_ref[r0:r0 + SSM_HEADDIM, :] = (h_ref[r0:r0 + SSM_HEADDIM, :] * cd[hd:hd + 1, :]
                                             + states[r * SSM_HEADDIM:(r + 1) * SSM_HEADDIM, :])

    ssm_ref[0] = h_ref[...]
    yg = yacc_ref[...] * _silu(z_ref[...].astype(F32))
    ms = jnp.mean(yg * yg, axis=-1, keepdims=True)
    y_ref[...] = (yg * lax.rsqrt(ms + EPS) * ng_ref[...]).astype(BF16)


def _ssd(proj, dt_raw, cw, cb, dtb, a, dexp, ng, nb, nc):
    t = proj.shape[0]
    return pl.pallas_call(
        _ssd_kernel,
        grid=(nb, nc),
        in_specs=[
            pl.BlockSpec((CHUNK, CONV_CH), lambda b, c: (b * nc + c, COL_XBC_BLK)),
            pl.BlockSpec((CHUNK, SSM_INNER), lambda b, c: (b * nc + c, COL_Z_BLK)),
            pl.BlockSpec((CHUNK, LANE), lambda b, c: (b * nc + c, 0)),
            pl.BlockSpec((SSM_CONV, CONV_CH), lambda b, c: (0, 0)),
            pl.BlockSpec((1, CONV_CH), lambda b, c: (0, 0)),
            pl.BlockSpec((1, LANE), lambda b, c: (0, 0)),
            pl.BlockSpec((1, LANE), lambda b, c: (0, 0)),
            pl.BlockSpec((1, SSM_INNER), lambda b, c: (0, 0)),
            pl.BlockSpec((1, SSM_INNER), lambda b, c: (0, 0)),
        ],
        out_specs=[
            pl.BlockSpec((CHUNK, SSM_INNER), lambda b, c: (b * nc + c, 0)),
            pl.BlockSpec((1, SSM_CONV - 1, CONV_CH), lambda b, c: (b, 0, 0)),
            pl.BlockSpec((1, SSM_INNER, SSM_STATE), lambda b, c: (b, 0, 0)),
        ],
        out_shape=[
            jax.ShapeDtypeStruct((t, SSM_INNER), BF16),
            jax.ShapeDtypeStruct((nb, SSM_CONV - 1, CONV_CH), F32),
            jax.ShapeDtypeStruct((nb, SSM_INNER, SSM_STATE), F32),
        ],
        scratch_shapes=[
            pltpu.VMEM((SUBLANES, CONV_CH), F32),
            pltpu.VMEM((SSM_INNER, SSM_STATE), F32),
            pltpu.VMEM((CHUNK, SSM_INNER), F32),
        ],
        compiler_params=_cparams(("parallel", "arbitrary")),
        name="ssd_prompt",
    )(proj, proj, dt_raw, cw, cb, dtb, a, dexp, ng)


def _norm_rope_tile(xj, g2, cos_t, sin_lo, sin_hi, low_half):
    sq = xj * xj
    s_lo = jnp.sum(jnp.where(low_half, sq, 0.0), axis=-1, keepdims=True)
    s_hi = jnp.sum(jnp.where(low_half, 0.0, sq), axis=-1, keepdims=True)
    ms = jnp.where(low_half, s_lo, s_hi) * (1.0 / ATT_HD)
    xn = xj * lax.rsqrt(ms + EPS) * g2
    half = ATT_HD // 2
    return (xn * cos_t + pltpu.roll(xn, LANE - half, axis=1) * sin_lo
            + pltpu.roll(xn, half, axis=1) * sin_hi)


def _rope_tables(pos):
    half = ATT_HD // 2
    inv = ROPE_THETA ** (-jnp.arange(half, dtype=F32) / half)
    ang = pos.astype(F32)[:, None] * inv[None, :]
    cos = jnp.cos(ang)
    sin = jnp.sin(ang)
    zero = jnp.zeros_like(sin)
    cos_t = jnp.concatenate([cos, cos, cos, cos], axis=1)
    sin_lo = jnp.concatenate([-sin, zero, -sin, zero], axis=1)
    sin_hi = jnp.concatenate([zero, sin, zero, sin], axis=1)
    return cos_t, sin_lo, sin_hi


SWA_NSB = 2


def _swa_kernel(q_ref, k_ref, v_ref, qg_ref, kg_ref, cos_ref, slo_ref, shi_ref, sink_ref,
                y_ref, ko_ref, vo_ref, kd_ref, vd_ref, qs_ref):
    c = pl.program_id(1)
    w = WINDOW
    nsb = SWA_NSB
    grp = ATT_HEADS // ATT_KV
    rows = nsb * w
    lane = lax.broadcasted_iota(jnp.int32, (rows, LANE), 1)
    low_half = lane < ATT_HD
    low_half_w = lax.broadcasted_iota(jnp.int32, (w, LANE), 1) < ATT_HD
    cos_t = cos_ref[...]
    sin_lo = slo_ref[...]
    sin_hi = shi_ref[...]

    @pl.when(c == 0)
    def _():
        kd_ref[:, 0:w, :] = jnp.zeros((ATT_KV, w, LANE), BF16)
        vd_ref[:, 0:w, :] = jnp.zeros((ATT_KV, w, 2 * LANE), BF16)
        vd_ref[:, :, LANE:2 * LANE] = jnp.ones((ATT_KV, (nsb + 1) * w, LANE), BF16)

    kf = k_ref[...].astype(F32)
    vf = v_ref[...].astype(F32)
    for j in range(ATT_KVD // LANE):
        sl = slice(j * LANE, (j + 1) * LANE)
        kn = _norm_rope_tile(kf[:, sl], kg_ref[...], cos_t, sin_lo, sin_hi, low_half)
        ko_ref[0, :, sl] = kn[rows - w:rows]
        kr = pltpu.roll(kn, ATT_HD, axis=1)
        vj = vf[:, sl]
        vr = pltpu.roll(vj, ATT_HD, axis=1)
        kd_ref[2 * j, w:w + rows, :] = jnp.where(low_half, kn, kr).astype(BF16)
        kd_ref[2 * j + 1, w:w + rows, :] = jnp.where(low_half, kr, kn).astype(BF16)
        vd_ref[2 * j, w:w + rows, 0:LANE] = jnp.where(low_half, vj, vr).astype(BF16)
        vd_ref[2 * j + 1, w:w + rows, 0:LANE] = jnp.where(low_half, vr, vj).astype(BF16)
    vo_ref[0] = vf[rows - w:rows]

    qf = q_ref[...].astype(F32)
    for j in range(D // LANE):
        sl = slice(j * LANE, (j + 1) * LANE)
        qn = _norm_rope_tile(qf[:, sl], qg_ref[...] * (ATT_HD ** -0.5), cos_t, sin_lo, sin_hi, low_half)
        for par in range(2):
            h = 2 * j + par
            r0 = (h % grp) * w
            in_half = low_half if par == 0 else jnp.logical_not(low_half)
            qh = jnp.where(in_half, qn, 0.0).astype(BF16)
            for sb in range(nsb):
                qs_ref[h // grp, sb, r0:r0 + w, :] = qh[sb * w:(sb + 1) * w]

    qi = lax.broadcasted_iota(jnp.int32, (grp * w, 2 * w), 0) & (w - 1)
    kj = lax.broadcasted_iota(jnp.int32, (grp * w, 2 * w), 1)
    in_cur = (kj >= w) & ((kj - w) <= qi)
    in_prev = (kj < w) & (kj > qi)

    for sb in range(nsb):
        mask = in_cur | (in_prev & (c > 0) if sb == 0 else in_prev)
        for kv in range(ATT_KV):
            snk = jnp.concatenate(
                [jnp.broadcast_to(sink_ref[:, kv * grp + i:kv * grp + i + 1], (w, 1)) for i in range(grp)], axis=0)
            s = _dot_nt(qs_ref[kv, sb], kd_ref[kv, sb * w:(sb + 2) * w, :])
            s = jnp.where(mask, s, -jnp.inf)
            m = jnp.maximum(jnp.max(s, axis=-1, keepdims=True), snk)
            p = jnp.exp(s - m)
            den = jnp.sum(p, axis=-1, keepdims=True) + jnp.exp(snk - m)
            o = _dot(p.astype(BF16), vd_ref[kv, sb * w:(sb + 2) * w, 0:LANE]) * (1.0 / den)
            for a in range(grp // 2):
                col = kv * (grp // 2) + a
                y_ref[sb * w:(sb + 1) * w, col * LANE:(col + 1) * LANE] = jnp.where(
                    low_half_w, o[2 * a * w:(2 * a + 1) * w, :], o[(2 * a + 1) * w:(2 * a + 2) * w, :]
                ).astype(BF16)

    kd_ref[:, 0:w, :] = kd_ref[:, rows:rows + w, :]
    vd_ref[:, 0:w, 0:LANE] = vd_ref[:, rows:rows + w, 0:LANE]


def _swa(proj, qg2, kg2, cos_t, sin_lo, sin_hi, sink, nb, seq):
    t = proj.shape[0]
    w = WINDOW
    rows = SWA_NSB * w
    nc = seq // rows
    grp = ATT_HEADS // ATT_KV
    tab = pl.BlockSpec((rows, LANE), lambda b, c: (c, 0))
    return pl.pallas_call(
        _swa_kernel,
        grid=(nb, nc),
        in_specs=[
            pl.BlockSpec((rows, D), lambda b, c: (b * nc + c, COL_Q_BLK)),
            pl.BlockSpec((rows, ATT_KVD), lambda b, c: (b * nc + c, COL_K_BLK)),
            pl.BlockSpec((rows, ATT_KVD), lambda b, c: (b * nc + c, COL_V_BLK)),
            pl.BlockSpec((1, LANE), lambda b, c: (0, 0)),
            pl.BlockSpec((1, LANE), lambda b, c: (0, 0)),
            tab, tab, tab,
            pl.BlockSpec((1, ATT_HEADS), lambda b, c: (0, 0)),
        ],
        out_specs=[
            pl.BlockSpec((rows, D), lambda b, c: (b * nc + c, 0)),
            pl.BlockSpec((1, w, ATT_KVD), lambda b, c: (b, 0, 0)),
            pl.BlockSpec((1, w, ATT_KVD), lambda b, c: (b, 0, 0)),
        ],
        out_shape=[
            jax.ShapeDtypeStruct((t, D), BF16),
            jax.ShapeDtypeStruct((nb, w, ATT_KVD), F32),
            jax.ShapeDtypeStruct((nb, w, ATT_KVD), F32),
        ],
        scratch_shapes=[
            pltpu.VMEM((ATT_KV, rows + w, LANE), BF16),
            pltpu.VMEM((ATT_KV, rows + w, 2 * LANE), BF16),
            pltpu.VMEM((ATT_KV, SWA_NSB, grp * w, LANE), BF16),
        ],
        compiler_params=_cparams(("parallel", "arbitrary")),
        name="swa_prompt",
    )(proj, proj, proj, qg2, kg2, cos_t, sin_lo, sin_hi, sink)


def _memkv_kernel(m_ref, g_ref, w_ref, kg_ref, k_ref, v_ref):
    x = m_ref[...]
    ms = jnp.mean(x * x, axis=-1, keepdims=True)
    hn = (x * lax.rsqrt(ms + EPS) * g_ref[...]).astype(BF16)
    kv = _dot(hn, w_ref[...])
    for h in range(MEM_HEADS):
        kh = kv[:, h * MEM_HD:(h + 1) * MEM_HD]
        ms = jnp.mean(kh * kh, axis=-1, keepdims=True)
        k_ref[:, h * MEM_HD:(h + 1) * MEM_HD] = kh * lax.rsqrt(ms + EPS) * kg_ref[...]
    v_ref[...] = kv[:, D:]


def _memkv(mem, g, w, kg, tm):
    t = mem.shape[0]
    return pl.pallas_call(
        _memkv_kernel,
        grid=(t // tm,),
        in_specs=[
            pl.BlockSpec((tm, D), lambda i: (i, 0)),
            pl.BlockSpec((1, D), lambda i: (0, 0)),
            pl.BlockSpec((D, 2 * D), lambda i: (0, 0)),
            pl.BlockSpec((1, MEM_HD), lambda i: (0, 0)),
        ],
        out_specs=[
            pl.BlockSpec((tm, D), lambda i: (i, 0)),
            pl.BlockSpec((tm, D), lambda i: (i, 0)),
        ],
        out_shape=[jax.ShapeDtypeStruct((t, D), F32)] * 2,
        compiler_params=_cparams(("parallel",)),
        name="mem_kv",
    )(mem, g, w, kg)


def _memattn_kernel(q_ref, k_ref, v_ref, qg_ref, y_ref):
    q = q_ref[...].astype(F32)
    scale = MEM_HD ** -0.5
    for h in range(MEM_HEADS):
        sl = slice(h * MEM_HD, (h + 1) * MEM_HD)
        qh = q[:, sl]
        ms = jnp.mean(qh * qh, axis=-1, keepdims=True)
        qn = (qh * lax.rsqrt(ms + EPS) * qg_ref[...]).astype(BF16)
        s = _dot_nt(qn, k_ref[:, sl].astype(BF16)) * scale
        m = jnp.max(s, axis=-1, keepdims=True)
        p = jnp.exp(s - m)
        den = jnp.sum(p, axis=-1, keepdims=True)
        o = _dot(p.astype(BF16), v_ref[:, sl].astype(BF16)) / den
        y_ref[:, sl] = o.astype(BF16)


def _memattn(proj, mk, mv, qg, nb, seq, mem_len, tm):
    t = proj.shape[0]
    nt = seq // tm
    return pl.pallas_call(
        _memattn_kernel,
        grid=(nb, nt),
        in_specs=[
            pl.BlockSpec((tm, D), lambda b, i: (b * nt + i, COL_MQ_BLK)),
            pl.BlockSpec((mem_len, D), lambda b, i: (b, 0)),
            pl.BlockSpec((mem_len, D), lambda b, i: (b, 0)),
            pl.BlockSpec((1, MEM_HD), lambda b, i: (0, 0)),
        ],
        out_specs=pl.BlockSpec((tm, D), lambda b, i: (b * nt + i, 0)),
        out_shape=jax.ShapeDtypeStruct((t, D), BF16),
        compiler_params=_cparams(("parallel", "arbitrary")),
        name="mem_attn_prompt",
    )(proj, mk, mv, qg)


def _route(logits):
    lane = lax.broadcasted_iota(jnp.int32, logits.shape, 1)
    lanef = lane.astype(F32)
    is_g = lane < N_GROUPS
    neg = -jnp.inf
    big = 1e9
    gl = jnp.where(is_g, logits, neg)
    gmax = jnp.max(gl, axis=-1, keepdims=True)
    gidx = jnp.min(jnp.where(is_g & (logits == gmax), lanef, big), axis=-1, keepdims=True)
    pg_top = 1.0 / jnp.sum(jnp.where(is_g, jnp.exp(logits - gmax), 0.0), axis=-1, keepdims=True)
    lo = N_GROUPS + gidx * EPG
    in_g = (lanef >= lo) & (lanef < lo + EPG)
    m1 = jnp.max(jnp.where(in_g, logits, neg), axis=-1, keepdims=True)
    i1 = jnp.min(jnp.where(in_g & (logits == m1), lanef, big), axis=-1, keepdims=True)
    rest = in_g & (lanef != i1)
    m2 = jnp.max(jnp.where(rest, logits, neg), axis=-1, keepdims=True)
    i2 = jnp.min(jnp.where(rest & (logits == m2), lanef, big), axis=-1, keepdims=True)
    r = jnp.exp(m2 - m1)
    w1 = pg_top / (1.0 + r)
    w2 = pg_top * r / (1.0 + r)
    comb = jnp.where(lanef == i1, w1, 0.0) + jnp.where(lanef == i2, w2, 0.0)
    a = jnp.minimum(i1, i2) - lo
    b = jnp.maximum(i1, i2) - lo
    bucket = gidx * PAIRS_PER_GROUP + a * (2 * EPG - 1 - a) * 0.5 + (b - a - 1.0)
    return jnp.where(lane == 0, bucket, comb)


def _back_kernel(x_ref, gt_ref, ys_ref, ya_ref, ym_ref, ws_ref, wa_ref, wm_ref, wo_ref, g2_ref,
                 wr_ref, br_ref, x1_ref, h_ref, comb_ref, *, rows_for_dispatch):
    gt = gt_ref[...].astype(F32)
    merged = (_sigmoid(gt[:, 0:D]) * _dot(ys_ref[...].astype(BF16), ws_ref[...])
              + _sigmoid(gt[:, D:2 * D]) * _dot(ya_ref[...], wa_ref[...])
              + _sigmoid(gt[:, 2 * D:3 * D]) * _dot(ym_ref[...], wm_ref[...]))
    x1 = x_ref[...] + _dot(merged.astype(BF16), wo_ref[...])
    x1_ref[...] = x1
    ms = jnp.mean(x1 * x1, axis=-1, keepdims=True)
    h = x1 * lax.rsqrt(ms + EPS) * g2_ref[...]
    h_hi = h.astype(BF16)
    h_lo = (h - h_hi.astype(F32)).astype(BF16)
    wr = wr_ref[...]
    wr_hi = wr.astype(BF16)
    wr_lo = (wr - wr_hi.astype(F32)).astype(BF16)
    logits = (_dot(h_hi, wr_hi) + _dot(h_hi, wr_lo) + _dot(h_lo, wr_hi)) + br_ref[...]
    comb = _route(logits)
    comb_ref[...] = comb
    if rows_for_dispatch:
        h_ref[:, 0:D] = h
        h_ref[:, D:D + ROUTE_W] = comb
    else:
        h_ref[...] = h.astype(BF16)


def _back(x, proj, y_ssm, y_swa, y_mem, ws, wa, wm, wo, g2, wr, br, tm, rows_for_dispatch):
    t = x.shape[0]
    resident = lambda shape: pl.BlockSpec(shape, lambda i: (0, 0), pipeline_mode=pl.Buffered(1))
    hw = D + ROUTE_W if rows_for_dispatch else D
    return pl.pallas_call(
        functools.partial(_back_kernel, rows_for_dispatch=rows_for_dispatch),
        grid=(t // tm,),
        in_specs=[
            pl.BlockSpec((tm, D), lambda i: (i, 0)),
            pl.BlockSpec((tm, 3 * D), lambda i: (i, 0)),
            pl.BlockSpec((tm, SSM_INNER), lambda i: (i, 0)),
            pl.BlockSpec((tm, D), lambda i: (i, 0)),
            pl.BlockSpec((tm, D), lambda i: (i, 0)),
            resident((SSM_INNER, D)),
            resident((D, D)),
            resident((D, D)),
            resident((D, D)),
            resident((1, D)),
            resident((D, ROUTE_W)),
            resident((1, ROUTE_W)),
        ],
        out_specs=[
            pl.BlockSpec((tm, D), lambda i: (i, 0)),
            pl.BlockSpec((tm, hw), lambda i: (i, 0)),
            pl.BlockSpec((tm, ROUTE_W), lambda i: (i, 0)),
        ],
        out_shape=[
            jax.ShapeDtypeStruct((t, D), F32),
            jax.ShapeDtypeStruct((t, hw), F32 if rows_for_dispatch else BF16),
            jax.ShapeDtypeStruct((t, ROUTE_W), F32),
        ],
        compiler_params=_cparams(("parallel",)),
        name="back",
    )(x, proj, y_ssm, y_swa, y_mem, ws, wa, wm, wo, g2, wr, br)


PAIRS_PER_GROUP = EPG * (EPG - 1) // 2
N_BUCKETS = N_GROUPS * PAIRS_PER_GROUP
MOE_TR = 128
RANK_TM = 1024
COMBINE_TM = 512
ROW_W = D + ROUTE_W


def _bucket_experts():
    first, second = [], []
    for g in range(N_GROUPS):
        for a in range(EPG):
            for b in range(a + 1, EPG):
                first.append(g * EPG + a)
                second.append(g * EPG + b)
    return jnp.array(first, jnp.int32), jnp.array(second, jnp.int32)


def _rank_kernel(comb_ref, pos_ref, cnt_ref, carry_ref, offs_ref):
    ph = pl.program_id(0)
    i = pl.program_id(1)
    tm = comb_ref.shape[0]
    lane = lax.broadcasted_iota(jnp.int32, (tm, LANE), 1)
    gid = jnp.sum(jnp.where(lane == 0, comb_ref[...], 0.0), axis=-1, keepdims=True)
    onehot = (lane.astype(F32) == gid).astype(F32)
    colsum = jnp.sum(onehot, axis=0, keepdims=True)

    @pl.when((ph == 0) & (i == 0))
    def _():
        cnt_ref[...] = jnp.zeros_like(cnt_ref)

    @pl.when(ph == 0)
    def _():
        cnt_ref[...] += colsum

    @pl.when((ph == 1) & (i == 0))
    def _():
        padded = jnp.ceil(cnt_ref[...] * (1.0 / MOE_TR)) * MOE_TR
        r = lax.broadcasted_iota(jnp.int32, (LANE, LANE), 0)
        c = lax.broadcasted_iota(jnp.int32, (LANE, LANE), 1)
        offs_ref[...] = _dot_exact(padded, (r < c).astype(F32))
        carry_ref[...] = jnp.zeros_like(carry_ref)

    @pl.when(ph == 1)
    def _():
        rr = lax.broadcasted_iota(jnp.int32, (tm, tm), 0)
        cc = lax.broadcasted_iota(jnp.int32, (tm, tm), 1)
        before = _dot((cc < rr).astype(BF16), onehot.astype(BF16))
        slot = onehot * (offs_ref[...] + carry_ref[...] + before)
        pos = lax.dot_general(jnp.ones((8, LANE), F32), slot, (((1,), (1,)), ((), ())),
                              preferred_element_type=F32, precision=HIGHEST)
        pos_ref[0] = pos.astype(jnp.int32)
        carry_ref[...] += colsum


def _rank(comb):
    t = comb.shape[0]
    tm = RANK_TM
    nt = t // tm
    return pl.pallas_call(
        _rank_kernel,
        grid=(2, nt),
        in_specs=[pl.BlockSpec((tm, ROUTE_W), lambda p, i: (i, 0))],
        out_specs=[
            pl.BlockSpec((1, 8, tm), lambda p, i: (i * p, 0, 0)),
            pl.BlockSpec((1, LANE), lambda p, i: (0, 0)),
        ],
        out_shape=[
            jax.ShapeDtypeStruct((nt, 8, tm), jnp.int32),
            jax.ShapeDtypeStruct((1, LANE), F32),
        ],
        scratch_shapes=[pltpu.VMEM((1, LANE), F32), pltpu.VMEM((1, LANE), F32)],
        compiler_params=_cparams(("arbitrary", "arbitrary")),
        name="moe_rank",
    )(comb)


def _row_copy(src_hbm, dst_hbm, src_row, dst_row, sem):
    return pltpu.make_async_copy(src_hbm.at[pl.ds(src_row, 1)], dst_hbm.at[pl.ds(dst_row, 1)], sem)


PAD_BITS = MOE_TR.bit_length() - 1


DISPATCH_BUFS = 3


def _dispatch_kernel(ps_ref, pc_ref, nt_ref, pos_ref, rows_hbm, xs_hbm, rbuf, zbuf, lsem, sem, zsem):
    i = pl.program_id(0)
    n_steps = pl.num_programs(0)
    tm = rbuf.shape[1]
    slot = i % 2
    cur = i % DISPATCH_BUFS

    def load(block, b):
        return pltpu.make_async_copy(rows_hbm.at[pl.ds(pl.multiple_of(block * tm, tm), tm)], rbuf.at[b], lsem.at[b])

    def wait_rows(b, s):
        pltpu.make_async_copy(rbuf.at[b], xs_hbm.at[pl.ds(0, tm)], sem.at[s]).wait()

    @pl.when(i == 0)
    def _():
        load(0, 0).start()

    @pl.when(i + 1 < n_steps)
    def _():
        load(i + 1, (i + 1) % DISPATCH_BUFS).start()

    load(i, cur).wait()
    for r in range(tm):
        _row_copy(rbuf.at[cur], xs_hbm, r, pos_ref[0, r], sem.at[slot]).start()

    @pl.when(i > 0)
    def _():
        wait_rows((i + DISPATCH_BUFS - 1) % DISPATCH_BUFS, 1 - slot)

    @pl.when(i == pl.num_programs(0) - 1)
    def _():
        wait_rows(cur, slot)
        zbuf[...] = jnp.zeros_like(zbuf)

        def runs(b, fn):
            first = ps_ref[b]
            count = pc_ref[b]
            head = jnp.minimum((-first) & (SUBLANES - 1), count)
            for j in range(SUBLANES - 1):
                @pl.when(j < head)
                def _():
                    fn(pltpu.make_async_copy(zbuf.at[pl.ds(0, 1)], xs_hbm.at[pl.ds(first + j, 1)], zsem.at[0]))
            rest = count - head
            for k in range(SUBLANES.bit_length() - 1, PAD_BITS):
                size = 1 << k
                start = pl.multiple_of(first + head + ((rest >> (k + 1)) << (k + 1)), SUBLANES)

                @pl.when((rest & size) != 0)
                def _():
                    fn(pltpu.make_async_copy(zbuf.at[pl.ds(0, size)], xs_hbm.at[pl.ds(start, size)], zsem.at[0]))

        def issue(b, carry):
            runs(b, lambda cp: cp.start())
            return carry

        def drain(b, carry):
            runs(b, lambda cp: cp.wait())
            return carry

        lax.fori_loop(0, N_BUCKETS, issue, 0)
        lax.fori_loop(0, N_BUCKETS, drain, 0)

        def tile_copy(j):
            return pltpu.make_async_copy(zbuf, xs_hbm.at[pl.ds(pl.multiple_of(j * MOE_TR, MOE_TR), MOE_TR)],
                                         zsem.at[0])

        n_tiles = xs_hbm.shape[0] // MOE_TR
        lax.fori_loop(nt_ref[0], n_tiles, lambda j, c: (tile_copy(j).start(), c)[1], 0)
        lax.fori_loop(nt_ref[0], n_tiles, lambda j, c: (tile_copy(j).wait(), c)[1], 0)


def _dispatch(pad_start, pad_count, n_used, pos3, rows, n_slots):
    t = rows.shape[0]
    tm = RANK_TM
    grid_spec = pltpu.PrefetchScalarGridSpec(
        num_scalar_prefetch=3,
        grid=(t // tm,),
        in_specs=[
            pl.BlockSpec((None, 1, tm), lambda i, ps, pc, nt: (i, 0, 0), memory_space=pltpu.SMEM),
            pl.BlockSpec(memory_space=pl.ANY),
        ],
        out_specs=pl.BlockSpec(memory_space=pl.ANY),
        scratch_shapes=[
            pltpu.VMEM((DISPATCH_BUFS, tm, ROW_W), F32),
            pltpu.VMEM((MOE_TR, ROW_W), F32),
            pltpu.SemaphoreType.DMA((DISPATCH_BUFS,)),
            pltpu.SemaphoreType.DMA((2,)),
            pltpu.SemaphoreType.DMA((1,)),
        ],
    )
    return pl.pallas_call(
        _dispatch_kernel,
        grid_spec=grid_spec,
        out_shape=jax.ShapeDtypeStruct((n_slots, ROW_W), F32),
        compiler_params=_cparams(("arbitrary",)),
        name="moe_dispatch",
    )(pad_start, pad_count, n_used, pos3, rows)


def _gmoe_kernel(ea_ref, eb_ref, nt_ref, xs_ref, wga_ref, wua_ref, wda_ref, wgb_ref, wub_ref, wdb_ref, ys_ref):
    i = pl.program_id(0)

    @pl.when(i < nt_ref[0])
    def _():
        x = xs_ref[...]
        h = x[:, 0:D].astype(BF16)
        comb = x[:, D:D + ROUTE_W]
        lane = lax.broadcasted_iota(jnp.int32, comb.shape, 1)
        acc = None
        for e_ref, wg_ref, wu_ref, wd_ref in ((ea_ref, wga_ref, wua_ref, wda_ref), (eb_ref, wgb_ref, wub_ref, wdb_ref)):
            cw = jnp.sum(jnp.where(lane == N_GROUPS + e_ref[i], comb, 0.0), axis=-1, keepdims=True)
            act = _silu(_dot(h, wg_ref[...])) * _dot(h, wu_ref[...])
            part = _dot((act * cw).astype(BF16), wd_ref[...])
            acc = part if acc is None else acc + part
        ys_ref[...] = acc

    @pl.when(i >= nt_ref[0])
    def _():
        ys_ref[...] = jnp.zeros_like(ys_ref)


def _gmoe(exp_a, exp_b, n_used, xs, wg, wu, wd):
    n_tiles = xs.shape[0] // MOE_TR
    tr = MOE_TR
    wspec = lambda shape, which: pl.BlockSpec(
        (None,) + shape, lambda i, ea, eb, nt: ((ea, eb)[which][i], 0, 0))
    grid_spec = pltpu.PrefetchScalarGridSpec(
        num_scalar_prefetch=3,
        grid=(n_tiles,),
        in_specs=[
            pl.BlockSpec((tr, ROW_W), lambda i, ea, eb, nt: (jnp.where(i < nt[0], i, 0), 0)),
            wspec((D, D_FF), 0), wspec((D, D_FF), 0), wspec((D_FF, D), 0),
            wspec((D, D_FF), 1), wspec((D, D_FF), 1), wspec((D_FF, D), 1),
        ],
        out_specs=pl.BlockSpec((tr, D), lambda i, ea, eb, nt: (i, 0)),
    )
    return pl.pallas_call(
        _gmoe_kernel,
        grid_spec=grid_spec,
        out_shape=jax.ShapeDtypeStruct((n_tiles * tr, D), F32),
        compiler_params=_cparams(("arbitrary",)),
        name="moe_grouped",
    )(exp_a, exp_b, n_used, xs, wg, wu, wd, wg, wu, wd)


def _combine_kernel(cur_ref, nxt_ref, x1_ref, ys_hbm, y_ref, buf, sem):
    i = pl.program_id(0)
    slot = i % 2
    tm = x1_ref.shape[0]

    def start_gather(idx_ref, s):
        for r in range(tm):
            _row_copy(ys_hbm, buf.at[s], idx_ref[0, r], r, sem.at[s]).start()

    @pl.when(i == 0)
    def _():
        start_gather(cur_ref, 0)

    @pl.when(i + 1 < pl.num_programs(0))
    def _():
        start_gather(nxt_ref, 1 - slot)

    pltpu.make_async_copy(ys_hbm.at[pl.ds(0, tm)], buf.at[slot], sem.at[slot]).wait()
    y_ref[...] = x1_ref[...] + buf[slot]


def _combine(pos3, x1, ys):
    t = x1.shape[0]
    tm = pos3.shape[2]
    nt = t // tm
    idx = lambda f: pl.BlockSpec((None, 1, tm), f, memory_space=pltpu.SMEM)
    return pl.pallas_call(
        _combine_kernel,
        grid=(nt,),
        in_specs=[
            idx(lambda i: (i, 0, 0)),
            idx(lambda i: (jnp.minimum(i + 1, nt - 1), 0, 0)),
            pl.BlockSpec((tm, D), lambda i: (i, 0)),
            pl.BlockSpec(memory_space=pl.ANY),
        ],
        out_specs=pl.BlockSpec((tm, D), lambda i: (i, 0)),
        out_shape=jax.ShapeDtypeStruct((t, D), F32),
        scratch_shapes=[pltpu.VMEM((2, tm, D), F32), pltpu.SemaphoreType.DMA((2,))],
        compiler_params=_cparams(("arbitrary",)),
        name="moe_combine",
    )(pos3, pos3, x1, ys)


def _moe_routed(x1, rows, comb, wg, wu, wd):
    t = x1.shape[0]
    tr = MOE_TR
    n_tiles = (t + N_BUCKETS * (tr - 1)) // tr
    pos3, cnt = _rank(comb)
    pos = pos3[:, 0, :]
    count = cnt[0, :N_BUCKETS].astype(jnp.int32)
    padded = (count + tr - 1) // tr * tr
    ends = jnp.cumsum(padded)
    pad_start = ends - padded + count
    pad_count = padded - count
    n_used = (ends[-1:] // tr).astype(jnp.int32)
    tile_start = jnp.arange(n_tiles, dtype=jnp.int32) * tr
    tile_bucket = jnp.minimum(jnp.sum(tile_start[:, None] >= ends[None, :], axis=1), N_BUCKETS - 1)
    first, second = _bucket_experts()
    xs = _dispatch(pad_start, pad_count, n_used, pos.reshape(t // RANK_TM, 1, RANK_TM), rows, n_tiles * tr)
    ys = _gmoe(first[tile_bucket], second[tile_bucket], n_used, xs, wg, wu, wd)
    return _combine(pos.reshape(t // COMBINE_TM, 1, COMBINE_TM), x1, ys)


def _moe_kernel(x1_ref, h_ref, comb_ref, wg_ref, wu_ref, wd_ref, o_ref):
    e = pl.program_id(1)

    @pl.when(e == 0)
    def _():
        o_ref[...] = x1_ref[...]

    h = h_ref[...]
    lane = lax.broadcasted_iota(jnp.int32, comb_ref.shape, 1)
    cw = jnp.sum(jnp.where(lane == e + N_GROUPS, comb_ref[...], 0.0), axis=-1, keepdims=True)
    act = _silu(_dot(h, wg_ref[0])) * _dot(h, wu_ref[0])
    o_ref[...] += _dot((act * cw).astype(BF16), wd_ref[0])


def _moe(x1, h, comb, wg, wu, wd, tm):
    t = x1.shape[0]
    return pl.pallas_call(
        _moe_kernel,
        grid=(t // tm, N_EXPERTS),
        in_specs=[
            pl.BlockSpec((tm, D), lambda i, e: (i, 0)),
            pl.BlockSpec((tm, D), lambda i, e: (i, 0)),
            pl.BlockSpec((tm, ROUTE_W), lambda i, e: (i, 0)),
            pl.BlockSpec((1, D, D_FF), lambda i, e: (e, 0, 0)),
            pl.BlockSpec((1, D, D_FF), lambda i, e: (e, 0, 0)),
            pl.BlockSpec((1, D_FF, D), lambda i, e: (e, 0, 0)),
        ],
        out_specs=pl.BlockSpec((tm, D), lambda i, e: (i, 0)),
        out_shape=jax.ShapeDtypeStruct((t, D), F32),
        compiler_params=_cparams(("parallel", "arbitrary")),
        name="moe",
    )(x1, h, comb, wg, wu, wd)


def _sample_prep_kernel(xbc_ref, q_ref, k_ref, mq_ref, dt_ref, sc_ref, cw_ref, cb_ref, dtb_ref, a_ref, e_ref,
                        qg_ref, kg_ref, mqg_ref, cos_ref, slo_ref, shi_ref,
                        conv_ref, xs_ref, bc_ref, dtx_ref, dec_ref, qn_ref, kn_ref, mqn_ref):
    n = xbc_ref.shape[0]
    x_raw = xbc_ref[...].astype(F32)
    acc = x_raw * cw_ref[SSM_CONV - 1:SSM_CONV, :] + cb_ref[...]
    for j in range(SSM_CONV - 1):
        acc = acc + sc_ref[:, j * CONV_CH:(j + 1) * CONV_CH] * cw_ref[j:j + 1, :]
    conv_ref[:, 0:(SSM_CONV - 2) * CONV_CH] = sc_ref[:, CONV_CH:(SSM_CONV - 1) * CONV_CH]
    conv_ref[:, (SSM_CONV - 2) * CONV_CH:] = x_raw
    act = _silu(acc)
    xs = act[:, 0:SSM_INNER]
    xs_ref[...] = xs
    bc_ref[...] = act[:, SSM_INNER:]
    dt = _softplus(dt_ref[...] + dtb_ref[...])
    e = e_ref[...]
    dtx = _dot_exact(dt, e) * xs
    dec = _dot_exact(jnp.exp(dt * a_ref[...]), e)
    dtx_ref[...] = dtx.T
    dec_ref[...] = dec.T

    lane = lax.broadcasted_iota(jnp.int32, (n, LANE), 1)
    low_half = lane < ATT_HD
    cos_t = jnp.broadcast_to(cos_ref[...], (n, LANE))
    sin_lo = jnp.broadcast_to(slo_ref[...], (n, LANE))
    sin_hi = jnp.broadcast_to(shi_ref[...], (n, LANE))
    qf = q_ref[...].astype(F32)
    for j in range(D // LANE):
        sl = slice(j * LANE, (j + 1) * LANE)
        qn_ref[:, sl] = _norm_rope_tile(qf[:, sl], qg_ref[...], cos_t, sin_lo, sin_hi, low_half)
    kf = k_ref[...].astype(F32)
    for j in range(ATT_KVD // LANE):
        sl = slice(j * LANE, (j + 1) * LANE)
        kn_ref[:, sl] = _norm_rope_tile(kf[:, sl], kg_ref[...], cos_t, sin_lo, sin_hi, low_half)
    mq = mq_ref[...].astype(F32)
    for h in range(MEM_HEADS):
        sl = slice(h * MEM_HD, (h + 1) * MEM_HD)
        mh = mq[:, sl]
        ms = jnp.mean(mh * mh, axis=-1, keepdims=True)
        mqn_ref[:, sl] = mh * lax.rsqrt(ms + EPS) * mqg_ref[...]


def _sample_prep(proj, dt_raw, sc2d, cw, cb, dtb, a, e, qg2, kg2, mqg, cos_t, sin_lo, sin_hi):
    n = proj.shape[0]
    c2 = lambda i: (0, 0)
    full = lambda shape: pl.BlockSpec(shape, c2)
    return pl.pallas_call(
        _sample_prep_kernel,
        grid=(1,),
        in_specs=[
            pl.BlockSpec((n, CONV_CH), lambda i: (0, COL_XBC_BLK)),
            pl.BlockSpec((n, D), lambda i: (0, COL_Q_BLK)),
            pl.BlockSpec((n, ATT_KVD), lambda i: (0, COL_K_BLK)),
            pl.BlockSpec((n, D), lambda i: (0, COL_MQ_BLK)),
            full((n, LANE)),
            full((n, (SSM_CONV - 1) * CONV_CH)),
            full((SSM_CONV, CONV_CH)),
            full((1, CONV_CH)),
            full((1, LANE)),
            full((1, LANE)),
            full((LANE, SSM_INNER)),
            full((1, LANE)),
            full((1, LANE)),
            full((1, MEM_HD)),
            full((1, LANE)),
            full((1, LANE)),
            full((1, LANE)),
        ],
        out_specs=[
            full((n, (SSM_CONV - 1) * CONV_CH)),
            full((n, SSM_INNER)),
            full((n, 2 * SSM_GROUPS * SSM_STATE)),
            full((SSM_INNER, n)),
            full((SSM_INNER, n)),
            full((n, D)),
            full((n, ATT_KVD)),
            full((n, D)),
        ],
        out_shape=[
            jax.ShapeDtypeStruct((n, (SSM_CONV - 1) * CONV_CH), F32),
            jax.ShapeDtypeStruct((n, SSM_INNER), F32),
            jax.ShapeDtypeStruct((n, 2 * SSM_GROUPS * SSM_STATE), F32),
            jax.ShapeDtypeStruct((SSM_INNER, n), F32),
            jax.ShapeDtypeStruct((SSM_INNER, n), F32),
            jax.ShapeDtypeStruct((n, D), F32),
            jax.ShapeDtypeStruct((n, ATT_KVD), F32),
            jax.ShapeDtypeStruct((n, D), F32),
        ],
        compiler_params=_cparams(("arbitrary",)),
        name="sample_prep",
    )(proj, proj, proj, proj, dt_raw, sc2d, cw, cb, dtb, a, e, qg2, kg2, mqg, cos_t, sin_lo, sin_hi)


SSM_S_TILE = 8


def _sample_ssm_kernel(st_ref, dtx_ref, dec_ref, bc_ref, so_ref, yt_ref):
    i = pl.program_id(0)
    n = dtx_ref.shape[1]

    @pl.when(i == 0)
    def _():
        yt_ref[...] = jnp.zeros_like(yt_ref)

    gw = SSM_HPG * SSM_HEADDIM
    lane = lax.broadcasted_iota(jnp.int32, (gw, n), 1)
    nbc = SSM_GROUPS * SSM_STATE
    for s in range(SSM_S_TILE):
        sel = lane == (i * SSM_S_TILE + s)
        for g in range(SSM_GROUPS):
            rows = slice(g * gw, (g + 1) * gw)
            dtx_c = jnp.sum(jnp.where(sel, dtx_ref[rows, :], 0.0), axis=1, keepdims=True)
            dec_c = jnp.sum(jnp.where(sel, dec_ref[rows, :], 0.0), axis=1, keepdims=True)
            bm = bc_ref[s:s + 1, g * SSM_STATE:(g + 1) * SSM_STATE]
            cm = bc_ref[s:s + 1, nbc + g * SSM_STATE:nbc + (g + 1) * SSM_STATE]
            hn = st_ref[s, rows, :] * dec_c + dtx_c * bm
            so_ref[s, rows, :] = hn
            yc = _dot_nt(hn.astype(BF16), jnp.broadcast_to(cm, (n, SSM_STATE)).astype(BF16))
            yt_ref[rows, :] = jnp.where(sel, yc, yt_ref[rows, :])


def _sample_ssm(state, dtx_t, dec_t, bc):
    n = state.shape[0]
    return pl.pallas_call(
        _sample_ssm_kernel,
        grid=(n // SSM_S_TILE,),
        in_specs=[
            pl.BlockSpec((SSM_S_TILE, SSM_INNER, SSM_STATE), lambda i: (i, 0, 0)),
            pl.BlockSpec((SSM_INNER, n), lambda i: (0, 0)),
            pl.BlockSpec((SSM_INNER, n), lambda i: (0, 0)),
            pl.BlockSpec((SSM_S_TILE, 2 * SSM_GROUPS * SSM_STATE), lambda i: (i, 0)),
        ],
        out_specs=[
            pl.BlockSpec((SSM_S_TILE, SSM_INNER, SSM_STATE), lambda i: (i, 0, 0)),
            pl.BlockSpec((SSM_INNER, n), lambda i: (0, 0)),
        ],
        out_shape=[
            jax.ShapeDtypeStruct((n, SSM_INNER, SSM_STATE), F32),
            jax.ShapeDtypeStruct((SSM_INNER, n), F32),
        ],
        compiler_params=_cparams(("arbitrary",)),
        name="sample_ssm",
    )(state, dtx_t, dec_t, bc)


def _sample_post_kernel(yt_ref, xs_ref, z_ref, dexp_ref, ng_ref, y_ref):
    y = yt_ref[...].T + dexp_ref[...] * xs_ref[...]
    yg = y * _silu(z_ref[...].astype(F32))
    ms = jnp.mean(yg * yg, axis=-1, keepdims=True)
    y_ref[...] = yg * lax.rsqrt(ms + EPS) * ng_ref[...]


def _sample_post(y_t, xs, proj, dexp, ng):
    n = xs.shape[0]
    c2 = lambda i: (0, 0)
    return pl.pallas_call(
        _sample_post_kernel,
        grid=(1,),
        in_specs=[
            pl.BlockSpec((SSM_INNER, n), c2),
            pl.BlockSpec((n, SSM_INNER), c2),
            pl.BlockSpec((n, SSM_INNER), lambda i: (0, COL_Z_BLK)),
            pl.BlockSpec((1, SSM_INNER), c2),
            pl.BlockSpec((1, SSM_INNER), c2),
        ],
        out_specs=pl.BlockSpec((n, SSM_INNER), c2),
        out_shape=jax.ShapeDtypeStruct((n, SSM_INNER), F32),
        compiler_params=_cparams(("arbitrary",)),
        name="sample_post",
    )(y_t, xs, proj, dexp, ng)


SWA_S_TILE = 8


def _sample_swa_kernel(q_ref, kn_ref, v_ref, ck_ref, cv_ref, sink_ref, y_ref, ko_ref, vo_ref, sc_ref):
    w = WINDOW
    scale = ATT_HD ** -0.5
    newest = lax.broadcasted_iota(jnp.int32, (ATT_HD, w), 1) == w - 1
    units = [(s, kv) for s in range(SWA_S_TILE) for kv in range(ATT_KV)]
    for u, (s, kv) in enumerate(units):
        dims = slice(kv * ATT_HD, (kv + 1) * ATT_HD)
        kt = jnp.where(newest, kn_ref[dims, s:s + 1], pltpu.roll(ck_ref[s, kv], w - 1, axis=1))
        vt = jnp.where(newest, v_ref[dims, s:s + 1], pltpu.roll(cv_ref[s, kv], w - 1, axis=1))
        ko_ref[s, kv] = kt
        vo_ref[s, kv] = vt
        sc_ref[u * SUBLANES:(u + 1) * SUBLANES, :] = _dot(q_ref[s, kv].astype(BF16), kt.astype(BF16)) * scale
    sc = sc_ref[...]
    snk = jnp.concatenate([sink_ref[...]] * SWA_S_TILE, axis=0)
    m = jnp.maximum(jnp.max(sc, axis=-1, keepdims=True), snk)
    p = jnp.exp(sc - m)
    pn = p / (jnp.sum(p, axis=-1, keepdims=True) + jnp.exp(snk - m))
    for u, (s, kv) in enumerate(units):
        pu = pn[u * SUBLANES:(u + 1) * SUBLANES].astype(BF16)
        y_ref[s, kv] = _dot_nt(pu, vo_ref[s, kv].astype(BF16))


def _sample_swa(q4, kn4, v4, ck_t, cv_t, sink_col, l):
    n = q4.shape[0]
    w = WINDOW
    st = SWA_S_TILE
    cache = pl.BlockSpec((None, st, ATT_KV, ATT_HD, w), lambda i: (l, i, 0, 0, 0))
    new = pl.BlockSpec((None, ATT_KVD, st), lambda i: (i, 0, 0))
    out = pl.BlockSpec((st, ATT_KV, ATT_HD, w), lambda i: (i, 0, 0, 0))
    qspec = pl.BlockSpec((st, ATT_KV, SUBLANES, ATT_HD), lambda i: (i, 0, 0, 0))
    return pl.pallas_call(
        _sample_swa_kernel,
        grid=(n // st,),
        in_specs=[
            qspec, new, new, cache, cache,
            pl.BlockSpec((ATT_KV * SUBLANES, 1), lambda i: (0, 0)),
        ],
        out_specs=[qspec, out, out],
        out_shape=[
            jax.ShapeDtypeStruct((n, ATT_KV, SUBLANES, ATT_HD), F32),
            jax.ShapeDtypeStruct((n, ATT_KV, ATT_HD, w), F32),
            jax.ShapeDtypeStruct((n, ATT_KV, ATT_HD, w), F32),
        ],
        scratch_shapes=[pltpu.VMEM((st * ATT_KV * SUBLANES, w), F32)],
        compiler_params=_cparams(("parallel",)),
        name="sample_swa",
    )(q4, kn4, v4, ck_t, cv_t, sink_col)


MEM_S_TILE = 4


def _sample_mem_kernel(q_ref, k_ref, v_ref, y_ref):
    scale = MEM_HD ** -0.5
    for s in range(MEM_S_TILE):
        sc = jnp.sum(k_ref[s] * q_ref[s][None], axis=-1, keepdims=True) * scale
        m = jnp.max(sc, axis=0, keepdims=True)
        p = jnp.exp(sc - m)
        den = jnp.sum(p, axis=0, keepdims=True)
        o = jnp.sum(p * v_ref[s], axis=0, keepdims=True) / den
        y_ref[s] = o[0]


def _sample_mem(q3, ck, cv, l):
    n = q3.shape[0]
    mem_len = ck.shape[2]
    st = MEM_S_TILE
    cache = pl.BlockSpec((None, st, mem_len, MEM_HEADS, MEM_HD), lambda i: (l, i, 0, 0, 0))
    return pl.pallas_call(
        _sample_mem_kernel,
        grid=(n // st,),
        in_specs=[pl.BlockSpec((st, MEM_HEADS, MEM_HD), lambda i: (i, 0, 0)), cache, cache],
        out_specs=pl.BlockSpec((st, MEM_HEADS, MEM_HD), lambda i: (i, 0, 0)),
        out_shape=jax.ShapeDtypeStruct((n, MEM_HEADS, MEM_HD), F32),
        compiler_params=_cparams(("parallel",)),
        name="sample_mem",
    )(q3, ck, cv)


def _prep_weights(lw):
    w_in = lw['w_in']
    cols = [w_in[:, 0:OFF_Z], w_in[:, OFF_XBC:OFF_DT], w_in[:, OFF_Z:OFF_XBC], w_in[:, OFF_Q:OFF_K],
            w_in[:, OFF_MQ:OFF_MQ + D], w_in[:, OFF_K:OFF_V], w_in[:, OFF_V:OFF_MQ]]
    p = {}
    p['w_main'] = jnp.concatenate(cols, axis=1).astype(BF16)
    p['w_dt'] = jnp.pad(w_in[:, OFF_DT:OFF_Q], ((0, 0), (0, LANE - SSM_HEADS))).astype(BF16)
    p['norm1_g'] = lw['norm1_g'].reshape(1, D)
    p['cw'] = lw['ssm_conv_w']
    p['cb'] = lw['ssm_conv_b'].reshape(1, CONV_CH)
    pad_h = (0, LANE - SSM_HEADS)
    p['dtb'] = jnp.pad(lw['ssm_dt_bias'].astype(F32), pad_h).reshape(1, LANE)
    p['a'] = jnp.pad(-jnp.exp(lw['ssm_a_log'].astype(F32)), pad_h).reshape(1, LANE)
    p['dexp'] = jnp.repeat(lw['ssm_d'], SSM_HEADDIM).reshape(1, SSM_INNER)
    p['ssm_ng'] = lw['ssm_norm_g'].reshape(1, SSM_INNER)
    head_of_ch = jnp.arange(SSM_INNER) // SSM_HEADDIM
    p['e'] = (jnp.arange(LANE)[:, None] == head_of_ch[None, :]).astype(F32)
    p['qg2'] = jnp.tile(lw['att_q_norm_g'], 2).reshape(1, LANE)
    p['kg2'] = jnp.tile(lw['att_k_norm_g'], 2).reshape(1, LANE)
    p['sink_row'] = lw['att_sink'].astype(F32).reshape(1, ATT_HEADS)
    grp = ATT_HEADS // ATT_KV
    p['sink_col'] = jnp.pad(lw['att_sink'].astype(F32).reshape(ATT_KV, grp),
                            ((0, 0), (0, SUBLANES - grp))).reshape(ATT_KV * SUBLANES, 1)
    p['mem_g'] = lw['mem_norm_g'].reshape(1, D)
    p['w_mem_kv'] = lw['w_mem_kv'].astype(BF16)
    p['mqg'] = lw['mem_q_norm_g'].reshape(1, MEM_HD)
    p['mkg'] = lw['mem_k_norm_g'].reshape(1, MEM_HD)
    p['ws'] = lw['w_br_ssm'].astype(BF16)
    p['wa'] = lw['w_br_swa'].astype(BF16)
    p['wm'] = lw['w_br_mem'].astype(BF16)
    p['wo'] = lw['w_out'].astype(BF16)
    p['g2'] = lw['norm2_g'].reshape(1, D)
    pad_r = ROUTE_W - N_GROUPS - N_EXPERTS
    p['wr'] = jnp.pad(jnp.concatenate([lw['w_router_group'], lw['w_router_expert']], axis=1).astype(F32),
                      ((0, 0), (0, pad_r)))
    p['br'] = jnp.pad(jnp.concatenate([lw['b_router_group'], lw['b_router_expert']]).astype(F32),
                      (0, pad_r)).reshape(1, ROUTE_W)
    p['wg'] = lw['w_exp_gate'].astype(BF16)
    p['wu'] = lw['w_exp_up'].astype(BF16)
    p['wd'] = lw['w_exp_down'].astype(BF16)
    return p


def _pick(n, prefs):
    for c in prefs:
        if n % c == 0:
            return c
    return n


def _tail(x, proj, y_ssm, y_swa, y_mem, p):
    t = x.shape[0]
    routed = t % RANK_TM == 0
    x1, h, comb = _back(x, proj, y_ssm, y_swa, y_mem, p['ws'], p['wa'], p['wm'], p['wo'], p['g2'],
                        p['wr'], p['br'], _pick(t, (512, 256, 128)), routed)
    if routed:
        return _moe_routed(x1, h, comb, p['wg'], p['wu'], p['wd'])
    return _moe(x1, h, comb, p['wg'], p['wu'], p['wd'], _pick(t, (1024, 512, 256, 128)))


def _prompt_layer(x, mem, p):
    nb, seq, _ = x.shape
    t = nb * seq
    nc = seq // CHUNK
    xf = x.reshape(t, D)
    proj, dt_raw = _front(xf, p['norm1_g'], p['w_main'], p['w_dt'], _pick(t, (1024, 512, 256, 128)), 1536)
    y_ssm, conv_new, ssm_new = _ssd(proj, dt_raw, p['cw'], p['cb'], p['dtb'], p['a'], p['dexp'],
                                    p['ssm_ng'], nb, nc)
    cos_t, sin_lo, sin_hi = _rope_tables(jnp.arange(seq))
    y_swa, k_new, v_new = _swa(proj, p['qg2'], p['kg2'], cos_t, sin_lo, sin_hi, p['sink_row'], nb, seq)
    mem_len = mem.shape[1]
    mk, mv = _memkv(mem.reshape(nb * mem_len, D), p['mem_g'], p['w_mem_kv'], p['mkg'],
                    _pick(nb * mem_len, (512, 256, 128)))
    y_mem = _memattn(proj, mk, mv, p['mqg'], nb, seq, mem_len, _pick(seq, (512, 256, 128)))
    y = _tail(xf, proj, y_ssm, y_swa, y_mem, p)
    return (y.reshape(nb, seq, D), conv_new,
            ssm_new.reshape(nb, SSM_HEADS, SSM_HEADDIM, SSM_STATE),
            k_new.reshape(nb, WINDOW, ATT_KV, ATT_HD), v_new.reshape(nb, WINDOW, ATT_KV, ATT_HD),
            mk.reshape(nb, mem_len, MEM_HEADS, MEM_HD), mv.reshape(nb, mem_len, MEM_HEADS, MEM_HD))


def _sample_layer(x, conv_st, ssm_st, swa_k, swa_v, mem_k, mem_v, l, p):
    n = x.shape[0]
    xf = x.reshape(n, D)
    proj, dt_raw = _front(xf, p['norm1_g'], p['w_main'], p['w_dt'], n, 1536)
    cos_t, sin_lo, sin_hi = _rope_tables(jnp.full((1,), PAST_LEN, jnp.int32))
    sc2d = conv_st.reshape(n, (SSM_CONV - 1) * CONV_CH)
    conv_new, xs, bc, dtx_t, dec_t, qn, kn, mqn = _sample_prep(
        proj, dt_raw, sc2d, p['cw'], p['cb'], p['dtb'], p['a'], p['e'], p['qg2'], p['kg2'], p['mqg'],
        cos_t, sin_lo, sin_hi)
    ssm_new, y_t = _sample_ssm(ssm_st.reshape(n, SSM_INNER, SSM_STATE), dtx_t, dec_t, bc)
    y_ssm = _sample_post(y_t, xs, proj, p['dexp'], p['ssm_ng'])
    grp = ATT_HEADS // ATT_KV
    v_raw = proj[:, COL_V_BLK * ATT_KVD:(COL_V_BLK + 1) * ATT_KVD].astype(F32)
    to_t = (0, 1, 3, 4, 2)
    per_step = lambda a: a.T.reshape(ATT_KVD, n // SWA_S_TILE, SWA_S_TILE).transpose(1, 0, 2)
    y_swa, k_new, v_new = _sample_swa(
        jnp.pad(qn.reshape(n, ATT_KV, grp, ATT_HD), ((0, 0), (0, 0), (0, SUBLANES - grp), (0, 0))),
        per_step(kn), per_step(v_raw), swa_k.transpose(to_t), swa_v.transpose(to_t), p['sink_col'], l)
    y_swa = y_swa[:, :, 0:grp, :].reshape(n, D)
    k_new = k_new.transpose(0, 3, 1, 2)
    v_new = v_new.transpose(0, 3, 1, 2)
    y_mem = _sample_mem(mqn.reshape(n, MEM_HEADS, MEM_HD), mem_k, mem_v, l)
    y = _tail(xf, proj, y_ssm, y_swa.astype(BF16), y_mem.reshape(n, D).astype(BF16), p)
    return (y.reshape(n, 1, D), conv_new.reshape(n, SSM_CONV - 1, CONV_CH),
            ssm_new.reshape(n, SSM_HEADS, SSM_HEADDIM, SSM_STATE), k_new, v_new)


def kernel(x_prompt, x_sample, state_conv, state_ssm, cache_swa_k, cache_swa_v, cache_mem_k, cache_mem_v,
           mem_prompt, norm1_g, w_in, ssm_conv_w, ssm_conv_b, ssm_dt_bias, ssm_a_log, ssm_d, ssm_norm_g,
           att_q_norm_g, att_k_norm_g, att_sink, mem_norm_g, w_mem_kv, mem_q_norm_g, mem_k_norm_g,
           w_br_ssm, w_br_swa, w_br_mem, w_out, norm2_g, w_router_group, b_router_group,
           w_router_expert, b_router_expert, w_exp_gate, w_exp_up, w_exp_down):
    weights = dict(norm1_g=norm1_g, w_in=w_in, ssm_conv_w=ssm_conv_w, ssm_conv_b=ssm_conv_b,
                   ssm_dt_bias=ssm_dt_bias, ssm_a_log=ssm_a_log, ssm_d=ssm_d, ssm_norm_g=ssm_norm_g,
                   att_q_norm_g=att_q_norm_g, att_k_norm_g=att_k_norm_g, att_sink=att_sink,
                   mem_norm_g=mem_norm_g, w_mem_kv=w_mem_kv, mem_q_norm_g=mem_q_norm_g,
                   mem_k_norm_g=mem_k_norm_g, w_br_ssm=w_br_ssm, w_br_swa=w_br_swa, w_br_mem=w_br_mem,
                   w_out=w_out, norm2_g=norm2_g, w_router_group=w_router_group,
                   b_router_group=b_router_group, w_router_expert=w_router_expert,
                   b_router_expert=b_router_expert, w_exp_gate=w_exp_gate, w_exp_up=w_exp_up,
                   w_exp_down=w_exp_down)
    depth = w_in.shape[0]
    xp, xs = x_prompt, x_sample
    outs = [[] for _ in range(10)]
    for l in range(depth):
        p = _prep_weights({k: v[l] for k, v in weights.items()})
        xp, c1, c2, c3, c4, c5, c6 = _prompt_layer(xp, mem_prompt, p)
        xs, d1, d2, d3, d4 = _sample_layer(xs, state_conv[l], state_ssm[l], cache_swa_k, cache_swa_v,
                                           cache_mem_k, cache_mem_v, l, p)
        for lst, val in zip(outs, (c1, c2, c3, c4, c5, c6, d1, d2, d3, d4)):
            lst.append(val)
    return (xp, xs) + tuple(jnp.stack(o) for o in outs)
```

```python
import functools
import math

import jax
import jax.numpy as jnp
from jax import lax
from jax.experimental import pallas as pl
from jax.experimental.pallas import tpu as pltpu

F32 = jnp.float32
BF16 = jnp.bfloat16
HIGHEST = lax.Precision.HIGHEST

D = 1024
SSM_INNER = 2048
SSM_HEADDIM = 64
SSM_HEADS = 32
SSM_GROUPS = 4
SSM_HPG = SSM_HEADS // SSM_GROUPS
SSM_STATE = 128
SSM_CONV = 4
CONV_CH = SSM_INNER + 2 * SSM_GROUPS * SSM_STATE
CHUNK = 128
ATT_HEADS = 16
ATT_KV = 4
ATT_HD = 64
ATT_KVD = ATT_KV * ATT_HD
WINDOW = 128
ROPE_THETA = 10000.0
MEM_HEADS = 4
MEM_HD = 256
N_EXPERTS = 32
N_GROUPS = 4
EPG = 8
D_FF = 256
EPS = 1e-6
PAST_LEN = 16384

OFF_Z = 3 * D
OFF_XBC = OFF_Z + SSM_INNER
OFF_DT = OFF_XBC + CONV_CH
OFF_Q = OFF_DT + SSM_HEADS
OFF_K = OFF_Q + D
OFF_V = OFF_K + ATT_KVD
OFF_MQ = OFF_V + ATT_KVD

N_MAIN = 3 * D + CONV_CH + SSM_INNER + D + D + 2 * ATT_KVD
COL_XBC_BLK = 1
COL_Z_BLK = 3
COL_Q_BLK = 8
COL_MQ_BLK = 9
COL_K_BLK = 40
COL_V_BLK = 41
LANE = 128
ROUTE_W = 128

VMEM_LIMIT = 56 * 1024 * 1024


def _cparams(sem):
    return pltpu.CompilerParams(dimension_semantics=sem, vmem_limit_bytes=VMEM_LIMIT)


def _sigmoid(x):
    return 1.0 / (1.0 + jnp.exp(-x))


def _silu(x):
    h = 0.5 * x
    return h + h * jnp.tanh(h)


def _softplus(x):
    return jnp.maximum(x, 0.0) + jnp.log(1.0 + jnp.exp(-jnp.abs(x)))


def _dot(a, b):
    return jnp.dot(a, b, preferred_element_type=F32)


def _dot_nt(a, b):
    return lax.dot_general(a, b, (((1,), (1,)), ((), ())), preferred_element_type=F32)


def _dot_tn(a, b):
    return lax.dot_general(a, b, (((0,), (0,)), ((), ())), preferred_element_type=F32)


def _dot_exact(a, b):
    return jnp.dot(a, b, preferred_element_type=F32, precision=HIGHEST)


def _front_kernel(x_ref, g_ref, w_ref, wdt_ref, o_ref, dt_ref, hn_ref):
    @pl.when(pl.program_id(1) == 0)
    def _():
        x = x_ref[...]
        ms = jnp.mean(x * x, axis=-1, keepdims=True)
        hn = (x * lax.rsqrt(ms + EPS) * g_ref[...]).astype(BF16)
        hn_ref[...] = hn
        dt_ref[...] = _dot(hn, wdt_ref[...])

    o_ref[...] = _dot(hn_ref[...], w_ref[...]).astype(BF16)


def _front(x, g, w_main, w_dt, tm, tn):
    t = x.shape[0]
    return pl.pallas_call(
        _front_kernel,
        grid=(t // tm, N_MAIN // tn),
        in_specs=[
            pl.BlockSpec((tm, D), lambda i, j: (i, 0)),
            pl.BlockSpec((1, D), lambda i, j: (0, 0)),
            pl.BlockSpec((D, tn), lambda i, j: (0, j)),
            pl.BlockSpec((D, LANE), lambda i, j: (0, 0)),
        ],
        out_specs=[
            pl.BlockSpec((tm, tn), lambda i, j: (i, j)),
            pl.BlockSpec((tm, LANE), lambda i, j: (i, 0)),
        ],
        out_shape=[
            jax.ShapeDtypeStruct((t, N_MAIN), BF16),
            jax.ShapeDtypeStruct((t, LANE), F32),
        ],
        scratch_shapes=[pltpu.VMEM((tm, D), BF16)],
        compiler_params=_cparams(("parallel", "arbitrary")),
        name="front",
    )(x, g, w_main, w_dt)


SUBLANES = 8


def _ssd_kernel(xbc_ref, z_ref, dt_ref, cw_ref, cb_ref, dtb_ref, a_ref, dexp_ref, ng_ref,
                y_ref, conv_ref, ssm_ref, prev_ref, h_ref, yacc_ref):
    c = pl.program_id(1)

    @pl.when(c == 0)
    def _():
        prev_ref[...] = jnp.zeros_like(prev_ref)
        h_ref[...] = jnp.zeros_like(h_ref)

    x_raw = xbc_ref[...].astype(F32)
    prev = prev_ref[...]
    row8 = lax.broadcasted_iota(jnp.int32, (SUBLANES, CONV_CH), 0)
    row = lax.broadcasted_iota(jnp.int32, (CHUNK, CHUNK), 0)
    col = lax.broadcasted_iota(jnp.int32, (CHUNK, CHUNK), 1)
    causal = row >= col
    acc = x_raw * cw_ref[SSM_CONV - 1:SSM_CONV, :] + cb_ref[...]
    for s in range(1, SSM_CONV):
        xr = pltpu.roll(x_raw, s, axis=0)
        head = jnp.where(row8 < s, pltpu.roll(prev, s, axis=0), xr[0:SUBLANES])
        shifted = jnp.concatenate([head, xr[SUBLANES:]], axis=0)
        acc = acc + shifted * cw_ref[SSM_CONV - 1 - s:SSM_CONV - s, :]
    prev_ref[...] = x_raw[CHUNK - SUBLANES:CHUNK]
    conv_ref[0] = x_raw[CHUNK - (SSM_CONV - 1):CHUNK]
    act = _silu(acc)

    dt = _softplus(dt_ref[...] + dtb_ref[...])
    da = dt * a_ref[...]
    acum = _dot_exact(causal.astype(F32), da)
    acum_t = acum.T
    dt_t = dt.T
    exp_a = jnp.exp(acum)
    last = acum[CHUNK - 1:CHUNK, :]
    w_end = jnp.exp(last - acum) * dt
    cd = jnp.broadcast_to(jnp.exp(acum_t[:, CHUNK - 1:CHUNK]), (LANE, SSM_STATE))
    lane = lax.broadcasted_iota(jnp.int32, (CHUNK, LANE), 1)
    low_half = lane < SSM_HEADDIM

    def pair_cols(per_head, hd):
        return jnp.where(low_half, per_head[:, hd:hd + 1], per_head[:, hd + 1:hd + 2])

    xs_off = 0
    b_off = SSM_INNER
    c_off = SSM_INNER + SSM_GROUPS * SSM_STATE
    for g in range(SSM_GROUPS):
        bm = act[:, b_off + g * SSM_STATE:b_off + (g + 1) * SSM_STATE].astype(BF16)
        cm = act[:, c_off + g * SSM_STATE:c_off + (g + 1) * SSM_STATE].astype(BF16)
        cb = _dot_nt(cm, bm)
        gw = SSM_HPG * SSM_HEADDIM
        ch0 = g * gw
        h_prev = h_ref[ch0:ch0 + gw, :]
        y_off = _dot_nt(cm, h_prev.astype(BF16))
        xg = act[:, xs_off + ch0:xs_off + ch0 + gw]
        xw = []
        for pr in range(SSM_HPG // 2):
            hd0 = g * SSM_HPG + pr * 2
            xp32 = xg[:, pr * LANE:(pr + 1) * LANE]
            xpair = xp32.astype(BF16)
            yd = []
            for sub in range(2):
                hd = hd0 + sub
                seg = acum[:, hd:hd + 1] - acum_t[hd:hd + 1, :]
                decay = jnp.exp(jnp.where(causal, seg, -jnp.inf))
                wts = cb * decay * dt_t[hd:hd + 1, :]
                yd.append(_dot(wts.astype(BF16), xpair))
            cl = ch0 + pr * LANE
            y_pair = (jnp.where(low_half, yd[0], yd[1])
                      + y_off[:, pr * LANE:(pr + 1) * LANE] * pair_cols(exp_a, hd0)
                      + dexp_ref[:, cl:cl + LANE] * xp32)
            yacc_ref[:, cl:cl + LANE] = y_pair
            xw.append((xp32 * pair_cols(w_end, hd0)).astype(BF16))
        states = _dot_tn(jnp.concatenate(xw, axis=1), bm)
        for r in range(SSM_HPG):
            hd = g * SSM_HPG + r
            r0 = ch0 + r * SSM_HEADDIM
            h_ref[r0:r0 + SSM_HEADDIM, :] = (h_ref[r0:r0 + SSM_HEADDIM, :] * cd[hd:hd + 1, :]
                                             + states[r * SSM_HEADDIM:(r + 1) * SSM_HEADDIM, :])

    ssm_ref[0] = h_ref[...]
    yg = yacc_ref[...] * _silu(z_ref[...].astype(F32))
    ms = jnp.mean(yg * yg, axis=-1, keepdims=True)
    y_ref[...] = (yg * lax.rsqrt(ms + EPS) * ng_ref[...]).astype(BF16)


def _ssd(proj, dt_raw, cw, cb, dtb, a, dexp, ng, nb, nc):
    t = proj.shape[0]
    return pl.pallas_call(
        _ssd_kernel,
        grid=(nb, nc),
        in_specs=[
            pl.BlockSpec((CHUNK, CONV_CH), lambda b, c: (b * nc + c, COL_XBC_BLK)),
            pl.BlockSpec((CHUNK, SSM_INNER), lambda b, c: (b * nc + c, COL_Z_BLK)),
            pl.BlockSpec((CHUNK, LANE), lambda b, c: (b * nc + c, 0)),
            pl.BlockSpec((SSM_CONV, CONV_CH), lambda b, c: (0, 0)),
            pl.BlockSpec((1, CONV_CH), lambda b, c: (0, 0)),
            pl.BlockSpec((1, LANE), lambda b, c: (0, 0)),
            pl.BlockSpec((1, LANE), lambda b, c: (0, 0)),
            pl.BlockSpec((1, SSM_INNER), lambda b, c: (0, 0)),
            pl.BlockSpec((1, SSM_INNER), lambda b, c: (0, 0)),
        ],
        out_specs=[
            pl.BlockSpec((CHUNK, SSM_INNER), lambda b, c: (b * nc + c, 0)),
            pl.BlockSpec((1, SSM_CONV - 1, CONV_CH), lambda b, c: (b, 0, 0)),
            pl.BlockSpec((1, SSM_INNER, SSM_STATE), lambda b, c: (b, 0, 0)),
        ],
        out_shape=[
            jax.ShapeDtypeStruct((t, SSM_INNER), BF16),
            jax.ShapeDtypeStruct((nb, SSM_CONV - 1, CONV_CH), F32),
            jax.ShapeDtypeStruct((nb, SSM_INNER, SSM_STATE), F32),
        ],
        scratch_shapes=[
            pltpu.VMEM((SUBLANES, CONV_CH), F32),
            pltpu.VMEM((SSM_INNER, SSM_STATE), F32),
            pltpu.VMEM((CHUNK, SSM_INNER), F32),
        ],
        compiler_params=_cparams(("parallel", "arbitrary")),
        name="ssd_prompt",
    )(proj, proj, dt_raw, cw, cb, dtb, a, dexp, ng)


def _norm_rope_tile(xj, g2, cos_t, sin_lo, sin_hi, low_half):
    sq = xj * xj
    s_lo = jnp.sum(jnp.where(low_half, sq, 0.0), axis=-1, keepdims=True)
    s_hi = jnp.sum(jnp.where(low_half, 0.0, sq), axis=-1, keepdims=True)
    ms = jnp.where(low_half, s_lo, s_hi) * (1.0 / ATT_HD)
    xn = xj * lax.rsqrt(ms + EPS) * g2
    half = ATT_HD // 2
    return (xn * cos_t + pltpu.roll(xn, LANE - half, axis=1) * sin_lo
            + pltpu.roll(xn, half, axis=1) * sin_hi)


def _rope_tables(pos):
    half = ATT_HD // 2
    inv = ROPE_THETA ** (-jnp.arange(half, dtype=F32) / half)
    ang = pos.astype(F32)[:, None] * inv[None, :]
    cos = jnp.cos(ang)
    sin = jnp.sin(ang)
    zero = jnp.zeros_like(sin)
    cos_t = jnp.concatenate([cos, cos, cos, cos], axis=1)
    sin_lo = jnp.concatenate([-sin, zero, -sin, zero], axis=1)
    sin_hi = jnp.concatenate([zero, sin, zero, sin], axis=1)
    return cos_t, sin_lo, sin_hi


SWA_NSB = 2


def _swa_kernel(q_ref, k_ref, v_ref, qg_ref, kg_ref, cos_ref, slo_ref, shi_ref, sink_ref,
                y_ref, ko_ref, vo_ref, kd_ref, vd_ref, qs_ref):
    c = pl.program_id(1)
    w = WINDOW
    nsb = SWA_NSB
    grp = ATT_HEADS // ATT_KV
    rows = nsb * w
    lane = lax.broadcasted_iota(jnp.int32, (rows, LANE), 1)
    low_half = lane < ATT_HD
    low_half_w = lax.broadcasted_iota(jnp.int32, (w, LANE), 1) < ATT_HD
    cos_t = cos_ref[...]
    sin_lo = slo_ref[...]
    sin_hi = shi_ref[...]

    @pl.when(c == 0)
    def _():
        kd_ref[:, 0:w, :] = jnp.zeros((ATT_KV, w, LANE), BF16)
        vd_ref[:, 0:w, :] = jnp.zeros((ATT_KV, w, 2 * LANE), BF16)
        vd_ref[:, :, LANE:2 * LANE] = jnp.ones((ATT_KV, (nsb + 1) * w, LANE), BF16)

    kf = k_ref[...].astype(F32)
    vf = v_ref[...].astype(F32)
    for j in range(ATT_KVD // LANE):
        sl = slice(j * LANE, (j + 1) * LANE)
        kn = _norm_rope_tile(kf[:, sl], kg_ref[...], cos_t, sin_lo, sin_hi, low_half)
        ko_ref[0, :, sl] = kn[rows - w:rows]
        kr = pltpu.roll(kn, ATT_HD, axis=1)
        vj = vf[:, sl]
        vr = pltpu.roll(vj, ATT_HD, axis=1)
        kd_ref[2 * j, w:w + rows, :] = jnp.where(low_half, kn, kr).astype(BF16)
        kd_ref[2 * j + 1, w:w + rows, :] = jnp.where(low_half, kr, kn).astype(BF16)
        vd_ref[2 * j, w:w + rows, 0:LANE] = jnp.where(low_half, vj, vr).astype(BF16)
        vd_ref[2 * j + 1, w:w + rows, 0:LANE] = jnp.where(low_half, vr, vj).astype(BF16)
    vo_ref[0] = vf[rows - w:rows]

    qf = q_ref[...].astype(F32)
    for j in range(D // LANE):
        sl = slice(j * LANE, (j + 1) * LANE)
        qn = _norm_rope_tile(qf[:, sl], qg_ref[...] * (ATT_HD ** -0.5), cos_t, sin_lo, sin_hi, low_half)
        for par in range(2):
            h = 2 * j + par
            r0 = (h % grp) * w
            in_half = low_half if par == 0 else jnp.logical_not(low_half)
            qh = jnp.where(in_half, qn, 0.0).astype(BF16)
            for sb in range(nsb):
                qs_ref[h // grp, sb, r0:r0 + w, :] = qh[sb * w:(sb + 1) * w]

    qi = lax.broadcasted_iota(jnp.int32, (grp * w, 2 * w), 0) & (w - 1)
    kj = lax.broadcasted_iota(jnp.int32, (grp * w, 2 * w), 1)
    in_cur = (kj >= w) & ((kj - w) <= qi)
    in_prev = (kj < w) & (kj > qi)

    for sb in range(nsb):
        mask = in_cur | (in_prev & (c > 0) if sb == 0 else in_prev)
        for kv in range(ATT_KV):
            snk = jnp.concatenate(
                [jnp.broadcast_to(sink_ref[:, kv * grp + i:kv * grp + i + 1], (w, 1)) for i in range(grp)], axis=0)
            s = _dot_nt(qs_ref[kv, sb], kd_ref[kv, sb * w:(sb + 2) * w, :])
            s = jnp.where(mask, s, -jnp.inf)
            m = jnp.maximum(jnp.max(s, axis=-1, keepdims=True), snk)
            p = jnp.exp(s - m)
            den = jnp.sum(p, axis=-1, keepdims=True) + jnp.exp(snk - m)
            o = _dot(p.astype(BF16), vd_ref[kv, sb * w:(sb + 2) * w, 0:LANE]) * (1.0 / den)
            for a in range(grp // 2):
                col = kv * (grp // 2) + a
                y_ref[sb * w:(sb + 1) * w, col * LANE:(col + 1) * LANE] = jnp.where(
                    low_half_w, o[2 * a * w:(2 * a + 1) * w, :], o[(2 * a + 1) * w:(2 * a + 2) * w, :]
                ).astype(BF16)

    kd_ref[:, 0:w, :] = kd_ref[:, rows:rows + w, :]
    vd_ref[:, 0:w, 0:LANE] = vd_ref[:, rows:rows + w, 0:LANE]


def _swa(proj, qg2, kg2, cos_t, sin_lo, sin_hi, sink, nb, seq):
    t = proj.shape[0]
    w = WINDOW
    rows = SWA_NSB * w
    nc = seq // rows
    grp = ATT_HEADS // ATT_KV
    tab = pl.BlockSpec((rows, LANE), lambda b, c: (c, 0))
    return pl.pallas_call(
        _swa_kernel,
        grid=(nb, nc),
        in_specs=[
            pl.BlockSpec((rows, D), lambda b, c: (b * nc + c, COL_Q_BLK)),
            pl.BlockSpec((rows, ATT_KVD), lambda b, c: (b * nc + c, COL_K_BLK)),
            pl.BlockSpec((rows, ATT_KVD), lambda b, c: (b * nc + c, COL_V_BLK)),
            pl.BlockSpec((1, LANE), lambda b, c: (0, 0)),
            pl.BlockSpec((1, LANE), lambda b, c: (0, 0)),
            tab, tab, tab,
            pl.BlockSpec((1, ATT_HEADS), lambda b, c: (0, 0)),
        ],
        out_specs=[
            pl.BlockSpec((rows, D), lambda b, c: (b * nc + c, 0)),
            pl.BlockSpec((1, w, ATT_KVD), lambda b, c: (b, 0, 0)),
            pl.BlockSpec((1, w, ATT_KVD), lambda b, c: (b, 0, 0)),
        ],
        out_shape=[
            jax.ShapeDtypeStruct((t, D), BF16),
            jax.ShapeDtypeStruct((nb, w, ATT_KVD), F32),
            jax.ShapeDtypeStruct((nb, w, ATT_KVD), F32),
        ],
        scratch_shapes=[
            pltpu.VMEM((ATT_KV, rows + w, LANE), BF16),
            pltpu.VMEM((ATT_KV, rows + w, 2 * LANE), BF16),
            pltpu.VMEM((ATT_KV, SWA_NSB, grp * w, LANE), BF16),
        ],
        compiler_params=_cparams(("parallel", "arbitrary")),
        name="swa_prompt",
    )(proj, proj, proj, qg2, kg2, cos_t, sin_lo, sin_hi, sink)


def _memkv_kernel(m_ref, g_ref, w_ref, kg_ref, k_ref, v_ref):
    x = m_ref[...]
    ms = jnp.mean(x * x, axis=-1, keepdims=True)
    hn = (x * lax.rsqrt(ms + EPS) * g_ref[...]).astype(BF16)
    kv = _dot(hn, w_ref[...])
    for h in range(MEM_HEADS):
        kh = kv[:, h * MEM_HD:(h + 1) * MEM_HD]
        ms = jnp.mean(kh * kh, axis=-1, keepdims=True)
        k_ref[:, h * MEM_HD:(h + 1) * MEM_HD] = kh * lax.rsqrt(ms + EPS) * kg_ref[...]
    v_ref[...] = kv[:, D:]


def _memkv(mem, g, w, kg, tm):
    t = mem.shape[0]
    return pl.pallas_call(
        _memkv_kernel,
        grid=(t // tm,),
        in_specs=[
            pl.BlockSpec((tm, D), lambda i: (i, 0)),
            pl.BlockSpec((1, D), lambda i: (0, 0)),
            pl.BlockSpec((D, 2 * D), lambda i: (0, 0)),
            pl.BlockSpec((1, MEM_HD), lambda i: (0, 0)),
        ],
        out_specs=[
            pl.BlockSpec((tm, D), lambda i: (i, 0)),
            pl.BlockSpec((tm, D), lambda i: (i, 0)),
        ],
        out_shape=[jax.ShapeDtypeStruct((t, D), F32)] * 2,
        compiler_params=_cparams(("parallel",)),
        name="mem_kv",
    )(mem, g, w, kg)


def _memattn_kernel(q_ref, k_ref, v_ref, qg_ref, y_ref):
    q = q_ref[...].astype(F32)
    scale = MEM_HD ** -0.5
    for h in range(MEM_HEADS):
        sl = slice(h * MEM_HD, (h + 1) * MEM_HD)
        qh = q[:, sl]
        ms = jnp.mean(qh * qh, axis=-1, keepdims=True)
        qn = (qh * lax.rsqrt(ms + EPS) * qg_ref[...]).astype(BF16)
        s = _dot_nt(qn, k_ref[:, sl].astype(BF16)) * scale
        m = jnp.max(s, axis=-1, keepdims=True)
        p = jnp.exp(s - m)
        den = jnp.sum(p, axis=-1, keepdims=True)
        o = _dot(p.astype(BF16), v_ref[:, sl].astype(BF16)) / den
        y_ref[:, sl] = o.astype(BF16)


def _memattn(proj, mk, mv, qg, nb, seq, mem_len, tm):
    t = proj.shape[0]
    nt = seq // tm
    return pl.pallas_call(
        _memattn_kernel,
        grid=(nb, nt),
        in_specs=[
            pl.BlockSpec((tm, D), lambda b, i: (b * nt + i, COL_MQ_BLK)),
            pl.BlockSpec((mem_len, D), lambda b, i: (b, 0)),
            pl.BlockSpec((mem_len, D), lambda b, i: (b, 0)),
            pl.BlockSpec((1, MEM_HD), lambda b, i: (0, 0)),
        ],
        out_specs=pl.BlockSpec((tm, D), lambda b, i: (b * nt + i, 0)),
        out_shape=jax.ShapeDtypeStruct((t, D), BF16),
        compiler_params=_cparams(("parallel", "arbitrary")),
        name="mem_attn_prompt",
    )(proj, mk, mv, qg)


def _route(logits):
    lane = lax.broadcasted_iota(jnp.int32, logits.shape, 1)
    lanef = lane.astype(F32)
    is_g = lane < N_GROUPS
    neg = -jnp.inf
    big = 1e9
    gl = jnp.where(is_g, logits, neg)
    gmax = jnp.max(gl, axis=-1, keepdims=True)
    gidx = jnp.min(jnp.where(is_g & (logits == gmax), lanef, big), axis=-1, keepdims=True)
    pg_top = 1.0 / jnp.sum(jnp.where(is_g, jnp.exp(logits - gmax), 0.0), axis=-1, keepdims=True)
    lo = N_GROUPS + gidx * EPG
    in_g = (lanef >= lo) & (lanef < lo + EPG)
    m1 = jnp.max(jnp.where(in_g, logits, neg), axis=-1, keepdims=True)
    i1 = jnp.min(jnp.where(in_g & (logits == m1), lanef, big), axis=-1, keepdims=True)
    rest = in_g & (lanef != i1)
    m2 = jnp.max(jnp.where(rest, logits, neg), axis=-1, keepdims=True)
    i2 = jnp.min(jnp.where(rest & (logits == m2), lanef, big), axis=-1, keepdims=True)
    r = jnp.exp(m2 - m1)
    w1 = pg_top / (1.0 + r)
    w2 = pg_top * r / (1.0 + r)
    comb = jnp.where(lanef == i1, w1, 0.0) + jnp.where(lanef == i2, w2, 0.0)
    a = jnp.minimum(i1, i2) - lo
    b = jnp.maximum(i1, i2) - lo
    bucket = gidx * PAIRS_PER_GROUP + a * (2 * EPG - 1 - a) * 0.5 + (b - a - 1.0)
    return jnp.where(lane == 0, bucket, comb)


def _back_kernel(x_ref, gt_ref, ys_ref, ya_ref, ym_ref, ws_ref, wa_ref, wm_ref, wo_ref, g2_ref,
                 wr_ref, br_ref, x1_ref, h_ref, comb_ref, *, rows_for_dispatch):
    gt = gt_ref[...].astype(F32)
    merged = (_sigmoid(gt[:, 0:D]) * _dot(ys_ref[...].astype(BF16), ws_ref[...])
              + _sigmoid(gt[:, D:2 * D]) * _dot(ya_ref[...], wa_ref[...])
              + _sigmoid(gt[:, 2 * D:3 * D]) * _dot(ym_ref[...], wm_ref[...]))
    x1 = x_ref[...] + _dot(merged.astype(BF16), wo_ref[...])
    x1_ref[...] = x1
    ms = jnp.mean(x1 * x1, axis=-1, keepdims=True)
    h = x1 * lax.rsqrt(ms + EPS) * g2_ref[...]
    h_hi = h.astype(BF16)
    h_lo = (h - h_hi.astype(F32)).astype(BF16)
    wr = wr_ref[...]
    wr_hi = wr.astype(BF16)
    wr_lo = (wr - wr_hi.astype(F32)).astype(BF16)
    logits = (_dot(h_hi, wr_hi) + _dot(h_hi, wr_lo) + _dot(h_lo, wr_hi)) + br_ref[...]
    comb = _route(logits)
    comb_ref[...] = comb
    if rows_for_dispatch:
        h_ref[:, 0:D] = h
        h_ref[:, D:D + ROUTE_W] = comb
    else:
        h_ref[...] = h.astype(BF16)


def _back(x, proj, y_ssm, y_swa, y_mem, ws, wa, wm, wo, g2, wr, br, tm, rows_for_dispatch):
    t = x.shape[0]
    resident = lambda shape: pl.BlockSpec(shape, lambda i: (0, 0), pipeline_mode=pl.Buffered(1))
    hw = D + ROUTE_W if rows_for_dispatch else D
    return pl.pallas_call(
        functools.partial(_back_kernel, rows_for_dispatch=rows_for_dispatch),
        grid=(t // tm,),
        in_specs=[
            pl.BlockSpec((tm, D), lambda i: (i, 0)),
            pl.BlockSpec((tm, 3 * D), lambda i: (i, 0)),
            pl.BlockSpec((tm, SSM_INNER), lambda i: (i, 0)),
            pl.BlockSpec((tm, D), lambda i: (i, 0)),
            pl.BlockSpec((tm, D), lambda i: (i, 0)),
            resident((SSM_INNER, D)),
            resident((D, D)),
            resident((D, D)),
            resident((D, D)),
            resident((1, D)),
            resident((D, ROUTE_W)),
            resident((1, ROUTE_W)),
        ],
        out_specs=[
            pl.BlockSpec((tm, D), lambda i: (i, 0)),
            pl.BlockSpec((tm, hw), lambda i: (i, 0)),
            pl.BlockSpec((tm, ROUTE_W), lambda i: (i, 0)),
        ],
        out_shape=[
            jax.ShapeDtypeStruct((t, D), F32),
            jax.ShapeDtypeStruct((t, hw), F32 if rows_for_dispatch else BF16),
            jax.ShapeDtypeStruct((t, ROUTE_W), F32),
        ],
        compiler_params=_cparams(("parallel",)),
        name="back",
    )(x, proj, y_ssm, y_swa, y_mem, ws, wa, wm, wo, g2, wr, br)


PAIRS_PER_GROUP = EPG * (EPG - 1) // 2
N_BUCKETS = N_GROUPS * PAIRS_PER_GROUP
MOE_TR = 256
RANK_TM = 1024
COMBINE_TM = 512
ROW_W = D + ROUTE_W


def _bucket_experts():
    first, second = [], []
    for g in range(N_GROUPS):
        for a in range(EPG):
            for b in range(a + 1, EPG):
                first.append(g * EPG + a)
                second.append(g * EPG + b)
    return jnp.array(first, jnp.int32), jnp.array(second, jnp.int32)


def _rank_kernel(comb_ref, pos_ref, cnt_ref, carry_ref, offs_ref):
    ph = pl.program_id(0)
    i = pl.program_id(1)
    tm = comb_ref.shape[0]
    lane = lax.broadcasted_iota(jnp.int32, (tm, LANE), 1)
    gid = jnp.sum(jnp.where(lane == 0, comb_ref[...], 0.0), axis=-1, keepdims=True)
    onehot = (lane.astype(F32) == gid).astype(F32)
    colsum = jnp.sum(onehot, axis=0, keepdims=True)

    @pl.when((ph == 0) & (i == 0))
    def _():
        cnt_ref[...] = jnp.zeros_like(cnt_ref)

    @pl.when(ph == 0)
    def _():
        cnt_ref[...] += colsum

    @pl.when((ph == 1) & (i == 0))
    def _():
        padded = jnp.ceil(cnt_ref[...] * (1.0 / MOE_TR)) * MOE_TR
        r = lax.broadcasted_iota(jnp.int32, (LANE, LANE), 0)
        c = lax.broadcasted_iota(jnp.int32, (LANE, LANE), 1)
        offs_ref[...] = _dot_exact(padded, (r < c).astype(F32))
        carry_ref[...] = jnp.zeros_like(carry_ref)

    @pl.when(ph == 1)
    def _():
        rr = lax.broadcasted_iota(jnp.int32, (tm, tm), 0)
        cc = lax.broadcasted_iota(jnp.int32, (tm, tm), 1)
        before = _dot((cc < rr).astype(BF16), onehot.astype(BF16))
        slot = onehot * (offs_ref[...] + carry_ref[...] + before)
        pos = lax.dot_general(jnp.ones((8, LANE), F32), slot, (((1,), (1,)), ((), ())),
                              preferred_element_type=F32, precision=HIGHEST)
        pos_ref[0] = pos.astype(jnp.int32)
        carry_ref[...] += colsum


def _rank(comb):
    t = comb.shape[0]
    tm = RANK_TM
    nt = t // tm
    return pl.pallas_call(
        _rank_kernel,
        grid=(2, nt),
        in_specs=[pl.BlockSpec((tm, ROUTE_W), lambda p, i: (i, 0))],
        out_specs=[
            pl.BlockSpec((1, 8, tm), lambda p, i: (i * p, 0, 0)),
            pl.BlockSpec((1, LANE), lambda p, i: (0, 0)),
        ],
        out_shape=[
            jax.ShapeDtypeStruct((nt, 8, tm), jnp.int32),
            jax.ShapeDtypeStruct((1, LANE), F32),
        ],
        scratch_shapes=[pltpu.VMEM((1, LANE), F32), pltpu.VMEM((1, LANE), F32)],
        compiler_params=_cparams(("arbitrary", "arbitrary")),
        name="moe_rank",
    )(comb)


def _row_copy(src_hbm, dst_hbm, src_row, dst_row, sem):
    return pltpu.make_async_copy(src_hbm.at[pl.ds(src_row, 1)], dst_hbm.at[pl.ds(dst_row, 1)], sem)


PAD_BITS = MOE_TR.bit_length() - 1


DISPATCH_BUFS = 3


def _dispatch_kernel(ps_ref, pc_ref, nt_ref, pos_ref, rows_hbm, xs_hbm, rbuf, zbuf, lsem, sem, zsem):
    i = pl.program_id(0)
    n_steps = pl.num_programs(0)
    tm = rbuf.shape[1]
    slot = i % 2
    cur = i % DISPATCH_BUFS

    def load(block, b):
        return pltpu.make_async_copy(rows_hbm.at[pl.ds(pl.multiple_of(block * tm, tm), tm)], rbuf.at[b], lsem.at[b])

    def wait_rows(b, s):
        pltpu.make_async_copy(rbuf.at[b], xs_hbm.at[pl.ds(0, tm)], sem.at[s]).wait()

    @pl.when(i == 0)
    def _():
        load(0, 0).start()

    @pl.when(i + 1 < n_steps)
    def _():
        load(i + 1, (i + 1) % DISPATCH_BUFS).start()

    load(i, cur).wait()
    for r in range(tm):
        _row_copy(rbuf.at[cur], xs_hbm, r, pos_ref[0, r], sem.at[slot]).start()

    @pl.when(i > 0)
    def _():
        wait_rows((i + DISPATCH_BUFS - 1) % DISPATCH_BUFS, 1 - slot)

    @pl.when(i == pl.num_programs(0) - 1)
    def _():
        wait_rows(cur, slot)
        zbuf[...] = jnp.zeros_like(zbuf)

        def runs(b, fn):
            first = ps_ref[b]
            count = pc_ref[b]
            head = jnp.minimum((-first) & (SUBLANES - 1), count)
            for j in range(SUBLANES - 1):
                @pl.when(j < head)
                def _():
                    fn(pltpu.make_async_copy(zbuf.at[pl.ds(0, 1)], xs_hbm.at[pl.ds(first + j, 1)], zsem.at[0]))
            rest = count - head
            for k in range(SUBLANES.bit_length() - 1, PAD_BITS):
                size = 1 << k
                start = pl.multiple_of(first + head + ((rest >> (k + 1)) << (k + 1)), SUBLANES)

                @pl.when((rest & size) != 0)
                def _():
                    fn(pltpu.make_async_copy(zbuf.at[pl.ds(0, size)], xs_hbm.at[pl.ds(start, size)], zsem.at[0]))

        def issue(b, carry):
            runs(b, lambda cp: cp.start())
            return carry

        def drain(b, carry):
            runs(b, lambda cp: cp.wait())
            return carry

        lax.fori_loop(0, N_BUCKETS, issue, 0)
        lax.fori_loop(0, N_BUCKETS, drain, 0)

        def tile_copy(j):
            return pltpu.make_async_copy(zbuf, xs_hbm.at[pl.ds(pl.multiple_of(j * MOE_TR, MOE_TR), MOE_TR)],
                                         zsem.at[0])

        n_tiles = xs_hbm.shape[0] // MOE_TR
        lax.fori_loop(nt_ref[0], n_tiles, lambda j, c: (tile_copy(j).start(), c)[1], 0)
        lax.fori_loop(nt_ref[0], n_tiles, lambda j, c: (tile_copy(j).wait(), c)[1], 0)


def _dispatch(pad_start, pad_count, n_used, pos3, rows, n_slots):
    t = rows.shape[0]
    tm = RANK_TM
    grid_spec = pltpu.PrefetchScalarGridSpec(
        num_scalar_prefetch=3,
        grid=(t // tm,),
        in_specs=[
            pl.BlockSpec((None, 1, tm), lambda i, ps, pc, nt: (i, 0, 0), memory_space=pltpu.SMEM),
            pl.BlockSpec(memory_space=pl.ANY),
        ],
        out_specs=pl.BlockSpec(memory_space=pl.ANY),
        scratch_shapes=[
            pltpu.VMEM((DISPATCH_BUFS, tm, ROW_W), F32),
            pltpu.VMEM((MOE_TR, ROW_W), F32),
            pltpu.SemaphoreType.DMA((DISPATCH_BUFS,)),
            pltpu.SemaphoreType.DMA((2,)),
            pltpu.SemaphoreType.DMA((1,)),
        ],
    )
    return pl.pallas_call(
        _dispatch_kernel,
        grid_spec=grid_spec,
        out_shape=jax.ShapeDtypeStruct((n_slots, ROW_W), F32),
        compiler_params=_cparams(("arbitrary",)),
        name="moe_dispatch",
    )(pad_start, pad_count, n_used, pos3, rows)


def _expert(h, cw, w_ref):
    act = _silu(_dot(h, w_ref[:, 0:D_FF])) * _dot(h, w_ref[:, D_FF:2 * D_FF])
    return _dot_nt((act * cw).astype(BF16), w_ref[:, 2 * D_FF:3 * D_FF])


def _gmoe_kernel(ea_ref, eb_ref, nt_ref, xs_ref, wa_ref, wb_ref, ys_ref):
    i = pl.program_id(0)

    @pl.when(i < nt_ref[0])
    def _():
        x = xs_ref[...]
        h = x[:, 0:D].astype(BF16)
        comb = x[:, D:D + ROUTE_W]
        lane = lax.broadcasted_iota(jnp.int32, comb.shape, 1)
        acc = None
        for e_ref, w_ref in ((ea_ref, wa_ref), (eb_ref, wb_ref)):
            cw = jnp.sum(jnp.where(lane == N_GROUPS + e_ref[i], comb, 0.0), axis=-1, keepdims=True)
            part = _expert(h, cw, w_ref)
            acc = part if acc is None else acc + part
        ys_ref[...] = acc

    @pl.when(i >= nt_ref[0])
    def _():
        ys_ref[...] = jnp.zeros_like(ys_ref)


def _gmoe(exp_a, exp_b, n_used, xs, wcat):
    n_tiles = xs.shape[0] // MOE_TR
    tr = MOE_TR
    wspec = lambda which: pl.BlockSpec(
        (None, D, 3 * D_FF), lambda i, ea, eb, nt: ((ea, eb)[which][i], 0, 0))
    grid_spec = pltpu.PrefetchScalarGridSpec(
        num_scalar_prefetch=3,
        grid=(n_tiles,),
        in_specs=[
            pl.BlockSpec((tr, ROW_W), lambda i, ea, eb, nt: (jnp.where(i < nt[0], i, 0), 0)),
            wspec(0), wspec(1),
        ],
        out_specs=pl.BlockSpec((tr, D), lambda i, ea, eb, nt: (i, 0)),
    )
    return pl.pallas_call(
        _gmoe_kernel,
        grid_spec=grid_spec,
        out_shape=jax.ShapeDtypeStruct((n_tiles * tr, D), F32),
        compiler_params=_cparams(("arbitrary",)),
        name="moe_grouped",
    )(exp_a, exp_b, n_used, xs, wcat, wcat)


def _combine_kernel(cur_ref, nxt_ref, x1_ref, ys_hbm, y_ref, buf, sem):
    i = pl.program_id(0)
    slot = i % 2
    tm = x1_ref.shape[0]

    def start_gather(idx_ref, s):
        for r in range(tm):
            _row_copy(ys_hbm, buf.at[s], idx_ref[0, r], r, sem.at[s]).start()

    @pl.when(i == 0)
    def _():
        start_gather(cur_ref, 0)

    @pl.when(i + 1 < pl.num_programs(0))
    def _():
        start_gather(nxt_ref, 1 - slot)

    pltpu.make_async_copy(ys_hbm.at[pl.ds(0, tm)], buf.at[slot], sem.at[slot]).wait()
    y_ref[...] = x1_ref[...] + buf[slot]


def _combine(pos3, x1, ys):
    t = x1.shape[0]
    tm = pos3.shape[2]
    nt = t // tm
    idx = lambda f: pl.BlockSpec((None, 1, tm), f, memory_space=pltpu.SMEM)
    return pl.pallas_call(
        _combine_kernel,
        grid=(nt,),
        in_specs=[
            idx(lambda i: (i, 0, 0)),
            idx(lambda i: (jnp.minimum(i + 1, nt - 1), 0, 0)),
            pl.BlockSpec((tm, D), lambda i: (i, 0)),
            pl.BlockSpec(memory_space=pl.ANY),
        ],
        out_specs=pl.BlockSpec((tm, D), lambda i: (i, 0)),
        out_shape=jax.ShapeDtypeStruct((t, D), F32),
        scratch_shapes=[pltpu.VMEM((2, tm, D), F32), pltpu.SemaphoreType.DMA((2,))],
        compiler_params=_cparams(("arbitrary",)),
        name="moe_combine",
    )(pos3, pos3, x1, ys)


def _moe_routed(x1, rows, comb, wcat):
    t = x1.shape[0]
    tr = MOE_TR
    n_tiles = (t + N_BUCKETS * (tr - 1)) // tr
    pos3, cnt = _rank(comb)
    pos = pos3[:, 0, :]
    count = cnt[0, :N_BUCKETS].astype(jnp.int32)
    padded = (count + tr - 1) // tr * tr
    ends = jnp.cumsum(padded)
    pad_start = ends - padded + count
    pad_count = padded - count
    n_used = (ends[-1:] // tr).astype(jnp.int32)
    tile_start = jnp.arange(n_tiles, dtype=jnp.int32) * tr
    tile_bucket = jnp.minimum(jnp.sum(tile_start[:, None] >= ends[None, :], axis=1), N_BUCKETS - 1)
    first, second = _bucket_experts()
    xs = _dispatch(pad_start, pad_count, n_used, pos.reshape(t // RANK_TM, 1, RANK_TM), rows, n_tiles * tr)
    ys = _gmoe(first[tile_bucket], second[tile_bucket], n_used, xs, wcat)
    return _combine(pos.reshape(t // COMBINE_TM, 1, COMBINE_TM), x1, ys)


def _moe_kernel(x1_ref, h_ref, comb_ref, w_ref, o_ref):
    e = pl.program_id(1)

    @pl.when(e == 0)
    def _():
        o_ref[...] = x1_ref[...]

    lane = lax.broadcasted_iota(jnp.int32, comb_ref.shape, 1)
    cw = jnp.sum(jnp.where(lane == e + N_GROUPS, comb_ref[...], 0.0), axis=-1, keepdims=True)
    o_ref[...] += _expert(h_ref[...], cw, w_ref)


def _moe(x1, h, comb, wcat, tm):
    t = x1.shape[0]
    return pl.pallas_call(
        _moe_kernel,
        grid=(t // tm, N_EXPERTS),
        in_specs=[
            pl.BlockSpec((tm, D), lambda i, e: (i, 0)),
            pl.BlockSpec((tm, D), lambda i, e: (i, 0)),
            pl.BlockSpec((tm, ROUTE_W), lambda i, e: (i, 0)),
            pl.BlockSpec((None, D, 3 * D_FF), lambda i, e: (e, 0, 0)),
        ],
        out_specs=pl.BlockSpec((tm, D), lambda i, e: (i, 0)),
        out_shape=jax.ShapeDtypeStruct((t, D), F32),
        compiler_params=_cparams(("parallel", "arbitrary")),
        name="moe",
    )(x1, h, comb, wcat)


def _sample_prep_kernel(xbc_ref, q_ref, k_ref, mq_ref, dt_ref, sc_ref, cw_ref, cb_ref, dtb_ref, a_ref, e_ref,
                        qg_ref, kg_ref, mqg_ref, cos_ref, slo_ref, shi_ref,
                        conv_ref, xs_ref, bc_ref, dtx_ref, dec_ref, qn_ref, kn_ref, mqn_ref):
    n = xbc_ref.shape[0]
    x_raw = xbc_ref[...].astype(F32)
    acc = x_raw * cw_ref[SSM_CONV - 1:SSM_CONV, :] + cb_ref[...]
    for j in range(SSM_CONV - 1):
        acc = acc + sc_ref[:, j * CONV_CH:(j + 1) * CONV_CH] * cw_ref[j:j + 1, :]
    conv_ref[:, 0:(SSM_CONV - 2) * CONV_CH] = sc_ref[:, CONV_CH:(SSM_CONV - 1) * CONV_CH]
    conv_ref[:, (SSM_CONV - 2) * CONV_CH:] = x_raw
    act = _silu(acc)
    xs = act[:, 0:SSM_INNER]
    xs_ref[...] = xs
    bc_ref[...] = act[:, SSM_INNER:]
    dt = _softplus(dt_ref[...] + dtb_ref[...])
    e = e_ref[...]
    dtx = _dot_exact(dt, e) * xs
    dec = _dot_exact(jnp.exp(dt * a_ref[...]), e)
    dtx_ref[...] = dtx.T
    dec_ref[...] = dec.T

    lane = lax.broadcasted_iota(jnp.int32, (n, LANE), 1)
    low_half = lane < ATT_HD
    cos_t = jnp.broadcast_to(cos_ref[...], (n, LANE))
    sin_lo = jnp.broadcast_to(slo_ref[...], (n, LANE))
    sin_hi = jnp.broadcast_to(shi_ref[...], (n, LANE))
    qf = q_ref[...].astype(F32)
    for j in range(D // LANE):
        sl = slice(j * LANE, (j + 1) * LANE)
        qn_ref[:, sl] = _norm_rope_tile(qf[:, sl], qg_ref[...], cos_t, sin_lo, sin_hi, low_half)
    kf = k_ref[...].astype(F32)
    for j in range(ATT_KVD // LANE):
        sl = slice(j * LANE, (j + 1) * LANE)
        kn_ref[:, sl] = _norm_rope_tile(kf[:, sl], kg_ref[...], cos_t, sin_lo, sin_hi, low_half)
    mq = mq_ref[...].astype(F32)
    for h in range(MEM_HEADS):
        sl = slice(h * MEM_HD, (h + 1) * MEM_HD)
        mh = mq[:, sl]
        ms = jnp.mean(mh * mh, axis=-1, keepdims=True)
        mqn_ref[:, sl] = mh * lax.rsqrt(ms + EPS) * mqg_ref[...]


def _sample_prep(proj, dt_raw, sc2d, cw, cb, dtb, a, e, qg2, kg2, mqg, cos_t, sin_lo, sin_hi):
    n = proj.shape[0]
    c2 = lambda i: (0, 0)
    full = lambda shape: pl.BlockSpec(shape, c2)
    return pl.pallas_call(
        _sample_prep_kernel,
        grid=(1,),
        in_specs=[
            pl.BlockSpec((n, CONV_CH), lambda i: (0, COL_XBC_BLK)),
            pl.BlockSpec((n, D), lambda i: (0, COL_Q_BLK)),
            pl.BlockSpec((n, ATT_KVD), lambda i: (0, COL_K_BLK)),
            pl.BlockSpec((n, D), lambda i: (0, COL_MQ_BLK)),
            full((n, LANE)),
            full((n, (SSM_CONV - 1) * CONV_CH)),
            full((SSM_CONV, CONV_CH)),
            full((1, CONV_CH)),
            full((1, LANE)),
            full((1, LANE)),
            full((LANE, SSM_INNER)),
            full((1, LANE)),
            full((1, LANE)),
            full((1, MEM_HD)),
            full((1, LANE)),
            full((1, LANE)),
            full((1, LANE)),
        ],
        out_specs=[
            full((n, (SSM_CONV - 1) * CONV_CH)),
            full((n, SSM_INNER)),
            full((n, 2 * SSM_GROUPS * SSM_STATE)),
            full((SSM_INNER, n)),
            full((SSM_INNER, n)),
            full((n, D)),
            full((n, ATT_KVD)),
            full((n, D)),
        ],
        out_shape=[
            jax.ShapeDtypeStruct((n, (SSM_CONV - 1) * CONV_CH), F32),
            jax.ShapeDtypeStruct((n, SSM_INNER), F32),
            jax.ShapeDtypeStruct((n, 2 * SSM_GROUPS * SSM_STATE), F32),
            jax.ShapeDtypeStruct((SSM_INNER, n), F32),
            jax.ShapeDtypeStruct((SSM_INNER, n), F32),
            jax.ShapeDtypeStruct((n, D), F32),
            jax.ShapeDtypeStruct((n, ATT_KVD), F32),
            jax.ShapeDtypeStruct((n, D), F32),
        ],
        compiler_params=_cparams(("arbitrary",)),
        name="sample_prep",
    )(proj, proj, proj, proj, dt_raw, sc2d, cw, cb, dtb, a, e, qg2, kg2, mqg, cos_t, sin_lo, sin_hi)


SSM_S_TILE = 8


def _sample_ssm_kernel(st_ref, dtx_ref, dec_ref, bc_ref, so_ref, yt_ref):
    i = pl.program_id(0)
    n = dtx_ref.shape[1]

    @pl.when(i == 0)
    def _():
        yt_ref[...] = jnp.zeros_like(yt_ref)

    gw = SSM_HPG * SSM_HEADDIM
    lane = lax.broadcasted_iota(jnp.int32, (gw, n), 1)
    nbc = SSM_GROUPS * SSM_STATE
    for s in range(SSM_S_TILE):
        sel = lane == (i * SSM_S_TILE + s)
        for g in range(SSM_GROUPS):
            rows = slice(g * gw, (g + 1) * gw)
            dtx_c = jnp.sum(jnp.where(sel, dtx_ref[rows, :], 0.0), axis=1, keepdims=True)
            dec_c = jnp.sum(jnp.where(sel, dec_ref[rows, :], 0.0), axis=1, keepdims=True)
            bm = bc_ref[s:s + 1, g * SSM_STATE:(g + 1) * SSM_STATE]
            cm = bc_ref[s:s + 1, nbc + g * SSM_STATE:nbc + (g + 1) * SSM_STATE]
            hn = st_ref[s, rows, :] * dec_c + dtx_c * bm
            so_ref[s, rows, :] = hn
            yc = _dot_nt(hn.astype(BF16), jnp.broadcast_to(cm, (n, SSM_STATE)).astype(BF16))
            yt_ref[rows, :] = jnp.where(sel, yc, yt_ref[rows, :])


def _sample_ssm(state, dtx_t, dec_t, bc):
    n = state.shape[0]
    return pl.pallas_call(
        _sample_ssm_kernel,
        grid=(n // SSM_S_TILE,),
        in_specs=[
            pl.BlockSpec((SSM_S_TILE, SSM_INNER, SSM_STATE), lambda i: (i, 0, 0)),
            pl.BlockSpec((SSM_INNER, n), lambda i: (0, 0)),
            pl.BlockSpec((SSM_INNER, n), lambda i: (0, 0)),
            pl.BlockSpec((SSM_S_TILE, 2 * SSM_GROUPS * SSM_STATE), lambda i: (i, 0)),
        ],
        out_specs=[
            pl.BlockSpec((SSM_S_TILE, SSM_INNER, SSM_STATE), lambda i: (i, 0, 0)),
            pl.BlockSpec((SSM_INNER, n), lambda i: (0, 0)),
        ],
        out_shape=[
            jax.ShapeDtypeStruct((n, SSM_INNER, SSM_STATE), F32),
            jax.ShapeDtypeStruct((SSM_INNER, n), F32),
        ],
        compiler_params=_cparams(("arbitrary",)),
        name="sample_ssm",
    )(state, dtx_t, dec_t, bc)


def _sample_post_kernel(yt_ref, xs_ref, z_ref, dexp_ref, ng_ref, y_ref):
    y = yt_ref[...].T + dexp_ref[...] * xs_ref[...]
    yg = y * _silu(z_ref[...].astype(F32))
    ms = jnp.mean(yg * yg, axis=-1, keepdims=True)
    y_ref[...] = yg * lax.rsqrt(ms + EPS) * ng_ref[...]


def _sample_post(y_t, xs, proj, dexp, ng):
    n = xs.shape[0]
    c2 = lambda i: (0, 0)
    return pl.pallas_call(
        _sample_post_kernel,
        grid=(1,),
        in_specs=[
            pl.BlockSpec((SSM_INNER, n), c2),
            pl.BlockSpec((n, SSM_INNER), c2),
            pl.BlockSpec((n, SSM_INNER), lambda i: (0, COL_Z_BLK)),
            pl.BlockSpec((1, SSM_INNER), c2),
            pl.BlockSpec((1, SSM_INNER), c2),
        ],
        out_specs=pl.BlockSpec((n, SSM_INNER), c2),
        out_shape=jax.ShapeDtypeStruct((n, SSM_INNER), F32),
        compiler_params=_cparams(("arbitrary",)),
        name="sample_post",
    )(y_t, xs, proj, dexp, ng)


SWA_S_TILE = 8


def _sample_swa_kernel(q_ref, kn_ref, v_ref, ck_ref, cv_ref, sink_ref, y_ref, ko_ref, vo_ref, sc_ref):
    w = WINDOW
    scale = ATT_HD ** -0.5
    newest = lax.broadcasted_iota(jnp.int32, (ATT_HD, w), 1) == w - 1
    units = [(s, kv) for s in range(SWA_S_TILE) for kv in range(ATT_KV)]
    for u, (s, kv) in enumerate(units):
        dims = slice(kv * ATT_HD, (kv + 1) * ATT_HD)
        kt = jnp.where(newest, kn_ref[dims, s:s + 1], pltpu.roll(ck_ref[s, kv], w - 1, axis=1))
        vt = jnp.where(newest, v_ref[dims, s:s + 1], pltpu.roll(cv_ref[s, kv], w - 1, axis=1))
        ko_ref[s, kv] = kt
        vo_ref[s, kv] = vt
        sc_ref[u * SUBLANES:(u + 1) * SUBLANES, :] = _dot(q_ref[s, kv].astype(BF16), kt.astype(BF16)) * scale
    sc = sc_ref[...]
    snk = jnp.concatenate([sink_ref[...]] * SWA_S_TILE, axis=0)
    m = jnp.maximum(jnp.max(sc, axis=-1, keepdims=True), snk)
    p = jnp.exp(sc - m)
    pn = p / (jnp.sum(p, axis=-1, keepdims=True) + jnp.exp(snk - m))
    for u, (s, kv) in enumerate(units):
        pu = pn[u * SUBLANES:(u + 1) * SUBLANES].astype(BF16)
        y_ref[s, kv] = _dot_nt(pu, vo_ref[s, kv].astype(BF16))


def _sample_swa(q4, kn4, v4, ck_t, cv_t, sink_col, l):
    n = q4.shape[0]
    w = WINDOW
    st = SWA_S_TILE
    cache = pl.BlockSpec((None, st, ATT_KV, ATT_HD, w), lambda i: (l, i, 0, 0, 0))
    new = pl.BlockSpec((None, ATT_KVD, st), lambda i: (i, 0, 0))
    out = pl.BlockSpec((st, ATT_KV, ATT_HD, w), lambda i: (i, 0, 0, 0))
    qspec = pl.BlockSpec((st, ATT_KV, SUBLANES, ATT_HD), lambda i: (i, 0, 0, 0))
    return pl.pallas_call(
        _sample_swa_kernel,
        grid=(n // st,),
        in_specs=[
            qspec, new, new, cache, cache,
            pl.BlockSpec((ATT_KV * SUBLANES, 1), lambda i: (0, 0)),
        ],
        out_specs=[qspec, out, out],
        out_shape=[
            jax.ShapeDtypeStruct((n, ATT_KV, SUBLANES, ATT_HD), F32),
            jax.ShapeDtypeStruct((n, ATT_KV, ATT_HD, w), F32),
            jax.ShapeDtypeStruct((n, ATT_KV, ATT_HD, w), F32),
        ],
        scratch_shapes=[pltpu.VMEM((st * ATT_KV * SUBLANES, w), F32)],
        compiler_params=_cparams(("parallel",)),
        name="sample_swa",
    )(q4, kn4, v4, ck_t, cv_t, sink_col)


MEM_S_TILE = 4


def _sample_mem_kernel(q_ref, k_ref, v_ref, y_ref):
    scale = MEM_HD ** -0.5
    for s in range(MEM_S_TILE):
        sc = jnp.sum(k_ref[s] * q_ref[s][None], axis=-1, keepdims=True) * scale
        m = jnp.max(sc, axis=0, keepdims=True)
        p = jnp.exp(sc - m)
        den = jnp.sum(p, axis=0, keepdims=True)
        o = jnp.sum(p * v_ref[s], axis=0, keepdims=True) / den
        y_ref[s] = o[0]


def _sample_mem(q3, ck, cv, l):
    n = q3.shape[0]
    mem_len = ck.shape[2]
    st = MEM_S_TILE
    cache = pl.BlockSpec((None, st, mem_len, MEM_HEADS, MEM_HD), lambda i: (l, i, 0, 0, 0))
    return pl.pallas_call(
        _sample_mem_kernel,
        grid=(n // st,),
        in_specs=[pl.BlockSpec((st, MEM_HEADS, MEM_HD), lambda i: (i, 0, 0)), cache, cache],
        out_specs=pl.BlockSpec((st, MEM_HEADS, MEM_HD), lambda i: (i, 0, 0)),
        out_shape=jax.ShapeDtypeStruct((n, MEM_HEADS, MEM_HD), F32),
        compiler_params=_cparams(("parallel",)),
        name="sample_mem",
    )(q3, ck, cv)


def _prep_weights(lw):
    w_in = lw['w_in']
    cols = [w_in[:, 0:OFF_Z], w_in[:, OFF_XBC:OFF_DT], w_in[:, OFF_Z:OFF_XBC], w_in[:, OFF_Q:OFF_K],
            w_in[:, OFF_MQ:OFF_MQ + D], w_in[:, OFF_K:OFF_V], w_in[:, OFF_V:OFF_MQ]]
    p = {}
    p['w_main'] = jnp.concatenate(cols, axis=1).astype(BF16)
    p['w_dt'] = jnp.pad(w_in[:, OFF_DT:OFF_Q], ((0, 0), (0, LANE - SSM_HEADS))).astype(BF16)
    p['norm1_g'] = lw['norm1_g'].reshape(1, D)
    p['cw'] = lw['ssm_conv_w']
    p['cb'] = lw['ssm_conv_b'].reshape(1, CONV_CH)
    pad_h = (0, LANE - SSM_HEADS)
    p['dtb'] = jnp.pad(lw['ssm_dt_bias'].astype(F32), pad_h).reshape(1, LANE)
    p['a'] = jnp.pad(-jnp.exp(lw['ssm_a_log'].astype(F32)), pad_h).reshape(1, LANE)
    p['dexp'] = jnp.repeat(lw['ssm_d'], SSM_HEADDIM).reshape(1, SSM_INNER)
    p['ssm_ng'] = lw['ssm_norm_g'].reshape(1, SSM_INNER)
    head_of_ch = jnp.arange(SSM_INNER) // SSM_HEADDIM
    p['e'] = (jnp.arange(LANE)[:, None] == head_of_ch[None, :]).astype(F32)
    p['qg2'] = jnp.tile(lw['att_q_norm_g'], 2).reshape(1, LANE)
    p['kg2'] = jnp.tile(lw['att_k_norm_g'], 2).reshape(1, LANE)
    p['sink_row'] = lw['att_sink'].astype(F32).reshape(1, ATT_HEADS)
    grp = ATT_HEADS // ATT_KV
    p['sink_col'] = jnp.pad(lw['att_sink'].astype(F32).reshape(ATT_KV, grp),
                            ((0, 0), (0, SUBLANES - grp))).reshape(ATT_KV * SUBLANES, 1)
    p['mem_g'] = lw['mem_norm_g'].reshape(1, D)
    p['w_mem_kv'] = lw['w_mem_kv'].astype(BF16)
    p['mqg'] = lw['mem_q_norm_g'].reshape(1, MEM_HD)
    p['mkg'] = lw['mem_k_norm_g'].reshape(1, MEM_HD)
    p['ws'] = lw['w_br_ssm'].astype(BF16)
    p['wa'] = lw['w_br_swa'].astype(BF16)
    p['wm'] = lw['w_br_mem'].astype(BF16)
    p['wo'] = lw['w_out'].astype(BF16)
    p['g2'] = lw['norm2_g'].reshape(1, D)
    pad_r = ROUTE_W - N_GROUPS - N_EXPERTS
    p['wr'] = jnp.pad(jnp.concatenate([lw['w_router_group'], lw['w_router_expert']], axis=1).astype(F32),
                      ((0, 0), (0, pad_r)))
    p['br'] = jnp.pad(jnp.concatenate([lw['b_router_group'], lw['b_router_expert']]).astype(F32),
                      (0, pad_r)).reshape(1, ROUTE_W)
    p['wcat'] = jnp.concatenate([lw['w_exp_gate'], lw['w_exp_up'], lw['w_exp_down'].transpose(0, 2, 1)],
                                axis=2).astype(BF16)
    return p


def _pick(n, prefs):
    for c in prefs:
        if n % c == 0:
            return c
    return n


def _tail(x, proj, y_ssm, y_swa, y_mem, p):
    t = x.shape[0]
    routed = t % RANK_TM == 0
    x1, h, comb = _back(x, proj, y_ssm, y_swa, y_mem, p['ws'], p['wa'], p['wm'], p['wo'], p['g2'],
                        p['wr'], p['br'], _pick(t, (512, 256, 128)), routed)
    if routed:
        return _moe_routed(x1, h, comb, p['wcat'])
    return _moe(x1, h, comb, p['wcat'], _pick(t, (1024, 512, 256, 128)))


def _prompt_layer(x, mem, p):
    nb, seq, _ = x.shape
    t = nb * seq
    nc = seq // CHUNK
    xf = x.reshape(t, D)
    proj, dt_raw = _front(xf, p['norm1_g'], p['w_main'], p['w_dt'], _pick(t, (1024, 512, 256, 128)), 1536)
    y_ssm, conv_new, ssm_new = _ssd(proj, dt_raw, p['cw'], p['cb'], p['dtb'], p['a'], p['dexp'],
                                    p['ssm_ng'], nb, nc)
    cos_t, sin_lo, sin_hi = _rope_tables(jnp.arange(seq))
    y_swa, k_new, v_new = _swa(proj, p['qg2'], p['kg2'], cos_t, sin_lo, sin_hi, p['sink_row'], nb, seq)
    mem_len = mem.shape[1]
    mk, mv = _memkv(mem.reshape(nb * mem_len, D), p['mem_g'], p['w_mem_kv'], p['mkg'],
                    _pick(nb * mem_len, (512, 256, 128)))
    y_mem = _memattn(proj, mk, mv, p['mqg'], nb, seq, mem_len, _pick(seq, (512, 256, 128)))
    y = _tail(xf, proj, y_ssm, y_swa, y_mem, p)
    return (y.reshape(nb, seq, D), conv_new,
            ssm_new.reshape(nb, SSM_HEADS, SSM_HEADDIM, SSM_STATE),
            k_new.reshape(nb, WINDOW, ATT_KV, ATT_HD), v_new.reshape(nb, WINDOW, ATT_KV, ATT_HD),
            mk.reshape(nb, mem_len, MEM_HEADS, MEM_HD), mv.reshape(nb, mem_len, MEM_HEADS, MEM_HD))


def _sample_layer(x, conv_st, ssm_st, swa_k, swa_v, mem_k, mem_v, l, p):
    n = x.shape[0]
    xf = x.reshape(n, D)
    proj, dt_raw = _front(xf, p['norm1_g'], p['w_main'], p['w_dt'], n, 1536)
    cos_t, sin_lo, sin_hi = _rope_tables(jnp.full((1,), PAST_LEN, jnp.int32))
    sc2d = conv_st.reshape(n, (SSM_CONV - 1) * CONV_CH)
    conv_new, xs, bc, dtx_t, dec_t, qn, kn, mqn = _sample_prep(
        proj, dt_raw, sc2d, p['cw'], p['cb'], p['dtb'], p['a'], p['e'], p['qg2'], p['kg2'], p['mqg'],
        cos_t, sin_lo, sin_hi)
    ssm_new, y_t = _sample_ssm(ssm_st.reshape(n, SSM_INNER, SSM_STATE), dtx_t, dec_t, bc)
    y_ssm = _sample_post(y_t, xs, proj, p['dexp'], p['ssm_ng'])
    grp = ATT_HEADS // ATT_KV
    v_raw = proj[:, COL_V_BLK * ATT_KVD:(COL_V_BLK + 1) * ATT_KVD].astype(F32)
    to_t = (0, 1, 3, 4, 2)
    per_step = lambda a: a.T.reshape(ATT_KVD, n // SWA_S_TILE, SWA_S_TILE).transpose(1, 0, 2)
    y_swa, k_new, v_new = _sample_swa(
        jnp.pad(qn.reshape(n, ATT_KV, grp, ATT_HD), ((0, 0), (0, 0), (0, SUBLANES - grp), (0, 0))),
        per_step(kn), per_step(v_raw), swa_k.transpose(to_t), swa_v.transpose(to_t), p['sink_col'], l)
    y_swa = y_swa[:, :, 0:grp, :].reshape(n, D)
    k_new = k_new.transpose(0, 3, 1, 2)
    v_new = v_new.transpose(0, 3, 1, 2)
    y_mem = _sample_mem(mqn.reshape(n, MEM_HEADS, MEM_HD), mem_k, mem_v, l)
    y = _tail(xf, proj, y_ssm, y_swa.astype(BF16), y_mem.reshape(n, D).astype(BF16), p)
    return (y.reshape(n, 1, D), conv_new.reshape(n, SSM_CONV - 1, CONV_CH),
            ssm_new.reshape(n, SSM_HEADS, SSM_HEADDIM, SSM_STATE), k_new, v_new)


def kernel(x_prompt, x_sample, state_conv, state_ssm, cache_swa_k, cache_swa_v, cache_mem_k, cache_mem_v,
           mem_prompt, norm1_g, w_in, ssm_conv_w, ssm_conv_b, ssm_dt_bias, ssm_a_log, ssm_d, ssm_norm_g,
           att_q_norm_g, att_k_norm_g, att_sink, mem_norm_g, w_mem_kv, mem_q_norm_g, mem_k_norm_g,
           w_br_ssm, w_br_swa, w_br_mem, w_out, norm2_g, w_router_group, b_router_group,
           w_router_expert, b_router_expert, w_exp_gate, w_exp_up, w_exp_down):
    weights = dict(norm1_g=norm1_g, w_in=w_in, ssm_conv_w=ssm_conv_w, ssm_conv_b=ssm_conv_b,
                   ssm_dt_bias=ssm_dt_bias, ssm_a_log=ssm_a_log, ssm_d=ssm_d, ssm_norm_g=ssm_norm_g,
                   att_q_norm_g=att_q_norm_g, att_k_norm_g=att_k_norm_g, att_sink=att_sink,
                   mem_norm_g=mem_norm_g, w_mem_kv=w_mem_kv, mem_q_norm_g=mem_q_norm_g,
                   mem_k_norm_g=mem_k_norm_g, w_br_ssm=w_br_ssm, w_br_swa=w_br_swa, w_br_mem=w_br_mem,
                   w_out=w_out, norm2_g=norm2_g, w_router_group=w_router_group,
                   b_router_group=b_router_group, w_router_expert=w_router_expert,
                   b_router_expert=b_router_expert, w_exp_gate=w_exp_gate, w_exp_up=w_exp_up,
                   w_exp_down=w_exp_down)
    depth = w_in.shape[0]
    xp, xs = x_prompt, x_sample
    outs = [[] for _ in range(10)]
    for l in range(depth):
        p = _prep_weights({k: v[l] for k, v in weights.items()})
        xp, c1, c2, c3, c4, c5, c6 = _prompt_layer(xp, mem_prompt, p)
        xs, d1, d2, d3, d4 = _sample_layer(xs, state_conv[l], state_ssm[l], cache_swa_k, cache_swa_v,
                                           cache_mem_k, cache_mem_v, l, p)
        for lst, val in zip(outs, (c1, c2, c3, c4, c5, c6, d1, d2, d3, d4)):
            lst.append(val)
    return (xp, xs) + tuple(jnp.stack(o) for o in outs)
```

```python
import functools
import math

import jax
import jax.numpy as jnp
from jax import lax
from jax.experimental import pallas as pl
from jax.experimental.pallas import tpu as pltpu

F32 = jnp.float32
BF16 = jnp.bfloat16
HIGHEST = lax.Precision.HIGHEST

D = 1024
SSM_INNER = 2048
SSM_HEADDIM = 64
SSM_HEADS = 32
SSM_GROUPS = 4
SSM_HPG = SSM_HEADS // SSM_GROUPS
SSM_STATE = 128
SSM_CONV = 4
CONV_CH = SSM_INNER + 2 * SSM_GROUPS * SSM_STATE
CHUNK = 128
ATT_HEADS = 16
ATT_KV = 4
ATT_HD = 64
ATT_KVD = ATT_KV * ATT_HD
WINDOW = 128
ROPE_THETA = 10000.0
MEM_HEADS = 4
MEM_HD = 256
N_EXPERTS = 32
N_GROUPS = 4
EPG = 8
D_FF = 256
EPS = 1e-6
PAST_LEN = 16384

OFF_Z = 3 * D
OFF_XBC = OFF_Z + SSM_INNER
OFF_DT = OFF_XBC + CONV_CH
OFF_Q = OFF_DT + SSM_HEADS
OFF_K = OFF_Q + D
OFF_V = OFF_K + ATT_KVD
OFF_MQ = OFF_V + ATT_KVD

N_MAIN = 3 * D + CONV_CH + SSM_INNER + D + D + 2 * ATT_KVD
COL_XBC_BLK = 1
COL_Z_BLK = 3
COL_Q_BLK = 8
COL_MQ_BLK = 9
COL_K_BLK = 40
COL_V_BLK = 41
LANE = 128
ROUTE_W = 128

VMEM_LIMIT = 56 * 1024 * 1024


def _cparams(sem):
    return pltpu.CompilerParams(dimension_semantics=sem, vmem_limit_bytes=VMEM_LIMIT)


def _sigmoid(x):
    return 1.0 / (1.0 + jnp.exp(-x))


def _silu(x):
    h = 0.5 * x
    return h + h * jnp.tanh(h)


def _softplus(x):
    return jnp.maximum(x, 0.0) + jnp.log(1.0 + jnp.exp(-jnp.abs(x)))


def _dot(a, b):
    return jnp.dot(a, b, preferred_element_type=F32)


def _dot_nt(a, b):
    return lax.dot_general(a, b, (((1,), (1,)), ((), ())), preferred_element_type=F32)


def _dot_tn(a, b):
    return lax.dot_general(a, b, (((0,), (0,)), ((), ())), preferred_element_type=F32)


def _dot_exact(a, b):
    return jnp.dot(a, b, preferred_element_type=F32, precision=HIGHEST)


def _front_kernel(x_ref, g_ref, w_ref, wdt_ref, o_ref, dt_ref, hn_ref):
    @pl.when(pl.program_id(1) == 0)
    def _():
        x = x_ref[...]
        ms = jnp.mean(x * x, axis=-1, keepdims=True)
        hn = (x * lax.rsqrt(ms + EPS) * g_ref[...]).astype(BF16)
        hn_ref[...] = hn
        dt_ref[...] = _dot(hn, wdt_ref[...])

    o_ref[...] = _dot(hn_ref[...], w_ref[...]).astype(BF16)


def _front(x, g, w_main, w_dt, tm, tn):
    t = x.shape[0]
    return pl.pallas_call(
        _front_kernel,
        grid=(t // tm, N_MAIN // tn),
        in_specs=[
            pl.BlockSpec((tm, D), lambda i, j: (i, 0)),
            pl.BlockSpec((1, D), lambda i, j: (0, 0)),
            pl.BlockSpec((D, tn), lambda i, j: (0, j)),
            pl.BlockSpec((D, LANE), lambda i, j: (0, 0)),
        ],
        out_specs=[
            pl.BlockSpec((tm, tn), lambda i, j: (i, j)),
            pl.BlockSpec((tm, LANE), lambda i, j: (i, 0)),
        ],
        out_shape=[
            jax.ShapeDtypeStruct((t, N_MAIN), BF16),
            jax.ShapeDtypeStruct((t, LANE), F32),
        ],
        scratch_shapes=[pltpu.VMEM((tm, D), BF16)],
        compiler_params=_cparams(("parallel", "arbitrary")),
        name="front",
    )(x, g, w_main, w_dt)


SUBLANES = 8


def _ssd_kernel(xbc_ref, z_ref, dt_ref, cw_ref, cb_ref, dtb_ref, a_ref, dexp_ref, ng_ref,
                y_ref, conv_ref, ssm_ref, prev_ref, h_ref, yacc_ref):
    c = pl.program_id(1)

    @pl.when(c == 0)
    def _():
        prev_ref[...] = jnp.zeros_like(prev_ref)
        h_ref[...] = jnp.zeros_like(h_ref)

    x_raw = xbc_ref[...].astype(F32)
    prev = prev_ref[...]
    row8 = lax.broadcasted_iota(jnp.int32, (SUBLANES, CONV_CH), 0)
    row = lax.broadcasted_iota(jnp.int32, (CHUNK, CHUNK), 0)
    col = lax.broadcasted_iota(jnp.int32, (CHUNK, CHUNK), 1)
    causal = row >= col
    acc = x_raw * cw_ref[SSM_CONV - 1:SSM_CONV, :] + cb_ref[...]
    for s in range(1, SSM_CONV):
        xr = pltpu.roll(x_raw, s, axis=0)
        head = jnp.where(row8 < s, pltpu.roll(prev, s, axis=0), xr[0:SUBLANES])
        shifted = jnp.concatenate([head, xr[SUBLANES:]], axis=0)
        acc = acc + shifted * cw_ref[SSM_CONV - 1 - s:SSM_CONV - s, :]
    prev_ref[...] = x_raw[CHUNK - SUBLANES:CHUNK]
    conv_ref[0] = x_raw[CHUNK - (SSM_CONV - 1):CHUNK]
    act = _silu(acc)

    dt = _softplus(dt_ref[...] + dtb_ref[...])
    da = dt * a_ref[...]
    acum = _dot_exact(causal.astype(F32), da)
    acum_t = acum.T
    dt_t = dt.T
    exp_a = jnp.exp(acum)
    last = acum[CHUNK - 1:CHUNK, :]
    w_end = jnp.exp(last - acum) * dt
    cd = jnp.broadcast_to(jnp.exp(acum_t[:, CHUNK - 1:CHUNK]), (LANE, SSM_STATE))
    lane = lax.broadcasted_iota(jnp.int32, (CHUNK, LANE), 1)
    low_half = lane < SSM_HEADDIM

    def pair_cols(per_head, hd):
        return jnp.where(low_half, per_head[:, hd:hd + 1], per_head[:, hd + 1:hd + 2])

    xs_off = 0
    b_off = SSM_INNER
    c_off = SSM_INNER + SSM_GROUPS * SSM_STATE
    for g in range(SSM_GROUPS):
        bm = act[:, b_off + g * SSM_STATE:b_off + (g + 1) * SSM_STATE].astype(BF16)
        cm = act[:, c_off + g * SSM_STATE:c_off + (g + 1) * SSM_STATE].astype(BF16)
        cb = _dot_nt(cm, bm)
        gw = SSM_HPG * SSM_HEADDIM
        ch0 = g * gw
        h_prev = h_ref[ch0:ch0 + gw, :]
        y_off = _dot_nt(cm, h_prev.astype(BF16))
        xg = act[:, xs_off + ch0:xs_off + ch0 + gw]
        xw = []
        for pr in range(SSM_HPG // 2):
            hd0 = g * SSM_HPG + pr * 2
            xp32 = xg[:, pr * LANE:(pr + 1) * LANE]
            xpair = xp32.astype(BF16)
            yd = []
            for sub in range(2):
                hd = hd0 + sub
                seg = acum[:, hd:hd + 1] - acum_t[hd:hd + 1, :]
                decay = jnp.exp(jnp.where(causal, seg, -jnp.inf))
                wts = cb * decay * dt_t[hd:hd + 1, :]
                yd.append(_dot(wts.astype(BF16), xpair))
            cl = ch0 + pr * LANE
            y_pair = (jnp.where(low_half, yd[0], yd[1])
                      + y_off[:, pr * LANE:(pr + 1) * LANE] * pair_cols(exp_a, hd0)
                      + dexp_ref[:, cl:cl + LANE] * xp32)
            yacc_ref[:, cl:cl + LANE] = y_pair
            xw.append((xp32 * pair_cols(w_end, hd0)).astype(BF16))
        states = _dot_tn(jnp.concatenate(xw, axis=1), bm)
        for r in range(SSM_HPG):
            hd = g * SSM_HPG + r
            r0 = ch0 + r * SSM_HEADDIM
            h_ref[r0:r0 + SSM_HEADDIM, :] = (h_ref[r0:r0 + SSM_HEADDIM, :] * cd[hd:hd + 1, :]
                                             + states[r * SSM_HEADDIM:(r + 1) * SSM_HEADDIM, :])

    ssm_ref[0] = h_ref[...]
    yg = yacc_ref[...] * _silu(z_ref[...].astype(F32))
    ms = jnp.mean(yg * yg, axis=-1, keepdims=True)
    y_ref[...] = (yg * lax.rsqrt(ms + EPS) * ng_ref[...]).astype(BF16)


def _ssd(proj, dt_raw, cw, cb, dtb, a, dexp, ng, nb, nc):
    t = proj.shape[0]
    return pl.pallas_call(
        _ssd_kernel,
        grid=(nb, nc),
        in_specs=[
            pl.BlockSpec((CHUNK, CONV_CH), lambda b, c: (b * nc + c, COL_XBC_BLK)),
            pl.BlockSpec((CHUNK, SSM_INNER), lambda b, c: (b * nc + c, COL_Z_BLK)),
            pl.BlockSpec((CHUNK, LANE), lambda b, c: (b * nc + c, 0)),
            pl.BlockSpec((SSM_CONV, CONV_CH), lambda b, c: (0, 0)),
            pl.BlockSpec((1, CONV_CH), lambda b, c: (0, 0)),
            pl.BlockSpec((1, LANE), lambda b, c: (0, 0)),
            pl.BlockSpec((1, LANE), lambda b, c: (0, 0)),
            pl.BlockSpec((1, SSM_INNER), lambda b, c: (0, 0)),
            pl.BlockSpec((1, SSM_INNER), lambda b, c: (0, 0)),
        ],
        out_specs=[
            pl.BlockSpec((CHUNK, SSM_INNER), lambda b, c: (b * nc + c, 0)),
            pl.BlockSpec((1, SSM_CONV - 1, CONV_CH), lambda b, c: (b, 0, 0)),
            pl.BlockSpec((1, SSM_INNER, SSM_STATE), lambda b, c: (b, 0, 0)),
        ],
        out_shape=[
            jax.ShapeDtypeStruct((t, SSM_INNER), BF16),
            jax.ShapeDtypeStruct((nb, SSM_CONV - 1, CONV_CH), F32),
            jax.ShapeDtypeStruct((nb, SSM_INNER, SSM_STATE), F32),
        ],
        scratch_shapes=[
            pltpu.VMEM((SUBLANES, CONV_CH), F32),
            pltpu.VMEM((SSM_INNER, SSM_STATE), F32),
            pltpu.VMEM((CHUNK, SSM_INNER), F32),
        ],
        compiler_params=_cparams(("parallel", "arbitrary")),
        name="ssd_prompt",
    )(proj, proj, dt_raw, cw, cb, dtb, a, dexp, ng)


def _norm_rope_tile(xj, g2, cos_t, sin_lo, sin_hi, low_half):
    sq = xj * xj
    s_lo = jnp.sum(jnp.where(low_half, sq, 0.0), axis=-1, keepdims=True)
    s_hi = jnp.sum(jnp.where(low_half, 0.0, sq), axis=-1, keepdims=True)
    ms = jnp.where(low_half, s_lo, s_hi) * (1.0 / ATT_HD)
    xn = xj * lax.rsqrt(ms + EPS) * g2
    half = ATT_HD // 2
    return (xn * cos_t + pltpu.roll(xn, LANE - half, axis=1) * sin_lo
            + pltpu.roll(xn, half, axis=1) * sin_hi)


def _rope_tables(pos):
    half = ATT_HD // 2
    inv = ROPE_THETA ** (-jnp.arange(half, dtype=F32) / half)
    ang = pos.astype(F32)[:, None] * inv[None, :]
    cos = jnp.cos(ang)
    sin = jnp.sin(ang)
    zero = jnp.zeros_like(sin)
    cos_t = jnp.concatenate([cos, cos, cos, cos], axis=1)
    sin_lo = jnp.concatenate([-sin, zero, -sin, zero], axis=1)
    sin_hi = jnp.concatenate([zero, sin, zero, sin], axis=1)
    return cos_t, sin_lo, sin_hi


SWA_NSB = 2


def _swa_kernel(q_ref, k_ref, v_ref, qg_ref, kg_ref, cos_ref, slo_ref, shi_ref, sink_ref,
                y_ref, ko_ref, vo_ref, kd_ref, vd_ref, qs_ref):
    c = pl.program_id(1)
    w = WINDOW
    nsb = SWA_NSB
    grp = ATT_HEADS // ATT_KV
    rows = nsb * w
    lane = lax.broadcasted_iota(jnp.int32, (rows, LANE), 1)
    low_half = lane < ATT_HD
    low_half_w = lax.broadcasted_iota(jnp.int32, (w, LANE), 1) < ATT_HD
    cos_t = cos_ref[...]
    sin_lo = slo_ref[...]
    sin_hi = shi_ref[...]

    @pl.when(c == 0)
    def _():
        kd_ref[:, 0:w, :] = jnp.zeros((ATT_KV, w, LANE), BF16)
        vd_ref[:, 0:w, :] = jnp.zeros((ATT_KV, w, 2 * LANE), BF16)
        vd_ref[:, :, LANE:2 * LANE] = jnp.ones((ATT_KV, (nsb + 1) * w, LANE), BF16)

    kf = k_ref[...].astype(F32)
    vf = v_ref[...].astype(F32)
    for j in range(ATT_KVD // LANE):
        sl = slice(j * LANE, (j + 1) * LANE)
        kn = _norm_rope_tile(kf[:, sl], kg_ref[...], cos_t, sin_lo, sin_hi, low_half)
        ko_ref[0, :, sl] = kn[rows - w:rows]
        kr = pltpu.roll(kn, ATT_HD, axis=1)
        vj = vf[:, sl]
        vr = pltpu.roll(vj, ATT_HD, axis=1)
        kd_ref[2 * j, w:w + rows, :] = jnp.where(low_half, kn, kr).astype(BF16)
        kd_ref[2 * j + 1, w:w + rows, :] = jnp.where(low_half, kr, kn).astype(BF16)
        vd_ref[2 * j, w:w + rows, 0:LANE] = jnp.where(low_half, vj, vr).astype(BF16)
        vd_ref[2 * j + 1, w:w + rows, 0:LANE] = jnp.where(low_half, vr, vj).astype(BF16)
    vo_ref[0] = vf[rows - w:rows]

    qf = q_ref[...].astype(F32)
    for j in range(D // LANE):
        sl = slice(j * LANE, (j + 1) * LANE)
        qn = _norm_rope_tile(qf[:, sl], qg_ref[...] * (ATT_HD ** -0.5), cos_t, sin_lo, sin_hi, low_half)
        for par in range(2):
            h = 2 * j + par
            r0 = (h % grp) * w
            in_half = low_half if par == 0 else jnp.logical_not(low_half)
            qh = jnp.where(in_half, qn, 0.0).astype(BF16)
            for sb in range(nsb):
                qs_ref[h // grp, sb, r0:r0 + w, :] = qh[sb * w:(sb + 1) * w]

    qi = lax.broadcasted_iota(jnp.int32, (grp * w, 2 * w), 0) & (w - 1)
    kj = lax.broadcasted_iota(jnp.int32, (grp * w, 2 * w), 1)
    in_cur = (kj >= w) & ((kj - w) <= qi)
    in_prev = (kj < w) & (kj > qi)

    for sb in range(nsb):
        mask = in_cur | (in_prev & (c > 0) if sb == 0 else in_prev)
        for kv in range(ATT_KV):
            snk = jnp.concatenate(
                [jnp.broadcast_to(sink_ref[:, kv * grp + i:kv * grp + i + 1], (w, 1)) for i in range(grp)], axis=0)
            s = _dot_nt(qs_ref[kv, sb], kd_ref[kv, sb * w:(sb + 2) * w, :])
            s = jnp.where(mask, s, -jnp.inf)
            m = jnp.maximum(jnp.max(s, axis=-1, keepdims=True), snk)
            p = jnp.exp(s - m)
            den = jnp.sum(p, axis=-1, keepdims=True) + jnp.exp(snk - m)
            o = _dot(p.astype(BF16), vd_ref[kv, sb * w:(sb + 2) * w, 0:LANE]) * (1.0 / den)
            for a in range(grp // 2):
                col = kv * (grp // 2) + a
                y_ref[sb * w:(sb + 1) * w, col * LANE:(col + 1) * LANE] = jnp.where(
                    low_half_w, o[2 * a * w:(2 * a + 1) * w, :], o[(2 * a + 1) * w:(2 * a + 2) * w, :]
                ).astype(BF16)

    kd_ref[:, 0:w, :] = kd_ref[:, rows:rows + w, :]
    vd_ref[:, 0:w, 0:LANE] = vd_ref[:, rows:rows + w, 0:LANE]


def _swa(proj, qg2, kg2, cos_t, sin_lo, sin_hi, sink, nb, seq):
    t = proj.shape[0]
    w = WINDOW
    rows = SWA_NSB * w
    nc = seq // rows
    grp = ATT_HEADS // ATT_KV
    tab = pl.BlockSpec((rows, LANE), lambda b, c: (c, 0))
    return pl.pallas_call(
        _swa_kernel,
        grid=(nb, nc),
        in_specs=[
            pl.BlockSpec((rows, D), lambda b, c: (b * nc + c, COL_Q_BLK)),
            pl.BlockSpec((rows, ATT_KVD), lambda b, c: (b * nc + c, COL_K_BLK)),
            pl.BlockSpec((rows, ATT_KVD), lambda b, c: (b * nc + c, COL_V_BLK)),
            pl.BlockSpec((1, LANE), lambda b, c: (0, 0)),
            pl.BlockSpec((1, LANE), lambda b, c: (0, 0)),
            tab, tab, tab,
            pl.BlockSpec((1, ATT_HEADS), lambda b, c: (0, 0)),
        ],
        out_specs=[
            pl.BlockSpec((rows, D), lambda b, c: (b * nc + c, 0)),
            pl.BlockSpec((1, w, ATT_KVD), lambda b, c: (b, 0, 0)),
            pl.BlockSpec((1, w, ATT_KVD), lambda b, c: (b, 0, 0)),
        ],
        out_shape=[
            jax.ShapeDtypeStruct((t, D), BF16),
            jax.ShapeDtypeStruct((nb, w, ATT_KVD), F32),
            jax.ShapeDtypeStruct((nb, w, ATT_KVD), F32),
        ],
        scratch_shapes=[
            pltpu.VMEM((ATT_KV, rows + w, LANE), BF16),
            pltpu.VMEM((ATT_KV, rows + w, 2 * LANE), BF16),
            pltpu.VMEM((ATT_KV, SWA_NSB, grp * w, LANE), BF16),
        ],
        compiler_params=_cparams(("parallel", "arbitrary")),
        name="swa_prompt",
    )(proj, proj, proj, qg2, kg2, cos_t, sin_lo, sin_hi, sink)


def _memkv_kernel(m_ref, g_ref, w_ref, kg_ref, k_ref, v_ref):
    x = m_ref[...]
    ms = jnp.mean(x * x, axis=-1, keepdims=True)
    hn = (x * lax.rsqrt(ms + EPS) * g_ref[...]).astype(BF16)
    kv = _dot(hn, w_ref[...])
    for h in range(MEM_HEADS):
        kh = kv[:, h * MEM_HD:(h + 1) * MEM_HD]
        ms = jnp.mean(kh * kh, axis=-1, keepdims=True)
        k_ref[:, h * MEM_HD:(h + 1) * MEM_HD] = kh * lax.rsqrt(ms + EPS) * kg_ref[...]
    v_ref[...] = kv[:, D:]


def _memkv(mem, g, w, kg, tm):
    t = mem.shape[0]
    return pl.pallas_call(
        _memkv_kernel,
        grid=(t // tm,),
        in_specs=[
            pl.BlockSpec((tm, D), lambda i: (i, 0)),
            pl.BlockSpec((1, D), lambda i: (0, 0)),
            pl.BlockSpec((D, 2 * D), lambda i: (0, 0)),
            pl.BlockSpec((1, MEM_HD), lambda i: (0, 0)),
        ],
        out_specs=[
            pl.BlockSpec((tm, D), lambda i: (i, 0)),
            pl.BlockSpec((tm, D), lambda i: (i, 0)),
        ],
        out_shape=[jax.ShapeDtypeStruct((t, D), F32)] * 2,
        compiler_params=_cparams(("parallel",)),
        name="mem_kv",
    )(mem, g, w, kg)


def _memattn_kernel(q_ref, k_ref, v_ref, qg_ref, y_ref):
    q = q_ref[...].astype(F32)
    scale = MEM_HD ** -0.5
    for h in range(MEM_HEADS):
        sl = slice(h * MEM_HD, (h + 1) * MEM_HD)
        qh = q[:, sl]
        ms = jnp.mean(qh * qh, axis=-1, keepdims=True)
        qn = (qh * lax.rsqrt(ms + EPS) * qg_ref[...]).astype(BF16)
        s = _dot_nt(qn, k_ref[:, sl].astype(BF16)) * scale
        m = jnp.max(s, axis=-1, keepdims=True)
        p = jnp.exp(s - m)
        den = jnp.sum(p, axis=-1, keepdims=True)
        o = _dot(p.astype(BF16), v_ref[:, sl].astype(BF16)) / den
        y_ref[:, sl] = o.astype(BF16)


def _memattn(proj, mk, mv, qg, nb, seq, mem_len, tm):
    t = proj.shape[0]
    nt = seq // tm
    return pl.pallas_call(
        _memattn_kernel,
        grid=(nb, nt),
        in_specs=[
            pl.BlockSpec((tm, D), lambda b, i: (b * nt + i, COL_MQ_BLK)),
            pl.BlockSpec((mem_len, D), lambda b, i: (b, 0)),
            pl.BlockSpec((mem_len, D), lambda b, i: (b, 0)),
            pl.BlockSpec((1, MEM_HD), lambda b, i: (0, 0)),
        ],
        out_specs=pl.BlockSpec((tm, D), lambda b, i: (b * nt + i, 0)),
        out_shape=jax.ShapeDtypeStruct((t, D), BF16),
        compiler_params=_cparams(("parallel", "arbitrary")),
        name="mem_attn_prompt",
    )(proj, mk, mv, qg)


def _route(logits):
    lane = lax.broadcasted_iota(jnp.int32, logits.shape, 1)
    lanef = lane.astype(F32)
    is_g = lane < N_GROUPS
    neg = -jnp.inf
    big = 1e9
    gl = jnp.where(is_g, logits, neg)
    gmax = jnp.max(gl, axis=-1, keepdims=True)
    gidx = jnp.min(jnp.where(is_g & (logits == gmax), lanef, big), axis=-1, keepdims=True)
    pg_top = 1.0 / jnp.sum(jnp.where(is_g, jnp.exp(logits - gmax), 0.0), axis=-1, keepdims=True)
    lo = N_GROUPS + gidx * EPG
    in_g = (lanef >= lo) & (lanef < lo + EPG)
    m1 = jnp.max(jnp.where(in_g, logits, neg), axis=-1, keepdims=True)
    i1 = jnp.min(jnp.where(in_g & (logits == m1), lanef, big), axis=-1, keepdims=True)
    rest = in_g & (lanef != i1)
    m2 = jnp.max(jnp.where(rest, logits, neg), axis=-1, keepdims=True)
    i2 = jnp.min(jnp.where(rest & (logits == m2), lanef, big), axis=-1, keepdims=True)
    r = jnp.exp(m2 - m1)
    w1 = pg_top / (1.0 + r)
    w2 = pg_top * r / (1.0 + r)
    comb = jnp.where(lanef == i1, w1, 0.0) + jnp.where(lanef == i2, w2, 0.0)
    a = jnp.minimum(i1, i2) - lo
    b = jnp.maximum(i1, i2) - lo
    bucket = gidx * PAIRS_PER_GROUP + a * (2 * EPG - 1 - a) * 0.5 + (b - a - 1.0)
    return jnp.where(lane == 0, bucket, comb)


def _back_kernel(x_ref, gt_ref, ys_ref, ya_ref, ym_ref, ws_ref, wa_ref, wm_ref, wo_ref, g2_ref,
                 wr_ref, br_ref, x1_ref, h_ref, comb_ref, *, rows_for_dispatch):
    gt = gt_ref[...].astype(F32)
    merged = (_sigmoid(gt[:, 0:D]) * _dot(ys_ref[...].astype(BF16), ws_ref[...])
              + _sigmoid(gt[:, D:2 * D]) * _dot(ya_ref[...], wa_ref[...])
              + _sigmoid(gt[:, 2 * D:3 * D]) * _dot(ym_ref[...], wm_ref[...]))
    x1 = x_ref[...] + _dot(merged.astype(BF16), wo_ref[...])
    x1_ref[...] = x1
    ms = jnp.mean(x1 * x1, axis=-1, keepdims=True)
    h = x1 * lax.rsqrt(ms + EPS) * g2_ref[...]
    h_hi = h.astype(BF16)
    h_lo = (h - h_hi.astype(F32)).astype(BF16)
    wr = wr_ref[...]
    wr_hi = wr.astype(BF16)
    wr_lo = (wr - wr_hi.astype(F32)).astype(BF16)
    logits = (_dot(h_hi, wr_hi) + _dot(h_hi, wr_lo) + _dot(h_lo, wr_hi)) + br_ref[...]
    comb = _route(logits)
    comb_ref[...] = comb
    if rows_for_dispatch:
        h_ref[:, 0:D] = h
        h_ref[:, D:D + ROUTE_W] = comb
    else:
        h_ref[...] = h.astype(BF16)


def _back(x, proj, y_ssm, y_swa, y_mem, ws, wa, wm, wo, g2, wr, br, tm, rows_for_dispatch):
    t = x.shape[0]
    resident = lambda shape: pl.BlockSpec(shape, lambda i: (0, 0), pipeline_mode=pl.Buffered(1))
    hw = D + ROUTE_W if rows_for_dispatch else D
    return pl.pallas_call(
        functools.partial(_back_kernel, rows_for_dispatch=rows_for_dispatch),
        grid=(t // tm,),
        in_specs=[
            pl.BlockSpec((tm, D), lambda i: (i, 0)),
            pl.BlockSpec((tm, 3 * D), lambda i: (i, 0)),
            pl.BlockSpec((tm, SSM_INNER), lambda i: (i, 0)),
            pl.BlockSpec((tm, D), lambda i: (i, 0)),
            pl.BlockSpec((tm, D), lambda i: (i, 0)),
            resident((SSM_INNER, D)),
            resident((D, D)),
            resident((D, D)),
            resident((D, D)),
            resident((1, D)),
            resident((D, ROUTE_W)),
            resident((1, ROUTE_W)),
        ],
        out_specs=[
            pl.BlockSpec((tm, D), lambda i: (i, 0)),
            pl.BlockSpec((tm, hw), lambda i: (i, 0)),
            pl.BlockSpec((tm, ROUTE_W), lambda i: (i, 0)),
        ],
        out_shape=[
            jax.ShapeDtypeStruct((t, D), F32),
            jax.ShapeDtypeStruct((t, hw), F32 if rows_for_dispatch else BF16),
            jax.ShapeDtypeStruct((t, ROUTE_W), F32),
        ],
        compiler_params=_cparams(("parallel",)),
        name="back",
    )(x, proj, y_ssm, y_swa, y_mem, ws, wa, wm, wo, g2, wr, br)


PAIRS_PER_GROUP = EPG * (EPG - 1) // 2
N_BUCKETS = N_GROUPS * PAIRS_PER_GROUP
MOE_TR = 192
RANK_TM = 1024
COMBINE_TM = 512
ROW_W = D + ROUTE_W


def _bucket_experts():
    first, second = [], []
    for g in range(N_GROUPS):
        for a in range(EPG):
            for b in range(a + 1, EPG):
                first.append(g * EPG + a)
                second.append(g * EPG + b)
    return jnp.array(first, jnp.int32), jnp.array(second, jnp.int32)


def _rank_kernel(comb_ref, pos_ref, cnt_ref, carry_ref, offs_ref):
    ph = pl.program_id(0)
    i = pl.program_id(1)
    tm = comb_ref.shape[0]
    lane = lax.broadcasted_iota(jnp.int32, (tm, LANE), 1)
    gid = jnp.sum(jnp.where(lane == 0, comb_ref[...], 0.0), axis=-1, keepdims=True)
    onehot = (lane.astype(F32) == gid).astype(F32)
    colsum = jnp.sum(onehot, axis=0, keepdims=True)

    @pl.when((ph == 0) & (i == 0))
    def _():
        cnt_ref[...] = jnp.zeros_like(cnt_ref)

    @pl.when(ph == 0)
    def _():
        cnt_ref[...] += colsum

    @pl.when((ph == 1) & (i == 0))
    def _():
        padded = jnp.ceil(cnt_ref[...] * (1.0 / MOE_TR)) * MOE_TR
        r = lax.broadcasted_iota(jnp.int32, (LANE, LANE), 0)
        c = lax.broadcasted_iota(jnp.int32, (LANE, LANE), 1)
        offs_ref[...] = _dot_exact(padded, (r < c).astype(F32))
        carry_ref[...] = jnp.zeros_like(carry_ref)

    @pl.when(ph == 1)
    def _():
        rr = lax.broadcasted_iota(jnp.int32, (tm, tm), 0)
        cc = lax.broadcasted_iota(jnp.int32, (tm, tm), 1)
        before = _dot((cc < rr).astype(BF16), onehot.astype(BF16))
        slot = onehot * (offs_ref[...] + carry_ref[...] + before)
        pos = lax.dot_general(jnp.ones((8, LANE), F32), slot, (((1,), (1,)), ((), ())),
                              preferred_element_type=F32, precision=HIGHEST)
        pos_ref[0] = pos.astype(jnp.int32)
        carry_ref[...] += colsum


def _rank(comb):
    t = comb.shape[0]
    tm = RANK_TM
    nt = t // tm
    return pl.pallas_call(
        _rank_kernel,
        grid=(2, nt),
        in_specs=[pl.BlockSpec((tm, ROUTE_W), lambda p, i: (i, 0))],
        out_specs=[
            pl.BlockSpec((1, 8, tm), lambda p, i: (i * p, 0, 0)),
            pl.BlockSpec((1, LANE), lambda p, i: (0, 0)),
        ],
        out_shape=[
            jax.ShapeDtypeStruct((nt, 8, tm), jnp.int32),
            jax.ShapeDtypeStruct((1, LANE), F32),
        ],
        scratch_shapes=[pltpu.VMEM((1, LANE), F32), pltpu.VMEM((1, LANE), F32)],
        compiler_params=_cparams(("arbitrary", "arbitrary")),
        name="moe_rank",
    )(comb)


def _row_copy(src_hbm, dst_hbm, src_row, dst_row, sem):
    return pltpu.make_async_copy(src_hbm.at[pl.ds(src_row, 1)], dst_hbm.at[pl.ds(dst_row, 1)], sem)


PAD_BITS = (MOE_TR - 1).bit_length()


DISPATCH_BUFS = 3


def _dispatch_kernel(ps_ref, pc_ref, nt_ref, pos_ref, rows_hbm, xs_hbm, rbuf, zbuf, lsem, sem, zsem):
    i = pl.program_id(0)
    n_steps = pl.num_programs(0)
    tm = rbuf.shape[1]
    slot = i % 2
    cur = i % DISPATCH_BUFS

    def load(block, b):
        return pltpu.make_async_copy(rows_hbm.at[pl.ds(pl.multiple_of(block * tm, tm), tm)], rbuf.at[b], lsem.at[b])

    def wait_rows(b, s):
        pltpu.make_async_copy(rbuf.at[b], xs_hbm.at[pl.ds(0, tm)], sem.at[s]).wait()

    @pl.when(i == 0)
    def _():
        load(0, 0).start()

    @pl.when(i + 1 < n_steps)
    def _():
        load(i + 1, (i + 1) % DISPATCH_BUFS).start()

    load(i, cur).wait()
    for r in range(tm):
        _row_copy(rbuf.at[cur], xs_hbm, r, pos_ref[0, r], sem.at[slot]).start()

    @pl.when(i > 0)
    def _():
        wait_rows((i + DISPATCH_BUFS - 1) % DISPATCH_BUFS, 1 - slot)

    @pl.when(i == pl.num_programs(0) - 1)
    def _():
        wait_rows(cur, slot)
        zbuf[...] = jnp.zeros_like(zbuf)

        def runs(b, fn):
            first = ps_ref[b]
            count = pc_ref[b]
            head = jnp.minimum((-first) & (SUBLANES - 1), count)
            for j in range(SUBLANES - 1):
                @pl.when(j < head)
                def _():
                    fn(pltpu.make_async_copy(zbuf.at[pl.ds(0, 1)], xs_hbm.at[pl.ds(first + j, 1)], zsem.at[0]))
            rest = count - head
            for k in range(SUBLANES.bit_length() - 1, PAD_BITS):
                size = 1 << k
                start = pl.multiple_of(first + head + ((rest >> (k + 1)) << (k + 1)), SUBLANES)

                @pl.when((rest & size) != 0)
                def _():
                    fn(pltpu.make_async_copy(zbuf.at[pl.ds(0, size)], xs_hbm.at[pl.ds(start, size)], zsem.at[0]))

        def issue(b, carry):
            runs(b, lambda cp: cp.start())
            return carry

        def drain(b, carry):
            runs(b, lambda cp: cp.wait())
            return carry

        lax.fori_loop(0, N_BUCKETS, issue, 0)
        lax.fori_loop(0, N_BUCKETS, drain, 0)

        def tile_copy(j):
            return pltpu.make_async_copy(zbuf, xs_hbm.at[pl.ds(pl.multiple_of(j * MOE_TR, MOE_TR), MOE_TR)],
                                         zsem.at[0])

        n_tiles = xs_hbm.shape[0] // MOE_TR
        lax.fori_loop(nt_ref[0], n_tiles, lambda j, c: (tile_copy(j).start(), c)[1], 0)
        lax.fori_loop(nt_ref[0], n_tiles, lambda j, c: (tile_copy(j).wait(), c)[1], 0)


def _dispatch(pad_start, pad_count, n_used, pos3, rows, n_slots):
    t = rows.shape[0]
    tm = RANK_TM
    grid_spec = pltpu.PrefetchScalarGridSpec(
        num_scalar_prefetch=3,
        grid=(t // tm,),
        in_specs=[
            pl.BlockSpec((None, 1, tm), lambda i, ps, pc, nt: (i, 0, 0), memory_space=pltpu.SMEM),
            pl.BlockSpec(memory_space=pl.ANY),
        ],
        out_specs=pl.BlockSpec(memory_space=pl.ANY),
        scratch_shapes=[
            pltpu.VMEM((DISPATCH_BUFS, tm, ROW_W), F32),
            pltpu.VMEM((MOE_TR, ROW_W), F32),
            pltpu.SemaphoreType.DMA((DISPATCH_BUFS,)),
            pltpu.SemaphoreType.DMA((2,)),
            pltpu.SemaphoreType.DMA((1,)),
        ],
    )
    return pl.pallas_call(
        _dispatch_kernel,
        grid_spec=grid_spec,
        out_shape=jax.ShapeDtypeStruct((n_slots, ROW_W), F32),
        compiler_params=_cparams(("arbitrary",)),
        name="moe_dispatch",
    )(pad_start, pad_count, n_used, pos3, rows)


def _expert(h, cw, wg_ref, wu_ref, wd_ref):
    act = _silu(_dot(h, wg_ref[...])) * _dot(h, wu_ref[...])
    return _dot((act * cw).astype(BF16), wd_ref[...])


def _gmoe_kernel(ea_ref, eb_ref, nt_ref, xs_ref, wga_ref, wua_ref, wda_ref, wgb_ref, wub_ref, wdb_ref, ys_ref):
    i = pl.program_id(0)

    @pl.when(i < nt_ref[0])
    def _():
        x = xs_ref[...]
        h = x[:, 0:D].astype(BF16)
        comb = x[:, D:D + ROUTE_W]
        lane = lax.broadcasted_iota(jnp.int32, comb.shape, 1)
        acc = None
        for e_ref, w_refs in ((ea_ref, (wga_ref, wua_ref, wda_ref)), (eb_ref, (wgb_ref, wub_ref, wdb_ref))):
            cw = jnp.sum(jnp.where(lane == N_GROUPS + e_ref[i], comb, 0.0), axis=-1, keepdims=True)
            part = _expert(h, cw, *w_refs)
            acc = part if acc is None else acc + part
        ys_ref[...] = acc

    @pl.when(i >= nt_ref[0])
    def _():
        ys_ref[...] = jnp.zeros_like(ys_ref)


def _gmoe(exp_a, exp_b, n_used, xs, wg, wu, wd):
    n_tiles = xs.shape[0] // MOE_TR
    tr = MOE_TR
    wspec = lambda shape, which: pl.BlockSpec(
        (None,) + shape, lambda i, ea, eb, nt: ((ea, eb)[which][i], 0, 0))
    grid_spec = pltpu.PrefetchScalarGridSpec(
        num_scalar_prefetch=3,
        grid=(n_tiles,),
        in_specs=[
            pl.BlockSpec((tr, ROW_W), lambda i, ea, eb, nt: (jnp.where(i < nt[0], i, 0), 0)),
            wspec((D, D_FF), 0), wspec((D, D_FF), 0), wspec((D_FF, D), 0),
            wspec((D, D_FF), 1), wspec((D, D_FF), 1), wspec((D_FF, D), 1),
        ],
        out_specs=pl.BlockSpec((tr, D), lambda i, ea, eb, nt: (i, 0)),
    )
    return pl.pallas_call(
        _gmoe_kernel,
        grid_spec=grid_spec,
        out_shape=jax.ShapeDtypeStruct((n_tiles * tr, D), F32),
        compiler_params=_cparams(("arbitrary",)),
        name="moe_grouped",
    )(exp_a, exp_b, n_used, xs, wg, wu, wd, wg, wu, wd)


def _combine_kernel(cur_ref, nxt_ref, x1_ref, ys_hbm, y_ref, buf, sem):
    i = pl.program_id(0)
    slot = i % 2
    tm = x1_ref.shape[0]

    def start_gather(idx_ref, s):
        for r in range(tm):
            _row_copy(ys_hbm, buf.at[s], idx_ref[0, r], r, sem.at[s]).start()

    @pl.when(i == 0)
    def _():
        start_gather(cur_ref, 0)

    @pl.when(i + 1 < pl.num_programs(0))
    def _():
        start_gather(nxt_ref, 1 - slot)

    pltpu.make_async_copy(ys_hbm.at[pl.ds(0, tm)], buf.at[slot], sem.at[slot]).wait()
    y_ref[...] = x1_ref[...] + buf[slot]


def _combine(pos3, x1, ys):
    t = x1.shape[0]
    tm = pos3.shape[2]
    nt = t // tm
    idx = lambda f: pl.BlockSpec((None, 1, tm), f, memory_space=pltpu.SMEM)
    return pl.pallas_call(
        _combine_kernel,
        grid=(nt,),
        in_specs=[
            idx(lambda i: (i, 0, 0)),
            idx(lambda i: (jnp.minimum(i + 1, nt - 1), 0, 0)),
            pl.BlockSpec((tm, D), lambda i: (i, 0)),
            pl.BlockSpec(memory_space=pl.ANY),
        ],
        out_specs=pl.BlockSpec((tm, D), lambda i: (i, 0)),
        out_shape=jax.ShapeDtypeStruct((t, D), F32),
        scratch_shapes=[pltpu.VMEM((2, tm, D), F32), pltpu.SemaphoreType.DMA((2,))],
        compiler_params=_cparams(("arbitrary",)),
        name="moe_combine",
    )(pos3, pos3, x1, ys)


def _moe_routed(x1, rows, comb, wg, wu, wd):
    t = x1.shape[0]
    tr = MOE_TR
    n_tiles = (t + N_BUCKETS * (tr - 1)) // tr
    pos3, cnt = _rank(comb)
    pos = pos3[:, 0, :]
    count = cnt[0, :N_BUCKETS].astype(jnp.int32)
    padded = (count + tr - 1) // tr * tr
    ends = jnp.cumsum(padded)
    pad_start = ends - padded + count
    pad_count = padded - count
    n_used = (ends[-1:] // tr).astype(jnp.int32)
    tile_start = jnp.arange(n_tiles, dtype=jnp.int32) * tr
    tile_bucket = jnp.minimum(jnp.sum(tile_start[:, None] >= ends[None, :], axis=1), N_BUCKETS - 1)
    first, second = _bucket_experts()
    xs = _dispatch(pad_start, pad_count, n_used, pos.reshape(t // RANK_TM, 1, RANK_TM), rows, n_tiles * tr)
    ys = _gmoe(first[tile_bucket], second[tile_bucket], n_used, xs, wg, wu, wd)
    return _combine(pos.reshape(t // COMBINE_TM, 1, COMBINE_TM), x1, ys)


def _moe_kernel(x1_ref, h_ref, comb_ref, wg_ref, wu_ref, wd_ref, o_ref):
    e = pl.program_id(1)

    @pl.when(e == 0)
    def _():
        o_ref[...] = x1_ref[...]

    lane = lax.broadcasted_iota(jnp.int32, comb_ref.shape, 1)
    cw = jnp.sum(jnp.where(lane == e + N_GROUPS, comb_ref[...], 0.0), axis=-1, keepdims=True)
    o_ref[...] += _expert(h_ref[...], cw, wg_ref, wu_ref, wd_ref)


def _moe(x1, h, comb, wg, wu, wd, tm):
    t = x1.shape[0]
    return pl.pallas_call(
        _moe_kernel,
        grid=(t // tm, N_EXPERTS),
        in_specs=[
            pl.BlockSpec((tm, D), lambda i, e: (i, 0)),
            pl.BlockSpec((tm, D), lambda i, e: (i, 0)),
            pl.BlockSpec((tm, ROUTE_W), lambda i, e: (i, 0)),
            pl.BlockSpec((None, D, D_FF), lambda i, e: (e, 0, 0)),
            pl.BlockSpec((None, D, D_FF), lambda i, e: (e, 0, 0)),
            pl.BlockSpec((None, D_FF, D), lambda i, e: (e, 0, 0)),
        ],
        out_specs=pl.BlockSpec((tm, D), lambda i, e: (i, 0)),
        out_shape=jax.ShapeDtypeStruct((t, D), F32),
        compiler_params=_cparams(("parallel", "arbitrary")),
        name="moe",
    )(x1, h, comb, wg, wu, wd)


def _sample_prep_kernel(xbc_ref, q_ref, k_ref, mq_ref, dt_ref, sc_ref, cw_ref, cb_ref, dtb_ref, a_ref, e_ref,
                        qg_ref, kg_ref, mqg_ref, cos_ref, slo_ref, shi_ref,
                        conv_ref, xs_ref, bc_ref, dtx_ref, dec_ref, qn_ref, kn_ref, mqn_ref):
    n = xbc_ref.shape[0]
    x_raw = xbc_ref[...].astype(F32)
    acc = x_raw * cw_ref[SSM_CONV - 1:SSM_CONV, :] + cb_ref[...]
    for j in range(SSM_CONV - 1):
        acc = acc + sc_ref[:, j * CONV_CH:(j + 1) * CONV_CH] * cw_ref[j:j + 1, :]
    conv_ref[:, 0:(SSM_CONV - 2) * CONV_CH] = sc_ref[:, CONV_CH:(SSM_CONV - 1) * CONV_CH]
    conv_ref[:, (SSM_CONV - 2) * CONV_CH:] = x_raw
    act = _silu(acc)
    xs = act[:, 0:SSM_INNER]
    xs_ref[...] = xs
    bc_ref[...] = act[:, SSM_INNER:]
    dt = _softplus(dt_ref[...] + dtb_ref[...])
    e = e_ref[...]
    dtx = _dot_exact(dt, e) * xs
    dec = _dot_exact(jnp.exp(dt * a_ref[...]), e)
    dtx_ref[...] = dtx.T
    dec_ref[...] = dec.T

    lane = lax.broadcasted_iota(jnp.int32, (n, LANE), 1)
    low_half = lane < ATT_HD
    cos_t = jnp.broadcast_to(cos_ref[...], (n, LANE))
    sin_lo = jnp.broadcast_to(slo_ref[...], (n, LANE))
    sin_hi = jnp.broadcast_to(shi_ref[...], (n, LANE))
    qf = q_ref[...].astype(F32)
    for j in range(D // LANE):
        sl = slice(j * LANE, (j + 1) * LANE)
        qn_ref[:, sl] = _norm_rope_tile(qf[:, sl], qg_ref[...], cos_t, sin_lo, sin_hi, low_half)
    kf = k_ref[...].astype(F32)
    for j in range(ATT_KVD // LANE):
        sl = slice(j * LANE, (j + 1) * LANE)
        kn_ref[:, sl] = _norm_rope_tile(kf[:, sl], kg_ref[...], cos_t, sin_lo, sin_hi, low_half)
    mq = mq_ref[...].astype(F32)
    for h in range(MEM_HEADS):
        sl = slice(h * MEM_HD, (h + 1) * MEM_HD)
        mh = mq[:, sl]
        ms = jnp.mean(mh * mh, axis=-1, keepdims=True)
        mqn_ref[:, sl] = mh * lax.rsqrt(ms + EPS) * mqg_ref[...]


def _sample_prep(proj, dt_raw, sc2d, cw, cb, dtb, a, e, qg2, kg2, mqg, cos_t, sin_lo, sin_hi):
    n = proj.shape[0]
    c2 = lambda i: (0, 0)
    full = lambda shape: pl.BlockSpec(shape, c2)
    return pl.pallas_call(
        _sample_prep_kernel,
        grid=(1,),
        in_specs=[
            pl.BlockSpec((n, CONV_CH), lambda i: (0, COL_XBC_BLK)),
            pl.BlockSpec((n, D), lambda i: (0, COL_Q_BLK)),
            pl.BlockSpec((n, ATT_KVD), lambda i: (0, COL_K_BLK)),
            pl.BlockSpec((n, D), lambda i: (0, COL_MQ_BLK)),
            full((n, LANE)),
            full((n, (SSM_CONV - 1) * CONV_CH)),
            full((SSM_CONV, CONV_CH)),
            full((1, CONV_CH)),
            full((1, LANE)),
            full((1, LANE)),
            full((LANE, SSM_INNER)),
            full((1, LANE)),
            full((1, LANE)),
            full((1, MEM_HD)),
            full((1, LANE)),
            full((1, LANE)),
            full((1, LANE)),
        ],
        out_specs=[
            full((n, (SSM_CONV - 1) * CONV_CH)),
            full((n, SSM_INNER)),
            full((n, 2 * SSM_GROUPS * SSM_STATE)),
            full((SSM_INNER, n)),
            full((SSM_INNER, n)),
            full((n, D)),
            full((n, ATT_KVD)),
            full((n, D)),
        ],
        out_shape=[
            jax.ShapeDtypeStruct((n, (SSM_CONV - 1) * CONV_CH), F32),
            jax.ShapeDtypeStruct((n, SSM_INNER), F32),
            jax.ShapeDtypeStruct((n, 2 * SSM_GROUPS * SSM_STATE), F32),
            jax.ShapeDtypeStruct((SSM_INNER, n), F32),
            jax.ShapeDtypeStruct((SSM_INNER, n), F32),
            jax.ShapeDtypeStruct((n, D), F32),
            jax.ShapeDtypeStruct((n, ATT_KVD), F32),
            jax.ShapeDtypeStruct((n, D), F32),
        ],
        compiler_params=_cparams(("arbitrary",)),
        name="sample_prep",
    )(proj, proj, proj, proj, dt_raw, sc2d, cw, cb, dtb, a, e, qg2, kg2, mqg, cos_t, sin_lo, sin_hi)


SSM_S_TILE = 8


def _sample_ssm_kernel(st_ref, dtx_ref, dec_ref, bc_ref, so_ref, yt_ref):
    i = pl.program_id(0)
    n = dtx_ref.shape[1]

    @pl.when(i == 0)
    def _():
        yt_ref[...] = jnp.zeros_like(yt_ref)

    gw = SSM_HPG * SSM_HEADDIM
    lane = lax.broadcasted_iota(jnp.int32, (gw, n), 1)
    nbc = SSM_GROUPS * SSM_STATE
    for s in range(SSM_S_TILE):
        sel = lane == (i * SSM_S_TILE + s)
        for g in range(SSM_GROUPS):
            rows = slice(g * gw, (g + 1) * gw)
            dtx_c = jnp.sum(jnp.where(sel, dtx_ref[rows, :], 0.0), axis=1, keepdims=True)
            dec_c = jnp.sum(jnp.where(sel, dec_ref[rows, :], 0.0), axis=1, keepdims=True)
            bm = bc_ref[s:s + 1, g * SSM_STATE:(g + 1) * SSM_STATE]
            cm = bc_ref[s:s + 1, nbc + g * SSM_STATE:nbc + (g + 1) * SSM_STATE]
            hn = st_ref[s, rows, :] * dec_c + dtx_c * bm
            so_ref[s, rows, :] = hn
            yc = _dot_nt(hn.astype(BF16), jnp.broadcast_to(cm, (n, SSM_STATE)).astype(BF16))
            yt_ref[rows, :] = jnp.where(sel, yc, yt_ref[rows, :])


def _sample_ssm(state, dtx_t, dec_t, bc):
    n = state.shape[0]
    return pl.pallas_call(
        _sample_ssm_kernel,
        grid=(n // SSM_S_TILE,),
        in_specs=[
            pl.BlockSpec((SSM_S_TILE, SSM_INNER, SSM_STATE), lambda i: (i, 0, 0)),
            pl.BlockSpec((SSM_INNER, n), lambda i: (0, 0)),
            pl.BlockSpec((SSM_INNER, n), lambda i: (0, 0)),
            pl.BlockSpec((SSM_S_TILE, 2 * SSM_GROUPS * SSM_STATE), lambda i: (i, 0)),
        ],
        out_specs=[
            pl.BlockSpec((SSM_S_TILE, SSM_INNER, SSM_STATE), lambda i: (i, 0, 0)),
            pl.BlockSpec((SSM_INNER, n), lambda i: (0, 0)),
        ],
        out_shape=[
            jax.ShapeDtypeStruct((n, SSM_INNER, SSM_STATE), F32),
            jax.ShapeDtypeStruct((SSM_INNER, n), F32),
        ],
        compiler_params=_cparams(("arbitrary",)),
        name="sample_ssm",
    )(state, dtx_t, dec_t, bc)


def _sample_post_kernel(yt_ref, xs_ref, z_ref, dexp_ref, ng_ref, y_ref):
    y = yt_ref[...].T + dexp_ref[...] * xs_ref[...]
    yg = y * _silu(z_ref[...].astype(F32))
    ms = jnp.mean(yg * yg, axis=-1, keepdims=True)
    y_ref[...] = yg * lax.rsqrt(ms + EPS) * ng_ref[...]


def _sample_post(y_t, xs, proj, dexp, ng):
    n = xs.shape[0]
    c2 = lambda i: (0, 0)
    return pl.pallas_call(
        _sample_post_kernel,
        grid=(1,),
        in_specs=[
            pl.BlockSpec((SSM_INNER, n), c2),
            pl.BlockSpec((n, SSM_INNER), c2),
            pl.BlockSpec((n, SSM_INNER), lambda i: (0, COL_Z_BLK)),
            pl.BlockSpec((1, SSM_INNER), c2),
            pl.BlockSpec((1, SSM_INNER), c2),
        ],
        out_specs=pl.BlockSpec((n, SSM_INNER), c2),
        out_shape=jax.ShapeDtypeStruct((n, SSM_INNER), F32),
        compiler_params=_cparams(("arbitrary",)),
        name="sample_post",
    )(y_t, xs, proj, dexp, ng)


SWA_S_TILE = 8


def _sample_swa_kernel(q_ref, kn_ref, v_ref, ck_ref, cv_ref, sink_ref, y_ref, ko_ref, vo_ref, sc_ref):
    w = WINDOW
    scale = ATT_HD ** -0.5
    newest = lax.broadcasted_iota(jnp.int32, (ATT_HD, w), 1) == w - 1
    units = [(s, kv) for s in range(SWA_S_TILE) for kv in range(ATT_KV)]
    for u, (s, kv) in enumerate(units):
        dims = slice(kv * ATT_HD, (kv + 1) * ATT_HD)
        kt = jnp.where(newest, kn_ref[dims, s:s + 1], pltpu.roll(ck_ref[s, kv], w - 1, axis=1))
        vt = jnp.where(newest, v_ref[dims, s:s + 1], pltpu.roll(cv_ref[s, kv], w - 1, axis=1))
        ko_ref[s, kv] = kt
        vo_ref[s, kv] = vt
        sc_ref[u * SUBLANES:(u + 1) * SUBLANES, :] = _dot(q_ref[s, kv].astype(BF16), kt.astype(BF16)) * scale
    sc = sc_ref[...]
    snk = jnp.concatenate([sink_ref[...]] * SWA_S_TILE, axis=0)
    m = jnp.maximum(jnp.max(sc, axis=-1, keepdims=True), snk)
    p = jnp.exp(sc - m)
    pn = p / (jnp.sum(p, axis=-1, keepdims=True) + jnp.exp(snk - m))
    for u, (s, kv) in enumerate(units):
        pu = pn[u * SUBLANES:(u + 1) * SUBLANES].astype(BF16)
        y_ref[s, kv] = _dot_nt(pu, vo_ref[s, kv].astype(BF16))


def _sample_swa(q4, kn4, v4, ck_t, cv_t, sink_col, l):
    n = q4.shape[0]
    w = WINDOW
    st = SWA_S_TILE
    cache = pl.BlockSpec((None, st, ATT_KV, ATT_HD, w), lambda i: (l, i, 0, 0, 0))
    new = pl.BlockSpec((None, ATT_KVD, st), lambda i: (i, 0, 0))
    out = pl.BlockSpec((st, ATT_KV, ATT_HD, w), lambda i: (i, 0, 0, 0))
    qspec = pl.BlockSpec((st, ATT_KV, SUBLANES, ATT_HD), lambda i: (i, 0, 0, 0))
    return pl.pallas_call(
        _sample_swa_kernel,
        grid=(n // st,),
        in_specs=[
            qspec, new, new, cache, cache,
            pl.BlockSpec((ATT_KV * SUBLANES, 1), lambda i: (0, 0)),
        ],
        out_specs=[qspec, out, out],
        out_shape=[
            jax.ShapeDtypeStruct((n, ATT_KV, SUBLANES, ATT_HD), F32),
            jax.ShapeDtypeStruct((n, ATT_KV, ATT_HD, w), F32),
            jax.ShapeDtypeStruct((n, ATT_KV, ATT_HD, w), F32),
        ],
        scratch_shapes=[pltpu.VMEM((st * ATT_KV * SUBLANES, w), F32)],
        compiler_params=_cparams(("parallel",)),
        name="sample_swa",
    )(q4, kn4, v4, ck_t, cv_t, sink_col)


MEM_S_TILE = 4


def _sample_mem_kernel(q_ref, k_ref, v_ref, y_ref):
    scale = MEM_HD ** -0.5
    for s in range(MEM_S_TILE):
        sc = jnp.sum(k_ref[s] * q_ref[s][None], axis=-1, keepdims=True) * scale
        m = jnp.max(sc, axis=0, keepdims=True)
        p = jnp.exp(sc - m)
        den = jnp.sum(p, axis=0, keepdims=True)
        o = jnp.sum(p * v_ref[s], axis=0, keepdims=True) / den
        y_ref[s] = o[0]


def _sample_mem(q3, ck, cv, l):
    n = q3.shape[0]
    mem_len = ck.shape[2]
    st = MEM_S_TILE
    cache = pl.BlockSpec((None, st, mem_len, MEM_HEADS, MEM_HD), lambda i: (l, i, 0, 0, 0))
    return pl.pallas_call(
        _sample_mem_kernel,
        grid=(n // st,),
        in_specs=[pl.BlockSpec((st, MEM_HEADS, MEM_HD), lambda i: (i, 0, 0)), cache, cache],
        out_specs=pl.BlockSpec((st, MEM_HEADS, MEM_HD), lambda i: (i, 0, 0)),
        out_shape=jax.ShapeDtypeStruct((n, MEM_HEADS, MEM_HD), F32),
        compiler_params=_cparams(("parallel",)),
        name="sample_mem",
    )(q3, ck, cv)


def _prep_weights(lw):
    w_in = lw['w_in']
    cols = [w_in[:, 0:OFF_Z], w_in[:, OFF_XBC:OFF_DT], w_in[:, OFF_Z:OFF_XBC], w_in[:, OFF_Q:OFF_K],
            w_in[:, OFF_MQ:OFF_MQ + D], w_in[:, OFF_K:OFF_V], w_in[:, OFF_V:OFF_MQ]]
    p = {}
    p['w_main'] = jnp.concatenate(cols, axis=1).astype(BF16)
    p['w_dt'] = jnp.pad(w_in[:, OFF_DT:OFF_Q], ((0, 0), (0, LANE - SSM_HEADS))).astype(BF16)
    p['norm1_g'] = lw['norm1_g'].reshape(1, D)
    p['cw'] = lw['ssm_conv_w']
    p['cb'] = lw['ssm_conv_b'].reshape(1, CONV_CH)
    pad_h = (0, LANE - SSM_HEADS)
    p['dtb'] = jnp.pad(lw['ssm_dt_bias'].astype(F32), pad_h).reshape(1, LANE)
    p['a'] = jnp.pad(-jnp.exp(lw['ssm_a_log'].astype(F32)), pad_h).reshape(1, LANE)
    p['dexp'] = jnp.repeat(lw['ssm_d'], SSM_HEADDIM).reshape(1, SSM_INNER)
    p['ssm_ng'] = lw['ssm_norm_g'].reshape(1, SSM_INNER)
    head_of_ch = jnp.arange(SSM_INNER) // SSM_HEADDIM
    p['e'] = (jnp.arange(LANE)[:, None] == head_of_ch[None, :]).astype(F32)
    p['qg2'] = jnp.tile(lw['att_q_norm_g'], 2).reshape(1, LANE)
    p['kg2'] = jnp.tile(lw['att_k_norm_g'], 2).reshape(1, LANE)
    p['sink_row'] = lw['att_sink'].astype(F32).reshape(1, ATT_HEADS)
    grp = ATT_HEADS // ATT_KV
    p['sink_col'] = jnp.pad(lw['att_sink'].astype(F32).reshape(ATT_KV, grp),
                            ((0, 0), (0, SUBLANES - grp))).reshape(ATT_KV * SUBLANES, 1)
    p['mem_g'] = lw['mem_norm_g'].reshape(1, D)
    p['w_mem_kv'] = lw['w_mem_kv'].astype(BF16)
    p['mqg'] = lw['mem_q_norm_g'].reshape(1, MEM_HD)
    p['mkg'] = lw['mem_k_norm_g'].reshape(1, MEM_HD)
    p['ws'] = lw['w_br_ssm'].astype(BF16)
    p['wa'] = lw['w_br_swa'].astype(BF16)
    p['wm'] = lw['w_br_mem'].astype(BF16)
    p['wo'] = lw['w_out'].astype(BF16)
    p['g2'] = lw['norm2_g'].reshape(1, D)
    pad_r = ROUTE_W - N_GROUPS - N_EXPERTS
    p['wr'] = jnp.pad(jnp.concatenate([lw['w_router_group'], lw['w_router_expert']], axis=1).astype(F32),
                      ((0, 0), (0, pad_r)))
    p['br'] = jnp.pad(jnp.concatenate([lw['b_router_group'], lw['b_router_expert']]).astype(F32),
                      (0, pad_r)).reshape(1, ROUTE_W)
    p['wg'] = lw['w_exp_gate'].astype(BF16)
    p['wu'] = lw['w_exp_up'].astype(BF16)
    p['wd'] = lw['w_exp_down'].astype(BF16)
    return p


def _pick(n, prefs):
    for c in prefs:
        if n % c == 0:
            return c
    return n


def _tail(x, proj, y_ssm, y_swa, y_mem, p):
    t = x.shape[0]
    routed = t % RANK_TM == 0
    x1, h, comb = _back(x, proj, y_ssm, y_swa, y_mem, p['ws'], p['wa'], p['wm'], p['wo'], p['g2'],
                        p['wr'], p['br'], _pick(t, (512, 256, 128)), routed)
    if routed:
        return _moe_routed(x1, h, comb, p['wg'], p['wu'], p['wd'])
    return _moe(x1, h, comb, p['wg'], p['wu'], p['wd'], _pick(t, (1024, 512, 256, 128)))


def _prompt_layer(x, mem, p):
    nb, seq, _ = x.shape
    t = nb * seq
    nc = seq // CHUNK
    xf = x.reshape(t, D)
    proj, dt_raw = _front(xf, p['norm1_g'], p['w_main'], p['w_dt'], _pick(t, (1024, 512, 256, 128)), 1536)
    y_ssm, conv_new, ssm_new = _ssd(proj, dt_raw, p['cw'], p['cb'], p['dtb'], p['a'], p['dexp'],
                                    p['ssm_ng'], nb, nc)
    cos_t, sin_lo, sin_hi = _rope_tables(jnp.arange(seq))
    y_swa, k_new, v_new = _swa(proj, p['qg2'], p['kg2'], cos_t, sin_lo, sin_hi, p['sink_row'], nb, seq)
    mem_len = mem.shape[1]
    mk, mv = _memkv(mem.reshape(nb * mem_len, D), p['mem_g'], p['w_mem_kv'], p['mkg'],
                    _pick(nb * mem_len, (512, 256, 128)))
    y_mem = _memattn(proj, mk, mv, p['mqg'], nb, seq, mem_len, _pick(seq, (512, 256, 128)))
    y = _tail(xf, proj, y_ssm, y_swa, y_mem, p)
    return (y.reshape(nb, seq, D), conv_new,
            ssm_new.reshape(nb, SSM_HEADS, SSM_HEADDIM, SSM_STATE),
            k_new.reshape(nb, WINDOW, ATT_KV, ATT_HD), v_new.reshape(nb, WINDOW, ATT_KV, ATT_HD),
            mk.reshape(nb, mem_len, MEM_HEADS, MEM_HD), mv.reshape(nb, mem_len, MEM_HEADS, MEM_HD))


def _sample_layer(x, conv_st, ssm_st, swa_k, swa_v, mem_k, mem_v, l, p):
    n = x.shape[0]
    xf = x.reshape(n, D)
    proj, dt_raw = _front(xf, p['norm1_g'], p['w_main'], p['w_dt'], n, 1536)
    cos_t, sin_lo, sin_hi = _rope_tables(jnp.full((1,), PAST_LEN, jnp.int32))
    sc2d = conv_st.reshape(n, (SSM_CONV - 1) * CONV_CH)
    conv_new, xs, bc, dtx_t, dec_t, qn, kn, mqn = _sample_prep(
        proj, dt_raw, sc2d, p['cw'], p['cb'], p['dtb'], p['a'], p['e'], p['qg2'], p['kg2'], p['mqg'],
        cos_t, sin_lo, sin_hi)
    ssm_new, y_t = _sample_ssm(ssm_st.reshape(n, SSM_INNER, SSM_STATE), dtx_t, dec_t, bc)
    y_ssm = _sample_post(y_t, xs, proj, p['dexp'], p['ssm_ng'])
    grp = ATT_HEADS // ATT_KV
    v_raw = proj[:, COL_V_BLK * ATT_KVD:(COL_V_BLK + 1) * ATT_KVD].astype(F32)
    to_t = (0, 1, 3, 4, 2)
    per_step = lambda a: a.T.reshape(ATT_KVD, n // SWA_S_TILE, SWA_S_TILE).transpose(1, 0, 2)
    y_swa, k_new, v_new = _sample_swa(
        jnp.pad(qn.reshape(n, ATT_KV, grp, ATT_HD), ((0, 0), (0, 0), (0, SUBLANES - grp), (0, 0))),
        per_step(kn), per_step(v_raw), swa_k.transpose(to_t), swa_v.transpose(to_t), p['sink_col'], l)
    y_swa = y_swa[:, :, 0:grp, :].reshape(n, D)
    k_new = k_new.transpose(0, 3, 1, 2)
    v_new = v_new.transpose(0, 3, 1, 2)
    y_mem = _sample_mem(mqn.reshape(n, MEM_HEADS, MEM_HD), mem_k, mem_v, l)
    y = _tail(xf, proj, y_ssm, y_swa.astype(BF16), y_mem.reshape(n, D).astype(BF16), p)
    return (y.reshape(n, 1, D), conv_new.reshape(n, SSM_CONV - 1, CONV_CH),
            ssm_new.reshape(n, SSM_HEADS, SSM_HEADDIM, SSM_STATE), k_new, v_new)


def kernel(x_prompt, x_sample, state_conv, state_ssm, cache_swa_k, cache_swa_v, cache_mem_k, cache_mem_v,
           mem_prompt, norm1_g, w_in, ssm_conv_w, ssm_conv_b, ssm_dt_bias, ssm_a_log, ssm_d, ssm_norm_g,
           att_q_norm_g, att_k_norm_g, att_sink, mem_norm_g, w_mem_kv, mem_q_norm_g, mem_k_norm_g,
           w_br_ssm, w_br_swa, w_br_mem, w_out, norm2_g, w_router_group, b_router_group,
           w_router_expert, b_router_expert, w_exp_gate, w_exp_up, w_exp_down):
    weights = dict(norm1_g=norm1_g, w_in=w_in, ssm_conv_w=ssm_conv_w, ssm_conv_b=ssm_conv_b,
                   ssm_dt_bias=ssm_dt_bias, ssm_a_log=ssm_a_log, ssm_d=ssm_d, ssm_norm_g=ssm_norm_g,
                   att_q_norm_g=att_q_norm_g, att_k_norm_g=att_k_norm_g, att_sink=att_sink,
                   mem_norm_g=mem_norm_g, w_mem_kv=w_mem_kv, mem_q_norm_g=mem_q_norm_g,
                   mem_k_norm_g=mem_k_norm_g, w_br_ssm=w_br_ssm, w_br_swa=w_br_swa, w_br_mem=w_br_mem,
                   w_out=w_out, norm2_g=norm2_g, w_router_group=w_router_group,
                   b_router_group=b_router_group, w_router_expert=w_router_expert,
                   b_router_expert=b_router_expert, w_exp_gate=w_exp_gate, w_exp_up=w_exp_up,
                   w_exp_down=w_exp_down)
    depth = w_in.shape[0]
    xp, xs = x_prompt, x_sample
    outs = [[] for _ in range(10)]
    for l in range(depth):
        p = _prep_weights({k: v[l] for k, v in weights.items()})
        xp, c1, c2, c3, c4, c5, c6 = _prompt_layer(xp, mem_prompt, p)
        xs, d1, d2, d3, d4 = _sample_layer(xs, state_conv[l], state_ssm[l], cache_swa_k, cache_swa_v,
                                           cache_mem_k, cache_mem_v, l, p)
        for lst, val in zip(outs, (c1, c2, c3, c4, c5, c6, d1, d2, d3, d4)):
            lst.append(val)
    return (xp, xs) + tuple(jnp.stack(o) for o in outs)
```

```python
import functools
import math

import jax
import jax.numpy as jnp
from jax import lax
from jax.experimental import pallas as pl
from jax.experimental.pallas import tpu as pltpu

F32 = jnp.float32
BF16 = jnp.bfloat16
HIGHEST = lax.Precision.HIGHEST

D = 1024
SSM_INNER = 2048
SSM_HEADDIM = 64
SSM_HEADS = 32
SSM_GROUPS = 4
SSM_HPG = SSM_HEADS // SSM_GROUPS
SSM_STATE = 128
SSM_CONV = 4
CONV_CH = SSM_INNER + 2 * SSM_GROUPS * SSM_STATE
CHUNK = 128
ATT_HEADS = 16
ATT_KV = 4
ATT_HD = 64
ATT_KVD = ATT_KV * ATT_HD
WINDOW = 128
ROPE_THETA = 10000.0
MEM_HEADS = 4
MEM_HD = 256
N_EXPERTS = 32
N_GROUPS = 4
EPG = 8
D_FF = 256
EPS = 1e-6
PAST_LEN = 16384

OFF_Z = 3 * D
OFF_XBC = OFF_Z + SSM_INNER
OFF_DT = OFF_XBC + CONV_CH
OFF_Q = OFF_DT + SSM_HEADS
OFF_K = OFF_Q + D
OFF_V = OFF_K + ATT_KVD
OFF_MQ = OFF_V + ATT_KVD

N_MAIN = 3 * D + CONV_CH + SSM_INNER + D + D + 2 * ATT_KVD
COL_XBC_BLK = 1
COL_Z_BLK = 3
COL_Q_BLK = 8
COL_MQ_BLK = 9
COL_K_BLK = 40
COL_V_BLK = 41
LANE = 128
ROUTE_W = 128

VMEM_LIMIT = 56 * 1024 * 1024


def _cparams(sem):
    return pltpu.CompilerParams(dimension_semantics=sem, vmem_limit_bytes=VMEM_LIMIT)


def _sigmoid(x):
    return 1.0 / (1.0 + jnp.exp(-x))


def _silu(x):
    h = 0.5 * x
    return h + h * jnp.tanh(h)


def _softplus(x):
    return jnp.maximum(x, 0.0) + jnp.log(1.0 + jnp.exp(-jnp.abs(x)))


def _dot(a, b):
    return jnp.dot(a, b, preferred_element_type=F32)


def _dot_nt(a, b):
    return lax.dot_general(a, b, (((1,), (1,)), ((), ())), preferred_element_type=F32)


def _dot_tn(a, b):
    return lax.dot_general(a, b, (((0,), (0,)), ((), ())), preferred_element_type=F32)


def _dot_exact(a, b):
    return jnp.dot(a, b, preferred_element_type=F32, precision=HIGHEST)


def _front_kernel(x_ref, g_ref, w_ref, wdt_ref, o_ref, dt_ref, hn_ref):
    @pl.when(pl.program_id(1) == 0)
    def _():
        x = x_ref[...]
        ms = jnp.mean(x * x, axis=-1, keepdims=True)
        hn = (x * lax.rsqrt(ms + EPS) * g_ref[...]).astype(BF16)
        hn_ref[...] = hn
        dt_ref[...] = _dot(hn, wdt_ref[...])

    o_ref[...] = _dot(hn_ref[...], w_ref[...]).astype(BF16)


def _front(x, g, w_main, w_dt, tm, tn):
    t = x.shape[0]
    return pl.pallas_call(
        _front_kernel,
        grid=(t // tm, N_MAIN // tn),
        in_specs=[
            pl.BlockSpec((tm, D), lambda i, j: (i, 0)),
            pl.BlockSpec((1, D), lambda i, j: (0, 0)),
            pl.BlockSpec((D, tn), lambda i, j: (0, j)),
            pl.BlockSpec((D, LANE), lambda i, j: (0, 0)),
        ],
        out_specs=[
            pl.BlockSpec((tm, tn), lambda i, j: (i, j)),
            pl.BlockSpec((tm, LANE), lambda i, j: (i, 0)),
        ],
        out_shape=[
            jax.ShapeDtypeStruct((t, N_MAIN), BF16),
            jax.ShapeDtypeStruct((t, LANE), F32),
        ],
        scratch_shapes=[pltpu.VMEM((tm, D), BF16)],
        compiler_params=_cparams(("parallel", "arbitrary")),
        name="front",
    )(x, g, w_main, w_dt)


SUBLANES = 8


def _ssd_kernel(xbc_ref, z_ref, dt_ref, cw_ref, cb_ref, dtb_ref, a_ref, dexp_ref, ng_ref,
                y_ref, conv_ref, ssm_ref, prev_ref, h_ref, yacc_ref):
    c = pl.program_id(1)

    @pl.when(c == 0)
    def _():
        prev_ref[...] = jnp.zeros_like(prev_ref)
        h_ref[...] = jnp.zeros_like(h_ref)

    x_raw = xbc_ref[...].astype(F32)
    prev = prev_ref[...]
    row8 = lax.broadcasted_iota(jnp.int32, (SUBLANES, CONV_CH), 0)
    row = lax.broadcasted_iota(jnp.int32, (CHUNK, CHUNK), 0)
    col = lax.broadcasted_iota(jnp.int32, (CHUNK, CHUNK), 1)
    causal = row >= col
    acc = x_raw * cw_ref[SSM_CONV - 1:SSM_CONV, :] + cb_ref[...]
    for s in range(1, SSM_CONV):
        xr = pltpu.roll(x_raw, s, axis=0)
        head = jnp.where(row8 < s, pltpu.roll(prev, s, axis=0), xr[0:SUBLANES])
        shifted = jnp.concatenate([head, xr[SUBLANES:]], axis=0)
        acc = acc + shifted * cw_ref[SSM_CONV - 1 - s:SSM_CONV - s, :]
    prev_ref[...] = x_raw[CHUNK - SUBLANES:CHUNK]
    conv_ref[0] = x_raw[CHUNK - (SSM_CONV - 1):CHUNK]
    act = _silu(acc)

    dt = _softplus(dt_ref[...] + dtb_ref[...])
    da = dt * a_ref[...]
    acum = _dot_exact(causal.astype(F32), da)
    acum_t = acum.T
    dt_t = dt.T
    exp_a = jnp.exp(acum)
    last = acum[CHUNK - 1:CHUNK, :]
    w_end = jnp.exp(last - acum) * dt
    cd = jnp.broadcast_to(jnp.exp(acum_t[:, CHUNK - 1:CHUNK]), (LANE, SSM_STATE))
    lane = lax.broadcasted_iota(jnp.int32, (CHUNK, LANE), 1)
    low_half = lane < SSM_HEADDIM

    def pair_cols(per_head, hd):
        return jnp.where(low_half, per_head[:, hd:hd + 1], per_head[:, hd + 1:hd + 2])

    xs_off = 0
    b_off = SSM_INNER
    c_off = SSM_INNER + SSM_GROUPS * SSM_STATE
    for g in range(SSM_GROUPS):
        bm = act[:, b_off + g * SSM_STATE:b_off + (g + 1) * SSM_STATE].astype(BF16)
        cm = act[:, c_off + g * SSM_STATE:c_off + (g + 1) * SSM_STATE].astype(BF16)
        cb = _dot_nt(cm, bm)
        gw = SSM_HPG * SSM_HEADDIM
        ch0 = g * gw
        h_prev = h_ref[ch0:ch0 + gw, :]
        y_off = _dot_nt(cm, h_prev.astype(BF16))
        xg = act[:, xs_off + ch0:xs_off + ch0 + gw]
        xw = []
        for pr in range(SSM_HPG // 2):
            hd0 = g * SSM_HPG + pr * 2
            xp32 = xg[:, pr * LANE:(pr + 1) * LANE]
            xpair = xp32.astype(BF16)
            yd = []
            for sub in range(2):
                hd = hd0 + sub
                seg = acum[:, hd:hd + 1] - acum_t[hd:hd + 1, :]
                decay = jnp.exp(jnp.where(causal, seg, -jnp.inf))
                wts = cb * decay * dt_t[hd:hd + 1, :]
                yd.append(_dot(wts.astype(BF16), xpair))
            cl = ch0 + pr * LANE
            y_pair = (jnp.where(low_half, yd[0], yd[1])
                      + y_off[:, pr * LANE:(pr + 1) * LANE] * pair_cols(exp_a, hd0)
                      + dexp_ref[:, cl:cl + LANE] * xp32)
            yacc_ref[:, cl:cl + LANE] = y_pair
            xw.append((xp32 * pair_cols(w_end, hd0)).astype(BF16))
        states = _dot_tn(jnp.concatenate(xw, axis=1), bm)
        for r in range(SSM_HPG):
            hd = g * SSM_HPG + r
            r0 = ch0 + r * SSM_HEADDIM
            h_ref[r0:r0 + SSM_HEADDIM, :] = (h_ref[r0:r0 + SSM_HEADDIM, :] * cd[hd:hd + 1, :]
                                             + states[r * SSM_HEADDIM:(r + 1) * SSM_HEADDIM, :])

    ssm_ref[0] = h_ref[...]
    yg = yacc_ref[...] * _silu(z_ref[...].astype(F32))
    ms = jnp.mean(yg * yg, axis=-1, keepdims=True)
    y_ref[...] = (yg * lax.rsqrt(ms + EPS) * ng_ref[...]).astype(BF16)


def _ssd(proj, dt_raw, cw, cb, dtb, a, dexp, ng, nb, nc):
    t = proj.shape[0]
    return pl.pallas_call(
        _ssd_kernel,
        grid=(nb, nc),
        in_specs=[
            pl.BlockSpec((CHUNK, CONV_CH), lambda b, c: (b * nc + c, COL_XBC_BLK)),
            pl.BlockSpec((CHUNK, SSM_INNER), lambda b, c: (b * nc + c, COL_Z_BLK)),
            pl.BlockSpec((CHUNK, LANE), lambda b, c: (b * nc + c, 0)),
            pl.BlockSpec((SSM_CONV, CONV_CH), lambda b, c: (0, 0)),
            pl.BlockSpec((1, CONV_CH), lambda b, c: (0, 0)),
            pl.BlockSpec((1, LANE), lambda b, c: (0, 0)),
            pl.BlockSpec((1, LANE), lambda b, c: (0, 0)),
            pl.BlockSpec((1, SSM_INNER), lambda b, c: (0, 0)),
            pl.BlockSpec((1, SSM_INNER), lambda b, c: (0, 0)),
        ],
        out_specs=[
            pl.BlockSpec((CHUNK, SSM_INNER), lambda b, c: (b * nc + c, 0)),
            pl.BlockSpec((1, SSM_CONV - 1, CONV_CH), lambda b, c: (b, 0, 0)),
            pl.BlockSpec((1, SSM_INNER, SSM_STATE), lambda b, c: (b, 0, 0)),
        ],
        out_shape=[
            jax.ShapeDtypeStruct((t, SSM_INNER), BF16),
            jax.ShapeDtypeStruct((nb, SSM_CONV - 1, CONV_CH), F32),
            jax.ShapeDtypeStruct((nb, SSM_INNER, SSM_STATE), F32),
        ],
        scratch_shapes=[
            pltpu.VMEM((SUBLANES, CONV_CH), F32),
            pltpu.VMEM((SSM_INNER, SSM_STATE), F32),
            pltpu.VMEM((CHUNK, SSM_INNER), F32),
        ],
        compiler_params=_cparams(("parallel", "arbitrary")),
        name="ssd_prompt",
    )(proj, proj, dt_raw, cw, cb, dtb, a, dexp, ng)


def _norm_rope_tile(xj, g2, cos_t, sin_lo, sin_hi, low_half):
    sq = xj * xj
    s_lo = jnp.sum(jnp.where(low_half, sq, 0.0), axis=-1, keepdims=True)
    s_hi = jnp.sum(jnp.where(low_half, 0.0, sq), axis=-1, keepdims=True)
    ms = jnp.where(low_half, s_lo, s_hi) * (1.0 / ATT_HD)
    xn = xj * lax.rsqrt(ms + EPS) * g2
    half = ATT_HD // 2
    return (xn * cos_t + pltpu.roll(xn, LANE - half, axis=1) * sin_lo
            + pltpu.roll(xn, half, axis=1) * sin_hi)


def _rope_tables(pos):
    half = ATT_HD // 2
    inv = ROPE_THETA ** (-jnp.arange(half, dtype=F32) / half)
    ang = pos.astype(F32)[:, None] * inv[None, :]
    cos = jnp.cos(ang)
    sin = jnp.sin(ang)
    zero = jnp.zeros_like(sin)
    cos_t = jnp.concatenate([cos, cos, cos, cos], axis=1)
    sin_lo = jnp.concatenate([-sin, zero, -sin, zero], axis=1)
    sin_hi = jnp.concatenate([zero, sin, zero, sin], axis=1)
    return cos_t, sin_lo, sin_hi


SWA_NSB = 4


def _swa_kernel(q_ref, k_ref, v_ref, qg_ref, kg_ref, cos_ref, slo_ref, shi_ref, sink_ref,
                y_ref, ko_ref, vo_ref, kd_ref, vd_ref, qs_ref):
    c = pl.program_id(1)
    w = WINDOW
    nsb = SWA_NSB
    grp = ATT_HEADS // ATT_KV
    rows = nsb * w
    lane = lax.broadcasted_iota(jnp.int32, (rows, LANE), 1)
    low_half = lane < ATT_HD
    low_half_w = lax.broadcasted_iota(jnp.int32, (w, LANE), 1) < ATT_HD
    cos_t = cos_ref[...]
    sin_lo = slo_ref[...]
    sin_hi = shi_ref[...]

    @pl.when(c == 0)
    def _():
        kd_ref[:, 0:w, :] = jnp.zeros((ATT_KV, w, LANE), BF16)
        vd_ref[:, 0:w, :] = jnp.zeros((ATT_KV, w, 2 * LANE), BF16)
        vd_ref[:, :, LANE:2 * LANE] = jnp.ones((ATT_KV, (nsb + 1) * w, LANE), BF16)

    kf = k_ref[...].astype(F32)
    vf = v_ref[...].astype(F32)
    for j in range(ATT_KVD // LANE):
        sl = slice(j * LANE, (j + 1) * LANE)
        kn = _norm_rope_tile(kf[:, sl], kg_ref[...], cos_t, sin_lo, sin_hi, low_half)
        ko_ref[0, :, sl] = kn[rows - w:rows]
        kr = pltpu.roll(kn, ATT_HD, axis=1)
        vj = vf[:, sl]
        vr = pltpu.roll(vj, ATT_HD, axis=1)
        kd_ref[2 * j, w:w + rows, :] = jnp.where(low_half, kn, kr).astype(BF16)
        kd_ref[2 * j + 1, w:w + rows, :] = jnp.where(low_half, kr, kn).astype(BF16)
        vd_ref[2 * j, w:w + rows, 0:LANE] = jnp.where(low_half, vj, vr).astype(BF16)
        vd_ref[2 * j + 1, w:w + rows, 0:LANE] = jnp.where(low_half, vr, vj).astype(BF16)
    vo_ref[0] = vf[rows - w:rows]

    qf = q_ref[...].astype(F32)
    for j in range(D // LANE):
        sl = slice(j * LANE, (j + 1) * LANE)
        qn = _norm_rope_tile(qf[:, sl], qg_ref[...] * (ATT_HD ** -0.5), cos_t, sin_lo, sin_hi, low_half)
        for par in range(2):
            h = 2 * j + par
            r0 = (h % grp) * w
            in_half = low_half if par == 0 else jnp.logical_not(low_half)
            qh = jnp.where(in_half, qn, 0.0).astype(BF16)
            for sb in range(nsb):
                qs_ref[h // grp, sb, r0:r0 + w, :] = qh[sb * w:(sb + 1) * w]

    qi = lax.broadcasted_iota(jnp.int32, (grp * w, 2 * w), 0) & (w - 1)
    kj = lax.broadcasted_iota(jnp.int32, (grp * w, 2 * w), 1)
    in_cur = (kj >= w) & ((kj - w) <= qi)
    in_prev = (kj < w) & (kj > qi)

    for sb in range(nsb):
        mask = in_cur | (in_prev & (c > 0) if sb == 0 else in_prev)
        for kv in range(ATT_KV):
            snk = jnp.concatenate(
                [jnp.broadcast_to(sink_ref[:, kv * grp + i:kv * grp + i + 1], (w, 1)) for i in range(grp)], axis=0)
            s = _dot_nt(qs_ref[kv, sb], kd_ref[kv, sb * w:(sb + 2) * w, :])
            s = jnp.where(mask, s, -jnp.inf)
            m = jnp.maximum(jnp.max(s, axis=-1, keepdims=True), snk)
            p = jnp.exp(s - m)
            den = jnp.sum(p, axis=-1, keepdims=True) + jnp.exp(snk - m)
            o = _dot(p.astype(BF16), vd_ref[kv, sb * w:(sb + 2) * w, 0:LANE]) * (1.0 / den)
            for a in range(grp // 2):
                col = kv * (grp // 2) + a
                y_ref[sb * w:(sb + 1) * w, col * LANE:(col + 1) * LANE] = jnp.where(
                    low_half_w, o[2 * a * w:(2 * a + 1) * w, :], o[(2 * a + 1) * w:(2 * a + 2) * w, :]
                ).astype(BF16)

    kd_ref[:, 0:w, :] = kd_ref[:, rows:rows + w, :]
    vd_ref[:, 0:w, 0:LANE] = vd_ref[:, rows:rows + w, 0:LANE]


def _swa(proj, qg2, kg2, cos_t, sin_lo, sin_hi, sink, nb, seq):
    t = proj.shape[0]
    w = WINDOW
    rows = SWA_NSB * w
    nc = seq // rows
    grp = ATT_HEADS // ATT_KV
    tab = pl.BlockSpec((rows, LANE), lambda b, c: (c, 0))
    return pl.pallas_call(
        _swa_kernel,
        grid=(nb, nc),
        in_specs=[
            pl.BlockSpec((rows, D), lambda b, c: (b * nc + c, COL_Q_BLK)),
            pl.BlockSpec((rows, ATT_KVD), lambda b, c: (b * nc + c, COL_K_BLK)),
            pl.BlockSpec((rows, ATT_KVD), lambda b, c: (b * nc + c, COL_V_BLK)),
            pl.BlockSpec((1, LANE), lambda b, c: (0, 0)),
            pl.BlockSpec((1, LANE), lambda b, c: (0, 0)),
            tab, tab, tab,
            pl.BlockSpec((1, ATT_HEADS), lambda b, c: (0, 0)),
        ],
        out_specs=[
            pl.BlockSpec((rows, D), lambda b, c: (b * nc + c, 0)),
            pl.BlockSpec((1, w, ATT_KVD), lambda b, c: (b, 0, 0)),
            pl.BlockSpec((1, w, ATT_KVD), lambda b, c: (b, 0, 0)),
        ],
        out_shape=[
            jax.ShapeDtypeStruct((t, D), BF16),
            jax.ShapeDtypeStruct((nb, w, ATT_KVD), F32),
            jax.ShapeDtypeStruct((nb, w, ATT_KVD), F32),
        ],
        scratch_shapes=[
            pltpu.VMEM((ATT_KV, rows + w, LANE), BF16),
            pltpu.VMEM((ATT_KV, rows + w, 2 * LANE), BF16),
            pltpu.VMEM((ATT_KV, SWA_NSB, grp * w, LANE), BF16),
        ],
        compiler_params=_cparams(("parallel", "arbitrary")),
        name="swa_prompt",
    )(proj, proj, proj, qg2, kg2, cos_t, sin_lo, sin_hi, sink)


def _memkv_kernel(m_ref, g_ref, w_ref, kg_ref, k_ref, v_ref):
    x = m_ref[...]
    ms = jnp.mean(x * x, axis=-1, keepdims=True)
    hn = (x * lax.rsqrt(ms + EPS) * g_ref[...]).astype(BF16)
    kv = _dot(hn, w_ref[...])
    for h in range(MEM_HEADS):
        kh = kv[:, h * MEM_HD:(h + 1) * MEM_HD]
        ms = jnp.mean(kh * kh, axis=-1, keepdims=True)
        k_ref[:, h * MEM_HD:(h + 1) * MEM_HD] = kh * lax.rsqrt(ms + EPS) * kg_ref[...]
    v_ref[...] = kv[:, D:]


def _memkv(mem, g, w, kg, tm):
    t = mem.shape[0]
    return pl.pallas_call(
        _memkv_kernel,
        grid=(t // tm,),
        in_specs=[
            pl.BlockSpec((tm, D), lambda i: (i, 0)),
            pl.BlockSpec((1, D), lambda i: (0, 0)),
            pl.BlockSpec((D, 2 * D), lambda i: (0, 0)),
            pl.BlockSpec((1, MEM_HD), lambda i: (0, 0)),
        ],
        out_specs=[
            pl.BlockSpec((tm, D), lambda i: (i, 0)),
            pl.BlockSpec((tm, D), lambda i: (i, 0)),
        ],
        out_shape=[jax.ShapeDtypeStruct((t, D), F32)] * 2,
        compiler_params=_cparams(("parallel",)),
        name="mem_kv",
    )(mem, g, w, kg)


def _memattn_kernel(q_ref, k_ref, v_ref, qg_ref, y_ref):
    q = q_ref[...].astype(F32)
    scale = MEM_HD ** -0.5
    for h in range(MEM_HEADS):
        sl = slice(h * MEM_HD, (h + 1) * MEM_HD)
        qh = q[:, sl]
        ms = jnp.mean(qh * qh, axis=-1, keepdims=True)
        qn = (qh * lax.rsqrt(ms + EPS) * qg_ref[...]).astype(BF16)
        s = _dot_nt(qn, k_ref[:, sl].astype(BF16)) * scale
        m = jnp.max(s, axis=-1, keepdims=True)
        p = jnp.exp(s - m)
        den = jnp.sum(p, axis=-1, keepdims=True)
        o = _dot(p.astype(BF16), v_ref[:, sl].astype(BF16)) / den
        y_ref[:, sl] = o.astype(BF16)


def _memattn(proj, mk, mv, qg, nb, seq, mem_len, tm):
    t = proj.shape[0]
    nt = seq // tm
    return pl.pallas_call(
        _memattn_kernel,
        grid=(nb, nt),
        in_specs=[
            pl.BlockSpec((tm, D), lambda b, i: (b * nt + i, COL_MQ_BLK)),
            pl.BlockSpec((mem_len, D), lambda b, i: (b, 0)),
            pl.BlockSpec((mem_len, D), lambda b, i: (b, 0)),
            pl.BlockSpec((1, MEM_HD), lambda b, i: (0, 0)),
        ],
        out_specs=pl.BlockSpec((tm, D), lambda b, i: (b * nt + i, 0)),
        out_shape=jax.ShapeDtypeStruct((t, D), BF16),
        compiler_params=_cparams(("parallel", "arbitrary")),
        name="mem_attn_prompt",
    )(proj, mk, mv, qg)


def _route(logits):
    lane = lax.broadcasted_iota(jnp.int32, logits.shape, 1)
    lanef = lane.astype(F32)
    is_g = lane < N_GROUPS
    neg = -jnp.inf
    big = 1e9
    gl = jnp.where(is_g, logits, neg)
    gmax = jnp.max(gl, axis=-1, keepdims=True)
    gidx = jnp.min(jnp.where(is_g & (logits == gmax), lanef, big), axis=-1, keepdims=True)
    pg_top = 1.0 / jnp.sum(jnp.where(is_g, jnp.exp(logits - gmax), 0.0), axis=-1, keepdims=True)
    lo = N_GROUPS + gidx * EPG
    in_g = (lanef >= lo) & (lanef < lo + EPG)
    m1 = jnp.max(jnp.where(in_g, logits, neg), axis=-1, keepdims=True)
    i1 = jnp.min(jnp.where(in_g & (logits == m1), lanef, big), axis=-1, keepdims=True)
    rest = in_g & (lanef != i1)
    m2 = jnp.max(jnp.where(rest, logits, neg), axis=-1, keepdims=True)
    i2 = jnp.min(jnp.where(rest & (logits == m2), lanef, big), axis=-1, keepdims=True)
    r = jnp.exp(m2 - m1)
    w1 = pg_top / (1.0 + r)
    w2 = pg_top * r / (1.0 + r)
    comb = jnp.where(lanef == i1, w1, 0.0) + jnp.where(lanef == i2, w2, 0.0)
    a = jnp.minimum(i1, i2) - lo
    b = jnp.maximum(i1, i2) - lo
    bucket = gidx * PAIRS_PER_GROUP + a * (2 * EPG - 1 - a) * 0.5 + (b - a - 1.0)
    return jnp.where(lane == 0, bucket, comb)


def _back_kernel(x_ref, gt_ref, ys_ref, ya_ref, ym_ref, ws_ref, wa_ref, wm_ref, wo_ref, g2_ref,
                 wr_ref, br_ref, x1_ref, h_ref, comb_ref, *, rows_for_dispatch):
    gt = gt_ref[...].astype(F32)
    merged = (_sigmoid(gt[:, 0:D]) * _dot(ys_ref[...].astype(BF16), ws_ref[...])
              + _sigmoid(gt[:, D:2 * D]) * _dot(ya_ref[...], wa_ref[...])
              + _sigmoid(gt[:, 2 * D:3 * D]) * _dot(ym_ref[...], wm_ref[...]))
    x1 = x_ref[...] + _dot(merged.astype(BF16), wo_ref[...])
    x1_ref[...] = x1
    ms = jnp.mean(x1 * x1, axis=-1, keepdims=True)
    h = x1 * lax.rsqrt(ms + EPS) * g2_ref[...]
    h_hi = h.astype(BF16)
    h_lo = (h - h_hi.astype(F32)).astype(BF16)
    wr = wr_ref[...]
    wr_hi = wr.astype(BF16)
    wr_lo = (wr - wr_hi.astype(F32)).astype(BF16)
    logits = (_dot(h_hi, wr_hi) + _dot(h_hi, wr_lo) + _dot(h_lo, wr_hi)) + br_ref[...]
    comb = _route(logits)
    comb_ref[...] = comb
    if rows_for_dispatch:
        h_ref[:, 0:D] = h
        h_ref[:, D:D + ROUTE_W] = comb
    else:
        h_ref[...] = h.astype(BF16)


def _back(x, proj, y_ssm, y_swa, y_mem, ws, wa, wm, wo, g2, wr, br, tm, rows_for_dispatch):
    t = x.shape[0]
    resident = lambda shape: pl.BlockSpec(shape, lambda i: (0, 0), pipeline_mode=pl.Buffered(1))
    hw = D + ROUTE_W if rows_for_dispatch else D
    return pl.pallas_call(
        functools.partial(_back_kernel, rows_for_dispatch=rows_for_dispatch),
        grid=(t // tm,),
        in_specs=[
            pl.BlockSpec((tm, D), lambda i: (i, 0)),
            pl.BlockSpec((tm, 3 * D), lambda i: (i, 0)),
            pl.BlockSpec((tm, SSM_INNER), lambda i: (i, 0)),
            pl.BlockSpec((tm, D), lambda i: (i, 0)),
            pl.BlockSpec((tm, D), lambda i: (i, 0)),
            resident((SSM_INNER, D)),
            resident((D, D)),
            resident((D, D)),
            resident((D, D)),
            resident((1, D)),
            resident((D, ROUTE_W)),
            resident((1, ROUTE_W)),
        ],
        out_specs=[
            pl.BlockSpec((tm, D), lambda i: (i, 0)),
            pl.BlockSpec((tm, hw), lambda i: (i, 0)),
            pl.BlockSpec((tm, ROUTE_W), lambda i: (i, 0)),
        ],
        out_shape=[
            jax.ShapeDtypeStruct((t, D), F32),
            jax.ShapeDtypeStruct((t, hw), F32 if rows_for_dispatch else BF16),
            jax.ShapeDtypeStruct((t, ROUTE_W), F32),
        ],
        compiler_params=_cparams(("parallel",)),
        name="back",
    )(x, proj, y_ssm, y_swa, y_mem, ws, wa, wm, wo, g2, wr, br)


PAIRS_PER_GROUP = EPG * (EPG - 1) // 2
N_BUCKETS = N_GROUPS * PAIRS_PER_GROUP
MOE_TR = 192
RANK_TM = 1024
COMBINE_TM = 512
ROW_W = D + ROUTE_W


def _bucket_experts():
    first, second = [], []
    for g in range(N_GROUPS):
        for a in range(EPG):
            for b in range(a + 1, EPG):
                first.append(g * EPG + a)
                second.append(g * EPG + b)
    return jnp.array(first, jnp.int32), jnp.array(second, jnp.int32)


def _rank_kernel(comb_ref, pos_ref, cnt_ref, carry_ref, offs_ref):
    ph = pl.program_id(0)
    i = pl.program_id(1)
    tm = comb_ref.shape[0]
    lane = lax.broadcasted_iota(jnp.int32, (tm, LANE), 1)
    gid = jnp.sum(jnp.where(lane == 0, comb_ref[...], 0.0), axis=-1, keepdims=True)
    onehot = (lane.astype(F32) == gid).astype(F32)
    colsum = jnp.sum(onehot, axis=0, keepdims=True)

    @pl.when((ph == 0) & (i == 0))
    def _():
        cnt_ref[...] = jnp.zeros_like(cnt_ref)

    @pl.when(ph == 0)
    def _():
        cnt_ref[...] += colsum

    @pl.when((ph == 1) & (i == 0))
    def _():
        padded = jnp.ceil(cnt_ref[...] * (1.0 / MOE_TR)) * MOE_TR
        r = lax.broadcasted_iota(jnp.int32, (LANE, LANE), 0)
        c = lax.broadcasted_iota(jnp.int32, (LANE, LANE), 1)
        offs_ref[...] = _dot_exact(padded, (r < c).astype(F32))
        carry_ref[...] = jnp.zeros_like(carry_ref)

    @pl.when(ph == 1)
    def _():
        rr = lax.broadcasted_iota(jnp.int32, (tm, tm), 0)
        cc = lax.broadcasted_iota(jnp.int32, (tm, tm), 1)
        before = _dot((cc < rr).astype(BF16), onehot.astype(BF16))
        slot = onehot * (offs_ref[...] + carry_ref[...] + before)
        pos = lax.dot_general(jnp.ones((8, LANE), F32), slot, (((1,), (1,)), ((), ())),
                              preferred_element_type=F32, precision=HIGHEST)
        pos_ref[0] = pos.astype(jnp.int32)
        carry_ref[...] += colsum


def _rank(comb):
    t = comb.shape[0]
    tm = RANK_TM
    nt = t // tm
    return pl.pallas_call(
        _rank_kernel,
        grid=(2, nt),
        in_specs=[pl.BlockSpec((tm, ROUTE_W), lambda p, i: (i, 0))],
        out_specs=[
            pl.BlockSpec((1, 8, tm), lambda p, i: (i * p, 0, 0)),
            pl.BlockSpec((1, LANE), lambda p, i: (0, 0)),
        ],
        out_shape=[
            jax.ShapeDtypeStruct((nt, 8, tm), jnp.int32),
            jax.ShapeDtypeStruct((1, LANE), F32),
        ],
        scratch_shapes=[pltpu.VMEM((1, LANE), F32), pltpu.VMEM((1, LANE), F32)],
        compiler_params=_cparams(("arbitrary", "arbitrary")),
        name="moe_rank",
    )(comb)


def _row_copy(src_hbm, dst_hbm, src_row, dst_row, sem):
    return pltpu.make_async_copy(src_hbm.at[pl.ds(src_row, 1)], dst_hbm.at[pl.ds(dst_row, 1)], sem)


PAD_BITS = (MOE_TR - 1).bit_length()


DISPATCH_BUFS = 3


def _dispatch_kernel(ps_ref, pc_ref, nt_ref, pos_ref, rows_hbm, xs_hbm, rbuf, zbuf, lsem, sem, zsem):
    i = pl.program_id(0)
    n_steps = pl.num_programs(0)
    tm = rbuf.shape[1]
    slot = i % 2
    cur = i % DISPATCH_BUFS

    def load(block, b):
        return pltpu.make_async_copy(rows_hbm.at[pl.ds(pl.multiple_of(block * tm, tm), tm)], rbuf.at[b], lsem.at[b])

    def wait_rows(b, s):
        pltpu.make_async_copy(rbuf.at[b], xs_hbm.at[pl.ds(0, tm)], sem.at[s]).wait()

    @pl.when(i == 0)
    def _():
        load(0, 0).start()

    @pl.when(i + 1 < n_steps)
    def _():
        load(i + 1, (i + 1) % DISPATCH_BUFS).start()

    load(i, cur).wait()
    for r in range(tm):
        _row_copy(rbuf.at[cur], xs_hbm, r, pos_ref[0, r], sem.at[slot]).start()

    @pl.when(i > 0)
    def _():
        wait_rows((i + DISPATCH_BUFS - 1) % DISPATCH_BUFS, 1 - slot)

    @pl.when(i == pl.num_programs(0) - 1)
    def _():
        wait_rows(cur, slot)
        zbuf[...] = jnp.zeros_like(zbuf)

        def runs(b, fn):
            first = ps_ref[b]
            count = pc_ref[b]
            head = jnp.minimum((-first) & (SUBLANES - 1), count)
            for j in range(SUBLANES - 1):
                @pl.when(j < head)
                def _():
                    fn(pltpu.make_async_copy(zbuf.at[pl.ds(0, 1)], xs_hbm.at[pl.ds(first + j, 1)], zsem.at[0]))
            rest = count - head
            for k in range(SUBLANES.bit_length() - 1, PAD_BITS):
                size = 1 << k
                start = pl.multiple_of(first + head + ((rest >> (k + 1)) << (k + 1)), SUBLANES)

                @pl.when((rest & size) != 0)
                def _():
                    fn(pltpu.make_async_copy(zbuf.at[pl.ds(0, size)], xs_hbm.at[pl.ds(start, size)], zsem.at[0]))

        def issue(b, carry):
            runs(b, lambda cp: cp.start())
            return carry

        def drain(b, carry):
            runs(b, lambda cp: cp.wait())
            return carry

        lax.fori_loop(0, N_BUCKETS, issue, 0)
        lax.fori_loop(0, N_BUCKETS, drain, 0)

        def tile_copy(j):
            return pltpu.make_async_copy(zbuf, xs_hbm.at[pl.ds(pl.multiple_of(j * MOE_TR, MOE_TR), MOE_TR)],
                                         zsem.at[0])

        n_tiles = xs_hbm.shape[0] // MOE_TR
        lax.fori_loop(nt_ref[0], n_tiles, lambda j, c: (tile_copy(j).start(), c)[1], 0)
        lax.fori_loop(nt_ref[0], n_tiles, lambda j, c: (tile_copy(j).wait(), c)[1], 0)


def _dispatch(pad_start, pad_count, n_used, pos3, rows, n_slots):
    t = rows.shape[0]
    tm = RANK_TM
    grid_spec = pltpu.PrefetchScalarGridSpec(
        num_scalar_prefetch=3,
        grid=(t // tm,),
        in_specs=[
            pl.BlockSpec((None, 1, tm), lambda i, ps, pc, nt: (i, 0, 0), memory_space=pltpu.SMEM),
            pl.BlockSpec(memory_space=pl.ANY),
        ],
        out_specs=pl.BlockSpec(memory_space=pl.ANY),
        scratch_shapes=[
            pltpu.VMEM((DISPATCH_BUFS, tm, ROW_W), F32),
            pltpu.VMEM((MOE_TR, ROW_W), F32),
            pltpu.SemaphoreType.DMA((DISPATCH_BUFS,)),
            pltpu.SemaphoreType.DMA((2,)),
            pltpu.SemaphoreType.DMA((1,)),
        ],
    )
    return pl.pallas_call(
        _dispatch_kernel,
        grid_spec=grid_spec,
        out_shape=jax.ShapeDtypeStruct((n_slots, ROW_W), F32),
        compiler_params=_cparams(("arbitrary",)),
        name="moe_dispatch",
    )(pad_start, pad_count, n_used, pos3, rows)


def _expert(h, cw, wg_ref, wu_ref, wd_ref):
    act = _silu(_dot(h, wg_ref[...])) * _dot(h, wu_ref[...])
    return _dot((act * cw).astype(BF16), wd_ref[...])


def _gmoe_kernel(ea_ref, eb_ref, nt_ref, xs_ref, wga_ref, wua_ref, wda_ref, wgb_ref, wub_ref, wdb_ref, ys_ref):
    i = pl.program_id(0)

    @pl.when(i < nt_ref[0])
    def _():
        x = xs_ref[...]
        h = x[:, 0:D].astype(BF16)
        comb = x[:, D:D + ROUTE_W]
        lane = lax.broadcasted_iota(jnp.int32, comb.shape, 1)
        acc = None
        for e_ref, w_refs in ((ea_ref, (wga_ref, wua_ref, wda_ref)), (eb_ref, (wgb_ref, wub_ref, wdb_ref))):
            cw = jnp.sum(jnp.where(lane == N_GROUPS + e_ref[i], comb, 0.0), axis=-1, keepdims=True)
            part = _expert(h, cw, *w_refs)
            acc = part if acc is None else acc + part
        ys_ref[...] = acc

    @pl.when(i >= nt_ref[0])
    def _():
        ys_ref[...] = jnp.zeros_like(ys_ref)


def _gmoe(exp_a, exp_b, n_used, xs, wg, wu, wd):
    n_tiles = xs.shape[0] // MOE_TR
    tr = MOE_TR
    wspec = lambda shape, which: pl.BlockSpec(
        (None,) + shape, lambda i, ea, eb, nt: ((ea, eb)[which][i], 0, 0))
    grid_spec = pltpu.PrefetchScalarGridSpec(
        num_scalar_prefetch=3,
        grid=(n_tiles,),
        in_specs=[
            pl.BlockSpec((tr, ROW_W), lambda i, ea, eb, nt: (jnp.where(i < nt[0], i, 0), 0)),
            wspec((D, D_FF), 0), wspec((D, D_FF), 0), wspec((D_FF, D), 0),
            wspec((D, D_FF), 1), wspec((D, D_FF), 1), wspec((D_FF, D), 1),
        ],
        out_specs=pl.BlockSpec((tr, D), lambda i, ea, eb, nt: (i, 0)),
    )
    return pl.pallas_call(
        _gmoe_kernel,
        grid_spec=grid_spec,
        out_shape=jax.ShapeDtypeStruct((n_tiles * tr, D), F32),
        compiler_params=_cparams(("arbitrary",)),
        name="moe_grouped",
    )(exp_a, exp_b, n_used, xs, wg, wu, wd, wg, wu, wd)


def _combine_kernel(cur_ref, nxt_ref, x1_ref, ys_hbm, y_ref, buf, sem):
    i = pl.program_id(0)
    slot = i % 2
    tm = x1_ref.shape[0]

    def start_gather(idx_ref, s):
        for r in range(tm):
            _row_copy(ys_hbm, buf.at[s], idx_ref[0, r], r, sem.at[s]).start()

    @pl.when(i == 0)
    def _():
        start_gather(cur_ref, 0)

    @pl.when(i + 1 < pl.num_programs(0))
    def _():
        start_gather(nxt_ref, 1 - slot)

    pltpu.make_async_copy(ys_hbm.at[pl.ds(0, tm)], buf.at[slot], sem.at[slot]).wait()
    y_ref[...] = x1_ref[...] + buf[slot]


def _combine(pos3, x1, ys):
    t = x1.shape[0]
    tm = pos3.shape[2]
    nt = t // tm
    idx = lambda f: pl.BlockSpec((None, 1, tm), f, memory_space=pltpu.SMEM)
    return pl.pallas_call(
        _combine_kernel,
        grid=(nt,),
        in_specs=[
            idx(lambda i: (i, 0, 0)),
            idx(lambda i: (jnp.minimum(i + 1, nt - 1), 0, 0)),
            pl.BlockSpec((tm, D), lambda i: (i, 0)),
            pl.BlockSpec(memory_space=pl.ANY),
        ],
        out_specs=pl.BlockSpec((tm, D), lambda i: (i, 0)),
        out_shape=jax.ShapeDtypeStruct((t, D), F32),
        scratch_shapes=[pltpu.VMEM((2, tm, D), F32), pltpu.SemaphoreType.DMA((2,))],
        compiler_params=_cparams(("arbitrary",)),
        name="moe_combine",
    )(pos3, pos3, x1, ys)


def _moe_routed(x1, rows, comb, wg, wu, wd):
    t = x1.shape[0]
    tr = MOE_TR
    n_tiles = (t + N_BUCKETS * (tr - 1)) // tr
    pos3, cnt = _rank(comb)
    pos = pos3[:, 0, :]
    count = cnt[0, :N_BUCKETS].astype(jnp.int32)
    padded = (count + tr - 1) // tr * tr
    ends = jnp.cumsum(padded)
    pad_start = ends - padded + count
    pad_count = padded - count
    n_used = (ends[-1:] // tr).astype(jnp.int32)
    tile_start = jnp.arange(n_tiles, dtype=jnp.int32) * tr
    tile_bucket = jnp.minimum(jnp.sum(tile_start[:, None] >= ends[None, :], axis=1), N_BUCKETS - 1)
    first, second = _bucket_experts()
    xs = _dispatch(pad_start, pad_count, n_used, pos.reshape(t // RANK_TM, 1, RANK_TM), rows, n_tiles * tr)
    ys = _gmoe(first[tile_bucket], second[tile_bucket], n_used, xs, wg, wu, wd)
    return _combine(pos.reshape(t // COMBINE_TM, 1, COMBINE_TM), x1, ys)


def _moe_kernel(x1_ref, h_ref, comb_ref, wg_ref, wu_ref, wd_ref, o_ref):
    e = pl.program_id(1)

    @pl.when(e == 0)
    def _():
        o_ref[...] = x1_ref[...]

    lane = lax.broadcasted_iota(jnp.int32, comb_ref.shape, 1)
    cw = jnp.sum(jnp.where(lane == e + N_GROUPS, comb_ref[...], 0.0), axis=-1, keepdims=True)
    o_ref[...] += _expert(h_ref[...], cw, wg_ref, wu_ref, wd_ref)


def _moe(x1, h, comb, wg, wu, wd, tm):
    t = x1.shape[0]
    return pl.pallas_call(
        _moe_kernel,
        grid=(t // tm, N_EXPERTS),
        in_specs=[
            pl.BlockSpec((tm, D), lambda i, e: (i, 0)),
            pl.BlockSpec((tm, D), lambda i, e: (i, 0)),
            pl.BlockSpec((tm, ROUTE_W), lambda i, e: (i, 0)),
            pl.BlockSpec((None, D, D_FF), lambda i, e: (e, 0, 0)),
            pl.BlockSpec((None, D, D_FF), lambda i, e: (e, 0, 0)),
            pl.BlockSpec((None, D_FF, D), lambda i, e: (e, 0, 0)),
        ],
        out_specs=pl.BlockSpec((tm, D), lambda i, e: (i, 0)),
        out_shape=jax.ShapeDtypeStruct((t, D), F32),
        compiler_params=_cparams(("parallel", "arbitrary")),
        name="moe",
    )(x1, h, comb, wg, wu, wd)


def _sample_prep_kernel(xbc_ref, q_ref, k_ref, mq_ref, dt_ref, sc_ref, cw_ref, cb_ref, dtb_ref, a_ref, e_ref,
                        qg_ref, kg_ref, mqg_ref, cos_ref, slo_ref, shi_ref,
                        conv_ref, xs_ref, bc_ref, dtx_ref, dec_ref, qn_ref, kn_ref, mqn_ref):
    n = xbc_ref.shape[0]
    x_raw = xbc_ref[...].astype(F32)
    acc = x_raw * cw_ref[SSM_CONV - 1:SSM_CONV, :] + cb_ref[...]
    for j in range(SSM_CONV - 1):
        acc = acc + sc_ref[:, j * CONV_CH:(j + 1) * CONV_CH] * cw_ref[j:j + 1, :]
    conv_ref[:, 0:(SSM_CONV - 2) * CONV_CH] = sc_ref[:, CONV_CH:(SSM_CONV - 1) * CONV_CH]
    conv_ref[:, (SSM_CONV - 2) * CONV_CH:] = x_raw
    act = _silu(acc)
    xs = act[:, 0:SSM_INNER]
    xs_ref[...] = xs
    bc_ref[...] = act[:, SSM_INNER:]
    dt = _softplus(dt_ref[...] + dtb_ref[...])
    e = e_ref[...]
    dtx = _dot_exact(dt, e) * xs
    dec = _dot_exact(jnp.exp(dt * a_ref[...]), e)
    dtx_ref[...] = dtx.T
    dec_ref[...] = dec.T

    lane = lax.broadcasted_iota(jnp.int32, (n, LANE), 1)
    low_half = lane < ATT_HD
    cos_t = jnp.broadcast_to(cos_ref[...], (n, LANE))
    sin_lo = jnp.broadcast_to(slo_ref[...], (n, LANE))
    sin_hi = jnp.broadcast_to(shi_ref[...], (n, LANE))
    qf = q_ref[...].astype(F32)
    for j in range(D // LANE):
        sl = slice(j * LANE, (j + 1) * LANE)
        qn_ref[:, sl] = _norm_rope_tile(qf[:, sl], qg_ref[...], cos_t, sin_lo, sin_hi, low_half)
    kf = k_ref[...].astype(F32)
    for j in range(ATT_KVD // LANE):
        sl = slice(j * LANE, (j + 1) * LANE)
        kn_ref[:, sl] = _norm_rope_tile(kf[:, sl], kg_ref[...], cos_t, sin_lo, sin_hi, low_half)
    mq = mq_ref[...].astype(F32)
    for h in range(MEM_HEADS):
        sl = slice(h * MEM_HD, (h + 1) * MEM_HD)
        mh = mq[:, sl]
        ms = jnp.mean(mh * mh, axis=-1, keepdims=True)
        mqn_ref[:, sl] = mh * lax.rsqrt(ms + EPS) * mqg_ref[...]


def _sample_prep(proj, dt_raw, sc2d, cw, cb, dtb, a, e, qg2, kg2, mqg, cos_t, sin_lo, sin_hi):
    n = proj.shape[0]
    c2 = lambda i: (0, 0)
    full = lambda shape: pl.BlockSpec(shape, c2)
    return pl.pallas_call(
        _sample_prep_kernel,
        grid=(1,),
        in_specs=[
            pl.BlockSpec((n, CONV_CH), lambda i: (0, COL_XBC_BLK)),
            pl.BlockSpec((n, D), lambda i: (0, COL_Q_BLK)),
            pl.BlockSpec((n, ATT_KVD), lambda i: (0, COL_K_BLK)),
            pl.BlockSpec((n, D), lambda i: (0, COL_MQ_BLK)),
            full((n, LANE)),
            full((n, (SSM_CONV - 1) * CONV_CH)),
            full((SSM_CONV, CONV_CH)),
            full((1, CONV_CH)),
            full((1, LANE)),
            full((1, LANE)),
            full((LANE, SSM_INNER)),
            full((1, LANE)),
            full((1, LANE)),
            full((1, MEM_HD)),
            full((1, LANE)),
            full((1, LANE)),
            full((1, LANE)),
        ],
        out_specs=[
            full((n, (SSM_CONV - 1) * CONV_CH)),
            full((n, SSM_INNER)),
            full((n, 2 * SSM_GROUPS * SSM_STATE)),
            full((SSM_INNER, n)),
            full((SSM_INNER, n)),
            full((n, D)),
            full((n, ATT_KVD)),
            full((n, D)),
        ],
        out_shape=[
            jax.ShapeDtypeStruct((n, (SSM_CONV - 1) * CONV_CH), F32),
            jax.ShapeDtypeStruct((n, SSM_INNER), F32),
            jax.ShapeDtypeStruct((n, 2 * SSM_GROUPS * SSM_STATE), F32),
            jax.ShapeDtypeStruct((SSM_INNER, n), F32),
            jax.ShapeDtypeStruct((SSM_INNER, n), F32),
            jax.ShapeDtypeStruct((n, D), F32),
            jax.ShapeDtypeStruct((n, ATT_KVD), F32),
            jax.ShapeDtypeStruct((n, D), F32),
        ],
        compiler_params=_cparams(("arbitrary",)),
        name="sample_prep",
    )(proj, proj, proj, proj, dt_raw, sc2d, cw, cb, dtb, a, e, qg2, kg2, mqg, cos_t, sin_lo, sin_hi)


SSM_S_TILE = 8


def _sample_ssm_kernel(st_ref, dtx_ref, dec_ref, bc_ref, so_ref, yt_ref):
    i = pl.program_id(0)
    n = dtx_ref.shape[1]

    @pl.when(i == 0)
    def _():
        yt_ref[...] = jnp.zeros_like(yt_ref)

    gw = SSM_HPG * SSM_HEADDIM
    lane = lax.broadcasted_iota(jnp.int32, (gw, n), 1)
    nbc = SSM_GROUPS * SSM_STATE
    for s in range(SSM_S_TILE):
        sel = lane == (i * SSM_S_TILE + s)
        for g in range(SSM_GROUPS):
            rows = slice(g * gw, (g + 1) * gw)
            dtx_c = jnp.sum(jnp.where(sel, dtx_ref[rows, :], 0.0), axis=1, keepdims=True)
            dec_c = jnp.sum(jnp.where(sel, dec_ref[rows, :], 0.0), axis=1, keepdims=True)
            bm = bc_ref[s:s + 1, g * SSM_STATE:(g + 1) * SSM_STATE]
            cm = bc_ref[s:s + 1, nbc + g * SSM_STATE:nbc + (g + 1) * SSM_STATE]
            hn = st_ref[s, rows, :] * dec_c + dtx_c * bm
            so_ref[s, rows, :] = hn
            yc = _dot_nt(hn.astype(BF16), jnp.broadcast_to(cm, (n, SSM_STATE)).astype(BF16))
            yt_ref[rows, :] = jnp.where(sel, yc, yt_ref[rows, :])


def _sample_ssm(state, dtx_t, dec_t, bc):
    n = state.shape[0]
    return pl.pallas_call(
        _sample_ssm_kernel,
        grid=(n // SSM_S_TILE,),
        in_specs=[
            pl.BlockSpec((SSM_S_TILE, SSM_INNER, SSM_STATE), lambda i: (i, 0, 0)),
            pl.BlockSpec((SSM_INNER, n), lambda i: (0, 0)),
            pl.BlockSpec((SSM_INNER, n), lambda i: (0, 0)),
            pl.BlockSpec((SSM_S_TILE, 2 * SSM_GROUPS * SSM_STATE), lambda i: (i, 0)),
        ],
        out_specs=[
            pl.BlockSpec((SSM_S_TILE, SSM_INNER, SSM_STATE), lambda i: (i, 0, 0)),
            pl.BlockSpec((SSM_INNER, n), lambda i: (0, 0)),
        ],
        out_shape=[
            jax.ShapeDtypeStruct((n, SSM_INNER, SSM_STATE), F32),
            jax.ShapeDtypeStruct((SSM_INNER, n), F32),
        ],
        compiler_params=_cparams(("arbitrary",)),
        name="sample_ssm",
    )(state, dtx_t, dec_t, bc)


def _sample_post_kernel(yt_ref, xs_ref, z_ref, dexp_ref, ng_ref, y_ref):
    y = yt_ref[...].T + dexp_ref[...] * xs_ref[...]
    yg = y * _silu(z_ref[...].astype(F32))
    ms = jnp.mean(yg * yg, axis=-1, keepdims=True)
    y_ref[...] = yg * lax.rsqrt(ms + EPS) * ng_ref[...]


def _sample_post(y_t, xs, proj, dexp, ng):
    n = xs.shape[0]
    c2 = lambda i: (0, 0)
    return pl.pallas_call(
        _sample_post_kernel,
        grid=(1,),
        in_specs=[
            pl.BlockSpec((SSM_INNER, n), c2),
            pl.BlockSpec((n, SSM_INNER), c2),
            pl.BlockSpec((n, SSM_INNER), lambda i: (0, COL_Z_BLK)),
            pl.BlockSpec((1, SSM_INNER), c2),
            pl.BlockSpec((1, SSM_INNER), c2),
        ],
        out_specs=pl.BlockSpec((n, SSM_INNER), c2),
        out_shape=jax.ShapeDtypeStruct((n, SSM_INNER), F32),
        compiler_params=_cparams(("arbitrary",)),
        name="sample_post",
    )(y_t, xs, proj, dexp, ng)


SWA_S_TILE = 8


def _sample_swa_kernel(q_ref, kn_ref, v_ref, ck_ref, cv_ref, sink_ref, y_ref, ko_ref, vo_ref, sc_ref):
    w = WINDOW
    scale = ATT_HD ** -0.5
    newest = lax.broadcasted_iota(jnp.int32, (ATT_HD, w), 1) == w - 1
    units = [(s, kv) for s in range(SWA_S_TILE) for kv in range(ATT_KV)]
    for u, (s, kv) in enumerate(units):
        dims = slice(kv * ATT_HD, (kv + 1) * ATT_HD)
        kt = jnp.where(newest, kn_ref[dims, s:s + 1], pltpu.roll(ck_ref[s, kv], w - 1, axis=1))
        vt = jnp.where(newest, v_ref[dims, s:s + 1], pltpu.roll(cv_ref[s, kv], w - 1, axis=1))
        ko_ref[s, kv] = kt
        vo_ref[s, kv] = vt
        sc_ref[u * SUBLANES:(u + 1) * SUBLANES, :] = _dot(q_ref[s, kv].astype(BF16), kt.astype(BF16)) * scale
    sc = sc_ref[...]
    snk = jnp.concatenate([sink_ref[...]] * SWA_S_TILE, axis=0)
    m = jnp.maximum(jnp.max(sc, axis=-1, keepdims=True), snk)
    p = jnp.exp(sc - m)
    pn = p / (jnp.sum(p, axis=-1, keepdims=True) + jnp.exp(snk - m))
    for u, (s, kv) in enumerate(units):
        pu = pn[u * SUBLANES:(u + 1) * SUBLANES].astype(BF16)
        y_ref[s, kv] = _dot_nt(pu, vo_ref[s, kv].astype(BF16))


def _sample_swa(q4, kn4, v4, ck_t, cv_t, sink_col, l):
    n = q4.shape[0]
    w = WINDOW
    st = SWA_S_TILE
    cache = pl.BlockSpec((None, st, ATT_KV, ATT_HD, w), lambda i: (l, i, 0, 0, 0))
    new = pl.BlockSpec((None, ATT_KVD, st), lambda i: (i, 0, 0))
    out = pl.BlockSpec((st, ATT_KV, ATT_HD, w), lambda i: (i, 0, 0, 0))
    qspec = pl.BlockSpec((st, ATT_KV, SUBLANES, ATT_HD), lambda i: (i, 0, 0, 0))
    return pl.pallas_call(
        _sample_swa_kernel,
        grid=(n // st,),
        in_specs=[
            qspec, new, new, cache, cache,
            pl.BlockSpec((ATT_KV * SUBLANES, 1), lambda i: (0, 0)),
        ],
        out_specs=[qspec, out, out],
        out_shape=[
            jax.ShapeDtypeStruct((n, ATT_KV, SUBLANES, ATT_HD), F32),
            jax.ShapeDtypeStruct((n, ATT_KV, ATT_HD, w), F32),
            jax.ShapeDtypeStruct((n, ATT_KV, ATT_HD, w), F32),
        ],
        scratch_shapes=[pltpu.VMEM((st * ATT_KV * SUBLANES, w), F32)],
        compiler_params=_cparams(("parallel",)),
        name="sample_swa",
    )(q4, kn4, v4, ck_t, cv_t, sink_col)


MEM_S_TILE = 4


def _sample_mem_kernel(q_ref, k_ref, v_ref, y_ref):
    scale = MEM_HD ** -0.5
    for s in range(MEM_S_TILE):
        sc = jnp.sum(k_ref[s] * q_ref[s][None], axis=-1, keepdims=True) * scale
        m = jnp.max(sc, axis=0, keepdims=True)
        p = jnp.exp(sc - m)
        den = jnp.sum(p, axis=0, keepdims=True)
        o = jnp.sum(p * v_ref[s], axis=0, keepdims=True) / den
        y_ref[s] = o[0]


def _sample_mem(q3, ck, cv, l):
    n = q3.shape[0]
    mem_len = ck.shape[2]
    st = MEM_S_TILE
    cache = pl.BlockSpec((None, st, mem_len, MEM_HEADS, MEM_HD), lambda i: (l, i, 0, 0, 0))
    return pl.pallas_call(
        _sample_mem_kernel,
        grid=(n // st,),
        in_specs=[pl.BlockSpec((st, MEM_HEADS, MEM_HD), lambda i: (i, 0, 0)), cache, cache],
        out_specs=pl.BlockSpec((st, MEM_HEADS, MEM_HD), lambda i: (i, 0, 0)),
        out_shape=jax.ShapeDtypeStruct((n, MEM_HEADS, MEM_HD), F32),
        compiler_params=_cparams(("parallel",)),
        name="sample_mem",
    )(q3, ck, cv)


def _prep_weights(lw):
    w_in = lw['w_in']
    cols = [w_in[:, 0:OFF_Z], w_in[:, OFF_XBC:OFF_DT], w_in[:, OFF_Z:OFF_XBC], w_in[:, OFF_Q:OFF_K],
            w_in[:, OFF_MQ:OFF_MQ + D], w_in[:, OFF_K:OFF_V], w_in[:, OFF_V:OFF_MQ]]
    p = {}
    p['w_main'] = jnp.concatenate(cols, axis=1).astype(BF16)
    p['w_dt'] = jnp.pad(w_in[:, OFF_DT:OFF_Q], ((0, 0), (0, LANE - SSM_HEADS))).astype(BF16)
    p['norm1_g'] = lw['norm1_g'].reshape(1, D)
    p['cw'] = lw['ssm_conv_w']
    p['cb'] = lw['ssm_conv_b'].reshape(1, CONV_CH)
    pad_h = (0, LANE - SSM_HEADS)
    p['dtb'] = jnp.pad(lw['ssm_dt_bias'].astype(F32), pad_h).reshape(1, LANE)
    p['a'] = jnp.pad(-jnp.exp(lw['ssm_a_log'].astype(F32)), pad_h).reshape(1, LANE)
    p['dexp'] = jnp.repeat(lw['ssm_d'], SSM_HEADDIM).reshape(1, SSM_INNER)
    p['ssm_ng'] = lw['ssm_norm_g'].reshape(1, SSM_INNER)
    head_of_ch = jnp.arange(SSM_INNER) // SSM_HEADDIM
    p['e'] = (jnp.arange(LANE)[:, None] == head_of_ch[None, :]).astype(F32)
    p['qg2'] = jnp.tile(lw['att_q_norm_g'], 2).reshape(1, LANE)
    p['kg2'] = jnp.tile(lw['att_k_norm_g'], 2).reshape(1, LANE)
    p['sink_row'] = lw['att_sink'].astype(F32).reshape(1, ATT_HEADS)
    grp = ATT_HEADS // ATT_KV
    p['sink_col'] = jnp.pad(lw['att_sink'].astype(F32).reshape(ATT_KV, grp),
                            ((0, 0), (0, SUBLANES - grp))).reshape(ATT_KV * SUBLANES, 1)
    p['mem_g'] = lw['mem_norm_g'].reshape(1, D)
    p['w_mem_kv'] = lw['w_mem_kv'].astype(BF16)
    p['mqg'] = lw['mem_q_norm_g'].reshape(1, MEM_HD)
    p['mkg'] = lw['mem_k_norm_g'].reshape(1, MEM_HD)
    p['ws'] = lw['w_br_ssm'].astype(BF16)
    p['wa'] = lw['w_br_swa'].astype(BF16)
    p['wm'] = lw['w_br_mem'].astype(BF16)
    p['wo'] = lw['w_out'].astype(BF16)
    p['g2'] = lw['norm2_g'].reshape(1, D)
    pad_r = ROUTE_W - N_GROUPS - N_EXPERTS
    p['wr'] = jnp.pad(jnp.concatenate([lw['w_router_group'], lw['w_router_expert']], axis=1).astype(F32),
                      ((0, 0), (0, pad_r)))
    p['br'] = jnp.pad(jnp.concatenate([lw['b_router_group'], lw['b_router_expert']]).astype(F32),
                      (0, pad_r)).reshape(1, ROUTE_W)
    p['wg'] = lw['w_exp_gate'].astype(BF16)
    p['wu'] = lw['w_exp_up'].astype(BF16)
    p['wd'] = lw['w_exp_down'].astype(BF16)
    return p


def _pick(n, prefs):
    for c in prefs:
        if n % c == 0:
            return c
    return n


def _tail(x, proj, y_ssm, y_swa, y_mem, p):
    t = x.shape[0]
    routed = t % RANK_TM == 0
    x1, h, comb = _back(x, proj, y_ssm, y_swa, y_mem, p['ws'], p['wa'], p['wm'], p['wo'], p['g2'],
                        p['wr'], p['br'], _pick(t, (512, 256, 128)), routed)
    if routed:
        return _moe_routed(x1, h, comb, p['wg'], p['wu'], p['wd'])
    return _moe(x1, h, comb, p['wg'], p['wu'], p['wd'], _pick(t, (1024, 512, 256, 128)))


def _prompt_layer(x, mem, p):
    nb, seq, _ = x.shape
    t = nb * seq
    nc = seq // CHUNK
    xf = x.reshape(t, D)
    proj, dt_raw = _front(xf, p['norm1_g'], p['w_main'], p['w_dt'], _pick(t, (1024, 512, 256, 128)), 3584)
    y_ssm, conv_new, ssm_new = _ssd(proj, dt_raw, p['cw'], p['cb'], p['dtb'], p['a'], p['dexp'],
                                    p['ssm_ng'], nb, nc)
    cos_t, sin_lo, sin_hi = _rope_tables(jnp.arange(seq))
    y_swa, k_new, v_new = _swa(proj, p['qg2'], p['kg2'], cos_t, sin_lo, sin_hi, p['sink_row'], nb, seq)
    mem_len = mem.shape[1]
    mk, mv = _memkv(mem.reshape(nb * mem_len, D), p['mem_g'], p['w_mem_kv'], p['mkg'],
                    _pick(nb * mem_len, (512, 256, 128)))
    y_mem = _memattn(proj, mk, mv, p['mqg'], nb, seq, mem_len, _pick(seq, (512, 256, 128)))
    y = _tail(xf, proj, y_ssm, y_swa, y_mem, p)
    return (y.reshape(nb, seq, D), conv_new,
            ssm_new.reshape(nb, SSM_HEADS, SSM_HEADDIM, SSM_STATE),
            k_new.reshape(nb, WINDOW, ATT_KV, ATT_HD), v_new.reshape(nb, WINDOW, ATT_KV, ATT_HD),
            mk.reshape(nb, mem_len, MEM_HEADS, MEM_HD), mv.reshape(nb, mem_len, MEM_HEADS, MEM_HD))


def _sample_layer(x, conv_st, ssm_st, swa_k, swa_v, mem_k, mem_v, l, p):
    n = x.shape[0]
    xf = x.reshape(n, D)
    proj, dt_raw = _front(xf, p['norm1_g'], p['w_main'], p['w_dt'], n, 1536)
    cos_t, sin_lo, sin_hi = _rope_tables(jnp.full((1,), PAST_LEN, jnp.int32))
    sc2d = conv_st.reshape(n, (SSM_CONV - 1) * CONV_CH)
    conv_new, xs, bc, dtx_t, dec_t, qn, kn, mqn = _sample_prep(
        proj, dt_raw, sc2d, p['cw'], p['cb'], p['dtb'], p['a'], p['e'], p['qg2'], p['kg2'], p['mqg'],
        cos_t, sin_lo, sin_hi)
    ssm_new, y_t = _sample_ssm(ssm_st.reshape(n, SSM_INNER, SSM_STATE), dtx_t, dec_t, bc)
    y_ssm = _sample_post(y_t, xs, proj, p['dexp'], p['ssm_ng'])
    grp = ATT_HEADS // ATT_KV
    v_raw = proj[:, COL_V_BLK * ATT_KVD:(COL_V_BLK + 1) * ATT_KVD].astype(F32)
    to_t = (0, 1, 3, 4, 2)
    per_step = lambda a: a.T.reshape(ATT_KVD, n // SWA_S_TILE, SWA_S_TILE).transpose(1, 0, 2)
    y_swa, k_new, v_new = _sample_swa(
        jnp.pad(qn.reshape(n, ATT_KV, grp, ATT_HD), ((0, 0), (0, 0), (0, SUBLANES - grp), (0, 0))),
        per_step(kn), per_step(v_raw), swa_k.transpose(to_t), swa_v.transpose(to_t), p['sink_col'], l)
    y_swa = y_swa[:, :, 0:grp, :].reshape(n, D)
    k_new = k_new.transpose(0, 3, 1, 2)
    v_new = v_new.transpose(0, 3, 1, 2)
    y_mem = _sample_mem(mqn.reshape(n, MEM_HEADS, MEM_HD), mem_k, mem_v, l)
    y = _tail(xf, proj, y_ssm, y_swa.astype(BF16), y_mem.reshape(n, D).astype(BF16), p)
    return (y.reshape(n, 1, D), conv_new.reshape(n, SSM_CONV - 1, CONV_CH),
            ssm_new.reshape(n, SSM_HEADS, SSM_HEADDIM, SSM_STATE), k_new, v_new)


def kernel(x_prompt, x_sample, state_conv, state_ssm, cache_swa_k, cache_swa_v, cache_mem_k, cache_mem_v,
           mem_prompt, norm1_g, w_in, ssm_conv_w, ssm_conv_b, ssm_dt_bias, ssm_a_log, ssm_d, ssm_norm_g,
           att_q_norm_g, att_k_norm_g, att_sink, mem_norm_g, w_mem_kv, mem_q_norm_g, mem_k_norm_g,
           w_br_ssm, w_br_swa, w_br_mem, w_out, norm2_g, w_router_group, b_router_group,
           w_router_expert, b_router_expert, w_exp_gate, w_exp_up, w_exp_down):
    weights = dict(norm1_g=norm1_g, w_in=w_in, ssm_conv_w=ssm_conv_w, ssm_conv_b=ssm_conv_b,
                   ssm_dt_bias=ssm_dt_bias, ssm_a_log=ssm_a_log, ssm_d=ssm_d, ssm_norm_g=ssm_norm_g,
                   att_q_norm_g=att_q_norm_g, att_k_norm_g=att_k_norm_g, att_sink=att_sink,
                   mem_norm_g=mem_norm_g, w_mem_kv=w_mem_kv, mem_q_norm_g=mem_q_norm_g,
                   mem_k_norm_g=mem_k_norm_g, w_br_ssm=w_br_ssm, w_br_swa=w_br_swa, w_br_mem=w_br_mem,
                   w_out=w_out, norm2_g=norm2_g, w_router_group=w_router_group,
                   b_router_group=b_router_group, w_router_expert=w_router_expert,
                   b_router_expert=b_router_expert, w_exp_gate=w_exp_gate, w_exp_up=w_exp_up,
                   w_exp_down=w_exp_down)
    depth = w_in.shape[0]
    xp, xs = x_prompt, x_sample
    outs = [[] for _ in range(10)]
    for l in range(depth):
        p = _prep_weights({k: v[l] for k, v in weights.items()})
        xp, c1, c2, c3, c4, c5, c6 = _prompt_layer(xp, mem_prompt, p)
        xs, d1, d2, d3, d4 = _sample_layer(xs, state_conv[l], state_ssm[l], cache_swa_k, cache_swa_v,
                                           cache_mem_k, cache_mem_v, l, p)
        for lst, val in zip(outs, (c1, c2, c3, c4, c5, c6, d1, d2, d3, d4)):
            lst.append(val)
    return (xp, xs) + tuple(jnp.stack(o) for o in outs)
```
